```python
import numpy as np
import jax
import jax.numpy as jnp
from jax import lax

D_MODEL = 1024
BATCH = 8
SEQ = 2048
DEPTH = 1
DEC_BATCH = 8
DEC_SEQ = 32
PAST_LEN = 4096

CHUNK = 64
N_MEM = 256
EPS = 1e-6
A_HEADS = 4
A_HEAD_DIM = 128
A_WIDTH = A_HEADS * A_HEAD_DIM
B_WIDTH = D_MODEL
B_BLOCKS = 16
B_BLOCK_DIM = B_WIDTH // B_BLOCKS
CONV_W = 4
LRU_C = 8.0
C_HEADS = 4
C_HEAD_DIM = 128
C_WIDTH = C_HEADS * C_HEAD_DIM
N_BRANCH = 3
IN_COLS = 4 * A_WIDTH + 2 * B_WIDTH + 2 * C_WIDTH + N_BRANCH * D_MODEL

kernel_name = 'hybrid_hgrn2_rglru_memxattn_stream_step'


def _rmsnorm(x, g):
    xf = x.astype(jnp.float32)
    y = xf * lax.rsqrt(jnp.mean(xf * xf, axis=-1, keepdims=True) + EPS) * g.astype(jnp.float32)
    return y.astype(x.dtype)


def _hgrn2(q, logf, k, v, s0):
    bsz, seq, nh, _ = q.shape
    dv = v.shape[-1]
    c = min(CHUNK, seq)
    n = seq // c

    def blocks(t):
        return t.reshape(bsz, n, c, nh, t.shape[-1]).transpose(1, 0, 3, 2, 4)

    mask = jnp.tril(jnp.ones((c, c), dtype=bool))

    def step(s, inp):
        qc, lc, kc, vc = inp
        b = jnp.cumsum(lc, axis=2)
        qg = qc * jnp.exp(b)
        kg = kc * jnp.exp(-b)
        att = jnp.where(mask, jnp.einsum('bhtk,bhsk->bhts', qg, kg), 0.0)
        o = jnp.einsum('bhts,bhsv->bhtv', att, vc) + jnp.einsum('bhtk,bhkv->bhtv', qg, s)
        b_last = b[:, :, -1:, :]
        s = jnp.exp(b_last[:, :, 0, :])[..., None] * s + jnp.einsum('bhsk,bhsv->bhkv', kc * jnp.exp(b_last - b), vc)
        return s, o

    s_last, o = lax.scan(step, s0, (blocks(q), blocks(logf), blocks(k), blocks(v)))
    o = o.transpose(1, 0, 3, 2, 4).reshape(bsz, seq, nh, dv)
    return o, s_last


def _lru_scan(a, u, h0):
    u = u.at[:, 0].add(a[:, 0] * h0)

    def comb(left, right):
        al, ul = left
        ar, ur = right
        return al * ar, ar * ul + ur

    _, hs = lax.associative_scan(comb, (a, u), axis=1)
    return hs, hs[:, -1]


def _memory_kv(mem, g_mem, w_mem_k, w_mem_v):
    bsz = mem.shape[0]
    hm = _rmsnorm(mem, g_mem)
    mk = (hm @ w_mem_k).reshape(bsz, N_MEM, C_HEADS, C_HEAD_DIM)
    mv = (hm @ w_mem_v).reshape(bsz, N_MEM, C_HEADS, C_HEAD_DIM)
    return mk, mv


def _mixer_layer(x, mem_k, mem_v, s_hgrn, conv_ctx, h_lru, first_is_start, lb,
                 g_mix, w_in, g_a_out, w_a_down, w_conv, b_conv, w_lru_r, b_lru_r,
                 w_lru_i, b_lru_i, lru_lambda, w_b_down, w_c_down, w_out):
    f32 = jnp.float32
    bsz, seq, _ = x.shape
    h = _rmsnorm(x, g_mix)
    z = h @ w_in
    sizes = (A_WIDTH, A_WIDTH, A_WIDTH, A_WIDTH, B_WIDTH, B_WIDTH, C_WIDTH, C_WIDTH, D_MODEL, D_MODEL, D_MODEL)
    offs = [int(o) for o in np.cumsum(sizes)[:-1]]
    qa, fa, va, ga, xb, gb, qc, gc, za, zb, zc = jnp.split(z, offs, axis=-1)

    head4 = (bsz, seq, A_HEADS, A_HEAD_DIM)
    f = lb + (1.0 - lb) * jax.nn.sigmoid(fa.astype(f32))
    oa, s_hgrn_new = _hgrn2(jax.nn.silu(qa.astype(f32)).reshape(head4), jnp.log(f).reshape(head4),
                            (1.0 - f).reshape(head4), va.astype(f32).reshape(head4), s_hgrn.astype(f32))
    oa = _rmsnorm(oa, g_a_out.reshape(A_HEADS, A_HEAD_DIM)).reshape(bsz, seq, A_WIDTH).astype(x.dtype)
    pa = (oa * jax.nn.silu(ga)) @ w_a_down

    xp = jnp.concatenate([conv_ctx.astype(xb.dtype), xb], axis=1)
    xc = b_conv + sum(w_conv[j] * xp[:, j:j + seq] for j in range(CONV_W))
    conv_new = xp[:, seq:]
    xcb = xc.astype(f32).reshape(bsz, seq, B_BLOCKS, B_BLOCK_DIM)
    r = jax.nn.sigmoid(jnp.einsum('blnd,nde->blne', xcb, w_lru_r.astype(f32)) + b_lru_r.astype(f32).reshape(B_BLOCKS, B_BLOCK_DIM))
    ig = jax.nn.sigmoid(jnp.einsum('blnd,nde->blne', xcb, w_lru_i.astype(f32)) + b_lru_i.astype(f32).reshape(B_BLOCKS, B_BLOCK_DIM))
    log_a = -LRU_C * r * jax.nn.softplus(-lru_lambda.astype(f32).reshape(B_BLOCKS, B_BLOCK_DIM))
    mult = jnp.sqrt(-jnp.expm1(2.0 * log_a))
    if first_is_start:
        mult = jnp.where(jnp.arange(seq)[None, :, None, None] == 0, 1.0, mult)
    u = (mult * ig * xcb).reshape(bsz, seq, B_WIDTH)
    hb, h_lru_new = _lru_scan(jnp.exp(log_a).reshape(bsz, seq, B_WIDTH), u, h_lru.astype(f32))
    pb = (hb.astype(x.dtype) * jax.nn.silu(gb)) @ w_b_down

    qch = qc.reshape(bsz, seq, C_HEADS, C_HEAD_DIM)
    sc = jnp.einsum('blhd,bmhd->bhlm', qch, mem_k).astype(f32) * (C_HEAD_DIM ** -0.5)
    pr = jax.nn.softmax(sc, axis=-1).astype(x.dtype)
    oc = jnp.einsum('bhlm,bmhd->blhd', pr, mem_v).reshape(bsz, seq, C_WIDTH)
    pc = (oc * jax.nn.silu(gc)) @ w_c_down

    merged = jax.nn.sigmoid(za) * pa + jax.nn.sigmoid(zb) * pb + jax.nn.sigmoid(zc) * pc
    y = x + merged @ w_out
    return y, s_hgrn_new, conv_new, h_lru_new


def setup_inputs(seed: int = 0) -> dict:
    key = jax.random.key(seed)
    ks = jax.random.split(key, 32)
    f32 = jnp.float32

    def nrm(k, shape, s):
        return jax.random.normal(k, shape, f32) * s

    u = jax.random.uniform(ks[19], (DEPTH, B_WIDTH), f32, 0.9, 0.999)
    sg = u ** (1.0 / LRU_C)
    return {
        'x_prompt': nrm(ks[0], (BATCH, SEQ, D_MODEL), 1.0),
        'x_sample': nrm(ks[1], (DEC_BATCH, DEC_SEQ, D_MODEL), 1.0),
        'mem_prompt': nrm(ks[2], (BATCH, N_MEM, D_MODEL), 1.0),
        'cache_mem_k': nrm(ks[3], (DEPTH, DEC_BATCH, N_MEM, C_HEADS, C_HEAD_DIM), 1.0),
        'cache_mem_v': nrm(ks[4], (DEPTH, DEC_BATCH, N_MEM, C_HEADS, C_HEAD_DIM), 1.0),
        'state_hgrn': nrm(ks[5], (DEPTH, DEC_BATCH, A_HEADS, A_HEAD_DIM, A_HEAD_DIM), 0.5),
        'state_conv': nrm(ks[6], (DEPTH, DEC_BATCH, CONV_W - 1, B_WIDTH), 1.0),
        'state_lru': nrm(ks[7], (DEPTH, DEC_BATCH, B_WIDTH), 0.5),
        'g_mix': 1.0 + nrm(ks[8], (DEPTH, D_MODEL), 0.05),
        'w_in': nrm(ks[9], (DEPTH, D_MODEL, IN_COLS), D_MODEL ** -0.5),
        'lb_logits': nrm(ks[10], (DEPTH + 1, A_WIDTH), 0.1),
        'g_a_out': 1.0 + nrm(ks[11], (DEPTH, A_WIDTH), 0.05),
        'w_a_down': nrm(ks[12], (DEPTH, A_WIDTH, D_MODEL), A_WIDTH ** -0.5),
        'w_conv': nrm(ks[13], (DEPTH, CONV_W, B_WIDTH), CONV_W ** -0.5),
        'b_conv': nrm(ks[14], (DEPTH, B_WIDTH), 0.01),
        'w_lru_r': nrm(ks[15], (DEPTH, B_BLOCKS, B_BLOCK_DIM, B_BLOCK_DIM), B_BLOCK_DIM ** -0.5),
        'b_lru_r': nrm(ks[16], (DEPTH, B_WIDTH), 0.01),
        'w_lru_i': nrm(ks[17], (DEPTH, B_BLOCKS, B_BLOCK_DIM, B_BLOCK_DIM), B_BLOCK_DIM ** -0.5),
        'b_lru_i': nrm(ks[18], (DEPTH, B_WIDTH), 0.01),
        'lru_lambda': jnp.log(sg) - jnp.log1p(-sg),
        'w_b_down': nrm(ks[20], (DEPTH, B_WIDTH, D_MODEL), B_WIDTH ** -0.5),
        'g_mem': 1.0 + nrm(ks[21], (DEPTH, D_MODEL), 0.05),
        'w_mem_k': nrm(ks[22], (DEPTH, D_MODEL, C_WIDTH), D_MODEL ** -0.5),
        'w_mem_v': nrm(ks[23], (DEPTH, D_MODEL, C_WIDTH), D_MODEL ** -0.5),
        'w_c_down': nrm(ks[24], (DEPTH, C_WIDTH, D_MODEL), C_WIDTH ** -0.5),
        'w_out': nrm(ks[25], (DEPTH, D_MODEL, D_MODEL), D_MODEL ** -0.5),
        'g_final': 1.0 + nrm(ks[26], (D_MODEL,), 0.05),
    }


def reference(x_prompt, x_sample, mem_prompt, cache_mem_k, cache_mem_v, state_hgrn, state_conv, state_lru,
              g_mix, w_in, lb_logits, g_a_out, w_a_down, w_conv, b_conv, w_lru_r, b_lru_r, w_lru_i, b_lru_i,
              lru_lambda, w_b_down, g_mem, w_mem_k, w_mem_v, w_c_down, w_out, g_final):
    f32 = jnp.float32
    lb_all = jnp.cumsum(jax.nn.softmax(lb_logits.astype(f32), axis=0), axis=0)
    bp = x_prompt.shape[0]
    hp, hs = x_prompt, x_sample
    p_hgrn, p_conv, p_lru, p_mk, p_mv = [], [], [], [], []
    s_hgrn, s_conv, s_lru = [], [], []
    for l in range(DEPTH):
        lw = (g_mix[l], w_in[l], g_a_out[l], w_a_down[l], w_conv[l], b_conv[l], w_lru_r[l], b_lru_r[l],
              w_lru_i[l], b_lru_i[l], lru_lambda[l], w_b_down[l], w_c_down[l], w_out[l])
        mk, mv = _memory_kv(mem_prompt, g_mem[l], w_mem_k[l], w_mem_v[l])
        hp, a1, c1, r1 = _mixer_layer(hp, mk, mv,
                                      jnp.zeros((bp, A_HEADS, A_HEAD_DIM, A_HEAD_DIM), f32),
                                      jnp.zeros((bp, CONV_W - 1, B_WIDTH), x_prompt.dtype),
                                      jnp.zeros((bp, B_WIDTH), f32),
                                      True, lb_all[l], *lw)
        hs, a2, c2, r2 = _mixer_layer(hs, cache_mem_k[l], cache_mem_v[l], state_hgrn[l], state_conv[l], state_lru[l],
                                      False, lb_all[l], *lw)
        p_hgrn.append(a1.astype(x_prompt.dtype))
        p_conv.append(c1.astype(x_prompt.dtype))
        p_lru.append(r1.astype(x_prompt.dtype))
        p_mk.append(mk)
        p_mv.append(mv)
        s_hgrn.append(a2.astype(state_hgrn.dtype))
        s_conv.append(c2.astype(state_conv.dtype))
        s_lru.append(r2.astype(state_lru.dtype))
    y_prompt = _rmsnorm(hp, g_final)
    y_sample = _rmsnorm(hs, g_final)
    return (y_prompt, y_sample, jnp.stack(p_hgrn), jnp.stack(p_conv), jnp.stack(p_lru), jnp.stack(p_mk),
            jnp.stack(p_mv), jnp.stack(s_hgrn), jnp.stack(s_conv), jnp.stack(s_lru))
```

```python
import functools

import jax
import jax.numpy as jnp
from jax import lax
from jax.experimental import pallas as pl
from jax.experimental.pallas import tpu as pltpu

f32 = jnp.float32
bf16 = jnp.bfloat16

D_MODEL = 1024
N_MEM = 256
EPS = 1e-6
A_HEADS = 4
HEAD_DIM = 128
A_WIDTH = A_HEADS * HEAD_DIM
B_WIDTH = D_MODEL
B_BLOCKS = 16
B_BLOCK_DIM = B_WIDTH // B_BLOCKS
CONV_W = 4
LRU_C = 8.0
C_HEADS = 4
C_WIDTH = C_HEADS * HEAD_DIM
HGRN_CHUNK = 64
IN_COLS = 4 * A_WIDTH + 2 * B_WIDTH + 2 * C_WIDTH + 3 * D_MODEL

_QA, _FA, _VA, _GA = 0, A_WIDTH, 2 * A_WIDTH, 3 * A_WIDTH
_XB = 4 * A_WIDTH
_GB = _XB + B_WIDTH
_QC = _GB + B_WIDTH
_GC = _QC + C_WIDTH
_ZA = _GC + C_WIDTH
_ZB = _ZA + D_MODEL
_ZC = _ZB + D_MODEL

MXU_TILE = 256
LRU_GROUPS = B_WIDTH // MXU_TILE
SUBLANES = 8
CTX_ROWS = CONV_W - 1

PROMPT_TILE = 256
VMEM_LIMIT_BYTES = 60 * 1024 * 1024


def _rms(x, g):
    return x * lax.rsqrt(jnp.mean(x * x, axis=-1, keepdims=True) + EPS) * g


def _dot(a, b):
    return jnp.dot(a.astype(bf16), b.astype(bf16), preferred_element_type=f32)


def _dot_nt(a, b):
    return lax.dot_general(a.astype(bf16), b.astype(bf16), (((1,), (1,)), ((), ())),
                           preferred_element_type=f32)


def _dot_tn(a, b):
    return lax.dot_general(a.astype(bf16), b.astype(bf16), (((0,), (0,)), ((), ())),
                           preferred_element_type=f32)


def _silu(x):
    return x * jax.nn.sigmoid(x)


def _row_in_seg(rows, width, seg):
    return lax.broadcasted_iota(jnp.int32, (rows, width), 0) & (seg - 1)


def _cumsum_rows(x, seg, ris):
    d = 1
    while d < seg:
        x = x + jnp.where(ris >= d, pltpu.roll(x, d, 0), 0.0)
        d *= 2
    return x


def _linear_scan_rows(a, u, seg, ris):
    d = 1
    while d < seg:
        keep = ris >= d
        a_prev = jnp.where(keep, pltpu.roll(a, d, 0), 1.0)
        u_prev = jnp.where(keep, pltpu.roll(u, d, 0), 0.0)
        u = a * u_prev + u
        a = a * a_prev
        d *= 2
    return a, u


def _layer_tile(x, z_ref, oa_ref, xpad_ref, kv, states, w, *, seg, chunk, first_rows_start):
    rows = x.shape[0]
    nseg = rows // seg
    nchunk = seg // chunk

    h = _rms(x, w["g_mix"][...]).astype(bf16)
    col_blk = 1024
    for c0 in range(0, IN_COLS, col_blk):
        z_ref[:, c0:c0 + col_blk] = jnp.dot(h, w["w_in"][:, c0:c0 + col_blk], preferred_element_type=f32)

    lg = w["lb_logits"][...]
    l0, l1 = lg[0:1, :], lg[1:2, :]
    lmax = jnp.maximum(l0, l1)
    e0, e1 = jnp.exp(l0 - lmax), jnp.exp(l1 - lmax)
    lb = e0 / (e0 + e1)

    ris_a = _row_in_seg(rows, A_WIDTH, chunk)
    f = lb + (1.0 - lb) * jax.nn.sigmoid(z_ref[:, _FA:_FA + A_WIDTH])
    b = _cumsum_rows(jnp.log(f), chunk, ris_a)
    k = 1.0 - f
    qg = _silu(z_ref[:, _QA:_QA + A_WIDTH]) * jnp.exp(b)
    kg = k * jnp.exp(-b)
    v = z_ref[:, _VA:_VA + A_WIDTH]

    tt = lax.broadcasted_iota(jnp.int32, (rows, rows), 0)
    ss = lax.broadcasted_iota(jnp.int32, (rows, rows), 1)
    shift = chunk.bit_length() - 1
    causal = ((tt >> shift) == (ss >> shift)) & (ss <= tt)

    new_st = [[None] * A_HEADS for _ in range(nseg)]
    for hd in range(A_HEADS):
        sl = slice(hd * HEAD_DIM, (hd + 1) * HEAD_DIM)
        att = jnp.where(causal, _dot_nt(qg[:, sl], kg[:, sl]), 0.0)
        o_intra = _dot(att, v[:, sl])
        for s in range(nseg):
            st = states[s][0][hd]
            for c in range(nchunk):
                r0 = s * seg + c * chunk
                rs = slice(r0, r0 + chunk)
                oa_ref[rs, sl] = o_intra[rs, :] + _dot_nt(qg[rs, sl], st)
                b_last = b[r0 + chunk - 1:r0 + chunk, sl]
                kd = k[rs, sl] * jnp.exp(b_last - b[rs, sl])
                st = st * jnp.exp(b_last) + _dot_tn(v[rs, sl], kd)
            new_st[s][hd] = st

    g_a = w["g_a_out"][...]
    a_in = []
    for hd in range(A_HEADS):
        sl = slice(hd * HEAD_DIM, (hd + 1) * HEAD_DIM)
        a_in.append(_rms(oa_ref[:, sl], g_a[:, sl]))
    a_in = jnp.concatenate(a_in, axis=-1) * _silu(z_ref[:, _GA:_GA + A_WIDTH])
    pa = _dot(a_in, w["w_a_down"][...])

    pad = SUBLANES
    stride = pad + seg
    new_ctx = []
    for s in range(nseg):
        base = s * stride
        xpad_ref[base + pad - CTX_ROWS:base + pad, :] = states[s][1]
        xpad_ref[base + pad:base + pad + seg, :] = z_ref[s * seg:(s + 1) * seg, _XB:_XB + B_WIDTH]
    w_conv = w["w_conv"][...]
    xc = []
    for s in range(nseg):
        base = s * stride + pad - CTX_ROWS
        acc = w_conv[0:1, :] * xpad_ref[base:base + seg, :]
        for j in range(1, CONV_W):
            acc = acc + w_conv[j:j + 1, :] * xpad_ref[base + j:base + j + seg, :]
        xc.append(w["b_conv"][...] + acc)
        new_ctx.append(xpad_ref[base + seg:base + seg + CTX_ROWS, :])
    xc = xc[0] if nseg == 1 else jnp.concatenate(xc, axis=0)

    xc_b = xc.astype(bf16)
    r_pre, i_pre = [], []
    for g in range(LRU_GROUPS):
        gs = slice(g * MXU_TILE, (g + 1) * MXU_TILE)
        r_pre.append(jnp.dot(xc_b[:, gs], w["w_lru_r"][g], preferred_element_type=f32))
        i_pre.append(jnp.dot(xc_b[:, gs], w["w_lru_i"][g], preferred_element_type=f32))
    r = jax.nn.sigmoid(jnp.concatenate(r_pre, axis=-1) + w["b_lru_r"][...])
    ig = jax.nn.sigmoid(jnp.concatenate(i_pre, axis=-1) + w["b_lru_i"][...])
    neg_lam = -w["lru_lambda"][...]
    softplus = jnp.maximum(neg_lam, 0.0) + jnp.log1p(jnp.exp(-jnp.abs(neg_lam)))
    log_a = -LRU_C * r * softplus
    a = jnp.exp(log_a)
    mult = jnp.sqrt(-jnp.tanh(log_a) * (a * a + 1.0))
    ris_b = _row_in_seg(rows, B_WIDTH, seg)
    if first_rows_start is not None:
        mult = jnp.where((ris_b == 0) & first_rows_start, 1.0, mult)
    a_cum, hb = _linear_scan_rows(a, mult * ig * xc, seg, ris_b)
    if nseg == 1:
        h0 = states[0][2]
    else:
        h0 = jnp.concatenate([jnp.broadcast_to(states[s][2], (seg, B_WIDTH)) for s in range(nseg)], axis=0)
    hb = hb + a_cum * h0
    new_hl = [hb[(s + 1) * seg - 1:(s + 1) * seg, :] for s in range(nseg)]
    pb = _dot(hb * _silu(z_ref[:, _GB:_GB + B_WIDTH]), w["w_b_down"][...])

    scale = HEAD_DIM ** -0.5
    oc = []
    for s in range(nseg):
        mem_k, mem_v = kv(s)
        rs = slice(s * seg, (s + 1) * seg)
        heads = []
        for hd in range(C_HEADS):
            sl = slice(hd * HEAD_DIM, (hd + 1) * HEAD_DIM)
            sc = _dot_nt(z_ref[rs, _QC + hd * HEAD_DIM:_QC + (hd + 1) * HEAD_DIM], mem_k[:, sl]) * scale
            p = jnp.exp(sc - jnp.max(sc, axis=-1, keepdims=True))
            p = p / jnp.sum(p, axis=-1, keepdims=True)
            heads.append(_dot(p, mem_v[:, sl]))
        oc.append(jnp.concatenate(heads, axis=-1))
    oc = oc[0] if nseg == 1 else jnp.concatenate(oc, axis=0)
    pc = _dot(oc * _silu(z_ref[:, _GC:_GC + C_WIDTH]), w["w_c_down"][...])

    merged = (jax.nn.sigmoid(z_ref[:, _ZA:_ZA + D_MODEL]) * pa
              + jax.nn.sigmoid(z_ref[:, _ZB:_ZB + D_MODEL]) * pb
              + jax.nn.sigmoid(z_ref[:, _ZC:_ZC + D_MODEL]) * pc)
    y = x + _dot(merged, w["w_out"][...])
    y = _rms(y, w["g_final"][...])
    new_states = [(new_st[s], new_ctx[s], new_hl[s]) for s in range(nseg)]
    return y, new_states


_WEIGHT_NAMES = ("g_mix", "w_in", "lb_logits", "g_a_out", "w_a_down", "w_conv", "b_conv", "w_lru_r", "b_lru_r",
                 "w_lru_i", "b_lru_i", "lru_lambda", "w_b_down", "w_c_down", "w_out", "g_final")
_NW = len(_WEIGHT_NAMES)


def _prompt_kernel(*refs):
    x_ref, mk_ref, mv_ref = refs[:3]
    w = dict(zip(_WEIGHT_NAMES, refs[3:3 + _NW]))
    y_ref, hgrn_ref, conv_ref, lru_ref = refs[3 + _NW:7 + _NW]
    st_ref, ctx_ref, hl_ref, z_ref, oa_ref, xpad_ref = refs[7 + _NW:]
    t = pl.program_id(1)

    @pl.when(t == 0)
    def _():
        st_ref[...] = jnp.zeros_like(st_ref)
        ctx_ref[...] = jnp.zeros_like(ctx_ref)
        hl_ref[...] = jnp.zeros_like(hl_ref)

    states = [([st_ref[hd] for hd in range(A_HEADS)], ctx_ref[...], hl_ref[...])]
    y, new_states = _layer_tile(
        x_ref[0], z_ref, oa_ref, xpad_ref, lambda s: (mk_ref[0], mv_ref[0]), states, w,
        seg=PROMPT_TILE, chunk=HGRN_CHUNK, first_rows_start=(t == 0))
    y_ref[0] = y
    st, ctx, hl = new_states[0]
    for hd in range(A_HEADS):
        st_ref[hd] = st[hd]
    ctx_ref[...] = ctx
    hl_ref[...] = hl

    @pl.when(t == pl.num_programs(1) - 1)
    def _():
        for hd in range(A_HEADS):
            hgrn_ref[0, hd] = st[hd].T
        conv_ref[0] = ctx
        lru_ref[0] = hl


def _sample_kernel(nseq, seg, *refs):
    x_ref, mk_ref, mv_ref, hgrn_in, conv_in, lru_in = refs[:6]
    w = dict(zip(_WEIGHT_NAMES, refs[6:6 + _NW]))
    y_ref, hgrn_ref, conv_ref, lru_ref = refs[6 + _NW:10 + _NW]
    z_ref, oa_ref, xpad_ref = refs[10 + _NW:]
    states = [([hgrn_in[s, hd].T for hd in range(A_HEADS)], conv_in[s], lru_in[s]) for s in range(nseq)]
    y, new_states = _layer_tile(
        x_ref[...], z_ref, oa_ref, xpad_ref, lambda s: (mk_ref[s], mv_ref[s]), states, w,
        seg=seg, chunk=min(HGRN_CHUNK, seg), first_rows_start=None)
    y_ref[...] = y
    for s in range(nseq):
        st, ctx, hl = new_states[s]
        for hd in range(A_HEADS):
            hgrn_ref[s, hd] = st[hd].T
        conv_ref[s] = ctx
        lru_ref[s] = hl


def _memkv_kernel(mem_ref, g_ref, wk_ref, wv_ref, k_ref, v_ref):
    hm = _rms(mem_ref[0], g_ref[...]).astype(bf16)
    k_ref[0] = jnp.dot(hm, wk_ref[...], preferred_element_type=f32)
    v_ref[0] = jnp.dot(hm, wv_ref[...], preferred_element_type=f32)


def _const_spec(shape):
    nd = len(shape)
    return pl.BlockSpec(shape, lambda *_: (0,) * nd, pipeline_mode=pl.Buffered(1))


def _block_diag_tiles(wb):
    per = MXU_TILE // B_BLOCK_DIM
    wg = wb.reshape(LRU_GROUPS, per, B_BLOCK_DIM, B_BLOCK_DIM)
    eye = jnp.eye(per, dtype=wb.dtype)
    return jnp.einsum("gpde,pq->gpdqe", wg, eye).reshape(LRU_GROUPS, MXU_TILE, MXU_TILE)


def kernel(x_prompt, x_sample, mem_prompt, cache_mem_k, cache_mem_v, state_hgrn, state_conv, state_lru, g_mix, w_in, lb_logits, g_a_out, w_a_down, w_conv, b_conv, w_lru_r, b_lru_r, w_lru_i, b_lru_i, lru_lambda, w_b_down, g_mem, w_mem_k, w_mem_v, w_c_down, w_out, g_final):
    bsz, seq, _ = x_prompt.shape
    dec_b, dec_seq, _ = x_sample.shape
    assert g_mix.shape[0] == 1, "single-layer stack only"
    assert seq % PROMPT_TILE == 0 and PROMPT_TILE % HGRN_CHUNK == 0

    row = lambda a: a.reshape(1, -1).astype(f32)
    weights = dict(
        g_mix=row(g_mix[0]), w_in=w_in[0].astype(bf16), lb_logits=lb_logits.astype(f32), g_a_out=row(g_a_out[0]),
        w_a_down=w_a_down[0].astype(bf16), w_conv=w_conv[0].astype(f32), b_conv=row(b_conv[0]),
        w_lru_r=_block_diag_tiles(w_lru_r[0]).astype(bf16), b_lru_r=row(b_lru_r[0]),
        w_lru_i=_block_diag_tiles(w_lru_i[0]).astype(bf16), b_lru_i=row(b_lru_i[0]), lru_lambda=row(lru_lambda[0]),
        w_b_down=w_b_down[0].astype(bf16), w_c_down=w_c_down[0].astype(bf16), w_out=w_out[0].astype(bf16),
        g_final=row(g_final))
    wlist = [weights[n] for n in _WEIGHT_NAMES]
    wspecs = [_const_spec(a.shape) for a in wlist]
    params = pltpu.CompilerParams(vmem_limit_bytes=VMEM_LIMIT_BYTES)

    mk, mv = pl.pallas_call(
        _memkv_kernel,
        grid=(bsz,),
        in_specs=[pl.BlockSpec((1, N_MEM, D_MODEL), lambda b: (b, 0, 0)),
                  _const_spec((1, D_MODEL)), _const_spec((D_MODEL, C_WIDTH)), _const_spec((D_MODEL, C_WIDTH))],
        out_specs=[pl.BlockSpec((1, N_MEM, C_WIDTH), lambda b: (b, 0, 0))] * 2,
        out_shape=[jax.ShapeDtypeStruct((bsz, N_MEM, C_WIDTH), f32)] * 2,
        name="mem_kv",
    )(mem_prompt, row(g_mem[0]), w_mem_k[0].astype(bf16), w_mem_v[0].astype(bf16))

    tile = PROMPT_TILE
    y_p, hgrn_p, conv_p, lru_p = pl.pallas_call(
        _prompt_kernel,
        grid=(bsz, seq // tile),
        in_specs=[pl.BlockSpec((1, tile, D_MODEL), lambda b, t: (b, t, 0)),
                  pl.BlockSpec((1, N_MEM, C_WIDTH), lambda b, t: (b, 0, 0)),
                  pl.BlockSpec((1, N_MEM, C_WIDTH), lambda b, t: (b, 0, 0))] + wspecs,
        out_specs=[pl.BlockSpec((1, tile, D_MODEL), lambda b, t: (b, t, 0)),
                   pl.BlockSpec((1, A_HEADS, HEAD_DIM, HEAD_DIM), lambda b, t: (b, 0, 0, 0)),
                   pl.BlockSpec((1, CTX_ROWS, B_WIDTH), lambda b, t: (b, 0, 0)),
                   pl.BlockSpec((1, 1, B_WIDTH), lambda b, t: (b, 0, 0))],
        out_shape=[jax.ShapeDtypeStruct((bsz, seq, D_MODEL), f32),
                   jax.ShapeDtypeStruct((bsz, A_HEADS, HEAD_DIM, HEAD_DIM), f32),
                   jax.ShapeDtypeStruct((bsz, CTX_ROWS, B_WIDTH), f32),
                   jax.ShapeDtypeStruct((bsz, 1, B_WIDTH), f32)],
        scratch_shapes=[pltpu.VMEM((A_HEADS, HEAD_DIM, HEAD_DIM), f32),
                        pltpu.VMEM((CTX_ROWS, B_WIDTH), f32),
                        pltpu.VMEM((1, B_WIDTH), f32),
                        pltpu.VMEM((tile, IN_COLS), f32),
                        pltpu.VMEM((tile, A_WIDTH), f32),
                        pltpu.VMEM((SUBLANES + tile, B_WIDTH), f32)],
        compiler_params=pltpu.CompilerParams(vmem_limit_bytes=VMEM_LIMIT_BYTES,
                                             dimension_semantics=("arbitrary", "arbitrary")),
        name="prompt_layer",
    )(x_prompt, mk, mv, *wlist)

    rows = dec_b * dec_seq
    full = lambda shape: pl.BlockSpec(shape, lambda *_: (0,) * len(shape))
    y_s, hgrn_s, conv_s, lru_s = pl.pallas_call(
        functools.partial(_sample_kernel, dec_b, dec_seq),
        grid=(1,),
        in_specs=[full((rows, D_MODEL)), full((dec_b, N_MEM, C_WIDTH)), full((dec_b, N_MEM, C_WIDTH)),
                  full((dec_b, A_HEADS, HEAD_DIM, HEAD_DIM)), full((dec_b, CTX_ROWS, B_WIDTH)),
                  full((dec_b, 1, B_WIDTH))] + wspecs,
        out_specs=[full((rows, D_MODEL)), full((dec_b, A_HEADS, HEAD_DIM, HEAD_DIM)),
                   full((dec_b, CTX_ROWS, B_WIDTH)), full((dec_b, 1, B_WIDTH))],
        out_shape=[jax.ShapeDtypeStruct((rows, D_MODEL), f32),
                   jax.ShapeDtypeStruct((dec_b, A_HEADS, HEAD_DIM, HEAD_DIM), f32),
                   jax.ShapeDtypeStruct((dec_b, CTX_ROWS, B_WIDTH), f32),
                   jax.ShapeDtypeStruct((dec_b, 1, B_WIDTH), f32)],
        scratch_shapes=[pltpu.VMEM((rows, IN_COLS), f32),
                        pltpu.VMEM((rows, A_WIDTH), f32),
                        pltpu.VMEM((dec_b * (SUBLANES + dec_seq), B_WIDTH), f32)],
        compiler_params=params,
        name="sample_layer",
    )(x_sample.reshape(rows, D_MODEL), cache_mem_k[0].reshape(dec_b, N_MEM, C_WIDTH),
      cache_mem_v[0].reshape(dec_b, N_MEM, C_WIDTH), state_hgrn[0], state_conv[0],
      state_lru[0].reshape(dec_b, 1, B_WIDTH), *wlist)

    return (y_p, y_s.reshape(dec_b, dec_seq, D_MODEL), hgrn_p[None], conv_p[None],
            lru_p.reshape(1, bsz, B_WIDTH), mk.reshape(1, bsz, N_MEM, C_HEADS, HEAD_DIM),
            mv.reshape(1, bsz, N_MEM, C_HEADS, HEAD_DIM), hgrn_s[None], conv_s[None],
            lru_s.reshape(1, dec_b, B_WIDTH))
```

```python
import functools

import jax
import jax.numpy as jnp
from jax import lax
from jax.experimental import pallas as pl
from jax.experimental.pallas import tpu as pltpu

f32 = jnp.float32
bf16 = jnp.bfloat16

D_MODEL = 1024
N_MEM = 256
EPS = 1e-6
A_HEADS = 4
HEAD_DIM = 128
A_WIDTH = A_HEADS * HEAD_DIM
B_WIDTH = D_MODEL
B_BLOCKS = 16
B_BLOCK_DIM = B_WIDTH // B_BLOCKS
CONV_W = 4
LRU_C = 8.0
C_HEADS = 4
C_WIDTH = C_HEADS * HEAD_DIM
HGRN_CHUNK = 64
IN_COLS = 4 * A_WIDTH + 2 * B_WIDTH + 2 * C_WIDTH + 3 * D_MODEL

_QA, _FA, _VA, _GA = 0, A_WIDTH, 2 * A_WIDTH, 3 * A_WIDTH
_XB = 4 * A_WIDTH
_GB = _XB + B_WIDTH
_QC = _GB + B_WIDTH
_GC = _QC + C_WIDTH
_ZA = _GC + C_WIDTH
_ZB = _ZA + D_MODEL
_ZC = _ZB + D_MODEL

MXU_TILE = 256
LRU_GROUPS = B_WIDTH // MXU_TILE
SUBLANES = 8
CTX_ROWS = CONV_W - 1

PROMPT_TILE = 256
VMEM_LIMIT_BYTES = 60 * 1024 * 1024


def _rms(x, g):
    return x * lax.rsqrt(jnp.mean(x * x, axis=-1, keepdims=True) + EPS) * g


def _wt(ref_or_val):
    return pltpu.bitcast(ref_or_val, bf16)


def _dot(a, b):
    return jnp.dot(a.astype(bf16), b.astype(bf16), preferred_element_type=f32)


def _dot_nt(a, b):
    return lax.dot_general(a.astype(bf16), b.astype(bf16), (((1,), (1,)), ((), ())),
                           preferred_element_type=f32)


def _dot_tn(a, b):
    return lax.dot_general(a.astype(bf16), b.astype(bf16), (((0,), (0,)), ((), ())),
                           preferred_element_type=f32)


def _silu(x):
    return x * jax.nn.sigmoid(x)


def _row_in_seg(rows, width, seg):
    return lax.broadcasted_iota(jnp.int32, (rows, width), 0) & (seg - 1)


def _vreg_groups(x):
    rows, width = x.shape
    return x.reshape(rows // SUBLANES, SUBLANES, width)


def _cumsum_rows(x, seg):
    rows = x.shape[0]
    x3 = _vreg_groups(x)
    sub = lax.broadcasted_iota(jnp.int32, x3.shape, 1)
    d = 1
    while d < SUBLANES:
        x3 = x3 + jnp.where(sub >= d, pltpu.roll(x3, d, 1), 0.0)
        d *= 2
    per_seg = seg // SUBLANES
    out, carry = [], None
    for g in range(rows // SUBLANES):
        cur = x3[g]
        if g % per_seg:
            cur = cur + carry
        carry = cur[SUBLANES - 1:SUBLANES, :]
        out.append(cur)
    return jnp.concatenate(out, axis=0)


def _linear_scan_rows(a, u, h0, seg):
    rows = a.shape[0]
    a3, u3 = _vreg_groups(a), _vreg_groups(u)
    sub = lax.broadcasted_iota(jnp.int32, a3.shape, 1)
    d = 1
    while d < SUBLANES:
        keep = sub >= d
        u3 = a3 * jnp.where(keep, pltpu.roll(u3, d, 1), 0.0) + u3
        a3 = a3 * jnp.where(keep, pltpu.roll(a3, d, 1), 1.0)
        d *= 2
    per_seg = seg // SUBLANES
    out, last, carry = [], [], None
    for g in range(rows // SUBLANES):
        if g % per_seg == 0:
            carry = h0[g // per_seg]
        cur = u3[g] + a3[g] * carry
        carry = cur[SUBLANES - 1:SUBLANES, :]
        out.append(cur)
        if g % per_seg == per_seg - 1:
            last.append(carry)
    return jnp.concatenate(out, axis=0), last


def _layer_tile(x, z_ref, oa_ref, xpad_ref, kv, states, w, *, seg, chunk, first_rows_start):
    rows = x.shape[0]
    nseg = rows // seg
    nchunk = seg // chunk

    h = _rms(x, w["g_mix"][...]).astype(bf16)
    col_blk = 1024
    for c0 in range(0, IN_COLS, col_blk):
        z_ref[:, c0:c0 + col_blk] = jnp.dot(h, _wt(w["w_in"][:, c0:c0 + col_blk]), preferred_element_type=f32)

    lg = w["lb_logits"][...]
    l0, l1 = lg[0:1, :], lg[1:2, :]
    lmax = jnp.maximum(l0, l1)
    e0, e1 = jnp.exp(l0 - lmax), jnp.exp(l1 - lmax)
    lb = e0 / (e0 + e1)

    f = lb + (1.0 - lb) * jax.nn.sigmoid(z_ref[:, _FA:_FA + A_WIDTH])
    b = _cumsum_rows(jnp.log(f), chunk)
    k = 1.0 - f
    qg = _silu(z_ref[:, _QA:_QA + A_WIDTH]) * jnp.exp(b)
    kg = k * jnp.exp(-b)
    v = z_ref[:, _VA:_VA + A_WIDTH]

    tt = lax.broadcasted_iota(jnp.int32, (rows, rows), 0)
    ss = lax.broadcasted_iota(jnp.int32, (rows, rows), 1)
    shift = chunk.bit_length() - 1
    causal = ((tt >> shift) == (ss >> shift)) & (ss <= tt)

    new_st = [[None] * A_HEADS for _ in range(nseg)]
    for hd in range(A_HEADS):
        sl = slice(hd * HEAD_DIM, (hd + 1) * HEAD_DIM)
        att = jnp.where(causal, _dot_nt(qg[:, sl], kg[:, sl]), 0.0)
        o_intra = _dot(att, v[:, sl])
        for s in range(nseg):
            st = states[s][0][hd]
            for c in range(nchunk):
                r0 = s * seg + c * chunk
                rs = slice(r0, r0 + chunk)
                oa_ref[rs, sl] = o_intra[rs, :] + _dot_nt(qg[rs, sl], st)
                b_last = b[r0 + chunk - 1:r0 + chunk, sl]
                kd = k[rs, sl] * jnp.exp(b_last - b[rs, sl])
                st = st * jnp.exp(b_last) + _dot_tn(v[rs, sl], kd)
            new_st[s][hd] = st

    g_a = w["g_a_out"][...]
    a_in = []
    for hd in range(A_HEADS):
        sl = slice(hd * HEAD_DIM, (hd + 1) * HEAD_DIM)
        a_in.append(_rms(oa_ref[:, sl], g_a[:, sl]))
    a_in = jnp.concatenate(a_in, axis=-1) * _silu(z_ref[:, _GA:_GA + A_WIDTH])
    pa = _dot(a_in, _wt(w["w_a_down"][...]))

    pad = SUBLANES
    stride = pad + seg
    new_ctx = []
    for s in range(nseg):
        base = s * stride
        xpad_ref[base + pad - CTX_ROWS:base + pad, :] = states[s][1]
        xpad_ref[base + pad:base + pad + seg, :] = z_ref[s * seg:(s + 1) * seg, _XB:_XB + B_WIDTH]
    w_conv = w["w_conv"][...]
    xc = []
    for s in range(nseg):
        base = s * stride + pad - CTX_ROWS
        acc = w_conv[0:1, :] * xpad_ref[base:base + seg, :]
        for j in range(1, CONV_W):
            acc = acc + w_conv[j:j + 1, :] * xpad_ref[base + j:base + j + seg, :]
        xc.append(w["b_conv"][...] + acc)
        new_ctx.append(xpad_ref[base + seg:base + seg + CTX_ROWS, :])
    xc = xc[0] if nseg == 1 else jnp.concatenate(xc, axis=0)

    xc_b = xc.astype(bf16)
    r_pre, i_pre = [], []
    for g in range(LRU_GROUPS):
        gs = slice(g * MXU_TILE, (g + 1) * MXU_TILE)
        r_pre.append(jnp.dot(xc_b[:, gs], _wt(w["w_lru_r"][g]), preferred_element_type=f32))
        i_pre.append(jnp.dot(xc_b[:, gs], _wt(w["w_lru_i"][g]), preferred_element_type=f32))
    r = jax.nn.sigmoid(jnp.concatenate(r_pre, axis=-1) + w["b_lru_r"][...])
    ig = jax.nn.sigmoid(jnp.concatenate(i_pre, axis=-1) + w["b_lru_i"][...])
    neg_lam = -w["lru_lambda"][...]
    softplus = jnp.maximum(neg_lam, 0.0) + jnp.log1p(jnp.exp(-jnp.abs(neg_lam)))
    log_a = -LRU_C * r * softplus
    a = jnp.exp(log_a)
    mult = jnp.sqrt(-jnp.tanh(log_a) * (a * a + 1.0))
    if first_rows_start is not None:
        ris_b = _row_in_seg(rows, B_WIDTH, seg)
        mult = jnp.where((ris_b == 0) & first_rows_start, 1.0, mult)
    hb, new_hl = _linear_scan_rows(a, mult * ig * xc, [states[s][2] for s in range(nseg)], seg)
    pb = _dot(hb * _silu(z_ref[:, _GB:_GB + B_WIDTH]), _wt(w["w_b_down"][...]))

    scale = HEAD_DIM ** -0.5
    oc = []
    for s in range(nseg):
        mem_k, mem_v = kv(s)
        rs = slice(s * seg, (s + 1) * seg)
        heads = []
        for hd in range(C_HEADS):
            sl = slice(hd * HEAD_DIM, (hd + 1) * HEAD_DIM)
            sc = _dot_nt(z_ref[rs, _QC + hd * HEAD_DIM:_QC + (hd + 1) * HEAD_DIM], mem_k[:, sl]) * scale
            p = jnp.exp(sc - jnp.max(sc, axis=-1, keepdims=True))
            p = p / jnp.sum(p, axis=-1, keepdims=True)
            heads.append(_dot(p, mem_v[:, sl]))
        oc.append(jnp.concatenate(heads, axis=-1))
    oc = oc[0] if nseg == 1 else jnp.concatenate(oc, axis=0)
    pc = _dot(oc * _silu(z_ref[:, _GC:_GC + C_WIDTH]), _wt(w["w_c_down"][...]))

    merged = (jax.nn.sigmoid(z_ref[:, _ZA:_ZA + D_MODEL]) * pa
              + jax.nn.sigmoid(z_ref[:, _ZB:_ZB + D_MODEL]) * pb
              + jax.nn.sigmoid(z_ref[:, _ZC:_ZC + D_MODEL]) * pc)
    y = x + _dot(merged, _wt(w["w_out"][...]))
    y = _rms(y, w["g_final"][...])
    new_states = [(new_st[s], new_ctx[s], new_hl[s]) for s in range(nseg)]
    return y, new_states


_WEIGHT_NAMES = ("g_mix", "w_in", "lb_logits", "g_a_out", "w_a_down", "w_conv", "b_conv", "w_lru_r", "b_lru_r",
                 "w_lru_i", "b_lru_i", "lru_lambda", "w_b_down", "w_c_down", "w_out", "g_final")
_NW = len(_WEIGHT_NAMES)


def _prompt_kernel(*refs):
    x_ref, mk_ref, mv_ref = refs[:3]
    w = dict(zip(_WEIGHT_NAMES, refs[3:3 + _NW]))
    y_ref, hgrn_ref, conv_ref, lru_ref = refs[3 + _NW:7 + _NW]
    st_ref, ctx_ref, hl_ref, z_ref, oa_ref, xpad_ref = refs[7 + _NW:]
    t = pl.program_id(1)

    @pl.when(t == 0)
    def _():
        st_ref[...] = jnp.zeros_like(st_ref)
        ctx_ref[...] = jnp.zeros_like(ctx_ref)
        hl_ref[...] = jnp.zeros_like(hl_ref)

    states = [([st_ref[hd] for hd in range(A_HEADS)], ctx_ref[...], hl_ref[...])]
    y, new_states = _layer_tile(
        x_ref[0], z_ref, oa_ref, xpad_ref, lambda s: (mk_ref[0], mv_ref[0]), states, w,
        seg=PROMPT_TILE, chunk=HGRN_CHUNK, first_rows_start=(t == 0))
    y_ref[0] = y
    st, ctx, hl = new_states[0]
    for hd in range(A_HEADS):
        st_ref[hd] = st[hd]
    ctx_ref[...] = ctx
    hl_ref[...] = hl

    @pl.when(t == pl.num_programs(1) - 1)
    def _():
        for hd in range(A_HEADS):
            hgrn_ref[0, hd] = st[hd].T
        conv_ref[0] = ctx
        lru_ref[0] = hl


def _sample_kernel(nseq, seg, *refs):
    x_ref, mk_ref, mv_ref, hgrn_in, conv_in, lru_in = refs[:6]
    w = dict(zip(_WEIGHT_NAMES, refs[6:6 + _NW]))
    y_ref, hgrn_ref, conv_ref, lru_ref = refs[6 + _NW:10 + _NW]
    z_ref, oa_ref, xpad_ref = refs[10 + _NW:]
    states = [([hgrn_in[s, hd].T for hd in range(A_HEADS)], conv_in[s], lru_in[s]) for s in range(nseq)]
    y, new_states = _layer_tile(
        x_ref[...], z_ref, oa_ref, xpad_ref, lambda s: (mk_ref[s], mv_ref[s]), states, w,
        seg=seg, chunk=min(HGRN_CHUNK, seg), first_rows_start=None)
    y_ref[...] = y
    for s in range(nseq):
        st, ctx, hl = new_states[s]
        for hd in range(A_HEADS):
            hgrn_ref[s, hd] = st[hd].T
        conv_ref[s] = ctx
        lru_ref[s] = hl


def _memkv_kernel(mem_ref, g_ref, wk_ref, wv_ref, k_ref, v_ref):
    hm = _rms(mem_ref[0], g_ref[...]).astype(bf16)
    k_ref[0] = jnp.dot(hm, _wt(wk_ref[...]), preferred_element_type=f32)
    v_ref[0] = jnp.dot(hm, _wt(wv_ref[...]), preferred_element_type=f32)


def _const_spec(shape):
    nd = len(shape)
    return pl.BlockSpec(shape, lambda *_: (0,) * nd, pipeline_mode=pl.Buffered(1))


def _pack_rows(wm):
    *lead, k, n = wm.shape
    pairs = wm.astype(bf16).reshape(*lead, k // 2, 2, n)
    return lax.bitcast_convert_type(jnp.swapaxes(pairs, -1, -2), jnp.uint32)


def _block_diag_tiles(wb):
    per = MXU_TILE // B_BLOCK_DIM
    wg = wb.reshape(LRU_GROUPS, per, B_BLOCK_DIM, B_BLOCK_DIM)
    eye = jnp.eye(per, dtype=wb.dtype)
    return jnp.einsum("gpde,pq->gpdqe", wg, eye).reshape(LRU_GROUPS, MXU_TILE, MXU_TILE)


def kernel(x_prompt, x_sample, mem_prompt, cache_mem_k, cache_mem_v, state_hgrn, state_conv, state_lru, g_mix, w_in, lb_logits, g_a_out, w_a_down, w_conv, b_conv, w_lru_r, b_lru_r, w_lru_i, b_lru_i, lru_lambda, w_b_down, g_mem, w_mem_k, w_mem_v, w_c_down, w_out, g_final):
    bsz, seq, _ = x_prompt.shape
    dec_b, dec_seq, _ = x_sample.shape
    assert g_mix.shape[0] == 1, "single-layer stack only"
    assert seq % PROMPT_TILE == 0 and PROMPT_TILE % HGRN_CHUNK == 0

    row = lambda a: a.reshape(1, -1).astype(f32)
    weights = dict(
        g_mix=row(g_mix[0]), w_in=_pack_rows(w_in[0]), lb_logits=lb_logits.astype(f32), g_a_out=row(g_a_out[0]),
        w_a_down=_pack_rows(w_a_down[0]), w_conv=w_conv[0].astype(f32), b_conv=row(b_conv[0]),
        w_lru_r=_pack_rows(_block_diag_tiles(w_lru_r[0])), b_lru_r=row(b_lru_r[0]),
        w_lru_i=_pack_rows(_block_diag_tiles(w_lru_i[0])), b_lru_i=row(b_lru_i[0]), lru_lambda=row(lru_lambda[0]),
        w_b_down=_pack_rows(w_b_down[0]), w_c_down=_pack_rows(w_c_down[0]), w_out=_pack_rows(w_out[0]),
        g_final=row(g_final))
    wlist = [weights[n] for n in _WEIGHT_NAMES]
    wspecs = [_const_spec(a.shape) for a in wlist]
    params = pltpu.CompilerParams(vmem_limit_bytes=VMEM_LIMIT_BYTES)

    mk, mv = pl.pallas_call(
        _memkv_kernel,
        grid=(bsz,),
        in_specs=[pl.BlockSpec((1, N_MEM, D_MODEL), lambda b: (b, 0, 0)),
                  _const_spec((1, D_MODEL)), _const_spec((D_MODEL // 2, C_WIDTH)),
                  _const_spec((D_MODEL // 2, C_WIDTH))],
        out_specs=[pl.BlockSpec((1, N_MEM, C_WIDTH), lambda b: (b, 0, 0))] * 2,
        out_shape=[jax.ShapeDtypeStruct((bsz, N_MEM, C_WIDTH), f32)] * 2,
        name="mem_kv",
    )(mem_prompt, row(g_mem[0]), _pack_rows(w_mem_k[0]), _pack_rows(w_mem_v[0]))

    tile = PROMPT_TILE
    y_p, hgrn_p, conv_p, lru_p = pl.pallas_call(
        _prompt_kernel,
        grid=(bsz, seq // tile),
        in_specs=[pl.BlockSpec((1, tile, D_MODEL), lambda b, t: (b, t, 0)),
                  pl.BlockSpec((1, N_MEM, C_WIDTH), lambda b, t: (b, 0, 0)),
                  pl.BlockSpec((1, N_MEM, C_WIDTH), lambda b, t: (b, 0, 0))] + wspecs,
        out_specs=[pl.BlockSpec((1, tile, D_MODEL), lambda b, t: (b, t, 0)),
                   pl.BlockSpec((1, A_HEADS, HEAD_DIM, HEAD_DIM), lambda b, t: (b, 0, 0, 0)),
                   pl.BlockSpec((1, CTX_ROWS, B_WIDTH), lambda b, t: (b, 0, 0)),
                   pl.BlockSpec((1, 1, B_WIDTH), lambda b, t: (b, 0, 0))],
        out_shape=[jax.ShapeDtypeStruct((bsz, seq, D_MODEL), f32),
                   jax.ShapeDtypeStruct((bsz, A_HEADS, HEAD_DIM, HEAD_DIM), f32),
                   jax.ShapeDtypeStruct((bsz, CTX_ROWS, B_WIDTH), f32),
                   jax.ShapeDtypeStruct((bsz, 1, B_WIDTH), f32)],
        scratch_shapes=[pltpu.VMEM((A_HEADS, HEAD_DIM, HEAD_DIM), f32),
                        pltpu.VMEM((CTX_ROWS, B_WIDTH), f32),
                        pltpu.VMEM((1, B_WIDTH), f32),
                        pltpu.VMEM((tile, IN_COLS), f32),
                        pltpu.VMEM((tile, A_WIDTH), f32),
                        pltpu.VMEM((SUBLANES + tile, B_WIDTH), f32)],
        compiler_params=pltpu.CompilerParams(vmem_limit_bytes=VMEM_LIMIT_BYTES,
                                             dimension_semantics=("arbitrary", "arbitrary")),
        name="prompt_layer",
    )(x_prompt, mk, mv, *wlist)

    rows = dec_b * dec_seq
    full = lambda shape: pl.BlockSpec(shape, lambda *_: (0,) * len(shape))
    y_s, hgrn_s, conv_s, lru_s = pl.pallas_call(
        functools.partial(_sample_kernel, dec_b, dec_seq),
        grid=(1,),
        in_specs=[full((rows, D_MODEL)), full((dec_b, N_MEM, C_WIDTH)), full((dec_b, N_MEM, C_WIDTH)),
                  full((dec_b, A_HEADS, HEAD_DIM, HEAD_DIM)), full((dec_b, CTX_ROWS, B_WIDTH)),
                  full((dec_b, 1, B_WIDTH))] + wspecs,
        out_specs=[full((rows, D_MODEL)), full((dec_b, A_HEADS, HEAD_DIM, HEAD_DIM)),
                   full((dec_b, CTX_ROWS, B_WIDTH)), full((dec_b, 1, B_WIDTH))],
        out_shape=[jax.ShapeDtypeStruct((rows, D_MODEL), f32),
                   jax.ShapeDtypeStruct((dec_b, A_HEADS, HEAD_DIM, HEAD_DIM), f32),
                   jax.ShapeDtypeStruct((dec_b, CTX_ROWS, B_WIDTH), f32),
                   jax.ShapeDtypeStruct((dec_b, 1, B_WIDTH), f32)],
        scratch_shapes=[pltpu.VMEM((rows, IN_COLS), f32),
                        pltpu.VMEM((rows, A_WIDTH), f32),
                        pltpu.VMEM((dec_b * (SUBLANES + dec_seq), B_WIDTH), f32)],
        compiler_params=params,
        name="sample_layer",
    )(x_sample.reshape(rows, D_MODEL), cache_mem_k[0].reshape(dec_b, N_MEM, C_WIDTH),
      cache_mem_v[0].reshape(dec_b, N_MEM, C_WIDTH), state_hgrn[0], state_conv[0],
      state_lru[0].reshape(dec_b, 1, B_WIDTH), *wlist)

    return (y_p, y_s.reshape(dec_b, dec_seq, D_MODEL), hgrn_p[None], conv_p[None],
            lru_p.reshape(1, bsz, B_WIDTH), mk.reshape(1, bsz, N_MEM, C_HEADS, HEAD_DIM),
            mv.reshape(1, bsz, N_MEM, C_HEADS, HEAD_DIM), hgrn_s[None], conv_s[None],
            lru_s.reshape(1, dec_b, B_WIDTH))
```

```python
import functools

import jax
import jax.numpy as jnp
from jax import lax
from jax.experimental import pallas as pl
from jax.experimental.pallas import tpu as pltpu

f32 = jnp.float32
bf16 = jnp.bfloat16

D_MODEL = 1024
N_MEM = 256
EPS = 1e-6
A_HEADS = 4
HEAD_DIM = 128
A_WIDTH = A_HEADS * HEAD_DIM
B_WIDTH = D_MODEL
B_BLOCKS = 16
B_BLOCK_DIM = B_WIDTH // B_BLOCKS
CONV_W = 4
LRU_C = 8.0
C_HEADS = 4
C_WIDTH = C_HEADS * HEAD_DIM
HGRN_CHUNK = 64
IN_COLS = 4 * A_WIDTH + 2 * B_WIDTH + 2 * C_WIDTH + 3 * D_MODEL

_QA, _FA, _VA, _GA = 0, A_WIDTH, 2 * A_WIDTH, 3 * A_WIDTH
_XB = 4 * A_WIDTH
_GB = _XB + B_WIDTH
_QC = _GB + B_WIDTH
_GC = _QC + C_WIDTH
_ZA = _GC + C_WIDTH
_ZB = _ZA + D_MODEL
_ZC = _ZB + D_MODEL

MXU_TILE = 256
LRU_GROUPS = B_WIDTH // MXU_TILE
SUBLANES = 8
CTX_ROWS = CONV_W - 1

PROMPT_TILE = 256
VMEM_LIMIT_BYTES = 60 * 1024 * 1024


def _rms(x, g):
    return x * lax.rsqrt(jnp.mean(x * x, axis=-1, keepdims=True) + EPS) * g


def _wt(ref_or_val):
    return pltpu.bitcast(ref_or_val, bf16)


def _dot(a, b):
    return jnp.dot(a.astype(bf16), b.astype(bf16), preferred_element_type=f32)


def _dot_nt(a, b):
    return lax.dot_general(a.astype(bf16), b.astype(bf16), (((1,), (1,)), ((), ())),
                           preferred_element_type=f32)


def _dot_tn(a, b):
    return lax.dot_general(a.astype(bf16), b.astype(bf16), (((0,), (0,)), ((), ())),
                           preferred_element_type=f32)


def _silu(x):
    return x * jax.nn.sigmoid(x)


def _row_in_seg(rows, width, seg):
    return lax.broadcasted_iota(jnp.int32, (rows, width), 0) & (seg - 1)


def _vreg_groups(x):
    rows, width = x.shape
    return x.reshape(rows // SUBLANES, SUBLANES, width)


def _cumsum_rows(x, seg):
    rows = x.shape[0]
    x3 = _vreg_groups(x)
    sub = lax.broadcasted_iota(jnp.int32, x3.shape, 1)
    d = 1
    while d < SUBLANES:
        x3 = x3 + jnp.where(sub >= d, pltpu.roll(x3, d, 1), 0.0)
        d *= 2
    per_seg = seg // SUBLANES
    out, carry = [], None
    for g in range(rows // SUBLANES):
        cur = x3[g]
        if g % per_seg:
            cur = cur + carry
        carry = cur[SUBLANES - 1:SUBLANES, :]
        out.append(cur)
    return jnp.concatenate(out, axis=0)


def _linear_scan_rows(a, u, h0, seg):
    rows = a.shape[0]
    a3, u3 = _vreg_groups(a), _vreg_groups(u)
    sub = lax.broadcasted_iota(jnp.int32, a3.shape, 1)
    d = 1
    while d < SUBLANES:
        keep = sub >= d
        u3 = a3 * jnp.where(keep, pltpu.roll(u3, d, 1), 0.0) + u3
        a3 = a3 * jnp.where(keep, pltpu.roll(a3, d, 1), 1.0)
        d *= 2
    per_seg = seg // SUBLANES
    out, last, carry = [], [], None
    for g in range(rows // SUBLANES):
        if g % per_seg == 0:
            carry = h0[g // per_seg]
        cur = u3[g] + a3[g] * carry
        carry = cur[SUBLANES - 1:SUBLANES, :]
        out.append(cur)
        if g % per_seg == per_seg - 1:
            last.append(carry)
    return jnp.concatenate(out, axis=0), last


def _layer_tile(x, z_ref, oa_ref, xpad_ref, kv, states, w, *, seg, chunk, first_rows_start):
    rows = x.shape[0]
    nseg = rows // seg
    nchunk = seg // chunk

    h = _rms(x, w["g_mix"][...]).astype(bf16)
    col_blk = 1024
    for c0 in range(0, IN_COLS, col_blk):
        z_ref[:, c0:c0 + col_blk] = jnp.dot(h, _wt(w["w_in"][:, c0:c0 + col_blk]), preferred_element_type=f32)

    lg = w["lb_logits"][...]
    l0, l1 = lg[0:1, :], lg[1:2, :]
    lmax = jnp.maximum(l0, l1)
    e0, e1 = jnp.exp(l0 - lmax), jnp.exp(l1 - lmax)
    lb = e0 / (e0 + e1)

    f = lb + (1.0 - lb) * jax.nn.sigmoid(z_ref[:, _FA:_FA + A_WIDTH])
    b = _cumsum_rows(jnp.log(f), chunk)
    k = 1.0 - f
    qg = _silu(z_ref[:, _QA:_QA + A_WIDTH]) * jnp.exp(b)
    kg = k * jnp.exp(-b)
    v = z_ref[:, _VA:_VA + A_WIDTH]

    tt = lax.broadcasted_iota(jnp.int32, (rows, rows), 0)
    ss = lax.broadcasted_iota(jnp.int32, (rows, rows), 1)
    shift = chunk.bit_length() - 1
    causal = ((tt >> shift) == (ss >> shift)) & (ss <= tt)

    new_st = [[None] * A_HEADS for _ in range(nseg)]
    for hd in range(A_HEADS):
        sl = slice(hd * HEAD_DIM, (hd + 1) * HEAD_DIM)
        att = jnp.where(causal, _dot_nt(qg[:, sl], kg[:, sl]), 0.0)
        o_intra = _dot(att, v[:, sl])
        for s in range(nseg):
            st = states[s][0][hd]
            for c in range(nchunk):
                r0 = s * seg + c * chunk
                rs = slice(r0, r0 + chunk)
                oa_ref[rs, sl] = o_intra[rs, :] + _dot_nt(qg[rs, sl], st)
                b_last = b[r0 + chunk - 1:r0 + chunk, sl]
                kd = k[rs, sl] * jnp.exp(b_last - b[rs, sl])
                st = st * jnp.exp(b_last) + _dot_tn(v[rs, sl], kd)
            new_st[s][hd] = st

    g_a = w["g_a_out"][...]
    a_in = []
    for hd in range(A_HEADS):
        sl = slice(hd * HEAD_DIM, (hd + 1) * HEAD_DIM)
        a_in.append(_rms(oa_ref[:, sl], g_a[:, sl]))
    a_in = jnp.concatenate(a_in, axis=-1) * _silu(z_ref[:, _GA:_GA + A_WIDTH])
    pa = _dot(a_in, _wt(w["w_a_down"][...]))

    pad = SUBLANES
    stride = pad + seg
    new_ctx = []
    for s in range(nseg):
        base = s * stride
        xpad_ref[base + pad - CTX_ROWS:base + pad, :] = states[s][1]
        xpad_ref[base + pad:base + pad + seg, :] = z_ref[s * seg:(s + 1) * seg, _XB:_XB + B_WIDTH]
    w_conv = w["w_conv"][...]
    xc = []
    for s in range(nseg):
        base = s * stride + pad - CTX_ROWS
        acc = w_conv[0:1, :] * xpad_ref[base:base + seg, :]
        for j in range(1, CONV_W):
            acc = acc + w_conv[j:j + 1, :] * xpad_ref[base + j:base + j + seg, :]
        xc.append(w["b_conv"][...] + acc)
        new_ctx.append(xpad_ref[base + seg:base + seg + CTX_ROWS, :])
    xc = xc[0] if nseg == 1 else jnp.concatenate(xc, axis=0)

    xc_b = xc.astype(bf16)
    r_pre, i_pre = [], []
    for g in range(LRU_GROUPS):
        gs = slice(g * MXU_TILE, (g + 1) * MXU_TILE)
        r_pre.append(jnp.dot(xc_b[:, gs], _wt(w["w_lru_r"][g]), preferred_element_type=f32))
        i_pre.append(jnp.dot(xc_b[:, gs], _wt(w["w_lru_i"][g]), preferred_element_type=f32))
    r = jax.nn.sigmoid(jnp.concatenate(r_pre, axis=-1) + w["b_lru_r"][...])
    ig = jax.nn.sigmoid(jnp.concatenate(i_pre, axis=-1) + w["b_lru_i"][...])
    neg_lam = -w["lru_lambda"][...]
    softplus = jnp.maximum(neg_lam, 0.0) + jnp.log1p(jnp.exp(-jnp.abs(neg_lam)))
    log_a = -LRU_C * r * softplus
    a = jnp.exp(log_a)
    mult = jnp.sqrt(-jnp.tanh(log_a) * (a * a + 1.0))
    if first_rows_start is not None:
        ris_b = _row_in_seg(rows, B_WIDTH, seg)
        mult = jnp.where((ris_b == 0) & first_rows_start, 1.0, mult)
    hb, new_hl = _linear_scan_rows(a, mult * ig * xc, [states[s][2] for s in range(nseg)], seg)
    pb = _dot(hb * _silu(z_ref[:, _GB:_GB + B_WIDTH]), _wt(w["w_b_down"][...]))

    scale = HEAD_DIM ** -0.5
    oc = []
    for s in range(nseg):
        mem_k, mem_v = kv(s)
        rs = slice(s * seg, (s + 1) * seg)
        heads = []
        for hd in range(C_HEADS):
            sl = slice(hd * HEAD_DIM, (hd + 1) * HEAD_DIM)
            sc = _dot_nt(z_ref[rs, _QC + hd * HEAD_DIM:_QC + (hd + 1) * HEAD_DIM], mem_k[:, sl]) * scale
            p = jnp.exp(sc - jnp.max(sc, axis=-1, keepdims=True))
            p = p / jnp.sum(p, axis=-1, keepdims=True)
            heads.append(_dot(p, mem_v[:, sl]))
        oc.append(jnp.concatenate(heads, axis=-1))
    oc = oc[0] if nseg == 1 else jnp.concatenate(oc, axis=0)
    pc = _dot(oc * _silu(z_ref[:, _GC:_GC + C_WIDTH]), _wt(w["w_c_down"][...]))

    merged = (jax.nn.sigmoid(z_ref[:, _ZA:_ZA + D_MODEL]) * pa
              + jax.nn.sigmoid(z_ref[:, _ZB:_ZB + D_MODEL]) * pb
              + jax.nn.sigmoid(z_ref[:, _ZC:_ZC + D_MODEL]) * pc)
    y = x + _dot(merged, _wt(w["w_out"][...]))
    y = _rms(y, w["g_final"][...])
    new_states = [(new_st[s], new_ctx[s], new_hl[s]) for s in range(nseg)]
    return y, new_states


_WEIGHT_NAMES = ("g_mix", "w_in", "lb_logits", "g_a_out", "w_a_down", "w_conv", "b_conv", "w_lru_r", "b_lru_r",
                 "w_lru_i", "b_lru_i", "lru_lambda", "w_b_down", "w_c_down", "w_out", "g_final")
_NW = len(_WEIGHT_NAMES)


def _prompt_kernel(*refs):
    x_ref, mk_ref, mv_ref = refs[:3]
    w = dict(zip(_WEIGHT_NAMES, refs[3:3 + _NW]))
    y_ref, hgrn_ref, conv_ref, lru_ref = refs[3 + _NW:7 + _NW]
    st_ref, ctx_ref, hl_ref, z_ref, oa_ref, xpad_ref = refs[7 + _NW:]
    t = pl.program_id(1)

    @pl.when(t == 0)
    def _():
        st_ref[...] = jnp.zeros_like(st_ref)
        ctx_ref[...] = jnp.zeros_like(ctx_ref)
        hl_ref[...] = jnp.zeros_like(hl_ref)

    states = [([st_ref[hd] for hd in range(A_HEADS)], ctx_ref[...], hl_ref[...])]
    y, new_states = _layer_tile(
        x_ref[0], z_ref, oa_ref, xpad_ref, lambda s: (mk_ref[0], mv_ref[0]), states, w,
        seg=PROMPT_TILE, chunk=HGRN_CHUNK, first_rows_start=(t == 0))
    y_ref[0] = y
    st, ctx, hl = new_states[0]
    for hd in range(A_HEADS):
        st_ref[hd] = st[hd]
    ctx_ref[...] = ctx
    hl_ref[...] = hl

    @pl.when(t == pl.num_programs(1) - 1)
    def _():
        for hd in range(A_HEADS):
            hgrn_ref[0, hd] = st[hd].T
        conv_ref[0] = ctx
        lru_ref[0] = hl


def _sample_kernel(nseq, seg, *refs):
    x_ref, mk_ref, mv_ref, hgrn_in, conv_in, lru_in = refs[:6]
    w = dict(zip(_WEIGHT_NAMES, refs[6:6 + _NW]))
    y_ref, hgrn_ref, conv_ref, lru_ref = refs[6 + _NW:10 + _NW]
    z_ref, oa_ref, xpad_ref = refs[10 + _NW:]
    states = [([hgrn_in[s, hd].T for hd in range(A_HEADS)], conv_in[s], lru_in[s]) for s in range(nseq)]
    y, new_states = _layer_tile(
        x_ref[...], z_ref, oa_ref, xpad_ref, lambda s: (mk_ref[s], mv_ref[s]), states, w,
        seg=seg, chunk=min(HGRN_CHUNK, seg), first_rows_start=None)
    y_ref[...] = y
    for s in range(nseq):
        st, ctx, hl = new_states[s]
        for hd in range(A_HEADS):
            hgrn_ref[s, hd] = st[hd].T
        conv_ref[s] = ctx
        lru_ref[s] = hl


def _memkv_kernel(mem_ref, g_ref, wk_ref, wv_ref, k_ref, v_ref):
    hm = _rms(mem_ref[0], g_ref[...]).astype(bf16)
    k_ref[0] = jnp.dot(hm, _wt(wk_ref[...]), preferred_element_type=f32)
    v_ref[0] = jnp.dot(hm, _wt(wv_ref[...]), preferred_element_type=f32)


def _const_spec(shape):
    nd = len(shape)
    return pl.BlockSpec(shape, lambda *_: (0,) * nd, pipeline_mode=pl.Buffered(1))


PREP_STEPS = 8
_DENSE_WEIGHTS = ("w_in", "w_a_down", "w_b_down", "w_c_down", "w_out", "w_mem_k", "w_mem_v")
_U32 = jnp.uint32


def _prep_kernel(*refs):
    nd = len(_DENSE_WEIGHTS)
    dense_in, (lru_r_in, lru_i_in) = refs[:nd], refs[nd:nd + 2]
    dense_out, (lru_r_out, lru_i_out) = refs[nd + 2:2 * nd + 2], refs[2 * nd + 2:2 * nd + 4]
    tile_ref = refs[2 * nd + 4]
    for src, dst in zip(dense_in, dense_out):
        dst[...] = pltpu.bitcast(src[...].astype(bf16), _U32)

    @pl.when(pl.program_id(0) == 0)
    def _():
        per = MXU_TILE // B_BLOCK_DIM
        for src, dst in ((lru_r_in, lru_r_out), (lru_i_in, lru_i_out)):
            for g in range(LRU_GROUPS):
                tile_ref[...] = jnp.zeros_like(tile_ref)
                for p in range(per):
                    lo = p * B_BLOCK_DIM
                    tile_ref[lo:lo + B_BLOCK_DIM, lo:lo + B_BLOCK_DIM] = src[g * per + p]
                dst[g] = pltpu.bitcast(tile_ref[...].astype(bf16), _U32)


def _prep_weights(dense, lru_r, lru_i):
    in_specs, out_specs, out_shape = [], [], []
    for wm in dense:
        k, n = wm.shape
        assert k % (4 * SUBLANES * PREP_STEPS) == 0
        in_specs.append(pl.BlockSpec((k // PREP_STEPS, n), lambda i: (i, 0)))
        out_specs.append(pl.BlockSpec((k // (2 * PREP_STEPS), n), lambda i: (i, 0)))
        out_shape.append(jax.ShapeDtypeStruct((k // 2, n), _U32))
    blk = (B_BLOCKS, B_BLOCK_DIM, B_BLOCK_DIM)
    tiles = (LRU_GROUPS, MXU_TILE // 2, MXU_TILE)
    in_specs += [pl.BlockSpec(blk, lambda i: (0, 0, 0))] * 2
    out_specs += [pl.BlockSpec(tiles, lambda i: (0, 0, 0))] * 2
    out_shape += [jax.ShapeDtypeStruct(tiles, _U32)] * 2
    outs = pl.pallas_call(
        _prep_kernel, grid=(PREP_STEPS,), in_specs=in_specs, out_specs=out_specs, out_shape=out_shape,
        scratch_shapes=[pltpu.VMEM((MXU_TILE, MXU_TILE), f32)],
        compiler_params=pltpu.CompilerParams(vmem_limit_bytes=VMEM_LIMIT_BYTES,
                                             dimension_semantics=("arbitrary",)),
        name="prep_weights",
    )(*dense, lru_r, lru_i)
    return outs[:len(dense)], outs[len(dense)], outs[len(dense) + 1]


def kernel(x_prompt, x_sample, mem_prompt, cache_mem_k, cache_mem_v, state_hgrn, state_conv, state_lru, g_mix, w_in, lb_logits, g_a_out, w_a_down, w_conv, b_conv, w_lru_r, b_lru_r, w_lru_i, b_lru_i, lru_lambda, w_b_down, g_mem, w_mem_k, w_mem_v, w_c_down, w_out, g_final):
    bsz, seq, _ = x_prompt.shape
    dec_b, dec_seq, _ = x_sample.shape
    assert g_mix.shape[0] == 1, "single-layer stack only"
    assert seq % PROMPT_TILE == 0 and PROMPT_TILE % HGRN_CHUNK == 0

    row = lambda a: a.reshape(1, -1).astype(f32)
    dense = dict(w_in=w_in[0], w_a_down=w_a_down[0], w_b_down=w_b_down[0], w_c_down=w_c_down[0], w_out=w_out[0],
                 w_mem_k=w_mem_k[0], w_mem_v=w_mem_v[0])
    packed, lru_r_tiles, lru_i_tiles = _prep_weights([dense[n] for n in _DENSE_WEIGHTS], w_lru_r[0], w_lru_i[0])
    packed = dict(zip(_DENSE_WEIGHTS, packed))
    weights = dict(
        g_mix=row(g_mix[0]), w_in=packed["w_in"], lb_logits=lb_logits.astype(f32), g_a_out=row(g_a_out[0]),
        w_a_down=packed["w_a_down"], w_conv=w_conv[0].astype(f32), b_conv=row(b_conv[0]),
        w_lru_r=lru_r_tiles, b_lru_r=row(b_lru_r[0]), w_lru_i=lru_i_tiles, b_lru_i=row(b_lru_i[0]),
        lru_lambda=row(lru_lambda[0]), w_b_down=packed["w_b_down"], w_c_down=packed["w_c_down"],
        w_out=packed["w_out"], g_final=row(g_final))
    wlist = [weights[n] for n in _WEIGHT_NAMES]
    wspecs = [_const_spec(a.shape) for a in wlist]
    params = pltpu.CompilerParams(vmem_limit_bytes=VMEM_LIMIT_BYTES)

    mk, mv = pl.pallas_call(
        _memkv_kernel,
        grid=(bsz,),
        in_specs=[pl.BlockSpec((1, N_MEM, D_MODEL), lambda b: (b, 0, 0)),
                  _const_spec((1, D_MODEL)), _const_spec((D_MODEL // 2, C_WIDTH)),
                  _const_spec((D_MODEL // 2, C_WIDTH))],
        out_specs=[pl.BlockSpec((1, N_MEM, C_WIDTH), lambda b: (b, 0, 0))] * 2,
        out_shape=[jax.ShapeDtypeStruct((bsz, N_MEM, C_WIDTH), f32)] * 2,
        name="mem_kv",
    )(mem_prompt, row(g_mem[0]), packed["w_mem_k"], packed["w_mem_v"])

    tile = PROMPT_TILE
    y_p, hgrn_p, conv_p, lru_p = pl.pallas_call(
        _prompt_kernel,
        grid=(bsz, seq // tile),
        in_specs=[pl.BlockSpec((1, tile, D_MODEL), lambda b, t: (b, t, 0)),
                  pl.BlockSpec((1, N_MEM, C_WIDTH), lambda b, t: (b, 0, 0)),
                  pl.BlockSpec((1, N_MEM, C_WIDTH), lambda b, t: (b, 0, 0))] + wspecs,
        out_specs=[pl.BlockSpec((1, tile, D_MODEL), lambda b, t: (b, t, 0)),
                   pl.BlockSpec((1, A_HEADS, HEAD_DIM, HEAD_DIM), lambda b, t: (b, 0, 0, 0)),
                   pl.BlockSpec((1, CTX_ROWS, B_WIDTH), lambda b, t: (b, 0, 0)),
                   pl.BlockSpec((1, 1, B_WIDTH), lambda b, t: (b, 0, 0))],
        out_shape=[jax.ShapeDtypeStruct((bsz, seq, D_MODEL), f32),
                   jax.ShapeDtypeStruct((bsz, A_HEADS, HEAD_DIM, HEAD_DIM), f32),
                   jax.ShapeDtypeStruct((bsz, CTX_ROWS, B_WIDTH), f32),
                   jax.ShapeDtypeStruct((bsz, 1, B_WIDTH), f32)],
        scratch_shapes=[pltpu.VMEM((A_HEADS, HEAD_DIM, HEAD_DIM), f32),
                        pltpu.VMEM((CTX_ROWS, B_WIDTH), f32),
                        pltpu.VMEM((1, B_WIDTH), f32),
                        pltpu.VMEM((tile, IN_COLS), f32),
                        pltpu.VMEM((tile, A_WIDTH), f32),
                        pltpu.VMEM((SUBLANES + tile, B_WIDTH), f32)],
        compiler_params=pltpu.CompilerParams(vmem_limit_bytes=VMEM_LIMIT_BYTES,
                                             dimension_semantics=("arbitrary", "arbitrary")),
        name="prompt_layer",
    )(x_prompt, mk, mv, *wlist)

    rows = dec_b * dec_seq
    full = lambda shape: pl.BlockSpec(shape, lambda *_: (0,) * len(shape))
    y_s, hgrn_s, conv_s, lru_s = pl.pallas_call(
        functools.partial(_sample_kernel, dec_b, dec_seq),
        grid=(1,),
        in_specs=[full((rows, D_MODEL)), full((dec_b, N_MEM, C_WIDTH)), full((dec_b, N_MEM, C_WIDTH)),
                  full((dec_b, A_HEADS, HEAD_DIM, HEAD_DIM)), full((dec_b, CTX_ROWS, B_WIDTH)),
                  full((dec_b, 1, B_WIDTH))] + wspecs,
        out_specs=[full((rows, D_MODEL)), full((dec_b, A_HEADS, HEAD_DIM, HEAD_DIM)),
                   full((dec_b, CTX_ROWS, B_WIDTH)), full((dec_b, 1, B_WIDTH))],
        out_shape=[jax.ShapeDtypeStruct((rows, D_MODEL), f32),
                   jax.ShapeDtypeStruct((dec_b, A_HEADS, HEAD_DIM, HEAD_DIM), f32),
                   jax.ShapeDtypeStruct((dec_b, CTX_ROWS, B_WIDTH), f32),
                   jax.ShapeDtypeStruct((dec_b, 1, B_WIDTH), f32)],
        scratch_shapes=[pltpu.VMEM((rows, IN_COLS), f32),
                        pltpu.VMEM((rows, A_WIDTH), f32),
                        pltpu.VMEM((dec_b * (SUBLANES + dec_seq), B_WIDTH), f32)],
        compiler_params=params,
        name="sample_layer",
    )(x_sample.reshape(rows, D_MODEL), cache_mem_k[0].reshape(dec_b, N_MEM, C_WIDTH),
      cache_mem_v[0].reshape(dec_b, N_MEM, C_WIDTH), state_hgrn[0], state_conv[0],
      state_lru[0].reshape(dec_b, 1, B_WIDTH), *wlist)

    return (y_p, y_s.reshape(dec_b, dec_seq, D_MODEL), hgrn_p[None], conv_p[None],
            lru_p.reshape(1, bsz, B_WIDTH), mk.reshape(1, bsz, N_MEM, C_HEADS, HEAD_DIM),
            mv.reshape(1, bsz, N_MEM, C_HEADS, HEAD_DIM), hgrn_s[None], conv_s[None],
            lru_s.reshape(1, dec_b, B_WIDTH))
```

```python
import functools

import jax
import jax.numpy as jnp
from jax import lax
from jax.experimental import pallas as pl
from jax.experimental.pallas import tpu as pltpu

f32 = jnp.float32
bf16 = jnp.bfloat16

D_MODEL = 1024
N_MEM = 256
EPS = 1e-6
A_HEADS = 4
HEAD_DIM = 128
A_WIDTH = A_HEADS * HEAD_DIM
B_WIDTH = D_MODEL
B_BLOCKS = 16
B_BLOCK_DIM = B_WIDTH // B_BLOCKS
CONV_W = 4
LRU_C = 8.0
C_HEADS = 4
C_WIDTH = C_HEADS * HEAD_DIM
assert A_HEADS == C_HEADS
HGRN_CHUNK = 64
IN_COLS = 4 * A_WIDTH + 2 * B_WIDTH + 2 * C_WIDTH + 3 * D_MODEL

_QA, _FA, _VA, _GA = 0, A_WIDTH, 2 * A_WIDTH, 3 * A_WIDTH
_XB = 4 * A_WIDTH
_GB = _XB + B_WIDTH
_QC = _GB + B_WIDTH
_GC = _QC + C_WIDTH
_ZA = _GC + C_WIDTH
_ZB = _ZA + D_MODEL
_ZC = _ZB + D_MODEL

MXU_TILE = 256
LRU_GROUPS = B_WIDTH // MXU_TILE
SUBLANES = 8
CTX_ROWS = CONV_W - 1

PROMPT_TILE = 256
Z_BLK = 1024
Z_BLOCKS = IN_COLS // Z_BLK
VMEM_LIMIT_BYTES = 60 * 1024 * 1024


def _rms(x, g):
    return x * lax.rsqrt(jnp.mean(x * x, axis=-1, keepdims=True) + EPS) * g


def _wt(ref_or_val):
    return pltpu.bitcast(ref_or_val, bf16)


def _dot(a, b):
    return jnp.dot(a.astype(bf16), b.astype(bf16), preferred_element_type=f32)


def _dot_nt(a, b):
    return lax.dot_general(a.astype(bf16), b.astype(bf16), (((1,), (1,)), ((), ())),
                           preferred_element_type=f32)


def _dot_tn(a, b):
    return lax.dot_general(a.astype(bf16), b.astype(bf16), (((0,), (0,)), ((), ())),
                           preferred_element_type=f32)


def _silu(x):
    return x * jax.nn.sigmoid(x)


def _row_in_seg(rows, width, seg):
    return lax.broadcasted_iota(jnp.int32, (rows, width), 0) & (seg - 1)


def _vreg_groups(x):
    rows, width = x.shape
    return x.reshape(rows // SUBLANES, SUBLANES, width)


def _cumsum_rows(x, seg):
    rows = x.shape[0]
    x3 = _vreg_groups(x)
    sub = lax.broadcasted_iota(jnp.int32, x3.shape, 1)
    d = 1
    while d < SUBLANES:
        x3 = x3 + jnp.where(sub >= d, pltpu.roll(x3, d, 1), 0.0)
        d *= 2
    per_seg = seg // SUBLANES
    out, carry = [], None
    for g in range(rows // SUBLANES):
        cur = x3[g]
        if g % per_seg:
            cur = cur + carry
        carry = cur[SUBLANES - 1:SUBLANES, :]
        out.append(cur)
    return jnp.concatenate(out, axis=0)


def _linear_scan_rows(a, u, h0, seg):
    rows = a.shape[0]
    a3, u3 = _vreg_groups(a), _vreg_groups(u)
    sub = lax.broadcasted_iota(jnp.int32, a3.shape, 1)
    d = 1
    while d < SUBLANES:
        keep = sub >= d
        u3 = a3 * jnp.where(keep, pltpu.roll(u3, d, 1), 0.0) + u3
        a3 = a3 * jnp.where(keep, pltpu.roll(a3, d, 1), 1.0)
        d *= 2
    per_seg = seg // SUBLANES
    out, last, carry = [], [], None
    for g in range(rows // SUBLANES):
        if g % per_seg == 0:
            carry = h0[g // per_seg]
        cur = u3[g] + a3[g] * carry
        carry = cur[SUBLANES - 1:SUBLANES, :]
        out.append(cur)
        if g % per_seg == per_seg - 1:
            last.append(carry)
    return jnp.concatenate(out, axis=0), last


class _ZBuf:
    def __init__(self, refs):
        self.refs = refs

    def cols(self, c0, width, rows=slice(None)):
        blk, off = divmod(c0, Z_BLK)
        assert off + width <= Z_BLK
        return self.refs[blk][rows, off:off + width]


def _project_in_blocks(x, z, w):
    h = _rms(x, w["g_mix"][...]).astype(bf16)

    def block(c0):
        blk, off = divmod(c0, Z_BLK)
        z.refs[blk][:, off:off + MXU_TILE] = jnp.dot(h, _wt(w["w_in"][:, c0:c0 + MXU_TILE]),
                                                     preferred_element_type=f32)

    return [functools.partial(block, c0) for c0 in range(0, IN_COLS, MXU_TILE)]


def _project_in(x, z, w):
    for block in _project_in_blocks(x, z, w):
        block()


def _interleave(stages, blocks):
    blocks = list(blocks)
    while True:
        try:
            n = next(stages)
        except StopIteration as done:
            result = done.value
            break
        for _ in range(min(n, len(blocks))):
            blocks.pop(0)()
    for block in blocks:
        block()
    return result


def _mix(*args, **kwargs):
    return _interleave(_mix_stages(*args, **kwargs), [])


def _mix_stages(x, z, oa_ref, xpad_ref, kv, states, w, *, seg, chunk, first_rows_start):
    rows = x.shape[0]
    nseg = rows // seg
    nchunk = seg // chunk

    lg = w["lb_logits"][...]
    l0, l1 = lg[0:1, :], lg[1:2, :]
    lmax = jnp.maximum(l0, l1)
    e0, e1 = jnp.exp(l0 - lmax), jnp.exp(l1 - lmax)
    lb = e0 / (e0 + e1)

    f = lb + (1.0 - lb) * jax.nn.sigmoid(z.cols(_FA, A_WIDTH))
    b = _cumsum_rows(jnp.log(f), chunk)
    k = 1.0 - f
    qg = _silu(z.cols(_QA, A_WIDTH)) * jnp.exp(b)
    kg = k * jnp.exp(-b)
    v = z.cols(_VA, A_WIDTH)
    yield 3

    tt = lax.broadcasted_iota(jnp.int32, (rows, rows), 0)
    ss = lax.broadcasted_iota(jnp.int32, (rows, rows), 1)
    shift = chunk.bit_length() - 1
    causal = ((tt >> shift) == (ss >> shift)) & (ss <= tt)

    head_sl = [slice(hd * HEAD_DIM, (hd + 1) * HEAD_DIM) for hd in range(A_HEADS)]
    chunk_rows = [[slice(s * seg + c * chunk, s * seg + (c + 1) * chunk) for c in range(nchunk)]
                  for s in range(nseg)]
    scores = [_dot_nt(qg[:, sl], kg[:, sl]) for sl in head_sl]
    st_in = [[[states[s][0][hd]] for hd in range(A_HEADS)] for s in range(nseg)]
    for s in range(nseg):
        for c, rs in enumerate(chunk_rows[s]):
            b_last = b[rs.stop - 1:rs.stop, :]
            kd = k[rs, :] * jnp.exp(b_last - b[rs, :])
            decay = jnp.exp(b_last)
            for hd, sl in enumerate(head_sl):
                st_in[s][hd].append(st_in[s][hd][c] * decay[:, sl] + _dot_tn(v[rs, sl], kd[:, sl]))
    new_st = [[st_in[s][hd][nchunk] for hd in range(A_HEADS)] for s in range(nseg)]
    yield 2
    o_intra = [_dot(jnp.where(causal, scores[hd], 0.0), v[:, sl]) for hd, sl in enumerate(head_sl)]
    yield 1
    for hd, sl in enumerate(head_sl):
        for s in range(nseg):
            for c, rs in enumerate(chunk_rows[s]):
                oa_ref[rs, sl] = o_intra[hd][rs, :] + _dot_nt(qg[rs, sl], st_in[s][hd][c])
    yield 1

    g_a = w["g_a_out"][...]
    a_in = []
    for hd in range(A_HEADS):
        sl = slice(hd * HEAD_DIM, (hd + 1) * HEAD_DIM)
        a_in.append(_rms(oa_ref[:, sl], g_a[:, sl]))
    a_in = jnp.concatenate(a_in, axis=-1) * _silu(z.cols(_GA, A_WIDTH))
    pa = _dot(a_in, _wt(w["w_a_down"][...]))
    yield 1

    pad = SUBLANES
    stride = pad + seg
    new_ctx = []
    for s in range(nseg):
        base = s * stride
        xpad_ref[base + pad - CTX_ROWS:base + pad, :] = states[s][1]
        xpad_ref[base + pad:base + pad + seg, :] = z.cols(_XB, B_WIDTH, slice(s * seg, (s + 1) * seg))
    w_conv = w["w_conv"][...]
    xc = []
    for s in range(nseg):
        base = s * stride + pad - CTX_ROWS
        acc = w_conv[0:1, :] * xpad_ref[base:base + seg, :]
        for j in range(1, CONV_W):
            acc = acc + w_conv[j:j + 1, :] * xpad_ref[base + j:base + j + seg, :]
        xc.append(w["b_conv"][...] + acc)
        new_ctx.append(xpad_ref[base + seg:base + seg + CTX_ROWS, :])
    xc = xc[0] if nseg == 1 else jnp.concatenate(xc, axis=0)
    yield 5

    xc_b = xc.astype(bf16)
    r_pre, i_pre = [], []
    for g in range(LRU_GROUPS):
        gs = slice(g * MXU_TILE, (g + 1) * MXU_TILE)
        r_pre.append(jnp.dot(xc_b[:, gs], _wt(w["w_lru_r"][g]), preferred_element_type=f32))
        i_pre.append(jnp.dot(xc_b[:, gs], _wt(w["w_lru_i"][g]), preferred_element_type=f32))
    r = jax.nn.sigmoid(jnp.concatenate(r_pre, axis=-1) + w["b_lru_r"][...])
    ig = jax.nn.sigmoid(jnp.concatenate(i_pre, axis=-1) + w["b_lru_i"][...])
    yield 2
    neg_lam = -w["lru_lambda"][...]
    softplus = jnp.maximum(neg_lam, 0.0) + jnp.log1p(jnp.exp(-jnp.abs(neg_lam)))
    log_a = -LRU_C * r * softplus
    a = jnp.exp(log_a)
    mult = jnp.sqrt(-jnp.tanh(log_a) * (a * a + 1.0))
    if first_rows_start is not None:
        ris_b = _row_in_seg(rows, B_WIDTH, seg)
        mult = jnp.where((ris_b == 0) & first_rows_start, 1.0, mult)
    u = mult * ig * xc
    yield 6
    hb, new_hl = _linear_scan_rows(a, u, [states[s][2] for s in range(nseg)], seg)
    yield 4
    pb = _dot(hb * _silu(z.cols(_GB, B_WIDTH)), _wt(w["w_b_down"][...]))
    yield 1

    scale = HEAD_DIM ** -0.5
    mem = [kv(s) for s in range(nseg)]
    seg_rows = [slice(s * seg, (s + 1) * seg) for s in range(nseg)]
    sc = [[_dot_nt(z.cols(_QC + hd * HEAD_DIM, HEAD_DIM, seg_rows[s]), mem[s][0][:, sl]) * scale
           for hd, sl in enumerate(head_sl)] for s in range(nseg)]
    yield 1
    pr = []
    for s in range(nseg):
        pr.append([])
        for hd in range(C_HEADS):
            p = jnp.exp(sc[s][hd] - jnp.max(sc[s][hd], axis=-1, keepdims=True))
            pr[s].append(p / jnp.sum(p, axis=-1, keepdims=True))
    yield 1
    oc = [jnp.concatenate([_dot(pr[s][hd], mem[s][1][:, sl]) for hd, sl in enumerate(head_sl)], axis=-1)
          for s in range(nseg)]
    oc = oc[0] if nseg == 1 else jnp.concatenate(oc, axis=0)
    yield 1
    pc = _dot(oc * _silu(z.cols(_GC, C_WIDTH)), _wt(w["w_c_down"][...]))
    yield 1

    merged = (jax.nn.sigmoid(z.cols(_ZA, D_MODEL)) * pa
              + jax.nn.sigmoid(z.cols(_ZB, D_MODEL)) * pb
              + jax.nn.sigmoid(z.cols(_ZC, D_MODEL)) * pc)
    yield 2
    y = x +_dot(merged, _wt(w["w_out"][...]))
    y = _rms(y, w["g_final"][...])
    new_states = [(new_st[s], new_ctx[s], new_hl[s]) for s in range(nseg)]
    return y, new_states


_WEIGHT_NAMES = ("g_mix", "w_in", "lb_logits", "g_a_out", "w_a_down", "w_conv", "b_conv", "w_lru_r", "b_lru_r",
                 "w_lru_i", "b_lru_i", "lru_lambda", "w_b_down", "w_c_down", "w_out", "g_final")
_NW = len(_WEIGHT_NAMES)


def _prompt_kernel(steps_per_seq, *refs):
    x_ref, xn_ref, mk_ref, mv_ref = refs[:4]
    w = dict(zip(_WEIGHT_NAMES, refs[4:4 + _NW]))
    y_ref, hgrn_ref, conv_ref, lru_ref = refs[4 + _NW:8 + _NW]
    scratch = refs[8 + _NW:]
    st_ref, ctx_ref, hl_ref = scratch[:3]
    z_even, z_odd = _ZBuf(scratch[3:3 + Z_BLOCKS]), _ZBuf(scratch[3 + Z_BLOCKS:3 + 2 * Z_BLOCKS])
    oa_refs = scratch[3 + 2 * Z_BLOCKS:5 + 2 * Z_BLOCKS]
    xpad_refs = scratch[5 + 2 * Z_BLOCKS:7 + 2 * Z_BLOCKS]
    j = pl.program_id(0)
    tile = PROMPT_TILE
    seq_start = (j % steps_per_seq) == 0

    @pl.when(j == 0)
    def _():
        _project_in(x_ref[0:tile, :], z_even, w)

    @pl.when(seq_start)
    def _():
        st_ref[...] = jnp.zeros_like(st_ref)
        ctx_ref[...] = jnp.zeros_like(ctx_ref)
        hl_ref[...] = jnp.zeros_like(hl_ref)

    kv = lambda s: (mk_ref[0], mv_ref[0])
    states = [([st_ref[hd] for hd in range(A_HEADS)], ctx_ref[...], hl_ref[...])]

    y, states = _interleave(
        _mix_stages(x_ref[0:tile, :], z_even, oa_refs[0], xpad_refs[0], kv, states, w,
                    seg=tile, chunk=HGRN_CHUNK, first_rows_start=seq_start),
        _project_in_blocks(x_ref[tile:2 * tile, :], z_odd, w))
    y_ref[0:tile, :] = y

    y, states = _interleave(
        _mix_stages(x_ref[tile:2 * tile, :], z_odd, oa_refs[1], xpad_refs[1], kv, states, w,
                    seg=tile, chunk=HGRN_CHUNK, first_rows_start=None),
        _project_in_blocks(xn_ref[...], z_even, w))
    y_ref[tile:2 * tile, :] = y

    st, ctx, hl = states[0]
    for hd in range(A_HEADS):
        st_ref[hd] = st[hd]
    ctx_ref[...] = ctx
    hl_ref[...] = hl

    @pl.when((j % steps_per_seq) == steps_per_seq - 1)
    def _():
        for hd in range(A_HEADS):
            hgrn_ref[0, hd] = st[hd].T
        conv_ref[0] = ctx
        lru_ref[0] = hl


def _sample_kernel(nseq, seg, *refs):
    x_ref, mk_ref, mv_ref, hgrn_in, conv_in, lru_in = refs[:6]
    w = dict(zip(_WEIGHT_NAMES, refs[6:6 + _NW]))
    y_ref, hgrn_ref, conv_ref, lru_ref = refs[6 + _NW:10 + _NW]
    scratch = refs[10 + _NW:]
    z = _ZBuf(scratch[:Z_BLOCKS])
    oa_ref, xpad_ref = scratch[Z_BLOCKS:]
    states = [([hgrn_in[s, hd].T for hd in range(A_HEADS)], conv_in[s], lru_in[s]) for s in range(nseq)]
    _project_in(x_ref[...], z, w)
    y, new_states = _mix(
        x_ref[...], z, oa_ref, xpad_ref, lambda s: (mk_ref[s], mv_ref[s]), states, w,
        seg=seg, chunk=min(HGRN_CHUNK, seg), first_rows_start=None)
    y_ref[...] = y
    for s in range(nseq):
        st, ctx, hl = new_states[s]
        for hd in range(A_HEADS):
            hgrn_ref[s, hd] = st[hd].T
        conv_ref[s] = ctx
        lru_ref[s] = hl


def _memkv_kernel(mem_ref, g_ref, wk_ref, wv_ref, k_ref, v_ref):
    hm = _rms(mem_ref[0], g_ref[...]).astype(bf16)
    k_ref[0] = jnp.dot(hm, _wt(wk_ref[...]), preferred_element_type=f32)
    v_ref[0] = jnp.dot(hm, _wt(wv_ref[...]), preferred_element_type=f32)


def _const_spec(shape):
    nd = len(shape)
    return pl.BlockSpec(shape, lambda *_: (0,) * nd, pipeline_mode=pl.Buffered(1))


PREP_STEPS = 8
_DENSE_WEIGHTS = ("w_in", "w_a_down", "w_b_down", "w_c_down", "w_out", "w_mem_k", "w_mem_v")
_U32 = jnp.uint32


def _prep_kernel(*refs):
    nd = len(_DENSE_WEIGHTS)
    dense_in, (lru_r_in, lru_i_in) = refs[:nd], refs[nd:nd + 2]
    dense_out, (lru_r_out, lru_i_out) = refs[nd + 2:2 * nd + 2], refs[2 * nd + 2:2 * nd + 4]
    tile_ref = refs[2 * nd + 4]
    for src, dst in zip(dense_in, dense_out):
        dst[...] = pltpu.bitcast(src[...].astype(bf16), _U32)

    @pl.when(pl.program_id(0) == 0)
    def _():
        per = MXU_TILE // B_BLOCK_DIM
        for src, dst in ((lru_r_in, lru_r_out), (lru_i_in, lru_i_out)):
            for g in range(LRU_GROUPS):
                tile_ref[...] = jnp.zeros_like(tile_ref)
                for p in range(per):
                    lo = p * B_BLOCK_DIM
                    tile_ref[lo:lo + B_BLOCK_DIM, lo:lo + B_BLOCK_DIM] = src[g * per + p]
                dst[g] = pltpu.bitcast(tile_ref[...].astype(bf16), _U32)


def _prep_weights(dense, lru_r, lru_i):
    in_specs, out_specs, out_shape = [], [], []
    for wm in dense:
        k, n = wm.shape
        assert k % (4 * SUBLANES * PREP_STEPS) == 0
        in_specs.append(pl.BlockSpec((k // PREP_STEPS, n), lambda i: (i, 0)))
        out_specs.append(pl.BlockSpec((k // (2 * PREP_STEPS), n), lambda i: (i, 0)))
        out_shape.append(jax.ShapeDtypeStruct((k // 2, n), _U32))
    blk = (B_BLOCKS, B_BLOCK_DIM, B_BLOCK_DIM)
    tiles = (LRU_GROUPS, MXU_TILE // 2, MXU_TILE)
    in_specs += [pl.BlockSpec(blk, lambda i: (0, 0, 0))] * 2
    out_specs += [pl.BlockSpec(tiles, lambda i: (0, 0, 0))] * 2
    out_shape += [jax.ShapeDtypeStruct(tiles, _U32)] * 2
    outs = pl.pallas_call(
        _prep_kernel, grid=(PREP_STEPS,), in_specs=in_specs, out_specs=out_specs, out_shape=out_shape,
        scratch_shapes=[pltpu.VMEM((MXU_TILE, MXU_TILE), f32)],
        compiler_params=pltpu.CompilerParams(vmem_limit_bytes=VMEM_LIMIT_BYTES,
                                             dimension_semantics=("arbitrary",)),
        name="prep_weights",
    )(*dense, lru_r, lru_i)
    return outs[:len(dense)], outs[len(dense)], outs[len(dense) + 1]


def kernel(x_prompt, x_sample, mem_prompt, cache_mem_k, cache_mem_v, state_hgrn, state_conv, state_lru, g_mix, w_in, lb_logits, g_a_out, w_a_down, w_conv, b_conv, w_lru_r, b_lru_r, w_lru_i, b_lru_i, lru_lambda, w_b_down, g_mem, w_mem_k, w_mem_v, w_c_down, w_out, g_final):
    bsz, seq, _ = x_prompt.shape
    dec_b, dec_seq, _ = x_sample.shape
    assert g_mix.shape[0] == 1, "single-layer stack only"
    assert seq % (2 * PROMPT_TILE) == 0 and PROMPT_TILE % HGRN_CHUNK == 0

    row = lambda a: a.reshape(1, -1).astype(f32)
    dense = dict(w_in=w_in[0], w_a_down=w_a_down[0], w_b_down=w_b_down[0], w_c_down=w_c_down[0], w_out=w_out[0],
                 w_mem_k=w_mem_k[0], w_mem_v=w_mem_v[0])
    packed, lru_r_tiles, lru_i_tiles = _prep_weights([dense[n] for n in _DENSE_WEIGHTS], w_lru_r[0], w_lru_i[0])
    packed = dict(zip(_DENSE_WEIGHTS, packed))
    weights = dict(
        g_mix=row(g_mix[0]), w_in=packed["w_in"], lb_logits=lb_logits.astype(f32), g_a_out=row(g_a_out[0]),
        w_a_down=packed["w_a_down"], w_conv=w_conv[0].astype(f32), b_conv=row(b_conv[0]),
        w_lru_r=lru_r_tiles, b_lru_r=row(b_lru_r[0]), w_lru_i=lru_i_tiles, b_lru_i=row(b_lru_i[0]),
        lru_lambda=row(lru_lambda[0]), w_b_down=packed["w_b_down"], w_c_down=packed["w_c_down"],
        w_out=packed["w_out"], g_final=row(g_final))
    wlist = [weights[n] for n in _WEIGHT_NAMES]
    wspecs = [_const_spec(a.shape) for a in wlist]
    params = pltpu.CompilerParams(vmem_limit_bytes=VMEM_LIMIT_BYTES)

    mk, mv = pl.pallas_call(
        _memkv_kernel,
        grid=(bsz,),
        in_specs=[pl.BlockSpec((1, N_MEM, D_MODEL), lambda b: (b, 0, 0)),
                  _const_spec((1, D_MODEL)), _const_spec((D_MODEL // 2, C_WIDTH)),
                  _const_spec((D_MODEL // 2, C_WIDTH))],
        out_specs=[pl.BlockSpec((1, N_MEM, C_WIDTH), lambda b: (b, 0, 0))] * 2,
        out_shape=[jax.ShapeDtypeStruct((bsz, N_MEM, C_WIDTH), f32)] * 2,
        name="mem_kv",
    )(mem_prompt, row(g_mem[0]), packed["w_mem_k"], packed["w_mem_v"])

    tile = PROMPT_TILE
    n_tiles = bsz * seq // tile
    steps_per_seq = seq // (2 * tile)
    zbuf = [pltpu.VMEM((tile, Z_BLK), f32)] * Z_BLOCKS
    y_p, hgrn_p, conv_p, lru_p = pl.pallas_call(
        functools.partial(_prompt_kernel, steps_per_seq),
        grid=(n_tiles // 2,),
        in_specs=[pl.BlockSpec((2 * tile, D_MODEL), lambda j: (j, 0)),
                  pl.BlockSpec((tile, D_MODEL), lambda j: (jnp.minimum(2 * j + 2, n_tiles - 1), 0)),
                  pl.BlockSpec((1, N_MEM, C_WIDTH), lambda j: (j // steps_per_seq, 0, 0)),
                  pl.BlockSpec((1, N_MEM, C_WIDTH), lambda j: (j // steps_per_seq, 0, 0))] + wspecs,
        out_specs=[pl.BlockSpec((2 * tile, D_MODEL), lambda j: (j, 0)),
                   pl.BlockSpec((1, A_HEADS, HEAD_DIM, HEAD_DIM), lambda j: (j // steps_per_seq, 0, 0, 0)),
                   pl.BlockSpec((1, CTX_ROWS, B_WIDTH), lambda j: (j // steps_per_seq, 0, 0)),
                   pl.BlockSpec((1, 1, B_WIDTH), lambda j: (j // steps_per_seq, 0, 0))],
        out_shape=[jax.ShapeDtypeStruct((bsz * seq, D_MODEL), f32),
                   jax.ShapeDtypeStruct((bsz, A_HEADS, HEAD_DIM, HEAD_DIM), f32),
                   jax.ShapeDtypeStruct((bsz, CTX_ROWS, B_WIDTH), f32),
                   jax.ShapeDtypeStruct((bsz, 1, B_WIDTH), f32)],
        scratch_shapes=[pltpu.VMEM((A_HEADS, HEAD_DIM, HEAD_DIM), f32),
                        pltpu.VMEM((CTX_ROWS, B_WIDTH), f32),
                        pltpu.VMEM((1, B_WIDTH), f32)] + zbuf + zbuf
                       + [pltpu.VMEM((tile, A_WIDTH), f32)] * 2
                       + [pltpu.VMEM((SUBLANES + tile, B_WIDTH), f32)] * 2,
        compiler_params=pltpu.CompilerParams(vmem_limit_bytes=VMEM_LIMIT_BYTES,
                                             dimension_semantics=("arbitrary",)),
        name="prompt_layer",
    )(x_prompt.reshape(bsz * seq, D_MODEL), x_prompt.reshape(bsz * seq, D_MODEL), mk, mv, *wlist)
    y_p = y_p.reshape(bsz, seq, D_MODEL)

    rows = dec_b * dec_seq
    full = lambda shape: pl.BlockSpec(shape, lambda *_: (0,) * len(shape))
    y_s, hgrn_s, conv_s, lru_s = pl.pallas_call(
        functools.partial(_sample_kernel, dec_b, dec_seq),
        grid=(1,),
        in_specs=[full((rows, D_MODEL)), full((dec_b, N_MEM, C_WIDTH)), full((dec_b, N_MEM, C_WIDTH)),
                  full((dec_b, A_HEADS, HEAD_DIM, HEAD_DIM)), full((dec_b, CTX_ROWS, B_WIDTH)),
                  full((dec_b, 1, B_WIDTH))] + wspecs,
        out_specs=[full((rows, D_MODEL)), full((dec_b, A_HEADS, HEAD_DIM, HEAD_DIM)),
                   full((dec_b, CTX_ROWS, B_WIDTH)), full((dec_b, 1, B_WIDTH))],
        out_shape=[jax.ShapeDtypeStruct((rows, D_MODEL), f32),
                   jax.ShapeDtypeStruct((dec_b, A_HEADS, HEAD_DIM, HEAD_DIM), f32),
                   jax.ShapeDtypeStruct((dec_b, CTX_ROWS, B_WIDTH), f32),
                   jax.ShapeDtypeStruct((dec_b, 1, B_WIDTH), f32)],
        scratch_shapes=[pltpu.VMEM((rows, Z_BLK), f32)] * Z_BLOCKS
                       + [pltpu.VMEM((rows, A_WIDTH), f32),
                        pltpu.VMEM((dec_b * (SUBLANES + dec_seq), B_WIDTH), f32)],
        compiler_params=params,
        name="sample_layer",
    )(x_sample.reshape(rows, D_MODEL), cache_mem_k[0].reshape(dec_b, N_MEM, C_WIDTH),
      cache_mem_v[0].reshape(dec_b, N_MEM, C_WIDTH), state_hgrn[0], state_conv[0],
      state_lru[0].reshape(dec_b, 1, B_WIDTH), *wlist)

    return (y_p, y_s.reshape(dec_b, dec_seq, D_MODEL), hgrn_p[None], conv_p[None],
            lru_p.reshape(1, bsz, B_WIDTH), mk.reshape(1, bsz, N_MEM, C_HEADS, HEAD_DIM),
            mv.reshape(1, bsz, N_MEM, C_HEADS, HEAD_DIM), hgrn_s[None], conv_s[None],
            lru_s.reshape(1, dec_b, B_WIDTH))
```

```python
import functools

import jax
import jax.numpy as jnp
from jax import lax
from jax.experimental import pallas as pl
from jax.experimental.pallas import tpu as pltpu

f32 = jnp.float32
bf16 = jnp.bfloat16

D_MODEL = 1024
N_MEM = 256
EPS = 1e-6
A_HEADS = 4
HEAD_DIM = 128
A_WIDTH = A_HEADS * HEAD_DIM
B_WIDTH = D_MODEL
B_BLOCKS = 16
B_BLOCK_DIM = B_WIDTH // B_BLOCKS
CONV_W = 4
LRU_C = 8.0
C_HEADS = 4
C_WIDTH = C_HEADS * HEAD_DIM
assert A_HEADS == C_HEADS
HGRN_CHUNK = 64
IN_COLS = 4 * A_WIDTH + 2 * B_WIDTH + 2 * C_WIDTH + 3 * D_MODEL

_QA, _FA, _VA, _GA = 0, A_WIDTH, 2 * A_WIDTH, 3 * A_WIDTH
_XB = 4 * A_WIDTH
_GB = _XB + B_WIDTH
_QC = _GB + B_WIDTH
_GC = _QC + C_WIDTH
_ZA = _GC + C_WIDTH
_ZB = _ZA + D_MODEL
_ZC = _ZB + D_MODEL

MXU_TILE = 256
LRU_GROUPS = B_WIDTH // MXU_TILE
SUBLANES = 8
CTX_ROWS = CONV_W - 1

PROMPT_TILE = 256
Z_BLK = 1024
Z_BLOCKS = IN_COLS // Z_BLK
VMEM_LIMIT_BYTES = 60 * 1024 * 1024


def _rms(x, g):
    return x * lax.rsqrt(jnp.mean(x * x, axis=-1, keepdims=True) + EPS) * g


def _wt(ref_or_val):
    return pltpu.bitcast(ref_or_val, bf16)


def _dot(a, b):
    return jnp.dot(a.astype(bf16), b.astype(bf16), preferred_element_type=f32)


def _dot_nt(a, b):
    return lax.dot_general(a.astype(bf16), b.astype(bf16), (((1,), (1,)), ((), ())),
                           preferred_element_type=f32)


def _dot_tn(a, b):
    return lax.dot_general(a.astype(bf16), b.astype(bf16), (((0,), (0,)), ((), ())),
                           preferred_element_type=f32)


def _silu(x):
    return x * jax.nn.sigmoid(x)


KV_ROWS = (N_MEM * C_HEADS, HEAD_DIM)


def _head_rows(hd):
    return pl.ds(hd, N_MEM, stride=C_HEADS)


def _vreg_groups(x):
    rows, width = x.shape
    return x.reshape(rows // SUBLANES, SUBLANES, width)


def _cumsum_rows(x):
    rows = x.shape[0]
    x3 = _vreg_groups(x)
    sub = lax.broadcasted_iota(jnp.int32, x3.shape, 1)
    d = 1
    while d < SUBLANES:
        x3 = x3 + jnp.where(sub >= d, pltpu.roll(x3, d, 1), 0.0)
        d *= 2
    out, carry = [], None
    for g in range(rows // SUBLANES):
        cur = x3[g] if carry is None else x3[g] + carry
        carry = cur[SUBLANES - 1:SUBLANES, :]
        out.append(cur)
    return jnp.concatenate(out, axis=0)


def _linear_scan_rows(a, u, carry):
    rows = a.shape[0]
    a3, u3 = _vreg_groups(a), _vreg_groups(u)
    sub = lax.broadcasted_iota(jnp.int32, a3.shape, 1)
    d = 1
    while d < SUBLANES:
        keep = sub >= d
        u3 = a3 * jnp.where(keep, pltpu.roll(u3, d, 1), 0.0) + u3
        a3 = a3 * jnp.where(keep, pltpu.roll(a3, d, 1), 1.0)
        d *= 2
    out = []
    for g in range(rows // SUBLANES):
        cur = u3[g] + a3[g] * carry
        carry = cur[SUBLANES - 1:SUBLANES, :]
        out.append(cur)
    return jnp.concatenate(out, axis=0), carry


class _ZBuf:
    def __init__(self, refs):
        self.refs = refs

    def cols(self, c0, width, rows=slice(None)):
        blk, off = divmod(c0, Z_BLK)
        assert off + width <= Z_BLK
        return self.refs[blk][rows, off:off + width]


def _project_in_blocks(x, z, w):
    h = _rms(x, w["g_mix"][...]).astype(bf16)

    def block(c0):
        blk, off = divmod(c0, Z_BLK)
        z.refs[blk][:, off:off + MXU_TILE] = jnp.dot(h, _wt(w["w_in"][:, c0:c0 + MXU_TILE]),
                                                     preferred_element_type=f32)

    return [functools.partial(block, c0) for c0 in range(0, IN_COLS, MXU_TILE)]


def _project_in(x, z, w):
    for block in _project_in_blocks(x, z, w):
        block()


def _interleave(stages, blocks):
    blocks = list(blocks)
    while True:
        try:
            n = next(stages)
        except StopIteration as done:
            result = done.value
            break
        for _ in range(min(n, len(blocks))):
            blocks.pop(0)()
    for block in blocks:
        block()
    return result


def _mix(*args, **kwargs):
    return _interleave(_mix_stages(*args, **kwargs), [])


def _mix_stages(x, z, oa_ref, xpad_ref, kv, states, w, *, seg, chunk, first_rows_start):
    rows = x.shape[0]
    nseg = rows // seg
    nchunk = seg // chunk
    chunk_rows = [[slice(s * seg + c * chunk, s * seg + (c + 1) * chunk) for c in range(nchunk)]
                  for s in range(nseg)]
    all_chunks = [rs for per_seg in chunk_rows for rs in per_seg]
    head_sl = [slice(hd * HEAD_DIM, (hd + 1) * HEAD_DIM) for hd in range(A_HEADS)]
    cat_rows = lambda parts: parts[0] if len(parts) == 1 else jnp.concatenate(parts, axis=0)

    lg = w["lb_logits"][...]
    l0, l1 = lg[0:1, :], lg[1:2, :]
    lmax = jnp.maximum(l0, l1)
    e0, e1 = jnp.exp(l0 - lmax), jnp.exp(l1 - lmax)
    lb = e0 / (e0 + e1)

    qg, kg, v, kd, decay = [], [], [], [], []
    for rs in all_chunks:
        yield 1
        f = lb + (1.0 - lb) * jax.nn.sigmoid(z.cols(_FA, A_WIDTH, rs))
        b = _cumsum_rows(jnp.log(f))
        k = 1.0 - f
        qg.append(_silu(z.cols(_QA, A_WIDTH, rs)) * jnp.exp(b))
        kg.append(k * jnp.exp(-b))
        v.append(z.cols(_VA, A_WIDTH, rs))
        b_last = b[chunk - 1:chunk, :]
        kd.append(k * jnp.exp(b_last - b))
        decay.append(jnp.exp(b_last))
    qg_all, kg_all, v_all = cat_rows(qg), cat_rows(kg), cat_rows(v)

    scores = [_dot_nt(qg_all[:, sl], kg_all[:, sl]) for sl in head_sl]
    st_in = [[[states[s][0][hd]] for hd in range(A_HEADS)] for s in range(nseg)]
    for s in range(nseg):
        for c in range(nchunk):
            i = s * nchunk + c
            for hd, sl in enumerate(head_sl):
                st_in[s][hd].append(st_in[s][hd][c] * decay[i][:, sl] + _dot_tn(v[i][:, sl], kd[i][:, sl]))
    new_st = [[st_in[s][hd][nchunk] for hd in range(A_HEADS)] for s in range(nseg)]
    yield 1
    tt = lax.broadcasted_iota(jnp.int32, (rows, rows), 0)
    ss = lax.broadcasted_iota(jnp.int32, (rows, rows), 1)
    shift = chunk.bit_length() - 1
    causal = ((tt >> shift) == (ss >> shift)) & (ss <= tt)
    o_intra = [_dot(jnp.where(causal, scores[hd], 0.0), v_all[:, sl]) for hd, sl in enumerate(head_sl)]
    yield 1
    for hd, sl in enumerate(head_sl):
        for s in range(nseg):
            for c, rs in enumerate(chunk_rows[s]):
                oa_ref[rs, sl] = o_intra[hd][rs, :] + _dot_nt(qg[s * nchunk + c][:, sl], st_in[s][hd][c])

    g_a = w["g_a_out"][...]
    a_in = []
    for i, rs in enumerate(all_chunks):
        yield i % 2
        normed = jnp.concatenate([_rms(oa_ref[rs, sl], g_a[:, sl]) for sl in head_sl], axis=-1)
        a_in.append(normed * _silu(z.cols(_GA, A_WIDTH, rs)))
    pa = _dot(cat_rows(a_in), _wt(w["w_a_down"][...]))

    pad = SUBLANES
    stride = pad + seg
    for s in range(nseg):
        base = s * stride
        xpad_ref[base + pad - CTX_ROWS:base + pad, :] = states[s][1]
        xpad_ref[base + pad:base + pad + seg, :] = z.cols(_XB, B_WIDTH, slice(s * seg, (s + 1) * seg))
    w_conv = w["w_conv"][...]
    xc, new_ctx = [], []
    for s in range(nseg):
        base = s * stride + pad - CTX_ROWS
        for c in range(nchunk):
            yield 1
            r0 = base + c * chunk
            acc = w_conv[0:1, :] * xpad_ref[r0:r0 + chunk, :]
            for j in range(1, CONV_W):
                acc = acc + w_conv[j:j + 1, :] * xpad_ref[r0 + j:r0 + j + chunk, :]
            xc.append(w["b_conv"][...] + acc)
        new_ctx.append(xpad_ref[base + seg:base + seg + CTX_ROWS, :])

    xc_b = cat_rows(xc).astype(bf16)
    r_pre, i_pre = [], []
    for g in range(LRU_GROUPS):
        gs = slice(g * MXU_TILE, (g + 1) * MXU_TILE)
        r_pre.append(jnp.dot(xc_b[:, gs], _wt(w["w_lru_r"][g]), preferred_element_type=f32))
        i_pre.append(jnp.dot(xc_b[:, gs], _wt(w["w_lru_i"][g]), preferred_element_type=f32))
    r_pre, i_pre = jnp.concatenate(r_pre, axis=-1), jnp.concatenate(i_pre, axis=-1)
    neg_lam = -w["lru_lambda"][...]
    softplus = jnp.maximum(neg_lam, 0.0) + jnp.log1p(jnp.exp(-jnp.abs(neg_lam)))
    a_l, u_l = [], []
    for s in range(nseg):
        for c, rs in enumerate(chunk_rows[s]):
            yield 2
            r = jax.nn.sigmoid(r_pre[rs, :] + w["b_lru_r"][...])
            ig = jax.nn.sigmoid(i_pre[rs, :] + w["b_lru_i"][...])
            log_a = -LRU_C * r * softplus
            a = jnp.exp(log_a)
            mult = jnp.sqrt(-jnp.tanh(log_a) * (a * a + 1.0))
            if first_rows_start is not None and c == 0:
                first_row = lax.broadcasted_iota(jnp.int32, mult.shape, 0) == 0
                mult = jnp.where(first_row & first_rows_start, 1.0, mult)
            a_l.append(a)
            u_l.append(mult * ig * xc[s * nchunk + c])
    hb_gated, new_hl = [], []
    for s in range(nseg):
        carry = states[s][2]
        for c, rs in enumerate(chunk_rows[s]):
            yield 1
            i = s * nchunk + c
            hb, carry = _linear_scan_rows(a_l[i], u_l[i], carry)
            hb_gated.append(hb * _silu(z.cols(_GB, B_WIDTH, rs)))
        new_hl.append(carry)
    pb = _dot(cat_rows(hb_gated), _wt(w["w_b_down"][...]))

    scale = HEAD_DIM ** -0.5
    mem = [[kv(s, hd) for hd in range(C_HEADS)] for s in range(nseg)]
    seg_rows = [slice(s * seg, (s + 1) * seg) for s in range(nseg)]
    sc = [[_dot_nt(z.cols(_QC + hd * HEAD_DIM, HEAD_DIM, seg_rows[s]), mem[s][hd][0]) * scale
           for hd in range(C_HEADS)] for s in range(nseg)]
    pr = []
    for s in range(nseg):
        pr.append([])
        for hd in range(C_HEADS):
            yield 1
            p = jnp.exp(sc[s][hd] - jnp.max(sc[s][hd], axis=-1, keepdims=True))
            pr[s].append(p / jnp.sum(p, axis=-1, keepdims=True))
    oc = cat_rows([jnp.concatenate([_dot(pr[s][hd], mem[s][hd][1]) for hd in range(C_HEADS)], axis=-1)
                   for s in range(nseg)])
    yield 1
    pc = _dot(oc * _silu(z.cols(_GC, C_WIDTH)), _wt(w["w_c_down"][...]))

    merged = []
    for rs in all_chunks:
        yield 1
        merged.append(jax.nn.sigmoid(z.cols(_ZA, D_MODEL, rs)) * pa[rs, :]
                      + jax.nn.sigmoid(z.cols(_ZB, D_MODEL, rs)) * pb[rs, :]
                      + jax.nn.sigmoid(z.cols(_ZC, D_MODEL, rs)) * pc[rs, :])
    y = x + _dot(cat_rows(merged), _wt(w["w_out"][...]))
    y = _rms(y, w["g_final"][...])
    new_states = [(new_st[s], new_ctx[s], new_hl[s]) for s in range(nseg)]
    return y, new_states


_WEIGHT_NAMES = ("g_mix", "w_in", "lb_logits", "g_a_out", "w_a_down", "w_conv", "b_conv", "w_lru_r", "b_lru_r",
                 "w_lru_i", "b_lru_i", "lru_lambda", "w_b_down", "w_c_down", "w_out", "g_final")
_NW = len(_WEIGHT_NAMES)


def _store_seq_state(conv_ref, lru_ref, seq_idx, ctx, hl):
    for r in range(CTX_ROWS):
        conv_ref[r, pl.ds(seq_idx, 1), :] = ctx[r:r + 1, :]
    lru_ref[pl.ds(seq_idx, 1), :] = hl


def _prompt_kernel(steps_per_seq, *refs):
    x_ref, xn_ref, mk_ref, mv_ref = refs[:4]
    w = dict(zip(_WEIGHT_NAMES, refs[4:4 + _NW]))
    y_ref, hgrn_ref, conv_ref, lru_ref = refs[4 + _NW:8 + _NW]
    scratch = refs[8 + _NW:]
    st_ref, ctx_ref, hl_ref = scratch[:3]
    z_even, z_odd = _ZBuf(scratch[3:3 + Z_BLOCKS]), _ZBuf(scratch[3 + Z_BLOCKS:3 + 2 * Z_BLOCKS])
    oa_refs = scratch[3 + 2 * Z_BLOCKS:5 + 2 * Z_BLOCKS]
    xpad_refs = scratch[5 + 2 * Z_BLOCKS:7 + 2 * Z_BLOCKS]
    j = pl.program_id(0)
    tile = PROMPT_TILE
    seq_start = (j % steps_per_seq) == 0

    @pl.when(j == 0)
    def _():
        _project_in(x_ref[0:tile, :], z_even, w)

    @pl.when(seq_start)
    def _():
        st_ref[...] = jnp.zeros_like(st_ref)
        ctx_ref[...] = jnp.zeros_like(ctx_ref)
        hl_ref[...] = jnp.zeros_like(hl_ref)

    kv = lambda s, hd: (mk_ref[0, _head_rows(hd), :], mv_ref[0, _head_rows(hd), :])
    states = [([st_ref[hd] for hd in range(A_HEADS)], ctx_ref[...], hl_ref[...])]

    y, states = _interleave(
        _mix_stages(x_ref[0:tile, :], z_even, oa_refs[0], xpad_refs[0], kv, states, w,
                    seg=tile, chunk=HGRN_CHUNK, first_rows_start=seq_start),
        _project_in_blocks(x_ref[tile:2 * tile, :], z_odd, w))
    y_ref[0:tile, :] = y

    y, states = _interleave(
        _mix_stages(x_ref[tile:2 * tile, :], z_odd, oa_refs[1], xpad_refs[1], kv, states, w,
                    seg=tile, chunk=HGRN_CHUNK, first_rows_start=None),
        _project_in_blocks(xn_ref[...], z_even, w))
    y_ref[tile:2 * tile, :] = y

    st, ctx, hl = states[0]
    for hd in range(A_HEADS):
        st_ref[hd] = st[hd]
    ctx_ref[...] = ctx
    hl_ref[...] = hl

    @pl.when((j % steps_per_seq) == steps_per_seq - 1)
    def _():
        for hd in range(A_HEADS):
            hgrn_ref[0, hd] = st[hd].T
        _store_seq_state(conv_ref, lru_ref, j // steps_per_seq, ctx, hl)


def _sample_kernel(nseq, seg, *refs):
    x_ref, mk_ref, mv_ref, hgrn_in, conv_in, lru_in = refs[:6]
    w = dict(zip(_WEIGHT_NAMES, refs[6:6 + _NW]))
    y_ref, hgrn_ref, conv_ref, lru_ref = refs[6 + _NW:10 + _NW]
    scratch = refs[10 + _NW:]
    z = _ZBuf(scratch[:Z_BLOCKS])
    oa_ref, xpad_ref = scratch[Z_BLOCKS:]
    states = [([hgrn_in[s, hd].T for hd in range(A_HEADS)],
               jnp.concatenate([conv_in[r, s:s + 1, :] for r in range(CTX_ROWS)], axis=0),
               lru_in[s:s + 1, :]) for s in range(nseq)]
    _project_in(x_ref[...], z, w)
    y, new_states = _mix(
        x_ref[...], z, oa_ref, xpad_ref,
        lambda s, hd: (mk_ref[s, _head_rows(hd), :], mv_ref[s, _head_rows(hd), :]), states, w,
        seg=seg, chunk=min(HGRN_CHUNK, seg), first_rows_start=None)
    y_ref[...] = y
    for s in range(nseq):
        st, ctx, hl = new_states[s]
        for hd in range(A_HEADS):
            hgrn_ref[s, hd] = st[hd].T
        _store_seq_state(conv_ref, lru_ref, s, ctx, hl)


def _memkv_kernel(mem_ref, g_ref, wk_ref, wv_ref, k_ref, v_ref):
    hm = _rms(mem_ref[0], g_ref[...]).astype(bf16)
    k = jnp.dot(hm, _wt(wk_ref[...]), preferred_element_type=f32)
    v = jnp.dot(hm, _wt(wv_ref[...]), preferred_element_type=f32)
    for hd in range(C_HEADS):
        sl = slice(hd * HEAD_DIM, (hd + 1) * HEAD_DIM)
        k_ref[0, _head_rows(hd), :] = k[:, sl]
        v_ref[0, _head_rows(hd), :] = v[:, sl]


def _const_spec(shape):
    nd = len(shape)
    return pl.BlockSpec(shape, lambda *_: (0,) * nd, pipeline_mode=pl.Buffered(1))


PREP_STEPS = 8
_DENSE_WEIGHTS = ("w_in", "w_a_down", "w_b_down", "w_c_down", "w_out", "w_mem_k", "w_mem_v")
_U32 = jnp.uint32


def _prep_kernel(*refs):
    nd = len(_DENSE_WEIGHTS)
    dense_in, (lru_r_in, lru_i_in) = refs[:nd], refs[nd:nd + 2]
    dense_out, (lru_r_out, lru_i_out) = refs[nd + 2:2 * nd + 2], refs[2 * nd + 2:2 * nd + 4]
    tile_ref = refs[2 * nd + 4]
    for src, dst in zip(dense_in, dense_out):
        dst[...] = pltpu.bitcast(src[...].astype(bf16), _U32)

    @pl.when(pl.program_id(0) == 0)
    def _():
        per = MXU_TILE // B_BLOCK_DIM
        for src, dst in ((lru_r_in, lru_r_out), (lru_i_in, lru_i_out)):
            for g in range(LRU_GROUPS):
                tile_ref[...] = jnp.zeros_like(tile_ref)
                for p in range(per):
                    lo = p * B_BLOCK_DIM
                    tile_ref[lo:lo + B_BLOCK_DIM, lo:lo + B_BLOCK_DIM] = src[g * per + p]
                dst[g] = pltpu.bitcast(tile_ref[...].astype(bf16), _U32)


def _prep_weights(dense, lru_r, lru_i):
    in_specs, out_specs, out_shape = [], [], []
    for wm in dense:
        k, n = wm.shape
        assert k % (4 * SUBLANES * PREP_STEPS) == 0
        in_specs.append(pl.BlockSpec((k // PREP_STEPS, n), lambda i: (i, 0)))
        out_specs.append(pl.BlockSpec((k // (2 * PREP_STEPS), n), lambda i: (i, 0)))
        out_shape.append(jax.ShapeDtypeStruct((k // 2, n), _U32))
    blk = (B_BLOCKS, B_BLOCK_DIM, B_BLOCK_DIM)
    tiles = (LRU_GROUPS, MXU_TILE // 2, MXU_TILE)
    in_specs += [pl.BlockSpec(blk, lambda i: (0, 0, 0))] * 2
    out_specs += [pl.BlockSpec(tiles, lambda i: (0, 0, 0))] * 2
    out_shape += [jax.ShapeDtypeStruct(tiles, _U32)] * 2
    outs = pl.pallas_call(
        _prep_kernel, grid=(PREP_STEPS,), in_specs=in_specs, out_specs=out_specs, out_shape=out_shape,
        scratch_shapes=[pltpu.VMEM((MXU_TILE, MXU_TILE), f32)],
        compiler_params=pltpu.CompilerParams(vmem_limit_bytes=VMEM_LIMIT_BYTES,
                                             dimension_semantics=("arbitrary",)),
        name="prep_weights",
    )(*dense, lru_r, lru_i)
    return outs[:len(dense)], outs[len(dense)], outs[len(dense) + 1]


def kernel(x_prompt, x_sample, mem_prompt, cache_mem_k, cache_mem_v, state_hgrn, state_conv, state_lru, g_mix, w_in, lb_logits, g_a_out, w_a_down, w_conv, b_conv, w_lru_r, b_lru_r, w_lru_i, b_lru_i, lru_lambda, w_b_down, g_mem, w_mem_k, w_mem_v, w_c_down, w_out, g_final):
    bsz, seq, _ = x_prompt.shape
    dec_b, dec_seq, _ = x_sample.shape
    assert g_mix.shape[0] == 1, "single-layer stack only"
    assert seq % (2 * PROMPT_TILE) == 0 and PROMPT_TILE % HGRN_CHUNK == 0

    row = lambda a: a.reshape(1, -1).astype(f32)
    dense = dict(w_in=w_in[0], w_a_down=w_a_down[0], w_b_down=w_b_down[0], w_c_down=w_c_down[0], w_out=w_out[0],
                 w_mem_k=w_mem_k[0], w_mem_v=w_mem_v[0])
    packed, lru_r_tiles, lru_i_tiles = _prep_weights([dense[n] for n in _DENSE_WEIGHTS], w_lru_r[0], w_lru_i[0])
    packed = dict(zip(_DENSE_WEIGHTS, packed))
    weights = dict(
        g_mix=row(g_mix[0]), w_in=packed["w_in"], lb_logits=lb_logits.astype(f32), g_a_out=row(g_a_out[0]),
        w_a_down=packed["w_a_down"], w_conv=w_conv[0].astype(f32), b_conv=row(b_conv[0]),
        w_lru_r=lru_r_tiles, b_lru_r=row(b_lru_r[0]), w_lru_i=lru_i_tiles, b_lru_i=row(b_lru_i[0]),
        lru_lambda=row(lru_lambda[0]), w_b_down=packed["w_b_down"], w_c_down=packed["w_c_down"],
        w_out=packed["w_out"], g_final=row(g_final))
    wlist = [weights[n] for n in _WEIGHT_NAMES]
    wspecs = [_const_spec(a.shape) for a in wlist]
    params = pltpu.CompilerParams(vmem_limit_bytes=VMEM_LIMIT_BYTES)

    mk, mv = pl.pallas_call(
        _memkv_kernel,
        grid=(bsz,),
        in_specs=[pl.BlockSpec((1, N_MEM, D_MODEL), lambda b: (b, 0, 0)),
                  _const_spec((1, D_MODEL)), _const_spec((D_MODEL // 2, C_WIDTH)),
                  _const_spec((D_MODEL // 2, C_WIDTH))],
        out_specs=[pl.BlockSpec((1,) + KV_ROWS, lambda b: (b, 0, 0))] * 2,
        out_shape=[jax.ShapeDtypeStruct((bsz,) + KV_ROWS, f32)] * 2,
        name="mem_kv",
    )(mem_prompt, row(g_mem[0]), packed["w_mem_k"], packed["w_mem_v"])

    tile = PROMPT_TILE
    n_tiles = bsz * seq // tile
    steps_per_seq = seq // (2 * tile)
    zbuf = [pltpu.VMEM((tile, Z_BLK), f32)] * Z_BLOCKS
    y_p, hgrn_p, conv_p, lru_p = pl.pallas_call(
        functools.partial(_prompt_kernel, steps_per_seq),
        grid=(n_tiles // 2,),
        in_specs=[pl.BlockSpec((2 * tile, D_MODEL), lambda j: (j, 0)),
                  pl.BlockSpec((tile, D_MODEL), lambda j: (jnp.minimum(2 * j + 2, n_tiles - 1), 0)),
                  pl.BlockSpec((1,) + KV_ROWS, lambda j: (j // steps_per_seq, 0, 0)),
                  pl.BlockSpec((1,) + KV_ROWS, lambda j: (j // steps_per_seq, 0, 0))] + wspecs,
        out_specs=[pl.BlockSpec((2 * tile, D_MODEL), lambda j: (j, 0)),
                   pl.BlockSpec((1, A_HEADS, HEAD_DIM, HEAD_DIM), lambda j: (j // steps_per_seq, 0, 0, 0)),
                   pl.BlockSpec((CTX_ROWS, bsz, B_WIDTH), lambda j: (0, 0, 0)),
                   pl.BlockSpec((bsz, B_WIDTH), lambda j: (0, 0))],
        out_shape=[jax.ShapeDtypeStruct((bsz * seq, D_MODEL), f32),
                   jax.ShapeDtypeStruct((bsz, A_HEADS, HEAD_DIM, HEAD_DIM), f32),
                   jax.ShapeDtypeStruct((CTX_ROWS, bsz, B_WIDTH), f32),
                   jax.ShapeDtypeStruct((bsz, B_WIDTH), f32)],
        scratch_shapes=[pltpu.VMEM((A_HEADS, HEAD_DIM, HEAD_DIM), f32),
                        pltpu.VMEM((CTX_ROWS, B_WIDTH), f32),
                        pltpu.VMEM((1, B_WIDTH), f32)] + zbuf + zbuf
                       + [pltpu.VMEM((tile, A_WIDTH), f32)] * 2
                       + [pltpu.VMEM((SUBLANES + tile, B_WIDTH), f32)] * 2,
        compiler_params=pltpu.CompilerParams(vmem_limit_bytes=VMEM_LIMIT_BYTES,
                                             dimension_semantics=("arbitrary",)),
        name="prompt_layer",
    )(x_prompt.reshape(bsz * seq, D_MODEL), x_prompt.reshape(bsz * seq, D_MODEL), mk, mv, *wlist)
    y_p = y_p.reshape(bsz, seq, D_MODEL)

    rows = dec_b * dec_seq
    full = lambda shape: pl.BlockSpec(shape, lambda *_: (0,) * len(shape))
    y_s, hgrn_s, conv_s, lru_s = pl.pallas_call(
        functools.partial(_sample_kernel, dec_b, dec_seq),
        grid=(1,),
        in_specs=[full((rows, D_MODEL)), full((dec_b,) + KV_ROWS), full((dec_b,) + KV_ROWS),
                  full((dec_b, A_HEADS, HEAD_DIM, HEAD_DIM)), full((CTX_ROWS, dec_b, B_WIDTH)),
                  full((dec_b, B_WIDTH))] + wspecs,
        out_specs=[full((rows, D_MODEL)), full((dec_b, A_HEADS, HEAD_DIM, HEAD_DIM)),
                   full((CTX_ROWS, dec_b, B_WIDTH)), full((dec_b, B_WIDTH))],
        out_shape=[jax.ShapeDtypeStruct((rows, D_MODEL), f32),
                   jax.ShapeDtypeStruct((dec_b, A_HEADS, HEAD_DIM, HEAD_DIM), f32),
                   jax.ShapeDtypeStruct((CTX_ROWS, dec_b, B_WIDTH), f32),
                   jax.ShapeDtypeStruct((dec_b, B_WIDTH), f32)],
        scratch_shapes=[pltpu.VMEM((rows, Z_BLK), f32)] * Z_BLOCKS
                       + [pltpu.VMEM((rows, A_WIDTH), f32),
                        pltpu.VMEM((dec_b * (SUBLANES + dec_seq), B_WIDTH), f32)],
        compiler_params=params,
        name="sample_layer",
    )(x_sample.reshape(rows, D_MODEL), cache_mem_k.reshape((dec_b,) + KV_ROWS),
      cache_mem_v.reshape((dec_b,) + KV_ROWS), state_hgrn[0], jnp.swapaxes(state_conv[0], 0, 1),
      state_lru[0], *wlist)

    return (y_p, y_s.reshape(dec_b, dec_seq, D_MODEL), hgrn_p[None], jnp.swapaxes(conv_p, 0, 1)[None],
            lru_p[None], mk.reshape(1, bsz, N_MEM, C_HEADS, HEAD_DIM),
            mv.reshape(1, bsz, N_MEM, C_HEADS, HEAD_DIM), hgrn_s[None], jnp.swapaxes(conv_s, 0, 1)[None],
            lru_s[None])
```

```python
import functools

import jax
import jax.numpy as jnp
from jax import lax
from jax.experimental import pallas as pl
from jax.experimental.pallas import tpu as pltpu

f32 = jnp.float32
bf16 = jnp.bfloat16

D_MODEL = 1024
N_MEM = 256
EPS = 1e-6
A_HEADS = 4
HEAD_DIM = 128
A_WIDTH = A_HEADS * HEAD_DIM
B_WIDTH = D_MODEL
B_BLOCKS = 16
B_BLOCK_DIM = B_WIDTH // B_BLOCKS
CONV_W = 4
LRU_C = 8.0
C_HEADS = 4
C_WIDTH = C_HEADS * HEAD_DIM
assert A_HEADS == C_HEADS
HGRN_CHUNK = 64
IN_COLS = 4 * A_WIDTH + 2 * B_WIDTH + 2 * C_WIDTH + 3 * D_MODEL

_QA, _FA, _VA, _GA = 0, A_WIDTH, 2 * A_WIDTH, 3 * A_WIDTH
_XB = 4 * A_WIDTH
_GB = _XB + B_WIDTH
_QC = _GB + B_WIDTH
_GC = _QC + C_WIDTH
_ZA = _GC + C_WIDTH
_ZB = _ZA + D_MODEL
_ZC = _ZB + D_MODEL

MXU_TILE = 256
LRU_GROUPS = B_WIDTH // MXU_TILE
SUBLANES = 8
CTX_ROWS = CONV_W - 1

PROMPT_TILE = 256
Z_BLK = 1024
Z_BLOCKS = IN_COLS // Z_BLK
VMEM_LIMIT_BYTES = 62 * 1024 * 1024


def _rms(x, g):
    return x * lax.rsqrt(jnp.mean(x * x, axis=-1, keepdims=True) + EPS) * g


def _wt(ref_or_val):
    return pltpu.bitcast(ref_or_val, bf16)


def _dot(a, b):
    return jnp.dot(a.astype(bf16), b.astype(bf16), preferred_element_type=f32)


def _dot_nt(a, b):
    return lax.dot_general(a.astype(bf16), b.astype(bf16), (((1,), (1,)), ((), ())),
                           preferred_element_type=f32)


def _dot_tn(a, b):
    return lax.dot_general(a.astype(bf16), b.astype(bf16), (((0,), (0,)), ((), ())),
                           preferred_element_type=f32)


def _silu(x):
    return x * jax.nn.sigmoid(x)


KV_ROWS = (N_MEM * C_HEADS, HEAD_DIM)


def _head_rows(hd):
    return pl.ds(hd, N_MEM, stride=C_HEADS)


def _vreg_groups(x):
    rows, width = x.shape
    return x.reshape(rows // SUBLANES, SUBLANES, width)


def _cumsum_rows(x):
    rows = x.shape[0]
    x3 = _vreg_groups(x)
    sub = lax.broadcasted_iota(jnp.int32, x3.shape, 1)
    d = 1
    while d < SUBLANES:
        x3 = x3 + jnp.where(sub >= d, pltpu.roll(x3, d, 1), 0.0)
        d *= 2
    out, carry = [], None
    for g in range(rows // SUBLANES):
        cur = x3[g] if carry is None else x3[g] + carry
        carry = cur[SUBLANES - 1:SUBLANES, :]
        out.append(cur)
    return jnp.concatenate(out, axis=0)


def _linear_scan_rows(a, u, carry):
    rows = a.shape[0]
    a3, u3 = _vreg_groups(a), _vreg_groups(u)
    sub = lax.broadcasted_iota(jnp.int32, a3.shape, 1)
    d = 1
    while d < SUBLANES:
        keep = sub >= d
        u3 = a3 * jnp.where(keep, pltpu.roll(u3, d, 1), 0.0) + u3
        a3 = a3 * jnp.where(keep, pltpu.roll(a3, d, 1), 1.0)
        d *= 2
    out = []
    for g in range(rows // SUBLANES):
        cur = u3[g] + a3[g] * carry
        carry = cur[SUBLANES - 1:SUBLANES, :]
        out.append(cur)
    return jnp.concatenate(out, axis=0), carry


class _ZBuf:
    def __init__(self, refs):
        self.refs = refs

    def cols(self, c0, width, rows=slice(None)):
        blk, off = divmod(c0, Z_BLK)
        assert off + width <= Z_BLK
        return self.refs[blk][rows, off:off + width]


def _project_in_blocks(x, z, w):
    h = _rms(x, w["g_mix"][...]).astype(bf16)

    def block(c0):
        blk, off = divmod(c0, Z_BLK)
        z.refs[blk][:, off:off + MXU_TILE] = jnp.dot(h, _wt(w["w_in"][:, c0:c0 + MXU_TILE]),
                                                     preferred_element_type=f32)

    return [functools.partial(block, c0) for c0 in range(0, IN_COLS, MXU_TILE)]


def _project_in(x, z, w):
    for block in _project_in_blocks(x, z, w):
        block()


def _interleave(stages, blocks):
    blocks = list(blocks)
    while True:
        try:
            n = next(stages)
        except StopIteration as done:
            result = done.value
            break
        for _ in range(min(n, len(blocks))):
            blocks.pop(0)()
    for block in blocks:
        block()
    return result


def _mix(*args, **kwargs):
    return _interleave(_mix_stages(*args, **kwargs), [])


def _mix_stages(x, z, oa_ref, xpad_ref, kv, states, w, *, seg, chunk, first_rows_start):
    rows = x.shape[0]
    nseg = rows // seg
    nchunk = seg // chunk
    chunk_rows = [[slice(s * seg + c * chunk, s * seg + (c + 1) * chunk) for c in range(nchunk)]
                  for s in range(nseg)]
    all_chunks = [rs for per_seg in chunk_rows for rs in per_seg]
    head_sl = [slice(hd * HEAD_DIM, (hd + 1) * HEAD_DIM) for hd in range(A_HEADS)]
    cat_rows = lambda parts: parts[0] if len(parts) == 1 else jnp.concatenate(parts, axis=0)


    scale = HEAD_DIM ** -0.5
    mem = [[kv(s, hd) for hd in range(C_HEADS)] for s in range(nseg)]
    seg_rows = [slice(s * seg, (s + 1) * seg) for s in range(nseg)]
    sc = [[_dot_nt(z.cols(_QC + hd * HEAD_DIM, HEAD_DIM, seg_rows[s]), mem[s][hd][0]) * scale
           for hd in range(C_HEADS)] for s in range(nseg)]

    lg = w["lb_logits"][...]
    l0, l1 = lg[0:1, :], lg[1:2, :]
    lmax = jnp.maximum(l0, l1)
    e0, e1 = jnp.exp(l0 - lmax), jnp.exp(l1 - lmax)
    lb = e0 / (e0 + e1)

    qg, kg, v, kd, decay = [], [], [], [], []
    for rs in all_chunks:
        yield 1
        f = lb + (1.0 - lb) * jax.nn.sigmoid(z.cols(_FA, A_WIDTH, rs))
        b = _cumsum_rows(jnp.log(f))
        k = 1.0 - f
        qg.append(_silu(z.cols(_QA, A_WIDTH, rs)) * jnp.exp(b))
        kg.append(k * jnp.exp(-b))
        v.append(z.cols(_VA, A_WIDTH, rs))
        b_last = b[chunk - 1:chunk, :]
        kd.append(k * jnp.exp(b_last - b))
        decay.append(jnp.exp(b_last))
    qg_all, kg_all, v_all = cat_rows(qg), cat_rows(kg), cat_rows(v)

    scores = [_dot_nt(qg_all[:, sl], kg_all[:, sl]) for sl in head_sl]
    st_in = [[[states[s][0][hd]] for hd in range(A_HEADS)] for s in range(nseg)]
    for s in range(nseg):
        for c in range(nchunk):
            i = s * nchunk + c
            for hd, sl in enumerate(head_sl):
                st_in[s][hd].append(st_in[s][hd][c] * decay[i][:, sl] + _dot_tn(v[i][:, sl], kd[i][:, sl]))
    new_st = [[st_in[s][hd][nchunk] for hd in range(A_HEADS)] for s in range(nseg)]

    pad = SUBLANES
    stride = pad + seg
    for s in range(nseg):
        base = s * stride
        xpad_ref[base + pad - CTX_ROWS:base + pad, :] = states[s][1]
        xpad_ref[base + pad:base + pad + seg, :] = z.cols(_XB, B_WIDTH, slice(s * seg, (s + 1) * seg))
    w_conv = w["w_conv"][...]
    xc, new_ctx = [], []
    for s in range(nseg):
        base = s * stride + pad - CTX_ROWS
        for c in range(nchunk):
            yield 1
            r0 = base + c * chunk
            acc = w_conv[0:1, :] * xpad_ref[r0:r0 + chunk, :]
            for j in range(1, CONV_W):
                acc = acc + w_conv[j:j + 1, :] * xpad_ref[r0 + j:r0 + j + chunk, :]
            xc.append(w["b_conv"][...] + acc)
        new_ctx.append(xpad_ref[base + seg:base + seg + CTX_ROWS, :])

    xc_b = cat_rows(xc).astype(bf16)
    r_pre, i_pre = [], []
    for g in range(LRU_GROUPS):
        gs = slice(g * MXU_TILE, (g + 1) * MXU_TILE)
        r_pre.append(jnp.dot(xc_b[:, gs], _wt(w["w_lru_r"][g]), preferred_element_type=f32))
        i_pre.append(jnp.dot(xc_b[:, gs], _wt(w["w_lru_i"][g]), preferred_element_type=f32))
    r_pre, i_pre = jnp.concatenate(r_pre, axis=-1), jnp.concatenate(i_pre, axis=-1)

    pr = []
    for s in range(nseg):
        pr.append([])
        for hd in range(C_HEADS):
            yield 1
            p = jnp.exp(sc[s][hd] - jnp.max(sc[s][hd], axis=-1, keepdims=True))
            pr[s].append(p / jnp.sum(p, axis=-1, keepdims=True))
    oc = cat_rows([jnp.concatenate([_dot(pr[s][hd], mem[s][hd][1]) for hd in range(C_HEADS)], axis=-1)
                   for s in range(nseg)])

    yield 1
    tt = lax.broadcasted_iota(jnp.int32, (rows, rows), 0)
    ss = lax.broadcasted_iota(jnp.int32, (rows, rows), 1)
    shift = chunk.bit_length() - 1
    causal = ((tt >> shift) == (ss >> shift)) & (ss <= tt)
    o_intra = [_dot(jnp.where(causal, scores[hd], 0.0), v_all[:, sl]) for hd, sl in enumerate(head_sl)]
    yield 1
    for hd, sl in enumerate(head_sl):
        for s in range(nseg):
            for c, rs in enumerate(chunk_rows[s]):
                oa_ref[rs, sl] = o_intra[hd][rs, :] + _dot_nt(qg[s * nchunk + c][:, sl], st_in[s][hd][c])

    neg_lam = -w["lru_lambda"][...]
    softplus = jnp.maximum(neg_lam, 0.0) + jnp.log1p(jnp.exp(-jnp.abs(neg_lam)))
    a_l, u_l = [], []
    for s in range(nseg):
        for c, rs in enumerate(chunk_rows[s]):
            yield 2
            r = jax.nn.sigmoid(r_pre[rs, :] + w["b_lru_r"][...])
            ig = jax.nn.sigmoid(i_pre[rs, :] + w["b_lru_i"][...])
            log_a = -LRU_C * r * softplus
            a = jnp.exp(log_a)
            mult = jnp.sqrt(-jnp.tanh(log_a) * (a * a + 1.0))
            if first_rows_start is not None and c == 0:
                first_row = lax.broadcasted_iota(jnp.int32, mult.shape, 0) == 0
                mult = jnp.where(first_row & first_rows_start, 1.0, mult)
            a_l.append(a)
            u_l.append(mult * ig * xc[s * nchunk + c])

    g_a = w["g_a_out"][...]
    a_in = []
    for i, rs in enumerate(all_chunks):
        yield i % 2
        normed = jnp.concatenate([_rms(oa_ref[rs, sl], g_a[:, sl]) for sl in head_sl], axis=-1)
        a_in.append(normed * _silu(z.cols(_GA, A_WIDTH, rs)))
    pa = _dot(cat_rows(a_in), _wt(w["w_a_down"][...]))

    yield 1
    pc = _dot(oc * _silu(z.cols(_GC, C_WIDTH)), _wt(w["w_c_down"][...]))

    hb_gated, new_hl = [], []
    for s in range(nseg):
        carry = states[s][2]
        for c, rs in enumerate(chunk_rows[s]):
            yield 1
            i = s * nchunk + c
            hb, carry = _linear_scan_rows(a_l[i], u_l[i], carry)
            hb_gated.append(hb * _silu(z.cols(_GB, B_WIDTH, rs)))
        new_hl.append(carry)
    pb = _dot(cat_rows(hb_gated), _wt(w["w_b_down"][...]))

    merged = []
    for rs in all_chunks:
        yield 1
        merged.append(jax.nn.sigmoid(z.cols(_ZA, D_MODEL, rs)) * pa[rs, :]
                      + jax.nn.sigmoid(z.cols(_ZB, D_MODEL, rs)) * pb[rs, :]
                      + jax.nn.sigmoid(z.cols(_ZC, D_MODEL, rs)) * pc[rs, :])
    y = x + _dot(cat_rows(merged), _wt(w["w_out"][...]))
    y = _rms(y, w["g_final"][...])
    new_states = [(new_st[s], new_ctx[s], new_hl[s]) for s in range(nseg)]
    return y, new_states


_WEIGHT_NAMES = ("g_mix", "w_in", "lb_logits", "g_a_out", "w_a_down", "w_conv", "b_conv", "w_lru_r", "b_lru_r",
                 "w_lru_i", "b_lru_i", "lru_lambda", "w_b_down", "w_c_down", "w_out", "g_final")
_NW = len(_WEIGHT_NAMES)


def _store_seq_state(conv_ref, lru_ref, seq_idx, ctx, hl):
    for r in range(CTX_ROWS):
        conv_ref[r, pl.ds(seq_idx, 1), :] = ctx[r:r + 1, :]
    lru_ref[pl.ds(seq_idx, 1), :] = hl


def _prompt_kernel(steps_per_seq, *refs):
    x_ref, xn_ref, mk_ref, mv_ref = refs[:4]
    w = dict(zip(_WEIGHT_NAMES, refs[4:4 + _NW]))
    y_ref, hgrn_ref, conv_ref, lru_ref = refs[4 + _NW:8 + _NW]
    scratch = refs[8 + _NW:]
    st_ref, ctx_ref, hl_ref = scratch[:3]
    z_even, z_odd = _ZBuf(scratch[3:3 + Z_BLOCKS]), _ZBuf(scratch[3 + Z_BLOCKS:3 + 2 * Z_BLOCKS])
    oa_refs = scratch[3 + 2 * Z_BLOCKS:5 + 2 * Z_BLOCKS]
    xpad_refs = scratch[5 + 2 * Z_BLOCKS:7 + 2 * Z_BLOCKS]
    j = pl.program_id(0)
    tile = PROMPT_TILE
    seq_start = (j % steps_per_seq) == 0

    @pl.when(j == 0)
    def _():
        _project_in(x_ref[0:tile, :], z_even, w)

    @pl.when(seq_start)
    def _():
        st_ref[...] = jnp.zeros_like(st_ref)
        ctx_ref[...] = jnp.zeros_like(ctx_ref)
        hl_ref[...] = jnp.zeros_like(hl_ref)

    kv = lambda s, hd: (mk_ref[0, _head_rows(hd), :], mv_ref[0, _head_rows(hd), :])
    states = [([st_ref[hd] for hd in range(A_HEADS)], ctx_ref[...], hl_ref[...])]

    y, states = _interleave(
        _mix_stages(x_ref[0:tile, :], z_even, oa_refs[0], xpad_refs[0], kv, states, w,
                    seg=tile, chunk=HGRN_CHUNK, first_rows_start=seq_start),
        _project_in_blocks(x_ref[tile:2 * tile, :], z_odd, w))
    y_ref[0:tile, :] = y

    y, states = _interleave(
        _mix_stages(x_ref[tile:2 * tile, :], z_odd, oa_refs[1], xpad_refs[1], kv, states, w,
                    seg=tile, chunk=HGRN_CHUNK, first_rows_start=None),
        _project_in_blocks(xn_ref[...], z_even, w))
    y_ref[tile:2 * tile, :] = y

    st, ctx, hl = states[0]
    for hd in range(A_HEADS):
        st_ref[hd] = st[hd]
    ctx_ref[...] = ctx
    hl_ref[...] = hl

    @pl.when((j % steps_per_seq) == steps_per_seq - 1)
    def _():
        for hd in range(A_HEADS):
            hgrn_ref[0, hd] = st[hd].T
        _store_seq_state(conv_ref, lru_ref, j // steps_per_seq, ctx, hl)


def _sample_kernel(nseq, seg, *refs):
    x_ref, mk_ref, mv_ref, hgrn_in, conv_in, lru_in = refs[:6]
    w = dict(zip(_WEIGHT_NAMES, refs[6:6 + _NW]))
    y_ref, hgrn_ref, conv_ref, lru_ref = refs[6 + _NW:10 + _NW]
    scratch = refs[10 + _NW:]
    z = _ZBuf(scratch[:Z_BLOCKS])
    oa_ref, xpad_ref = scratch[Z_BLOCKS:]
    states = [([hgrn_in[s, hd].T for hd in range(A_HEADS)],
               jnp.concatenate([conv_in[r, s:s + 1, :] for r in range(CTX_ROWS)], axis=0),
               lru_in[s:s + 1, :]) for s in range(nseq)]
    _project_in(x_ref[...], z, w)
    y, new_states = _mix(
        x_ref[...], z, oa_ref, xpad_ref,
        lambda s, hd: (mk_ref[s, _head_rows(hd), :], mv_ref[s, _head_rows(hd), :]), states, w,
        seg=seg, chunk=min(HGRN_CHUNK, seg), first_rows_start=None)
    y_ref[...] = y
    for s in range(nseq):
        st, ctx, hl = new_states[s]
        for hd in range(A_HEADS):
            hgrn_ref[s, hd] = st[hd].T
        _store_seq_state(conv_ref, lru_ref, s, ctx, hl)


def _memkv_kernel(mem_ref, g_ref, wk_ref, wv_ref, k_ref, v_ref):
    hm = _rms(mem_ref[0], g_ref[...]).astype(bf16)
    k = jnp.dot(hm, _wt(wk_ref[...]), preferred_element_type=f32)
    v = jnp.dot(hm, _wt(wv_ref[...]), preferred_element_type=f32)
    for hd in range(C_HEADS):
        sl = slice(hd * HEAD_DIM, (hd + 1) * HEAD_DIM)
        k_ref[0, _head_rows(hd), :] = k[:, sl]
        v_ref[0, _head_rows(hd), :] = v[:, sl]


def _const_spec(shape):
    nd = len(shape)
    return pl.BlockSpec(shape, lambda *_: (0,) * nd, pipeline_mode=pl.Buffered(1))


PREP_STEPS = 8
_DENSE_WEIGHTS = ("w_in", "w_a_down", "w_b_down", "w_c_down", "w_out", "w_mem_k", "w_mem_v")
_U32 = jnp.uint32


def _prep_kernel(*refs):
    nd = len(_DENSE_WEIGHTS)
    dense_in, (lru_r_in, lru_i_in) = refs[:nd], refs[nd:nd + 2]
    dense_out, (lru_r_out, lru_i_out) = refs[nd + 2:2 * nd + 2], refs[2 * nd + 2:2 * nd + 4]
    tile_ref = refs[2 * nd + 4]
    for src, dst in zip(dense_in, dense_out):
        dst[...] = pltpu.bitcast(src[...].astype(bf16), _U32)

    @pl.when(pl.program_id(0) == 0)
    def _():
        per = MXU_TILE // B_BLOCK_DIM
        for src, dst in ((lru_r_in, lru_r_out), (lru_i_in, lru_i_out)):
            for g in range(LRU_GROUPS):
                tile_ref[...] = jnp.zeros_like(tile_ref)
                for p in range(per):
                    lo = p * B_BLOCK_DIM
                    tile_ref[lo:lo + B_BLOCK_DIM, lo:lo + B_BLOCK_DIM] = src[g * per + p]
                dst[g] = pltpu.bitcast(tile_ref[...].astype(bf16), _U32)


def _prep_weights(dense, lru_r, lru_i):
    in_specs, out_specs, out_shape = [], [], []
    for wm in dense:
        k, n = wm.shape
        assert k % (4 * SUBLANES * PREP_STEPS) == 0
        in_specs.append(pl.BlockSpec((k // PREP_STEPS, n), lambda i: (i, 0)))
        out_specs.append(pl.BlockSpec((k // (2 * PREP_STEPS), n), lambda i: (i, 0)))
        out_shape.append(jax.ShapeDtypeStruct((k // 2, n), _U32))
    blk = (B_BLOCKS, B_BLOCK_DIM, B_BLOCK_DIM)
    tiles = (LRU_GROUPS, MXU_TILE // 2, MXU_TILE)
    in_specs += [pl.BlockSpec(blk, lambda i: (0, 0, 0))] * 2
    out_specs += [pl.BlockSpec(tiles, lambda i: (0, 0, 0))] * 2
    out_shape += [jax.ShapeDtypeStruct(tiles, _U32)] * 2
    outs = pl.pallas_call(
        _prep_kernel, grid=(PREP_STEPS,), in_specs=in_specs, out_specs=out_specs, out_shape=out_shape,
        scratch_shapes=[pltpu.VMEM((MXU_TILE, MXU_TILE), f32)],
        compiler_params=pltpu.CompilerParams(vmem_limit_bytes=VMEM_LIMIT_BYTES,
                                             dimension_semantics=("arbitrary",)),
        name="prep_weights",
    )(*dense, lru_r, lru_i)
    return outs[:len(dense)], outs[len(dense)], outs[len(dense) + 1]


def kernel(x_prompt, x_sample, mem_prompt, cache_mem_k, cache_mem_v, state_hgrn, state_conv, state_lru, g_mix, w_in, lb_logits, g_a_out, w_a_down, w_conv, b_conv, w_lru_r, b_lru_r, w_lru_i, b_lru_i, lru_lambda, w_b_down, g_mem, w_mem_k, w_mem_v, w_c_down, w_out, g_final):
    bsz, seq, _ = x_prompt.shape
    dec_b, dec_seq, _ = x_sample.shape
    assert g_mix.shape[0] == 1, "single-layer stack only"
    assert seq % (2 * PROMPT_TILE) == 0 and PROMPT_TILE % HGRN_CHUNK == 0

    row = lambda a: a.reshape(1, -1).astype(f32)
    dense = dict(w_in=w_in[0], w_a_down=w_a_down[0], w_b_down=w_b_down[0], w_c_down=w_c_down[0], w_out=w_out[0],
                 w_mem_k=w_mem_k[0], w_mem_v=w_mem_v[0])
    packed, lru_r_tiles, lru_i_tiles = _prep_weights([dense[n] for n in _DENSE_WEIGHTS], w_lru_r[0], w_lru_i[0])
    packed = dict(zip(_DENSE_WEIGHTS, packed))
    weights = dict(
        g_mix=row(g_mix[0]), w_in=packed["w_in"], lb_logits=lb_logits.astype(f32), g_a_out=row(g_a_out[0]),
        w_a_down=packed["w_a_down"], w_conv=w_conv[0].astype(f32), b_conv=row(b_conv[0]),
        w_lru_r=lru_r_tiles, b_lru_r=row(b_lru_r[0]), w_lru_i=lru_i_tiles, b_lru_i=row(b_lru_i[0]),
        lru_lambda=row(lru_lambda[0]), w_b_down=packed["w_b_down"], w_c_down=packed["w_c_down"],
        w_out=packed["w_out"], g_final=row(g_final))
    wlist = [weights[n] for n in _WEIGHT_NAMES]
    wspecs = [_const_spec(a.shape) for a in wlist]
    params = pltpu.CompilerParams(vmem_limit_bytes=VMEM_LIMIT_BYTES)

    mk, mv = pl.pallas_call(
        _memkv_kernel,
        grid=(bsz,),
        in_specs=[pl.BlockSpec((1, N_MEM, D_MODEL), lambda b: (b, 0, 0)),
                  _const_spec((1, D_MODEL)), _const_spec((D_MODEL // 2, C_WIDTH)),
                  _const_spec((D_MODEL // 2, C_WIDTH))],
        out_specs=[pl.BlockSpec((1,) + KV_ROWS, lambda b: (b, 0, 0))] * 2,
        out_shape=[jax.ShapeDtypeStruct((bsz,) + KV_ROWS, f32)] * 2,
        name="mem_kv",
    )(mem_prompt, row(g_mem[0]), packed["w_mem_k"], packed["w_mem_v"])

    tile = PROMPT_TILE
    n_tiles = bsz * seq // tile
    steps_per_seq = seq // (2 * tile)
    zbuf = [pltpu.VMEM((tile, Z_BLK), f32)] * Z_BLOCKS
    y_p, hgrn_p, conv_p, lru_p = pl.pallas_call(
        functools.partial(_prompt_kernel, steps_per_seq),
        grid=(n_tiles // 2,),
        in_specs=[pl.BlockSpec((2 * tile, D_MODEL), lambda j: (j, 0)),
                  pl.BlockSpec((tile, D_MODEL), lambda j: (jnp.minimum(2 * j + 2, n_tiles - 1), 0)),
                  pl.BlockSpec((1,) + KV_ROWS, lambda j: (j // steps_per_seq, 0, 0)),
                  pl.BlockSpec((1,) + KV_ROWS, lambda j: (j // steps_per_seq, 0, 0))] + wspecs,
        out_specs=[pl.BlockSpec((2 * tile, D_MODEL), lambda j: (j, 0)),
                   pl.BlockSpec((1, A_HEADS, HEAD_DIM, HEAD_DIM), lambda j: (j // steps_per_seq, 0, 0, 0)),
                   pl.BlockSpec((CTX_ROWS, bsz, B_WIDTH), lambda j: (0, 0, 0)),
                   pl.BlockSpec((bsz, B_WIDTH), lambda j: (0, 0))],
        out_shape=[jax.ShapeDtypeStruct((bsz * seq, D_MODEL), f32),
                   jax.ShapeDtypeStruct((bsz, A_HEADS, HEAD_DIM, HEAD_DIM), f32),
                   jax.ShapeDtypeStruct((CTX_ROWS, bsz, B_WIDTH), f32),
                   jax.ShapeDtypeStruct((bsz, B_WIDTH), f32)],
        scratch_shapes=[pltpu.VMEM((A_HEADS, HEAD_DIM, HEAD_DIM), f32),
                        pltpu.VMEM((CTX_ROWS, B_WIDTH), f32),
                        pltpu.VMEM((1, B_WIDTH), f32)] + zbuf + zbuf
                       + [pltpu.VMEM((tile, A_WIDTH), f32)] * 2
                       + [pltpu.VMEM((SUBLANES + tile, B_WIDTH), f32)] * 2,
        compiler_params=pltpu.CompilerParams(vmem_limit_bytes=VMEM_LIMIT_BYTES,
                                             dimension_semantics=("arbitrary",)),
        name="prompt_layer",
    )(x_prompt.reshape(bsz * seq, D_MODEL), x_prompt.reshape(bsz * seq, D_MODEL), mk, mv, *wlist)
    y_p = y_p.reshape(bsz, seq, D_MODEL)

    rows = dec_b * dec_seq
    full = lambda shape: pl.BlockSpec(shape, lambda *_: (0,) * len(shape))
    y_s, hgrn_s, conv_s, lru_s = pl.pallas_call(
        functools.partial(_sample_kernel, dec_b, dec_seq),
        grid=(1,),
        in_specs=[full((rows, D_MODEL)), full((dec_b,) + KV_ROWS), full((dec_b,) + KV_ROWS),
                  full((dec_b, A_HEADS, HEAD_DIM, HEAD_DIM)), full((CTX_ROWS, dec_b, B_WIDTH)),
                  full((dec_b, B_WIDTH))] + wspecs,
        out_specs=[full((rows, D_MODEL)), full((dec_b, A_HEADS, HEAD_DIM, HEAD_DIM)),
                   full((CTX_ROWS, dec_b, B_WIDTH)), full((dec_b, B_WIDTH))],
        out_shape=[jax.ShapeDtypeStruct((rows, D_MODEL), f32),
                   jax.ShapeDtypeStruct((dec_b, A_HEADS, HEAD_DIM, HEAD_DIM), f32),
                   jax.ShapeDtypeStruct((CTX_ROWS, dec_b, B_WIDTH), f32),
                   jax.ShapeDtypeStruct((dec_b, B_WIDTH), f32)],
        scratch_shapes=[pltpu.VMEM((rows, Z_BLK), f32)] * Z_BLOCKS
                       + [pltpu.VMEM((rows, A_WIDTH), f32),
                        pltpu.VMEM((dec_b * (SUBLANES + dec_seq), B_WIDTH), f32)],
        compiler_params=params,
        name="sample_layer",
    )(x_sample.reshape(rows, D_MODEL), cache_mem_k.reshape((dec_b,) + KV_ROWS),
      cache_mem_v.reshape((dec_b,) + KV_ROWS), state_hgrn[0], jnp.swapaxes(state_conv[0], 0, 1),
      state_lru[0], *wlist)

    return (y_p, y_s.reshape(dec_b, dec_seq, D_MODEL), hgrn_p[None], jnp.swapaxes(conv_p, 0, 1)[None],
            lru_p[None], mk.reshape(1, bsz, N_MEM, C_HEADS, HEAD_DIM),
            mv.reshape(1, bsz, N_MEM, C_HEADS, HEAD_DIM), hgrn_s[None], jnp.swapaxes(conv_s, 0, 1)[None],
            lru_s[None])
```

```python
import functools

import jax
import jax.numpy as jnp
from jax import lax
from jax.experimental import pallas as pl
from jax.experimental.pallas import tpu as pltpu

f32 = jnp.float32
bf16 = jnp.bfloat16

D_MODEL = 1024
N_MEM = 256
EPS = 1e-6
A_HEADS = 4
HEAD_DIM = 128
A_WIDTH = A_HEADS * HEAD_DIM
B_WIDTH = D_MODEL
B_BLOCKS = 16
B_BLOCK_DIM = B_WIDTH // B_BLOCKS
CONV_W = 4
LRU_C = 8.0
C_HEADS = 4
C_WIDTH = C_HEADS * HEAD_DIM
assert A_HEADS == C_HEADS
HGRN_CHUNK = 64
IN_COLS = 4 * A_WIDTH + 2 * B_WIDTH + 2 * C_WIDTH + 3 * D_MODEL

_QA, _FA, _VA, _GA = 0, A_WIDTH, 2 * A_WIDTH, 3 * A_WIDTH
_XB = 4 * A_WIDTH
_GB = _XB + B_WIDTH
_QC = _GB + B_WIDTH
_GC = _QC + C_WIDTH
_ZA = _GC + C_WIDTH
_ZB = _ZA + D_MODEL
_ZC = _ZB + D_MODEL

MXU_TILE = 256
LRU_GROUPS = B_WIDTH // MXU_TILE
SUBLANES = 8
CTX_ROWS = CONV_W - 1

PROMPT_TILE = 256
VEC_ROWS = 32
Z_BLK = 1024
Z_BLOCKS = IN_COLS // Z_BLK
VMEM_LIMIT_BYTES = 60 * 1024 * 1024


def _rms(x, g):
    return x * lax.rsqrt(jnp.mean(x * x, axis=-1, keepdims=True) + EPS) * g


def _wt(ref_or_val):
    return pltpu.bitcast(ref_or_val, bf16)


def _dot(a, b):
    return jnp.dot(a.astype(bf16), b.astype(bf16), preferred_element_type=f32)


def _dot_nt(a, b):
    return lax.dot_general(a.astype(bf16), b.astype(bf16), (((1,), (1,)), ((), ())),
                           preferred_element_type=f32)


def _dot_tn(a, b):
    return lax.dot_general(a.astype(bf16), b.astype(bf16), (((0,), (0,)), ((), ())),
                           preferred_element_type=f32)


def _silu(x):
    return x * jax.nn.sigmoid(x)


KV_ROWS = (N_MEM * C_HEADS, HEAD_DIM)


def _head_rows(hd):
    return pl.ds(hd, N_MEM, stride=C_HEADS)


def _vreg_groups(x):
    rows, width = x.shape
    return x.reshape(rows // SUBLANES, SUBLANES, width)


def _cumsum_rows(x):
    rows = x.shape[0]
    x3 = _vreg_groups(x)
    sub = lax.broadcasted_iota(jnp.int32, x3.shape, 1)
    d = 1
    while d < SUBLANES:
        x3 = x3 + jnp.where(sub >= d, pltpu.roll(x3, d, 1), 0.0)
        d *= 2
    out, carry = [], None
    for g in range(rows // SUBLANES):
        cur = x3[g] if carry is None else x3[g] + carry
        carry = cur[SUBLANES - 1:SUBLANES, :]
        out.append(cur)
    return jnp.concatenate(out, axis=0)


def _linear_scan_rows(a, u, carry):
    rows = a.shape[0]
    a3, u3 = _vreg_groups(a), _vreg_groups(u)
    sub = lax.broadcasted_iota(jnp.int32, a3.shape, 1)
    d = 1
    while d < SUBLANES:
        keep = sub >= d
        u3 = a3 * jnp.where(keep, pltpu.roll(u3, d, 1), 0.0) + u3
        a3 = a3 * jnp.where(keep, pltpu.roll(a3, d, 1), 1.0)
        d *= 2
    out = []
    for g in range(rows // SUBLANES):
        cur = u3[g] + a3[g] * carry
        carry = cur[SUBLANES - 1:SUBLANES, :]
        out.append(cur)
    return jnp.concatenate(out, axis=0), carry


class _ZBuf:
    def __init__(self, refs):
        self.refs = refs

    def cols(self, c0, width, rows=slice(None)):
        blk, off = divmod(c0, Z_BLK)
        assert off + width <= Z_BLK
        return self.refs[blk][rows, off:off + width]


def _project_in_blocks(x, z, w):
    h = _rms(x, w["g_mix"][...]).astype(bf16)

    def block(c0):
        blk, off = divmod(c0, Z_BLK)
        z.refs[blk][:, off:off + MXU_TILE] = jnp.dot(h, _wt(w["w_in"][:, c0:c0 + MXU_TILE]),
                                                     preferred_element_type=f32)

    return [functools.partial(block, c0) for c0 in range(0, IN_COLS, MXU_TILE)]


def _project_in(x, z, w):
    for block in _project_in_blocks(x, z, w):
        block()


def _interleave(stages, blocks):
    blocks = list(blocks)
    while True:
        try:
            n = next(stages)
        except StopIteration as done:
            result = done.value
            break
        for _ in range(min(n, len(blocks))):
            blocks.pop(0)()
    for block in blocks:
        block()
    return result


def _mix(*args, **kwargs):
    return _interleave(_mix_stages(*args, **kwargs), [])


def _mix_stages(x, z, oa_ref, xpad_ref, kv, states, w, *, seg, chunk, first_rows_start):
    rows = x.shape[0]
    nseg = rows // seg
    nchunk = seg // chunk
    chunk_rows = [[slice(s * seg + c * chunk, s * seg + (c + 1) * chunk) for c in range(nchunk)]
                  for s in range(nseg)]
    all_chunks = [rs for per_seg in chunk_rows for rs in per_seg]
    vrows = min(VEC_ROWS, seg)
    nvec = seg // vrows
    vec_rows = [[slice(s * seg + c * vrows, s * seg + (c + 1) * vrows) for c in range(nvec)] for s in range(nseg)]
    all_vec = [rs for per_seg in vec_rows for rs in per_seg]
    head_sl = [slice(hd * HEAD_DIM, (hd + 1) * HEAD_DIM) for hd in range(A_HEADS)]
    cat_rows = lambda parts: parts[0] if len(parts) == 1 else jnp.concatenate(parts, axis=0)

    lg = w["lb_logits"][...]
    l0, l1 = lg[0:1, :], lg[1:2, :]
    lmax = jnp.maximum(l0, l1)
    e0, e1 = jnp.exp(l0 - lmax), jnp.exp(l1 - lmax)
    lb = e0 / (e0 + e1)

    qg, kg, v, kd, decay = [], [], [], [], []
    for rs in all_chunks:
        yield 1
        f = lb + (1.0 - lb) * jax.nn.sigmoid(z.cols(_FA, A_WIDTH, rs))
        b = _cumsum_rows(jnp.log(f))
        k = 1.0 - f
        qg.append(_silu(z.cols(_QA, A_WIDTH, rs)) * jnp.exp(b))
        kg.append(k * jnp.exp(-b))
        v.append(z.cols(_VA, A_WIDTH, rs))
        b_last = b[chunk - 1:chunk, :]
        kd.append(k * jnp.exp(b_last - b))
        decay.append(jnp.exp(b_last))
    qg_all, kg_all, v_all = cat_rows(qg), cat_rows(kg), cat_rows(v)

    scores = [_dot_nt(qg_all[:, sl], kg_all[:, sl]) for sl in head_sl]
    st_in = [[[states[s][0][hd]] for hd in range(A_HEADS)] for s in range(nseg)]
    for s in range(nseg):
        for c in range(nchunk):
            i = s * nchunk + c
            for hd, sl in enumerate(head_sl):
                st_in[s][hd].append(st_in[s][hd][c] * decay[i][:, sl] + _dot_tn(v[i][:, sl], kd[i][:, sl]))
    new_st = [[st_in[s][hd][nchunk] for hd in range(A_HEADS)] for s in range(nseg)]
    yield 1
    tt = lax.broadcasted_iota(jnp.int32, (rows, rows), 0)
    ss = lax.broadcasted_iota(jnp.int32, (rows, rows), 1)
    shift = chunk.bit_length() - 1
    causal = ((tt >> shift) == (ss >> shift)) & (ss <= tt)
    o_intra = [_dot(jnp.where(causal, scores[hd], 0.0), v_all[:, sl]) for hd, sl in enumerate(head_sl)]
    yield 1
    for hd, sl in enumerate(head_sl):
        for s in range(nseg):
            for c, rs in enumerate(chunk_rows[s]):
                oa_ref[rs, sl] = o_intra[hd][rs, :] + _dot_nt(qg[s * nchunk + c][:, sl], st_in[s][hd][c])

    g_a = w["g_a_out"][...]
    a_in = []
    for i, rs in enumerate(all_vec):
        yield int(i % 4 == 1)
        normed = jnp.concatenate([_rms(oa_ref[rs, sl], g_a[:, sl]) for sl in head_sl], axis=-1)
        a_in.append(normed * _silu(z.cols(_GA, A_WIDTH, rs)))
    pa = _dot(cat_rows(a_in), _wt(w["w_a_down"][...]))

    pad = SUBLANES
    stride = pad + seg
    for s in range(nseg):
        base = s * stride
        xpad_ref[base + pad - CTX_ROWS:base + pad, :] = states[s][1]
        xpad_ref[base + pad:base + pad + seg, :] = z.cols(_XB, B_WIDTH, slice(s * seg, (s + 1) * seg))
    w_conv = w["w_conv"][...]
    xc, new_ctx = [], []
    for s in range(nseg):
        base = s * stride + pad - CTX_ROWS
        for c in range(nvec):
            yield c % 2
            r0 = base + c * vrows
            acc = w_conv[0:1, :] * xpad_ref[r0:r0 + vrows, :]
            for j in range(1, CONV_W):
                acc = acc + w_conv[j:j + 1, :] * xpad_ref[r0 + j:r0 + j + vrows, :]
            xc.append(w["b_conv"][...] + acc)
        new_ctx.append(xpad_ref[base + seg:base + seg + CTX_ROWS, :])

    xc_b = cat_rows(xc).astype(bf16)
    r_pre, i_pre = [], []
    for g in range(LRU_GROUPS):
        gs = slice(g * MXU_TILE, (g + 1) * MXU_TILE)
        r_pre.append(jnp.dot(xc_b[:, gs], _wt(w["w_lru_r"][g]), preferred_element_type=f32))
        i_pre.append(jnp.dot(xc_b[:, gs], _wt(w["w_lru_i"][g]), preferred_element_type=f32))
    r_pre, i_pre = jnp.concatenate(r_pre, axis=-1), jnp.concatenate(i_pre, axis=-1)
    neg_lam = -w["lru_lambda"][...]
    softplus = jnp.maximum(neg_lam, 0.0) + jnp.log1p(jnp.exp(-jnp.abs(neg_lam)))
    a_l, u_l = [], []
    for s in range(nseg):
        for c, rs in enumerate(vec_rows[s]):
            yield 1
            r = jax.nn.sigmoid(r_pre[rs, :] + w["b_lru_r"][...])
            ig = jax.nn.sigmoid(i_pre[rs, :] + w["b_lru_i"][...])
            log_a = -LRU_C * r * softplus
            a = jnp.exp(log_a)
            mult = jnp.sqrt(-jnp.tanh(log_a) * (a * a + 1.0))
            if first_rows_start is not None and c == 0:
                first_row = lax.broadcasted_iota(jnp.int32, mult.shape, 0) == 0
                mult = jnp.where(first_row & first_rows_start, 1.0, mult)
            a_l.append(a)
            u_l.append(mult * ig * xc[s * nvec + c])
    hb_gated, new_hl = [], []
    for s in range(nseg):
        carry = states[s][2]
        for c, rs in enumerate(vec_rows[s]):
            yield c % 2
            i = s * nvec + c
            hb, carry = _linear_scan_rows(a_l[i], u_l[i], carry)
            hb_gated.append(hb * _silu(z.cols(_GB, B_WIDTH, rs)))
        new_hl.append(carry)
    pb = _dot(cat_rows(hb_gated), _wt(w["w_b_down"][...]))

    scale = HEAD_DIM ** -0.5
    mem = [[kv(s, hd) for hd in range(C_HEADS)] for s in range(nseg)]
    seg_rows = [slice(s * seg, (s + 1) * seg) for s in range(nseg)]
    sc = [[_dot_nt(z.cols(_QC + hd * HEAD_DIM, HEAD_DIM, seg_rows[s]), mem[s][hd][0]) * scale
           for hd in range(C_HEADS)] for s in range(nseg)]
    pr = []
    for s in range(nseg):
        pr.append([])
        for hd in range(C_HEADS):
            yield 1
            p = jnp.exp(sc[s][hd] - jnp.max(sc[s][hd], axis=-1, keepdims=True))
            pr[s].append(p / jnp.sum(p, axis=-1, keepdims=True))
    oc = cat_rows([jnp.concatenate([_dot(pr[s][hd], mem[s][hd][1]) for hd in range(C_HEADS)], axis=-1)
                   for s in range(nseg)])
    yield 1
    pc = _dot(oc * _silu(z.cols(_GC, C_WIDTH)), _wt(w["w_c_down"][...]))

    merged = []
    for i, rs in enumerate(all_vec):
        yield i % 2
        merged.append(jax.nn.sigmoid(z.cols(_ZA, D_MODEL, rs)) * pa[rs, :]
                      + jax.nn.sigmoid(z.cols(_ZB, D_MODEL, rs)) * pb[rs, :]
                      + jax.nn.sigmoid(z.cols(_ZC, D_MODEL, rs)) * pc[rs, :])
    y = x + _dot(cat_rows(merged), _wt(w["w_out"][...]))
    y = _rms(y, w["g_final"][...])
    new_states = [(new_st[s], new_ctx[s], new_hl[s]) for s in range(nseg)]
    return y, new_states


_WEIGHT_NAMES = ("g_mix", "w_in", "lb_logits", "g_a_out", "w_a_down", "w_conv", "b_conv", "w_lru_r", "b_lru_r",
                 "w_lru_i", "b_lru_i", "lru_lambda", "w_b_down", "w_c_down", "w_out", "g_final")
_NW = len(_WEIGHT_NAMES)


def _store_seq_state(conv_ref, lru_ref, seq_idx, ctx, hl):
    for r in range(CTX_ROWS):
        conv_ref[r, pl.ds(seq_idx, 1), :] = ctx[r:r + 1, :]
    lru_ref[pl.ds(seq_idx, 1), :] = hl


def _prompt_kernel(steps_per_seq, *refs):
    x_ref, xn_ref, mk_ref, mv_ref = refs[:4]
    w = dict(zip(_WEIGHT_NAMES, refs[4:4 + _NW]))
    y_ref, hgrn_ref, conv_ref, lru_ref = refs[4 + _NW:8 + _NW]
    scratch = refs[8 + _NW:]
    st_ref, ctx_ref, hl_ref = scratch[:3]
    z_even, z_odd = _ZBuf(scratch[3:3 + Z_BLOCKS]), _ZBuf(scratch[3 + Z_BLOCKS:3 + 2 * Z_BLOCKS])
    oa_refs = scratch[3 + 2 * Z_BLOCKS:5 + 2 * Z_BLOCKS]
    xpad_refs = scratch[5 + 2 * Z_BLOCKS:7 + 2 * Z_BLOCKS]
    j = pl.program_id(0)
    tile = PROMPT_TILE
    seq_start = (j % steps_per_seq) == 0

    @pl.when(j == 0)
    def _():
        _project_in(x_ref[0:tile, :], z_even, w)

    @pl.when(seq_start)
    def _():
        st_ref[...] = jnp.zeros_like(st_ref)
        ctx_ref[...] = jnp.zeros_like(ctx_ref)
        hl_ref[...] = jnp.zeros_like(hl_ref)

    kv = lambda s, hd: (mk_ref[0, _head_rows(hd), :], mv_ref[0, _head_rows(hd), :])
    states = [([st_ref[hd] for hd in range(A_HEADS)], ctx_ref[...], hl_ref[...])]

    y, states = _interleave(
        _mix_stages(x_ref[0:tile, :], z_even, oa_refs[0], xpad_refs[0], kv, states, w,
                    seg=tile, chunk=HGRN_CHUNK, first_rows_start=seq_start),
        _project_in_blocks(x_ref[tile:2 * tile, :], z_odd, w))
    y_ref[0:tile, :] = y

    y, states = _interleave(
        _mix_stages(x_ref[tile:2 * tile, :], z_odd, oa_refs[1], xpad_refs[1], kv, states, w,
                    seg=tile, chunk=HGRN_CHUNK, first_rows_start=None),
        _project_in_blocks(xn_ref[...], z_even, w))
    y_ref[tile:2 * tile, :] = y

    st, ctx, hl = states[0]
    for hd in range(A_HEADS):
        st_ref[hd] = st[hd]
    ctx_ref[...] = ctx
    hl_ref[...] = hl

    @pl.when((j % steps_per_seq) == steps_per_seq - 1)
    def _():
        for hd in range(A_HEADS):
            hgrn_ref[0, hd] = st[hd].T
        _store_seq_state(conv_ref, lru_ref, j // steps_per_seq, ctx, hl)


def _sample_kernel(nseq, seg, *refs):
    x_ref, mk_ref, mv_ref, hgrn_in, conv_in, lru_in = refs[:6]
    w = dict(zip(_WEIGHT_NAMES, refs[6:6 + _NW]))
    y_ref, hgrn_ref, conv_ref, lru_ref = refs[6 + _NW:10 + _NW]
    scratch = refs[10 + _NW:]
    z = _ZBuf(scratch[:Z_BLOCKS])
    oa_ref, xpad_ref = scratch[Z_BLOCKS:]
    states = [([hgrn_in[s, hd].T for hd in range(A_HEADS)],
               jnp.concatenate([conv_in[r, s:s + 1, :] for r in range(CTX_ROWS)], axis=0),
               lru_in[s:s + 1, :]) for s in range(nseq)]
    _project_in(x_ref[...], z, w)
    y, new_states = _mix(
        x_ref[...], z, oa_ref, xpad_ref,
        lambda s, hd: (mk_ref[s, _head_rows(hd), :], mv_ref[s, _head_rows(hd), :]), states, w,
        seg=seg, chunk=min(HGRN_CHUNK, seg), first_rows_start=None)
    y_ref[...] = y
    for s in range(nseq):
        st, ctx, hl = new_states[s]
        for hd in range(A_HEADS):
            hgrn_ref[s, hd] = st[hd].T
        _store_seq_state(conv_ref, lru_ref, s, ctx, hl)


def _memkv_kernel(mem_ref, g_ref, wk_ref, wv_ref, k_ref, v_ref):
    hm = _rms(mem_ref[0], g_ref[...]).astype(bf16)
    k = jnp.dot(hm, _wt(wk_ref[...]), preferred_element_type=f32)
    v = jnp.dot(hm, _wt(wv_ref[...]), preferred_element_type=f32)
    for hd in range(C_HEADS):
        sl = slice(hd * HEAD_DIM, (hd + 1) * HEAD_DIM)
        k_ref[0, _head_rows(hd), :] = k[:, sl]
        v_ref[0, _head_rows(hd), :] = v[:, sl]


def _const_spec(shape):
    nd = len(shape)
    return pl.BlockSpec(shape, lambda *_: (0,) * nd, pipeline_mode=pl.Buffered(1))


PREP_STEPS = 8
_DENSE_WEIGHTS = ("w_in", "w_a_down", "w_b_down", "w_c_down", "w_out", "w_mem_k", "w_mem_v")
_U32 = jnp.uint32


def _prep_kernel(*refs):
    nd = len(_DENSE_WEIGHTS)
    dense_in, (lru_r_in, lru_i_in) = refs[:nd], refs[nd:nd + 2]
    dense_out, (lru_r_out, lru_i_out) = refs[nd + 2:2 * nd + 2], refs[2 * nd + 2:2 * nd + 4]
    tile_ref = refs[2 * nd + 4]
    for src, dst in zip(dense_in, dense_out):
        dst[...] = pltpu.bitcast(src[...].astype(bf16), _U32)

    @pl.when(pl.program_id(0) == 0)
    def _():
        per = MXU_TILE // B_BLOCK_DIM
        for src, dst in ((lru_r_in, lru_r_out), (lru_i_in, lru_i_out)):
            for g in range(LRU_GROUPS):
                tile_ref[...] = jnp.zeros_like(tile_ref)
                for p in range(per):
                    lo = p * B_BLOCK_DIM
                    tile_ref[lo:lo + B_BLOCK_DIM, lo:lo + B_BLOCK_DIM] = src[g * per + p]
                dst[g] = pltpu.bitcast(tile_ref[...].astype(bf16), _U32)


def _prep_weights(dense, lru_r, lru_i):
    in_specs, out_specs, out_shape = [], [], []
    for wm in dense:
        k, n = wm.shape
        assert k % (4 * SUBLANES * PREP_STEPS) == 0
        in_specs.append(pl.BlockSpec((k // PREP_STEPS, n), lambda i: (i, 0)))
        out_specs.append(pl.BlockSpec((k // (2 * PREP_STEPS), n), lambda i: (i, 0)))
        out_shape.append(jax.ShapeDtypeStruct((k // 2, n), _U32))
    blk = (B_BLOCKS, B_BLOCK_DIM, B_BLOCK_DIM)
    tiles = (LRU_GROUPS, MXU_TILE // 2, MXU_TILE)
    in_specs += [pl.BlockSpec(blk, lambda i: (0, 0, 0))] * 2
    out_specs += [pl.BlockSpec(tiles, lambda i: (0, 0, 0))] * 2
    out_shape += [jax.ShapeDtypeStruct(tiles, _U32)] * 2
    outs = pl.pallas_call(
        _prep_kernel, grid=(PREP_STEPS,), in_specs=in_specs, out_specs=out_specs, out_shape=out_shape,
        scratch_shapes=[pltpu.VMEM((MXU_TILE, MXU_TILE), f32)],
        compiler_params=pltpu.CompilerParams(vmem_limit_bytes=VMEM_LIMIT_BYTES,
                                             dimension_semantics=("arbitrary",)),
        name="prep_weights",
    )(*dense, lru_r, lru_i)
    return outs[:len(dense)], outs[len(dense)], outs[len(dense) + 1]


def kernel(x_prompt, x_sample, mem_prompt, cache_mem_k, cache_mem_v, state_hgrn, state_conv, state_lru, g_mix, w_in, lb_logits, g_a_out, w_a_down, w_conv, b_conv, w_lru_r, b_lru_r, w_lru_i, b_lru_i, lru_lambda, w_b_down, g_mem, w_mem_k, w_mem_v, w_c_down, w_out, g_final):
    bsz, seq, _ = x_prompt.shape
    dec_b, dec_seq, _ = x_sample.shape
    assert g_mix.shape[0] == 1, "single-layer stack only"
    assert seq % (2 * PROMPT_TILE) == 0 and PROMPT_TILE % HGRN_CHUNK == 0

    row = lambda a: a.reshape(1, -1).astype(f32)
    dense = dict(w_in=w_in[0], w_a_down=w_a_down[0], w_b_down=w_b_down[0], w_c_down=w_c_down[0], w_out=w_out[0],
                 w_mem_k=w_mem_k[0], w_mem_v=w_mem_v[0])
    packed, lru_r_tiles, lru_i_tiles = _prep_weights([dense[n] for n in _DENSE_WEIGHTS], w_lru_r[0], w_lru_i[0])
    packed = dict(zip(_DENSE_WEIGHTS, packed))
    weights = dict(
        g_mix=row(g_mix[0]), w_in=packed["w_in"], lb_logits=lb_logits.astype(f32), g_a_out=row(g_a_out[0]),
        w_a_down=packed["w_a_down"], w_conv=w_conv[0].astype(f32), b_conv=row(b_conv[0]),
        w_lru_r=lru_r_tiles, b_lru_r=row(b_lru_r[0]), w_lru_i=lru_i_tiles, b_lru_i=row(b_lru_i[0]),
        lru_lambda=row(lru_lambda[0]), w_b_down=packed["w_b_down"], w_c_down=packed["w_c_down"],
        w_out=packed["w_out"], g_final=row(g_final))
    wlist = [weights[n] for n in _WEIGHT_NAMES]
    wspecs = [_const_spec(a.shape) for a in wlist]
    params = pltpu.CompilerParams(vmem_limit_bytes=VMEM_LIMIT_BYTES)

    mk, mv = pl.pallas_call(
        _memkv_kernel,
        grid=(bsz,),
        in_specs=[pl.BlockSpec((1, N_MEM, D_MODEL), lambda b: (b, 0, 0)),
                  _const_spec((1, D_MODEL)), _const_spec((D_MODEL // 2, C_WIDTH)),
                  _const_spec((D_MODEL // 2, C_WIDTH))],
        out_specs=[pl.BlockSpec((1,) + KV_ROWS, lambda b: (b, 0, 0))] * 2,
        out_shape=[jax.ShapeDtypeStruct((bsz,) + KV_ROWS, f32)] * 2,
        name="mem_kv",
    )(mem_prompt, row(g_mem[0]), packed["w_mem_k"], packed["w_mem_v"])

    tile = PROMPT_TILE
    n_tiles = bsz * seq // tile
    steps_per_seq = seq // (2 * tile)
    zbuf = [pltpu.VMEM((tile, Z_BLK), f32)] * Z_BLOCKS
    y_p, hgrn_p, conv_p, lru_p = pl.pallas_call(
        functools.partial(_prompt_kernel, steps_per_seq),
        grid=(n_tiles // 2,),
        in_specs=[pl.BlockSpec((2 * tile, D_MODEL), lambda j: (j, 0)),
                  pl.BlockSpec((tile, D_MODEL), lambda j: (jnp.minimum(2 * j + 2, n_tiles - 1), 0)),
                  pl.BlockSpec((1,) + KV_ROWS, lambda j: (j // steps_per_seq, 0, 0)),
                  pl.BlockSpec((1,) + KV_ROWS, lambda j: (j // steps_per_seq, 0, 0))] + wspecs,
        out_specs=[pl.BlockSpec((2 * tile, D_MODEL), lambda j: (j, 0)),
                   pl.BlockSpec((1, A_HEADS, HEAD_DIM, HEAD_DIM), lambda j: (j // steps_per_seq, 0, 0, 0)),
                   pl.BlockSpec((CTX_ROWS, bsz, B_WIDTH), lambda j: (0, 0, 0)),
                   pl.BlockSpec((bsz, B_WIDTH), lambda j: (0, 0))],
        out_shape=[jax.ShapeDtypeStruct((bsz * seq, D_MODEL), f32),
                   jax.ShapeDtypeStruct((bsz, A_HEADS, HEAD_DIM, HEAD_DIM), f32),
                   jax.ShapeDtypeStruct((CTX_ROWS, bsz, B_WIDTH), f32),
                   jax.ShapeDtypeStruct((bsz, B_WIDTH), f32)],
        scratch_shapes=[pltpu.VMEM((A_HEADS, HEAD_DIM, HEAD_DIM), f32),
                        pltpu.VMEM((CTX_ROWS, B_WIDTH), f32),
                        pltpu.VMEM((1, B_WIDTH), f32)] + zbuf + zbuf
                       + [pltpu.VMEM((tile, A_WIDTH), f32)] * 2
                       + [pltpu.VMEM((SUBLANES + tile, B_WIDTH), f32)] * 2,
        compiler_params=pltpu.CompilerParams(vmem_limit_bytes=VMEM_LIMIT_BYTES,
                                             dimension_semantics=("arbitrary",)),
        name="prompt_layer",
    )(x_prompt.reshape(bsz * seq, D_MODEL), x_prompt.reshape(bsz * seq, D_MODEL), mk, mv, *wlist)
    y_p = y_p.reshape(bsz, seq, D_MODEL)

    rows = dec_b * dec_seq
    full = lambda shape: pl.BlockSpec(shape, lambda *_: (0,) * len(shape))
    y_s, hgrn_s, conv_s, lru_s = pl.pallas_call(
        functools.partial(_sample_kernel, dec_b, dec_seq),
        grid=(1,),
        in_specs=[full((rows, D_MODEL)), full((dec_b,) + KV_ROWS), full((dec_b,) + KV_ROWS),
                  full((dec_b, A_HEADS, HEAD_DIM, HEAD_DIM)), full((CTX_ROWS, dec_b, B_WIDTH)),
                  full((dec_b, B_WIDTH))] + wspecs,
        out_specs=[full((rows, D_MODEL)), full((dec_b, A_HEADS, HEAD_DIM, HEAD_DIM)),
                   full((CTX_ROWS, dec_b, B_WIDTH)), full((dec_b, B_WIDTH))],
        out_shape=[jax.ShapeDtypeStruct((rows, D_MODEL), f32),
                   jax.ShapeDtypeStruct((dec_b, A_HEADS, HEAD_DIM, HEAD_DIM), f32),
                   jax.ShapeDtypeStruct((CTX_ROWS, dec_b, B_WIDTH), f32),
                   jax.ShapeDtypeStruct((dec_b, B_WIDTH), f32)],
        scratch_shapes=[pltpu.VMEM((rows, Z_BLK), f32)] * Z_BLOCKS
                       + [pltpu.VMEM((rows, A_WIDTH), f32),
                        pltpu.VMEM((dec_b * (SUBLANES + dec_seq), B_WIDTH), f32)],
        compiler_params=params,
        name="sample_layer",
    )(x_sample.reshape(rows, D_MODEL), cache_mem_k.reshape((dec_b,) + KV_ROWS),
      cache_mem_v.reshape((dec_b,) + KV_ROWS), state_hgrn[0], jnp.swapaxes(state_conv[0], 0, 1),
      state_lru[0], *wlist)

    return (y_p, y_s.reshape(dec_b, dec_seq, D_MODEL), hgrn_p[None], jnp.swapaxes(conv_p, 0, 1)[None],
            lru_p[None], mk.reshape(1, bsz, N_MEM, C_HEADS, HEAD_DIM),
            mv.reshape(1, bsz, N_MEM, C_HEADS, HEAD_DIM), hgrn_s[None], jnp.swapaxes(conv_s, 0, 1)[None],
            lru_s[None])
```

```python
import functools

import jax
import jax.numpy as jnp
from jax import lax
from jax.experimental import pallas as pl
from jax.experimental.pallas import tpu as pltpu

f32 = jnp.float32
bf16 = jnp.bfloat16

D_MODEL = 1024
N_MEM = 256
EPS = 1e-6
A_HEADS = 4
HEAD_DIM = 128
A_WIDTH = A_HEADS * HEAD_DIM
B_WIDTH = D_MODEL
B_BLOCKS = 16
B_BLOCK_DIM = B_WIDTH // B_BLOCKS
CONV_W = 4
LRU_C = 8.0
C_HEADS = 4
C_WIDTH = C_HEADS * HEAD_DIM
assert A_HEADS == C_HEADS
HGRN_CHUNK = 64
IN_COLS = 4 * A_WIDTH + 2 * B_WIDTH + 2 * C_WIDTH + 3 * D_MODEL

_QA, _FA, _VA, _GA = 0, A_WIDTH, 2 * A_WIDTH, 3 * A_WIDTH
_XB = 4 * A_WIDTH
_GB = _XB + B_WIDTH
_QC = _GB + B_WIDTH
_GC = _QC + C_WIDTH
_ZA = _GC + C_WIDTH
_ZB = _ZA + D_MODEL
_ZC = _ZB + D_MODEL

MXU_TILE = 256
LRU_GROUPS = B_WIDTH // MXU_TILE
SUBLANES = 8
CTX_ROWS = CONV_W - 1

PROMPT_TILE = 256
Z_BLK = 1024
Z_BLOCKS = IN_COLS // Z_BLK
VMEM_LIMIT_BYTES = 60 * 1024 * 1024


def _rms(x, g):
    return x * lax.rsqrt(jnp.mean(x * x, axis=-1, keepdims=True) + EPS) * g


def _wt(ref_or_val):
    return pltpu.bitcast(ref_or_val, bf16)


def _dot(a, b):
    return jnp.dot(a.astype(bf16), b.astype(bf16), preferred_element_type=f32)


def _dot_nt(a, b):
    return lax.dot_general(a.astype(bf16), b.astype(bf16), (((1,), (1,)), ((), ())),
                           preferred_element_type=f32)


def _dot_tn(a, b):
    return lax.dot_general(a.astype(bf16), b.astype(bf16), (((0,), (0,)), ((), ())),
                           preferred_element_type=f32)


def _silu(x):
    return x * jax.nn.sigmoid(x)


KV_ROWS = (N_MEM * C_HEADS, HEAD_DIM)


def _head_rows(hd):
    return pl.ds(hd, N_MEM, stride=C_HEADS)


def _vreg_groups(x):
    rows, width = x.shape
    return x.reshape(rows // SUBLANES, SUBLANES, width)


def _cumsum_rows(x):
    rows = x.shape[0]
    x3 = _vreg_groups(x)
    sub = lax.broadcasted_iota(jnp.int32, x3.shape, 1)
    d = 1
    while d < SUBLANES:
        x3 = x3 + jnp.where(sub >= d, pltpu.roll(x3, d, 1), 0.0)
        d *= 2
    out, carry = [], None
    for g in range(rows // SUBLANES):
        cur = x3[g] if carry is None else x3[g] + carry
        carry = cur[SUBLANES - 1:SUBLANES, :]
        out.append(cur)
    return jnp.concatenate(out, axis=0)


def _linear_scan_rows(a, u, carry):
    rows = a.shape[0]
    a3, u3 = _vreg_groups(a), _vreg_groups(u)
    sub = lax.broadcasted_iota(jnp.int32, a3.shape, 1)
    d = 1
    while d < SUBLANES:
        keep = sub >= d
        u3 = a3 * jnp.where(keep, pltpu.roll(u3, d, 1), 0.0) + u3
        a3 = a3 * jnp.where(keep, pltpu.roll(a3, d, 1), 1.0)
        d *= 2
    out = []
    for g in range(rows // SUBLANES):
        cur = u3[g] + a3[g] * carry
        carry = cur[SUBLANES - 1:SUBLANES, :]
        out.append(cur)
    return jnp.concatenate(out, axis=0), carry


class _ZBuf:
    def __init__(self, refs):
        self.refs = refs

    def cols(self, c0, width, rows=slice(None)):
        blk, off = divmod(c0, Z_BLK)
        assert off + width <= Z_BLK
        return self.refs[blk][rows, off:off + width]


def _project_in_blocks(x, z, w):
    h = _rms(x, w["g_mix"][...]).astype(bf16)

    def block(c0):
        blk, off = divmod(c0, Z_BLK)
        z.refs[blk][:, off:off + MXU_TILE] = jnp.dot(h, _wt(w["w_in"][:, c0:c0 + MXU_TILE]),
                                                     preferred_element_type=f32)

    return [functools.partial(block, c0) for c0 in range(0, IN_COLS, MXU_TILE)]


def _project_in(x, z, w):
    for block in _project_in_blocks(x, z, w):
        block()


def _interleave(stages, blocks):
    blocks = list(blocks)
    while True:
        try:
            n = next(stages)
        except StopIteration as done:
            result = done.value
            break
        for _ in range(min(n, len(blocks))):
            blocks.pop(0)()
    for block in blocks:
        block()
    return result


def _mix(*args, **kwargs):
    return _interleave(_mix_stages(*args, **kwargs), [])


def _mix_stages(x, z, oa_ref, xpad_ref, kv, states, w, *, seg, chunk, first_rows_start):
    rows = x.shape[0]
    nseg = rows // seg
    nchunk = seg // chunk
    chunk_rows = [[slice(s * seg + c * chunk, s * seg + (c + 1) * chunk) for c in range(nchunk)]
                  for s in range(nseg)]
    all_chunks = [rs for per_seg in chunk_rows for rs in per_seg]
    head_sl = [slice(hd * HEAD_DIM, (hd + 1) * HEAD_DIM) for hd in range(A_HEADS)]
    cat_rows = lambda parts: parts[0] if len(parts) == 1 else jnp.concatenate(parts, axis=0)

    lg = w["lb_logits"][...]
    l0, l1 = lg[0:1, :], lg[1:2, :]
    lmax = jnp.maximum(l0, l1)
    e0, e1 = jnp.exp(l0 - lmax), jnp.exp(l1 - lmax)
    lb = e0 / (e0 + e1)

    qg, kg, v, kd, decay = [], [], [], [], []
    for rs in all_chunks:
        yield 1
        f = lb + (1.0 - lb) * jax.nn.sigmoid(z.cols(_FA, A_WIDTH, rs))
        b = _cumsum_rows(jnp.log(f))
        k = 1.0 - f
        qg.append(_silu(z.cols(_QA, A_WIDTH, rs)) * jnp.exp(b))
        kg.append(k * jnp.exp(-b))
        v.append(z.cols(_VA, A_WIDTH, rs))
        b_last = b[chunk - 1:chunk, :]
        kd.append(k * jnp.exp(b_last - b))
        decay.append(jnp.exp(b_last))
    qg_all, kg_all, v_all = cat_rows(qg), cat_rows(kg), cat_rows(v)

    scores = [_dot_nt(qg_all[:, sl], kg_all[:, sl]) for sl in head_sl]
    st_in = [[[states[s][0][hd]] for hd in range(A_HEADS)] for s in range(nseg)]
    for s in range(nseg):
        for c in range(nchunk):
            i = s * nchunk + c
            for hd, sl in enumerate(head_sl):
                st_in[s][hd].append(st_in[s][hd][c] * decay[i][:, sl] + _dot_tn(v[i][:, sl], kd[i][:, sl]))
    new_st = [[st_in[s][hd][nchunk] for hd in range(A_HEADS)] for s in range(nseg)]
    yield 1
    tt = lax.broadcasted_iota(jnp.int32, (rows, rows), 0)
    ss = lax.broadcasted_iota(jnp.int32, (rows, rows), 1)
    shift = chunk.bit_length() - 1
    causal = ((tt >> shift) == (ss >> shift)) & (ss <= tt)
    o_intra = [_dot(jnp.where(causal, scores[hd], 0.0), v_all[:, sl]) for hd, sl in enumerate(head_sl)]
    yield 1
    for hd, sl in enumerate(head_sl):
        for s in range(nseg):
            for c, rs in enumerate(chunk_rows[s]):
                oa_ref[rs, sl] = o_intra[hd][rs, :] + _dot_nt(qg[s * nchunk + c][:, sl], st_in[s][hd][c])

    g_a = w["g_a_out"][...]
    a_in = []
    for i, rs in enumerate(all_chunks):
        yield i % 2
        normed = jnp.concatenate([_rms(oa_ref[rs, sl], g_a[:, sl]) for sl in head_sl], axis=-1)
        a_in.append(normed * _silu(z.cols(_GA, A_WIDTH, rs)))
    pa = _dot(cat_rows(a_in), _wt(w["w_a_down"][...]))

    pad = SUBLANES
    stride = pad + seg
    for s in range(nseg):
        base = s * stride
        xpad_ref[base + pad - CTX_ROWS:base + pad, :] = states[s][1]
        xpad_ref[base + pad:base + pad + seg, :] = z.cols(_XB, B_WIDTH, slice(s * seg, (s + 1) * seg))
    w_conv = w["w_conv"][...]
    xc, new_ctx = [], []
    for s in range(nseg):
        base = s * stride + pad - CTX_ROWS
        for c in range(nchunk):
            yield 1
            r0 = base + c * chunk
            acc = w_conv[0:1, :] * xpad_ref[r0:r0 + chunk, :]
            for j in range(1, CONV_W):
                acc = acc + w_conv[j:j + 1, :] * xpad_ref[r0 + j:r0 + j + chunk, :]
            xc.append(w["b_conv"][...] + acc)
        new_ctx.append(xpad_ref[base + seg:base + seg + CTX_ROWS, :])

    xc_b = cat_rows(xc).astype(bf16)
    r_pre, i_pre = [], []
    for g in range(LRU_GROUPS):
        gs = slice(g * MXU_TILE, (g + 1) * MXU_TILE)
        r_pre.append(jnp.dot(xc_b[:, gs], _wt(w["w_lru_r"][g]), preferred_element_type=f32))
        i_pre.append(jnp.dot(xc_b[:, gs], _wt(w["w_lru_i"][g]), preferred_element_type=f32))
    r_pre, i_pre = jnp.concatenate(r_pre, axis=-1), jnp.concatenate(i_pre, axis=-1)
    neg_lam = -w["lru_lambda"][...]
    softplus = jnp.maximum(neg_lam, 0.0) + jnp.log1p(jnp.exp(-jnp.abs(neg_lam)))
    a_l, u_l = [], []
    for s in range(nseg):
        for c, rs in enumerate(chunk_rows[s]):
            yield 2
            r = jax.nn.sigmoid(r_pre[rs, :] + w["b_lru_r"][...])
            ig = jax.nn.sigmoid(i_pre[rs, :] + w["b_lru_i"][...])
            log_a = -LRU_C * r * softplus
            a = jnp.exp(log_a)
            mult = jnp.sqrt(-jnp.tanh(log_a) * (a * a + 1.0))
            if first_rows_start is not None and c == 0:
                first_row = lax.broadcasted_iota(jnp.int32, mult.shape, 0) == 0
                mult = jnp.where(first_row & first_rows_start, 1.0, mult)
            a_l.append(a)
            u_l.append(mult * ig * xc[s * nchunk + c])
    hb_gated, new_hl = [], []
    for s in range(nseg):
        carry = states[s][2]
        for c, rs in enumerate(chunk_rows[s]):
            yield 1
            i = s * nchunk + c
            hb, carry = _linear_scan_rows(a_l[i], u_l[i], carry)
            hb_gated.append(hb * _silu(z.cols(_GB, B_WIDTH, rs)))
        new_hl.append(carry)
    pb = _dot(cat_rows(hb_gated), _wt(w["w_b_down"][...]))

    scale = HEAD_DIM ** -0.5
    mem = [[kv(s, hd) for hd in range(C_HEADS)] for s in range(nseg)]
    seg_rows = [slice(s * seg, (s + 1) * seg) for s in range(nseg)]
    sc = [[_dot_nt(z.cols(_QC + hd * HEAD_DIM, HEAD_DIM, seg_rows[s]), mem[s][hd][0]) * scale
           for hd in range(C_HEADS)] for s in range(nseg)]
    pr = []
    for s in range(nseg):
        pr.append([])
        for hd in range(C_HEADS):
            yield 1
            p = jnp.exp(sc[s][hd] - jnp.max(sc[s][hd], axis=-1, keepdims=True))
            pr[s].append(p / jnp.sum(p, axis=-1, keepdims=True))
    oc = cat_rows([jnp.concatenate([_dot(pr[s][hd], mem[s][hd][1]) for hd in range(C_HEADS)], axis=-1)
                   for s in range(nseg)])
    yield 1
    pc = _dot(oc * _silu(z.cols(_GC, C_WIDTH)), _wt(w["w_c_down"][...]))

    merged = []
    for rs in all_chunks:
        yield 1
        merged.append(jax.nn.sigmoid(z.cols(_ZA, D_MODEL, rs)) * pa[rs, :]
                      + jax.nn.sigmoid(z.cols(_ZB, D_MODEL, rs)) * pb[rs, :]
                      + jax.nn.sigmoid(z.cols(_ZC, D_MODEL, rs)) * pc[rs, :])
    y = x + _dot(cat_rows(merged), _wt(w["w_out"][...]))
    y = _rms(y, w["g_final"][...])
    new_states = [(new_st[s], new_ctx[s], new_hl[s]) for s in range(nseg)]
    return y, new_states


_WEIGHT_NAMES = ("g_mix", "w_in", "lb_logits", "g_a_out", "w_a_down", "w_conv", "b_conv", "w_lru_r", "b_lru_r",
                 "w_lru_i", "b_lru_i", "lru_lambda", "w_b_down", "w_c_down", "w_out", "g_final")
_NW = len(_WEIGHT_NAMES)


def _store_seq_state(conv_ref, lru_ref, seq_idx, ctx, hl):
    for r in range(CTX_ROWS):
        conv_ref[r, pl.ds(seq_idx, 1), :] = ctx[r:r + 1, :]
    lru_ref[pl.ds(seq_idx, 1), :] = hl


def _prompt_kernel(steps_per_seq, *refs):
    x_ref, xn_ref, mk_ref, mv_ref = refs[:4]
    w = dict(zip(_WEIGHT_NAMES, refs[4:4 + _NW]))
    y_ref, hgrn_ref, conv_ref, lru_ref = refs[4 + _NW:8 + _NW]
    scratch = refs[8 + _NW:]
    st_ref, ctx_ref, hl_ref = scratch[:3]
    z_even, z_odd = _ZBuf(scratch[3:3 + Z_BLOCKS]), _ZBuf(scratch[3 + Z_BLOCKS:3 + 2 * Z_BLOCKS])
    oa_refs = scratch[3 + 2 * Z_BLOCKS:5 + 2 * Z_BLOCKS]
    xpad_refs = scratch[5 + 2 * Z_BLOCKS:7 + 2 * Z_BLOCKS]
    j = pl.program_id(0)
    tile = PROMPT_TILE
    seq_start = (j % steps_per_seq) == 0

    @pl.when(j == 0)
    def _():
        _project_in(x_ref[0:tile, :], z_even, w)

    @pl.when(seq_start)
    def _():
        st_ref[...] = jnp.zeros_like(st_ref)
        ctx_ref[...] = jnp.zeros_like(ctx_ref)
        hl_ref[...] = jnp.zeros_like(hl_ref)

    kv = lambda s, hd: (mk_ref[0, _head_rows(hd), :], mv_ref[0, _head_rows(hd), :])
    states = [([st_ref[hd] for hd in range(A_HEADS)], ctx_ref[...], hl_ref[...])]

    y, states = _interleave(
        _mix_stages(x_ref[0:tile, :], z_even, oa_refs[0], xpad_refs[0], kv, states, w,
                    seg=tile, chunk=HGRN_CHUNK, first_rows_start=seq_start),
        _project_in_blocks(x_ref[tile:2 * tile, :], z_odd, w))
    y_ref[0:tile, :] = y

    y, states = _interleave(
        _mix_stages(x_ref[tile:2 * tile, :], z_odd, oa_refs[1], xpad_refs[1], kv, states, w,
                    seg=tile, chunk=HGRN_CHUNK, first_rows_start=None),
        _project_in_blocks(xn_ref[...], z_even, w))
    y_ref[tile:2 * tile, :] = y

    st, ctx, hl = states[0]
    for hd in range(A_HEADS):
        st_ref[hd] = st[hd]
    ctx_ref[...] = ctx
    hl_ref[...] = hl

    @pl.when((j % steps_per_seq) == steps_per_seq - 1)
    def _():
        for hd in range(A_HEADS):
            hgrn_ref[0, hd] = st[hd].T
        _store_seq_state(conv_ref, lru_ref, j // steps_per_seq, ctx, hl)


def _sample_kernel(nseq, seg, *refs):
    x_ref, mk_ref, mv_ref, hgrn_in, conv_in, lru_in = refs[:6]
    w = dict(zip(_WEIGHT_NAMES, refs[6:6 + _NW]))
    y_ref, hgrn_ref, conv_ref, lru_ref = refs[6 + _NW:10 + _NW]
    z_ref, h_ref, oa_ref, xpad_ref = refs[10 + _NW:]
    k = pl.program_id(0)

    @pl.when(k == 0)
    def _():
        h_ref[...] = _rms(x_ref[...], w["g_mix"][...]).astype(bf16)

    z_ref[k] = jnp.dot(h_ref[...], _wt(w["w_in"][...]), preferred_element_type=f32)

    @pl.when(k == Z_BLOCKS - 1)
    def _():
        z = _ZBuf([z_ref.at[blk] for blk in range(Z_BLOCKS)])
        states = [([hgrn_in[s, hd].T for hd in range(A_HEADS)],
                   jnp.concatenate([conv_in[r, s:s + 1, :] for r in range(CTX_ROWS)], axis=0),
                   lru_in[s:s + 1, :]) for s in range(nseq)]
        y, new_states = _mix(
            x_ref[...], z, oa_ref, xpad_ref,
            lambda s, hd: (mk_ref[s, _head_rows(hd), :], mv_ref[s, _head_rows(hd), :]), states, w,
            seg=seg, chunk=min(HGRN_CHUNK, seg), first_rows_start=None)
        y_ref[...] = y
        for s in range(nseq):
            st, ctx, hl = new_states[s]
            for hd in range(A_HEADS):
                hgrn_ref[s, hd] = st[hd].T
            _store_seq_state(conv_ref, lru_ref, s, ctx, hl)


def _const_spec(shape):
    nd = len(shape)
    return pl.BlockSpec(shape, lambda *_: (0,) * nd, pipeline_mode=pl.Buffered(1))


PREP_STEPS = 8
_DENSE_WEIGHTS = ("w_in", "w_a_down", "w_b_down", "w_c_down", "w_out")
_U32 = jnp.uint32


def _prep_kernel(*refs):
    nd = len(_DENSE_WEIGHTS)
    dense_in, (lru_r_in, lru_i_in) = refs[:nd], refs[nd:nd + 2]
    mem_ref, g_mem_ref, wk_in, wv_in = refs[nd + 2:nd + 6]
    outs = refs[nd + 6:]
    dense_out, (lru_r_out, lru_i_out), (k_ref, v_ref) = outs[:nd], outs[nd:nd + 2], outs[nd + 2:nd + 4]
    tile_ref, wk_ref, wv_ref = outs[nd + 4:]
    for src, dst in zip(dense_in, dense_out):
        dst[...] = pltpu.bitcast(src[...].astype(bf16), _U32)

    @pl.when(pl.program_id(0) == 0)
    def _():
        wk_ref[...] = pltpu.bitcast(wk_in[...].astype(bf16), _U32)
        wv_ref[...] = pltpu.bitcast(wv_in[...].astype(bf16), _U32)
        per = MXU_TILE // B_BLOCK_DIM
        for src, dst in ((lru_r_in, lru_r_out), (lru_i_in, lru_i_out)):
            for g in range(LRU_GROUPS):
                tile_ref[...] = jnp.zeros_like(tile_ref)
                for p in range(per):
                    lo = p * B_BLOCK_DIM
                    tile_ref[lo:lo + B_BLOCK_DIM, lo:lo + B_BLOCK_DIM] = src[g * per + p]
                dst[g] = pltpu.bitcast(tile_ref[...].astype(bf16), _U32)

    hm = _rms(mem_ref[0], g_mem_ref[...]).astype(bf16)
    k = jnp.dot(hm, _wt(wk_ref[...]), preferred_element_type=f32)
    v = jnp.dot(hm, _wt(wv_ref[...]), preferred_element_type=f32)
    for hd in range(C_HEADS):
        sl = slice(hd * HEAD_DIM, (hd + 1) * HEAD_DIM)
        k_ref[0, _head_rows(hd), :] = k[:, sl]
        v_ref[0, _head_rows(hd), :] = v[:, sl]


def _prep_weights(dense, lru_r, lru_i, mem, g_mem, w_mem_k, w_mem_v):
    bsz = mem.shape[0]
    assert bsz == PREP_STEPS
    in_specs, out_specs, out_shape = [], [], []
    for wm in dense:
        k, n = wm.shape
        assert k % (4 * SUBLANES * PREP_STEPS) == 0
        in_specs.append(pl.BlockSpec((k // PREP_STEPS, n), lambda i: (i, 0)))
        out_specs.append(pl.BlockSpec((k // (2 * PREP_STEPS), n), lambda i: (i, 0)))
        out_shape.append(jax.ShapeDtypeStruct((k // 2, n), _U32))
    blk = (B_BLOCKS, B_BLOCK_DIM, B_BLOCK_DIM)
    tiles = (LRU_GROUPS, MXU_TILE // 2, MXU_TILE)
    in_specs += [pl.BlockSpec(blk, lambda i: (0, 0, 0))] * 2
    out_specs += [pl.BlockSpec(tiles, lambda i: (0, 0, 0))] * 2
    out_shape += [jax.ShapeDtypeStruct(tiles, _U32)] * 2
    in_specs += [pl.BlockSpec((1, N_MEM, D_MODEL), lambda i: (i, 0, 0)), _const_spec((1, D_MODEL)),
                 _const_spec((D_MODEL, C_WIDTH)), _const_spec((D_MODEL, C_WIDTH))]
    out_specs += [pl.BlockSpec((1,) + KV_ROWS, lambda i: (i, 0, 0))] * 2
    out_shape += [jax.ShapeDtypeStruct((bsz,) + KV_ROWS, f32)] * 2
    outs = pl.pallas_call(
        _prep_kernel, grid=(PREP_STEPS,), in_specs=in_specs, out_specs=out_specs, out_shape=out_shape,
        scratch_shapes=[pltpu.VMEM((MXU_TILE, MXU_TILE), f32),
                        pltpu.VMEM((D_MODEL // 2, C_WIDTH), _U32), pltpu.VMEM((D_MODEL // 2, C_WIDTH), _U32)],
        compiler_params=pltpu.CompilerParams(vmem_limit_bytes=VMEM_LIMIT_BYTES,
                                             dimension_semantics=("arbitrary",)),
        name="prep_weights",
    )(*dense, lru_r, lru_i, mem, g_mem, w_mem_k, w_mem_v)
    nd = len(dense)
    return outs[:nd], outs[nd], outs[nd + 1], outs[nd + 2], outs[nd + 3]


def kernel(x_prompt, x_sample, mem_prompt, cache_mem_k, cache_mem_v, state_hgrn, state_conv, state_lru, g_mix, w_in, lb_logits, g_a_out, w_a_down, w_conv, b_conv, w_lru_r, b_lru_r, w_lru_i, b_lru_i, lru_lambda, w_b_down, g_mem, w_mem_k, w_mem_v, w_c_down, w_out, g_final):
    bsz, seq, _ = x_prompt.shape
    dec_b, dec_seq, _ = x_sample.shape
    assert g_mix.shape[0] == 1, "single-layer stack only"
    assert seq % (2 * PROMPT_TILE) == 0 and PROMPT_TILE % HGRN_CHUNK == 0

    row = lambda a: a.reshape(1, -1).astype(f32)
    dense = dict(w_in=w_in[0], w_a_down=w_a_down[0], w_b_down=w_b_down[0], w_c_down=w_c_down[0], w_out=w_out[0])
    packed, lru_r_tiles, lru_i_tiles, mk, mv = _prep_weights(
        [dense[n] for n in _DENSE_WEIGHTS], w_lru_r[0], w_lru_i[0], mem_prompt, row(g_mem[0]), w_mem_k[0], w_mem_v[0])
    packed = dict(zip(_DENSE_WEIGHTS, packed))
    weights = dict(
        g_mix=row(g_mix[0]), w_in=packed["w_in"], lb_logits=lb_logits.astype(f32), g_a_out=row(g_a_out[0]),
        w_a_down=packed["w_a_down"], w_conv=w_conv[0].astype(f32), b_conv=row(b_conv[0]),
        w_lru_r=lru_r_tiles, b_lru_r=row(b_lru_r[0]), w_lru_i=lru_i_tiles, b_lru_i=row(b_lru_i[0]),
        lru_lambda=row(lru_lambda[0]), w_b_down=packed["w_b_down"], w_c_down=packed["w_c_down"],
        w_out=packed["w_out"], g_final=row(g_final))
    wlist = [weights[n] for n in _WEIGHT_NAMES]
    wspecs = [_const_spec(a.shape) for a in wlist]

    tile = PROMPT_TILE
    n_tiles = bsz * seq // tile
    steps_per_seq = seq // (2 * tile)
    zbuf = [pltpu.VMEM((tile, Z_BLK), f32)] * Z_BLOCKS
    y_p, hgrn_p, conv_p, lru_p = pl.pallas_call(
        functools.partial(_prompt_kernel, steps_per_seq),
        grid=(n_tiles // 2,),
        in_specs=[pl.BlockSpec((2 * tile, D_MODEL), lambda j: (j, 0)),
                  pl.BlockSpec((tile, D_MODEL), lambda j: (jnp.minimum(2 * j + 2, n_tiles - 1), 0)),
                  pl.BlockSpec((1,) + KV_ROWS, lambda j: (j // steps_per_seq, 0, 0)),
                  pl.BlockSpec((1,) + KV_ROWS, lambda j: (j // steps_per_seq, 0, 0))] + wspecs,
        out_specs=[pl.BlockSpec((2 * tile, D_MODEL), lambda j: (j, 0)),
                   pl.BlockSpec((1, A_HEADS, HEAD_DIM, HEAD_DIM), lambda j: (j // steps_per_seq, 0, 0, 0)),
                   pl.BlockSpec((CTX_ROWS, bsz, B_WIDTH), lambda j: (0, 0, 0)),
                   pl.BlockSpec((bsz, B_WIDTH), lambda j: (0, 0))],
        out_shape=[jax.ShapeDtypeStruct((bsz * seq, D_MODEL), f32),
                   jax.ShapeDtypeStruct((bsz, A_HEADS, HEAD_DIM, HEAD_DIM), f32),
                   jax.ShapeDtypeStruct((CTX_ROWS, bsz, B_WIDTH), f32),
                   jax.ShapeDtypeStruct((bsz, B_WIDTH), f32)],
        scratch_shapes=[pltpu.VMEM((A_HEADS, HEAD_DIM, HEAD_DIM), f32),
                        pltpu.VMEM((CTX_ROWS, B_WIDTH), f32),
                        pltpu.VMEM((1, B_WIDTH), f32)] + zbuf + zbuf
                       + [pltpu.VMEM((tile, A_WIDTH), f32)] * 2
                       + [pltpu.VMEM((SUBLANES + tile, B_WIDTH), f32)] * 2,
        compiler_params=pltpu.CompilerParams(vmem_limit_bytes=VMEM_LIMIT_BYTES,
                                             dimension_semantics=("arbitrary",)),
        name="prompt_layer",
    )(x_prompt.reshape(bsz * seq, D_MODEL), x_prompt.reshape(bsz * seq, D_MODEL), mk, mv, *wlist)
    y_p = y_p.reshape(bsz, seq, D_MODEL)

    rows = dec_b * dec_seq
    full = lambda shape: pl.BlockSpec(shape, lambda *_: (0,) * len(shape))
    w_in_stream = pl.BlockSpec((D_MODEL // 2, Z_BLK), lambda k: (0, k))
    sample_wspecs = [w_in_stream if n == "w_in" else s for n, s in zip(_WEIGHT_NAMES, wspecs)]
    y_s, hgrn_s, conv_s, lru_s = pl.pallas_call(
        functools.partial(_sample_kernel, dec_b, dec_seq),
        grid=(Z_BLOCKS,),
        in_specs=[_const_spec((rows, D_MODEL)), _const_spec((dec_b,) + KV_ROWS), _const_spec((dec_b,) + KV_ROWS),
                  _const_spec((dec_b, A_HEADS, HEAD_DIM, HEAD_DIM)), _const_spec((CTX_ROWS, dec_b, B_WIDTH)),
                  _const_spec((dec_b, B_WIDTH))] + sample_wspecs,
        out_specs=[full((rows, D_MODEL)), full((dec_b, A_HEADS, HEAD_DIM, HEAD_DIM)),
                   full((CTX_ROWS, dec_b, B_WIDTH)), full((dec_b, B_WIDTH))],
        out_shape=[jax.ShapeDtypeStruct((rows, D_MODEL), f32),
                   jax.ShapeDtypeStruct((dec_b, A_HEADS, HEAD_DIM, HEAD_DIM), f32),
                   jax.ShapeDtypeStruct((CTX_ROWS, dec_b, B_WIDTH), f32),
                   jax.ShapeDtypeStruct((dec_b, B_WIDTH), f32)],
        scratch_shapes=[pltpu.VMEM((Z_BLOCKS, rows, Z_BLK), f32),
                        pltpu.VMEM((rows, D_MODEL), bf16),
                        pltpu.VMEM((rows, A_WIDTH), f32),
                        pltpu.VMEM((dec_b * (SUBLANES + dec_seq), B_WIDTH), f32)],
        compiler_params=pltpu.CompilerParams(vmem_limit_bytes=VMEM_LIMIT_BYTES,
                                             dimension_semantics=("arbitrary",)),
        name="sample_layer",
    )(x_sample.reshape(rows, D_MODEL), cache_mem_k.reshape((dec_b,) + KV_ROWS),
      cache_mem_v.reshape((dec_b,) + KV_ROWS), state_hgrn[0], jnp.swapaxes(state_conv[0], 0, 1),
      state_lru[0], *wlist)

    return (y_p, y_s.reshape(dec_b, dec_seq, D_MODEL), hgrn_p[None], jnp.swapaxes(conv_p, 0, 1)[None],
            lru_p[None], mk.reshape(1, bsz, N_MEM, C_HEADS, HEAD_DIM),
            mv.reshape(1, bsz, N_MEM, C_HEADS, HEAD_DIM), hgrn_s[None], jnp.swapaxes(conv_s, 0, 1)[None],
            lru_s[None])
```

```python
import functools

import jax
import jax.numpy as jnp
from jax import lax
from jax.experimental import pallas as pl
from jax.experimental.pallas import tpu as pltpu

f32 = jnp.float32
bf16 = jnp.bfloat16

D_MODEL = 1024
N_MEM = 256
EPS = 1e-6
A_HEADS = 4
HEAD_DIM = 128
A_WIDTH = A_HEADS * HEAD_DIM
B_WIDTH = D_MODEL
B_BLOCKS = 16
B_BLOCK_DIM = B_WIDTH // B_BLOCKS
CONV_W = 4
LRU_C = 8.0
C_HEADS = 4
C_WIDTH = C_HEADS * HEAD_DIM
assert A_HEADS == C_HEADS
HGRN_CHUNK = 64
IN_COLS = 4 * A_WIDTH + 2 * B_WIDTH + 2 * C_WIDTH + 3 * D_MODEL

_QA, _FA, _VA, _GA = 0, A_WIDTH, 2 * A_WIDTH, 3 * A_WIDTH
_XB = 4 * A_WIDTH
_GB = _XB + B_WIDTH
_QC = _GB + B_WIDTH
_GC = _QC + C_WIDTH
_ZA = _GC + C_WIDTH
_ZB = _ZA + D_MODEL
_ZC = _ZB + D_MODEL

MXU_TILE = 256
LRU_GROUPS = B_WIDTH // MXU_TILE
SUBLANES = 8
CTX_ROWS = CONV_W - 1

PROMPT_TILE = 256
Z_BLK = 1024
Z_BLOCKS = IN_COLS // Z_BLK
VMEM_LIMIT_BYTES = 60 * 1024 * 1024


def _rms(x, g):
    return x * lax.rsqrt(jnp.mean(x * x, axis=-1, keepdims=True) + EPS) * g


def _wt(ref_or_val):
    return pltpu.bitcast(ref_or_val, bf16)


def _dot(a, b):
    return jnp.dot(a.astype(bf16), b.astype(bf16), preferred_element_type=f32)


def _dot_nt(a, b):
    return lax.dot_general(a.astype(bf16), b.astype(bf16), (((1,), (1,)), ((), ())),
                           preferred_element_type=f32)


def _dot_tn(a, b):
    return lax.dot_general(a.astype(bf16), b.astype(bf16), (((0,), (0,)), ((), ())),
                           preferred_element_type=f32)


def _silu(x):
    return x * jax.nn.sigmoid(x)


KV_ROWS = (N_MEM * C_HEADS, HEAD_DIM)


def _head_rows(hd):
    return pl.ds(hd, N_MEM, stride=C_HEADS)


def _vreg_groups(x):
    rows, width = x.shape
    return x.reshape(rows // SUBLANES, SUBLANES, width)


def _cumsum_rows(x):
    rows = x.shape[0]
    x3 = _vreg_groups(x)
    sub = lax.broadcasted_iota(jnp.int32, x3.shape, 1)
    d = 1
    while d < SUBLANES:
        x3 = x3 + jnp.where(sub >= d, pltpu.roll(x3, d, 1), 0.0)
        d *= 2
    out, carry = [], None
    for g in range(rows // SUBLANES):
        cur = x3[g] if carry is None else x3[g] + carry
        carry = cur[SUBLANES - 1:SUBLANES, :]
        out.append(cur)
    return jnp.concatenate(out, axis=0)


def _linear_scan_rows(a, u, carry):
    rows = a.shape[0]
    a3, u3 = _vreg_groups(a), _vreg_groups(u)
    sub = lax.broadcasted_iota(jnp.int32, a3.shape, 1)
    d = 1
    while d < SUBLANES:
        keep = sub >= d
        u3 = a3 * jnp.where(keep, pltpu.roll(u3, d, 1), 0.0) + u3
        a3 = a3 * jnp.where(keep, pltpu.roll(a3, d, 1), 1.0)
        d *= 2
    out = []
    for g in range(rows // SUBLANES):
        cur = u3[g] + a3[g] * carry
        carry = cur[SUBLANES - 1:SUBLANES, :]
        out.append(cur)
    return jnp.concatenate(out, axis=0), carry


class _ZBuf:
    def __init__(self, refs):
        self.refs = refs

    def cols(self, c0, width, rows=slice(None)):
        blk, off = divmod(c0, Z_BLK)
        assert off + width <= Z_BLK
        return self.refs[blk][rows, off:off + width]


def _project_in_blocks(x, z, w):
    h = _rms(x, w["g_mix"][...]).astype(bf16)

    def block(c0):
        blk, off = divmod(c0, Z_BLK)
        z.refs[blk][:, off:off + MXU_TILE] = jnp.dot(h, _wt(w["w_in"][:, c0:c0 + MXU_TILE]),
                                                     preferred_element_type=f32)

    return [functools.partial(block, c0) for c0 in range(0, IN_COLS, MXU_TILE)]


def _project_in(x, z, w):
    for block in _project_in_blocks(x, z, w):
        block()


def _interleave(stages, blocks):
    blocks = list(blocks)
    while True:
        try:
            n = next(stages)
        except StopIteration as done:
            result = done.value
            break
        for _ in range(min(n, len(blocks))):
            blocks.pop(0)()
    for block in blocks:
        block()
    return result


def _mix(*args, **kwargs):
    return _interleave(_mix_stages(*args, **kwargs), [])


def _mix_stages(x, z, oa_ref, xpad_ref, kv, states, w, *, seg, chunk, first_rows_start):
    rows = x.shape[0]
    nseg = rows // seg
    nchunk = seg // chunk
    chunk_rows = [[slice(s * seg + c * chunk, s * seg + (c + 1) * chunk) for c in range(nchunk)]
                  for s in range(nseg)]
    all_chunks = [rs for per_seg in chunk_rows for rs in per_seg]
    head_sl = [slice(hd * HEAD_DIM, (hd + 1) * HEAD_DIM) for hd in range(A_HEADS)]
    cat_rows = lambda parts: parts[0] if len(parts) == 1 else jnp.concatenate(parts, axis=0)

    lg = w["lb_logits"][...]
    l0, l1 = lg[0:1, :], lg[1:2, :]
    lmax = jnp.maximum(l0, l1)
    e0, e1 = jnp.exp(l0 - lmax), jnp.exp(l1 - lmax)
    lb = e0 / (e0 + e1)

    qg, kg, v, kd, decay = [], [], [], [], []
    for rs in all_chunks:
        yield 1
        f = lb + (1.0 - lb) * jax.nn.sigmoid(z.cols(_FA, A_WIDTH, rs))
        b = _cumsum_rows(jnp.log(f))
        k = 1.0 - f
        qg.append(_silu(z.cols(_QA, A_WIDTH, rs)) * jnp.exp(b))
        kg.append(k * jnp.exp(-b))
        v.append(z.cols(_VA, A_WIDTH, rs))
        b_last = b[chunk - 1:chunk, :]
        kd.append(k * jnp.exp(b_last - b))
        decay.append(jnp.exp(b_last))
    qg_all, kg_all, v_all = cat_rows(qg), cat_rows(kg), cat_rows(v)

    scores = [_dot_nt(qg_all[:, sl], kg_all[:, sl]) for sl in head_sl]
    st_in = [[[states[s][0][hd]] for hd in range(A_HEADS)] for s in range(nseg)]
    for s in range(nseg):
        for c in range(nchunk):
            i = s * nchunk + c
            for hd, sl in enumerate(head_sl):
                st_in[s][hd].append(st_in[s][hd][c] * decay[i][:, sl] + _dot_tn(v[i][:, sl], kd[i][:, sl]))
    new_st = [[st_in[s][hd][nchunk] for hd in range(A_HEADS)] for s in range(nseg)]
    yield 1
    tt = lax.broadcasted_iota(jnp.int32, (rows, rows), 0)
    ss = lax.broadcasted_iota(jnp.int32, (rows, rows), 1)
    shift = chunk.bit_length() - 1
    causal = ((tt >> shift) == (ss >> shift)) & (ss <= tt)
    o_intra = [_dot(jnp.where(causal, scores[hd], 0.0), v_all[:, sl]) for hd, sl in enumerate(head_sl)]
    yield 1
    for hd, sl in enumerate(head_sl):
        for s in range(nseg):
            for c, rs in enumerate(chunk_rows[s]):
                oa_ref[rs, sl] = o_intra[hd][rs, :] + _dot_nt(qg[s * nchunk + c][:, sl], st_in[s][hd][c])

    g_a = w["g_a_out"][...]
    a_in = []
    for i, rs in enumerate(all_chunks):
        yield i % 2
        normed = jnp.concatenate([_rms(oa_ref[rs, sl], g_a[:, sl]) for sl in head_sl], axis=-1)
        a_in.append(normed * _silu(z.cols(_GA, A_WIDTH, rs)))
    pa = _dot(cat_rows(a_in), _wt(w["w_a_down"][...]))

    pad = SUBLANES
    stride = pad + seg
    for s in range(nseg):
        base = s * stride
        xpad_ref[base + pad - CTX_ROWS:base + pad, :] = states[s][1]
        xpad_ref[base + pad:base + pad + seg, :] = z.cols(_XB, B_WIDTH, slice(s * seg, (s + 1) * seg))
    w_conv = w["w_conv"][...]
    xc, new_ctx = [], []
    for s in range(nseg):
        base = s * stride + pad - CTX_ROWS
        for c in range(nchunk):
            yield 1
            r0 = base + c * chunk
            acc = w_conv[0:1, :] * xpad_ref[r0:r0 + chunk, :]
            for j in range(1, CONV_W):
                acc = acc + w_conv[j:j + 1, :] * xpad_ref[r0 + j:r0 + j + chunk, :]
            xc.append(w["b_conv"][...] + acc)
        new_ctx.append(xpad_ref[base + seg:base + seg + CTX_ROWS, :])

    xc_b = cat_rows(xc).astype(bf16)
    r_pre, i_pre = [], []
    for g in range(LRU_GROUPS):
        gs = slice(g * MXU_TILE, (g + 1) * MXU_TILE)
        r_pre.append(jnp.dot(xc_b[:, gs], _wt(w["w_lru_r"][g]), preferred_element_type=f32))
        i_pre.append(jnp.dot(xc_b[:, gs], _wt(w["w_lru_i"][g]), preferred_element_type=f32))
    r_pre, i_pre = jnp.concatenate(r_pre, axis=-1), jnp.concatenate(i_pre, axis=-1)
    neg_lam = -w["lru_lambda"][...]
    softplus = jnp.maximum(neg_lam, 0.0) + jnp.log1p(jnp.exp(-jnp.abs(neg_lam)))
    a_l, u_l = [], []
    for s in range(nseg):
        for c, rs in enumerate(chunk_rows[s]):
            yield 2
            r = jax.nn.sigmoid(r_pre[rs, :] + w["b_lru_r"][...])
            ig = jax.nn.sigmoid(i_pre[rs, :] + w["b_lru_i"][...])
            log_a = -LRU_C * r * softplus
            a = jnp.exp(log_a)
            mult = jnp.sqrt(-jnp.tanh(log_a) * (a * a + 1.0))
            if first_rows_start is not None and c == 0:
                first_row = lax.broadcasted_iota(jnp.int32, mult.shape, 0) == 0
                mult = jnp.where(first_row & first_rows_start, 1.0, mult)
            a_l.append(a)
            u_l.append(mult * ig * xc[s * nchunk + c])
    hb_gated, new_hl = [], []
    for s in range(nseg):
        carry = states[s][2]
        for c, rs in enumerate(chunk_rows[s]):
            yield 1
            i = s * nchunk + c
            hb, carry = _linear_scan_rows(a_l[i], u_l[i], carry)
            hb_gated.append(hb * _silu(z.cols(_GB, B_WIDTH, rs)))
        new_hl.append(carry)
    pb = _dot(cat_rows(hb_gated), _wt(w["w_b_down"][...]))

    scale = HEAD_DIM ** -0.5
    mem = [[kv(s, hd) for hd in range(C_HEADS)] for s in range(nseg)]
    seg_rows = [slice(s * seg, (s + 1) * seg) for s in range(nseg)]
    sc = [[_dot_nt(z.cols(_QC + hd * HEAD_DIM, HEAD_DIM, seg_rows[s]), mem[s][hd][0]) * scale
           for hd in range(C_HEADS)] for s in range(nseg)]
    pr = []
    for s in range(nseg):
        pr.append([])
        for hd in range(C_HEADS):
            yield 1
            p = jnp.exp(sc[s][hd] - jnp.max(sc[s][hd], axis=-1, keepdims=True))
            pr[s].append(p / jnp.sum(p, axis=-1, keepdims=True))
    oc = cat_rows([jnp.concatenate([_dot(pr[s][hd], mem[s][hd][1]) for hd in range(C_HEADS)], axis=-1)
                   for s in range(nseg)])
    yield 1
    pc = _dot(oc * _silu(z.cols(_GC, C_WIDTH)), _wt(w["w_c_down"][...]))

    merged = []
    for rs in all_chunks:
        yield 1
        merged.append(jax.nn.sigmoid(z.cols(_ZA, D_MODEL, rs)) * pa[rs, :]
                      + jax.nn.sigmoid(z.cols(_ZB, D_MODEL, rs)) * pb[rs, :]
                      + jax.nn.sigmoid(z.cols(_ZC, D_MODEL, rs)) * pc[rs, :])
    y = x + _dot(cat_rows(merged), _wt(w["w_out"][...]))
    y = _rms(y, w["g_final"][...])
    new_states = [(new_st[s], new_ctx[s], new_hl[s]) for s in range(nseg)]
    return y, new_states


_WEIGHT_NAMES = ("g_mix", "w_in", "lb_logits", "g_a_out", "w_a_down", "w_conv", "b_conv", "w_lru_r", "b_lru_r",
                 "w_lru_i", "b_lru_i", "lru_lambda", "w_b_down", "w_c_down", "w_out", "g_final")
_NW = len(_WEIGHT_NAMES)


def _store_seq_state(conv_ref, lru_ref, seq_idx, ctx, hl):
    for r in range(CTX_ROWS):
        conv_ref[r, pl.ds(seq_idx, 1), :] = ctx[r:r + 1, :]
    lru_ref[pl.ds(seq_idx, 1), :] = hl


def _prompt_kernel(steps_per_seq, *refs):
    x_ref, xn_ref, mk_ref, mv_ref = refs[:4]
    w = dict(zip(_WEIGHT_NAMES, refs[4:4 + _NW]))
    y_ref, hgrn_ref, conv_ref, lru_ref = refs[4 + _NW:8 + _NW]
    scratch = refs[8 + _NW:]
    st_ref, ctx_ref, hl_ref = scratch[:3]
    z_even, z_odd = _ZBuf(scratch[3:3 + Z_BLOCKS]), _ZBuf(scratch[3 + Z_BLOCKS:3 + 2 * Z_BLOCKS])
    oa_refs = scratch[3 + 2 * Z_BLOCKS:5 + 2 * Z_BLOCKS]
    xpad_refs = scratch[5 + 2 * Z_BLOCKS:7 + 2 * Z_BLOCKS]
    j = pl.program_id(0)
    tile = PROMPT_TILE
    seq_start = (j % steps_per_seq) == 0

    @pl.when(j == 0)
    def _():
        _project_in(x_ref[0:tile, :], z_even, w)

    @pl.when(seq_start)
    def _():
        st_ref[...] = jnp.zeros_like(st_ref)
        ctx_ref[...] = jnp.zeros_like(ctx_ref)
        hl_ref[...] = jnp.zeros_like(hl_ref)

    kv = lambda s, hd: (mk_ref[0, _head_rows(hd), :], mv_ref[0, _head_rows(hd), :])
    states = [([st_ref[hd] for hd in range(A_HEADS)], ctx_ref[...], hl_ref[...])]

    y, states = _interleave(
        _mix_stages(x_ref[0:tile, :], z_even, oa_refs[0], xpad_refs[0], kv, states, w,
                    seg=tile, chunk=HGRN_CHUNK, first_rows_start=seq_start),
        _project_in_blocks(x_ref[tile:2 * tile, :], z_odd, w))
    y_ref[0:tile, :] = y

    y, states = _interleave(
        _mix_stages(x_ref[tile:2 * tile, :], z_odd, oa_refs[1], xpad_refs[1], kv, states, w,
                    seg=tile, chunk=HGRN_CHUNK, first_rows_start=None),
        _project_in_blocks(xn_ref[...], z_even, w))
    y_ref[tile:2 * tile, :] = y

    st, ctx, hl = states[0]
    for hd in range(A_HEADS):
        st_ref[hd] = st[hd]
    ctx_ref[...] = ctx
    hl_ref[...] = hl

    @pl.when((j % steps_per_seq) == steps_per_seq - 1)
    def _():
        for hd in range(A_HEADS):
            hgrn_ref[0, hd] = st[hd].T
        _store_seq_state(conv_ref, lru_ref, j // steps_per_seq, ctx, hl)


def _sample_kernel(nseq, seg, *refs):
    x_ref, mk_ref, mv_ref, hgrn_in, conv_in, lru_in = refs[:6]
    w = dict(zip(_WEIGHT_NAMES, refs[6:6 + _NW]))
    y_ref, hgrn_ref, conv_ref, lru_ref = refs[6 + _NW:10 + _NW]
    scratch = refs[10 + _NW:]
    z = _ZBuf(scratch[:Z_BLOCKS])
    oa_ref, xpad_ref = scratch[Z_BLOCKS:]
    states = [([hgrn_in[s, hd].T for hd in range(A_HEADS)],
               jnp.concatenate([conv_in[r, s:s + 1, :] for r in range(CTX_ROWS)], axis=0),
               lru_in[s:s + 1, :]) for s in range(nseq)]
    _project_in(x_ref[...], z, w)
    y, new_states = _mix(
        x_ref[...], z, oa_ref, xpad_ref,
        lambda s, hd: (mk_ref[s, _head_rows(hd), :], mv_ref[s, _head_rows(hd), :]), states, w,
        seg=seg, chunk=min(HGRN_CHUNK, seg), first_rows_start=None)
    y_ref[...] = y
    for s in range(nseq):
        st, ctx, hl = new_states[s]
        for hd in range(A_HEADS):
            hgrn_ref[s, hd] = st[hd].T
        _store_seq_state(conv_ref, lru_ref, s, ctx, hl)


def _const_spec(shape):
    nd = len(shape)
    return pl.BlockSpec(shape, lambda *_: (0,) * nd, pipeline_mode=pl.Buffered(1))


PREP_STEPS = 8
_DENSE_WEIGHTS = ("w_in", "w_a_down", "w_b_down", "w_c_down", "w_out")
_U32 = jnp.uint32


def _prep_kernel(*refs):
    nd = len(_DENSE_WEIGHTS)
    dense_in, (lru_r_in, lru_i_in) = refs[:nd], refs[nd:nd + 2]
    mem_ref, g_mem_ref, wk_in, wv_in = refs[nd + 2:nd + 6]
    outs = refs[nd + 6:]
    dense_out, (lru_r_out, lru_i_out), (k_ref, v_ref) = outs[:nd], outs[nd:nd + 2], outs[nd + 2:nd + 4]
    tile_ref, wk_ref, wv_ref = outs[nd + 4:]
    for src, dst in zip(dense_in, dense_out):
        dst[...] = pltpu.bitcast(src[...].astype(bf16), _U32)

    @pl.when(pl.program_id(0) == 0)
    def _():
        wk_ref[...] = pltpu.bitcast(wk_in[...].astype(bf16), _U32)
        wv_ref[...] = pltpu.bitcast(wv_in[...].astype(bf16), _U32)
        per = MXU_TILE // B_BLOCK_DIM
        for src, dst in ((lru_r_in, lru_r_out), (lru_i_in, lru_i_out)):
            for g in range(LRU_GROUPS):
                tile_ref[...] = jnp.zeros_like(tile_ref)
                for p in range(per):
                    lo = p * B_BLOCK_DIM
                    tile_ref[lo:lo + B_BLOCK_DIM, lo:lo + B_BLOCK_DIM] = src[g * per + p]
                dst[g] = pltpu.bitcast(tile_ref[...].astype(bf16), _U32)

    hm = _rms(mem_ref[0], g_mem_ref[...]).astype(bf16)
    k = jnp.dot(hm, _wt(wk_ref[...]), preferred_element_type=f32)
    v = jnp.dot(hm, _wt(wv_ref[...]), preferred_element_type=f32)
    for hd in range(C_HEADS):
        sl = slice(hd * HEAD_DIM, (hd + 1) * HEAD_DIM)
        k_ref[0, _head_rows(hd), :] = k[:, sl]
        v_ref[0, _head_rows(hd), :] = v[:, sl]


def _prep_weights(dense, lru_r, lru_i, mem, g_mem, w_mem_k, w_mem_v):
    bsz = mem.shape[0]
    assert bsz == PREP_STEPS
    in_specs, out_specs, out_shape = [], [], []
    for wm in dense:
        k, n = wm.shape
        assert k % (4 * SUBLANES * PREP_STEPS) == 0
        in_specs.append(pl.BlockSpec((k // PREP_STEPS, n), lambda i: (i, 0)))
        out_specs.append(pl.BlockSpec((k // (2 * PREP_STEPS), n), lambda i: (i, 0)))
        out_shape.append(jax.ShapeDtypeStruct((k // 2, n), _U32))
    blk = (B_BLOCKS, B_BLOCK_DIM, B_BLOCK_DIM)
    tiles = (LRU_GROUPS, MXU_TILE // 2, MXU_TILE)
    in_specs += [pl.BlockSpec(blk, lambda i: (0, 0, 0))] * 2
    out_specs += [pl.BlockSpec(tiles, lambda i: (0, 0, 0))] * 2
    out_shape += [jax.ShapeDtypeStruct(tiles, _U32)] * 2
    in_specs += [pl.BlockSpec((1, N_MEM, D_MODEL), lambda i: (i, 0, 0)), _const_spec((1, D_MODEL)),
                 _const_spec((D_MODEL, C_WIDTH)), _const_spec((D_MODEL, C_WIDTH))]
    out_specs += [pl.BlockSpec((1,) + KV_ROWS, lambda i: (i, 0, 0))] * 2
    out_shape += [jax.ShapeDtypeStruct((bsz,) + KV_ROWS, f32)] * 2
    outs = pl.pallas_call(
        _prep_kernel, grid=(PREP_STEPS,), in_specs=in_specs, out_specs=out_specs, out_shape=out_shape,
        scratch_shapes=[pltpu.VMEM((MXU_TILE, MXU_TILE), f32),
                        pltpu.VMEM((D_MODEL // 2, C_WIDTH), _U32), pltpu.VMEM((D_MODEL // 2, C_WIDTH), _U32)],
        compiler_params=pltpu.CompilerParams(vmem_limit_bytes=VMEM_LIMIT_BYTES,
                                             dimension_semantics=("arbitrary",)),
        name="prep_weights",
    )(*dense, lru_r, lru_i, mem, g_mem, w_mem_k, w_mem_v)
    nd = len(dense)
    return outs[:nd], outs[nd], outs[nd + 1], outs[nd + 2], outs[nd + 3]


def kernel(x_prompt, x_sample, mem_prompt, cache_mem_k, cache_mem_v, state_hgrn, state_conv, state_lru, g_mix, w_in, lb_logits, g_a_out, w_a_down, w_conv, b_conv, w_lru_r, b_lru_r, w_lru_i, b_lru_i, lru_lambda, w_b_down, g_mem, w_mem_k, w_mem_v, w_c_down, w_out, g_final):
    bsz, seq, _ = x_prompt.shape
    dec_b, dec_seq, _ = x_sample.shape
    assert g_mix.shape[0] == 1, "single-layer stack only"
    assert seq % (2 * PROMPT_TILE) == 0 and PROMPT_TILE % HGRN_CHUNK == 0

    row = lambda a: a.reshape(1, -1).astype(f32)
    dense = dict(w_in=w_in[0], w_a_down=w_a_down[0], w_b_down=w_b_down[0], w_c_down=w_c_down[0], w_out=w_out[0])
    packed, lru_r_tiles, lru_i_tiles, mk, mv = _prep_weights(
        [dense[n] for n in _DENSE_WEIGHTS], w_lru_r[0], w_lru_i[0], mem_prompt, row(g_mem[0]), w_mem_k[0], w_mem_v[0])
    packed = dict(zip(_DENSE_WEIGHTS, packed))
    weights = dict(
        g_mix=row(g_mix[0]), w_in=packed["w_in"], lb_logits=lb_logits.astype(f32), g_a_out=row(g_a_out[0]),
        w_a_down=packed["w_a_down"], w_conv=w_conv[0].astype(f32), b_conv=row(b_conv[0]),
        w_lru_r=lru_r_tiles, b_lru_r=row(b_lru_r[0]), w_lru_i=lru_i_tiles, b_lru_i=row(b_lru_i[0]),
        lru_lambda=row(lru_lambda[0]), w_b_down=packed["w_b_down"], w_c_down=packed["w_c_down"],
        w_out=packed["w_out"], g_final=row(g_final))
    wlist = [weights[n] for n in _WEIGHT_NAMES]
    wspecs = [_const_spec(a.shape) for a in wlist]

    tile = PROMPT_TILE
    n_tiles = bsz * seq // tile
    steps_per_seq = seq // (2 * tile)
    zbuf = [pltpu.VMEM((tile, Z_BLK), f32)] * Z_BLOCKS
    y_p, hgrn_p, conv_p, lru_p = pl.pallas_call(
        functools.partial(_prompt_kernel, steps_per_seq),
        grid=(n_tiles // 2,),
        in_specs=[pl.BlockSpec((2 * tile, D_MODEL), lambda j: (j, 0)),
                  pl.BlockSpec((tile, D_MODEL), lambda j: (jnp.minimum(2 * j + 2, n_tiles - 1), 0)),
                  pl.BlockSpec((1,) + KV_ROWS, lambda j: (j // steps_per_seq, 0, 0)),
                  pl.BlockSpec((1,) + KV_ROWS, lambda j: (j // steps_per_seq, 0, 0))] + wspecs,
        out_specs=[pl.BlockSpec((2 * tile, D_MODEL), lambda j: (j, 0)),
                   pl.BlockSpec((1, A_HEADS, HEAD_DIM, HEAD_DIM), lambda j: (j // steps_per_seq, 0, 0, 0)),
                   pl.BlockSpec((CTX_ROWS, bsz, B_WIDTH), lambda j: (0, 0, 0)),
                   pl.BlockSpec((bsz, B_WIDTH), lambda j: (0, 0))],
        out_shape=[jax.ShapeDtypeStruct((bsz * seq, D_MODEL), f32),
                   jax.ShapeDtypeStruct((bsz, A_HEADS, HEAD_DIM, HEAD_DIM), f32),
                   jax.ShapeDtypeStruct((CTX_ROWS, bsz, B_WIDTH), f32),
                   jax.ShapeDtypeStruct((bsz, B_WIDTH), f32)],
        scratch_shapes=[pltpu.VMEM((A_HEADS, HEAD_DIM, HEAD_DIM), f32),
                        pltpu.VMEM((CTX_ROWS, B_WIDTH), f32),
                        pltpu.VMEM((1, B_WIDTH), f32)] + zbuf + zbuf
                       + [pltpu.VMEM((tile, A_WIDTH), f32)] * 2
                       + [pltpu.VMEM((SUBLANES + tile, B_WIDTH), f32)] * 2,
        compiler_params=pltpu.CompilerParams(vmem_limit_bytes=VMEM_LIMIT_BYTES,
                                             dimension_semantics=("arbitrary",)),
        name="prompt_layer",
    )(x_prompt.reshape(bsz * seq, D_MODEL), x_prompt.reshape(bsz * seq, D_MODEL), mk, mv, *wlist)
    y_p = y_p.reshape(bsz, seq, D_MODEL)

    rows = dec_b * dec_seq
    full = lambda shape: pl.BlockSpec(shape, lambda *_: (0,) * len(shape))
    y_s, hgrn_s, conv_s, lru_s = pl.pallas_call(
        functools.partial(_sample_kernel, dec_b, dec_seq),
        grid=(1,),
        in_specs=[full((rows, D_MODEL)), full((dec_b,) + KV_ROWS), full((dec_b,) + KV_ROWS),
                  full((dec_b, A_HEADS, HEAD_DIM, HEAD_DIM)), full((CTX_ROWS, dec_b, B_WIDTH)),
                  full((dec_b, B_WIDTH))] + wspecs,
        out_specs=[full((rows, D_MODEL)), full((dec_b, A_HEADS, HEAD_DIM, HEAD_DIM)),
                   full((CTX_ROWS, dec_b, B_WIDTH)), full((dec_b, B_WIDTH))],
        out_shape=[jax.ShapeDtypeStruct((rows, D_MODEL), f32),
                   jax.ShapeDtypeStruct((dec_b, A_HEADS, HEAD_DIM, HEAD_DIM), f32),
                   jax.ShapeDtypeStruct((CTX_ROWS, dec_b, B_WIDTH), f32),
                   jax.ShapeDtypeStruct((dec_b, B_WIDTH), f32)],
        scratch_shapes=[pltpu.VMEM((rows, Z_BLK), f32)] * Z_BLOCKS
                       + [pltpu.VMEM((rows, A_WIDTH), f32),
                        pltpu.VMEM((dec_b * (SUBLANES + dec_seq), B_WIDTH), f32)],
        compiler_params=pltpu.CompilerParams(vmem_limit_bytes=VMEM_LIMIT_BYTES),
        name="sample_layer",
    )(x_sample.reshape(rows, D_MODEL), cache_mem_k.reshape((dec_b,) + KV_ROWS),
      cache_mem_v.reshape((dec_b,) + KV_ROWS), state_hgrn[0], jnp.swapaxes(state_conv[0], 0, 1),
      state_lru[0], *wlist)

    return (y_p, y_s.reshape(dec_b, dec_seq, D_MODEL), hgrn_p[None], jnp.swapaxes(conv_p, 0, 1)[None],
            lru_p[None], mk.reshape(1, bsz, N_MEM, C_HEADS, HEAD_DIM),
            mv.reshape(1, bsz, N_MEM, C_HEADS, HEAD_DIM), hgrn_s[None], jnp.swapaxes(conv_s, 0, 1)[None],
            lru_s[None])
```

```python
import functools

import jax
import jax.numpy as jnp
from jax import lax
from jax.experimental import pallas as pl
from jax.experimental.pallas import tpu as pltpu

f32 = jnp.float32
bf16 = jnp.bfloat16

D_MODEL = 1024
N_MEM = 256
EPS = 1e-6
A_HEADS = 4
HEAD_DIM = 128
A_WIDTH = A_HEADS * HEAD_DIM
B_WIDTH = D_MODEL
B_BLOCKS = 16
B_BLOCK_DIM = B_WIDTH // B_BLOCKS
CONV_W = 4
LRU_C = 8.0
C_HEADS = 4
C_WIDTH = C_HEADS * HEAD_DIM
assert A_HEADS == C_HEADS
HGRN_CHUNK = 64
IN_COLS = 4 * A_WIDTH + 2 * B_WIDTH + 2 * C_WIDTH + 3 * D_MODEL

_QA, _FA, _VA, _GA = 0, A_WIDTH, 2 * A_WIDTH, 3 * A_WIDTH
_XB = 4 * A_WIDTH
_GB = _XB + B_WIDTH
_QC = _GB + B_WIDTH
_GC = _QC + C_WIDTH
_ZA = _GC + C_WIDTH
_ZB = _ZA + D_MODEL
_ZC = _ZB + D_MODEL

MXU_TILE = 256
LRU_GROUPS = B_WIDTH // MXU_TILE
SUBLANES = 8
CTX_ROWS = CONV_W - 1

PROMPT_TILE = 256
Z_BLK = 1024
Z_BLOCKS = IN_COLS // Z_BLK
VMEM_LIMIT_BYTES = 60 * 1024 * 1024


def _rms(x, g):
    return x * lax.rsqrt(jnp.mean(x * x, axis=-1, keepdims=True) + EPS) * g


def _wt(ref_or_val):
    return pltpu.bitcast(ref_or_val, bf16)


def _dot(a, b):
    return jnp.dot(a.astype(bf16), b.astype(bf16), preferred_element_type=f32)


def _dot_nt(a, b):
    return lax.dot_general(a.astype(bf16), b.astype(bf16), (((1,), (1,)), ((), ())),
                           preferred_element_type=f32)


def _dot_tn(a, b):
    return lax.dot_general(a.astype(bf16), b.astype(bf16), (((0,), (0,)), ((), ())),
                           preferred_element_type=f32)


def _silu(x):
    return x * jax.nn.sigmoid(x)


KV_ROWS = (N_MEM * C_HEADS, HEAD_DIM)


def _head_rows(hd):
    return pl.ds(hd, N_MEM, stride=C_HEADS)


def _vreg_groups(x):
    rows, width = x.shape
    return x.reshape(rows // SUBLANES, SUBLANES, width)


def _cumsum_rows(x):
    rows = x.shape[0]
    x3 = _vreg_groups(x)
    sub = lax.broadcasted_iota(jnp.int32, x3.shape, 1)
    d = 1
    while d < SUBLANES:
        x3 = x3 + jnp.where(sub >= d, pltpu.roll(x3, d, 1), 0.0)
        d *= 2
    out, carry = [], None
    for g in range(rows // SUBLANES):
        cur = x3[g] if carry is None else x3[g] + carry
        carry = cur[SUBLANES - 1:SUBLANES, :]
        out.append(cur)
    return jnp.concatenate(out, axis=0)


def _linear_scan_rows(a, u, carry):
    rows = a.shape[0]
    a3, u3 = _vreg_groups(a), _vreg_groups(u)
    sub = lax.broadcasted_iota(jnp.int32, a3.shape, 1)
    d = 1
    while d < SUBLANES:
        keep = sub >= d
        u3 = a3 * jnp.where(keep, pltpu.roll(u3, d, 1), 0.0) + u3
        a3 = a3 * jnp.where(keep, pltpu.roll(a3, d, 1), 1.0)
        d *= 2
    out = []
    for g in range(rows // SUBLANES):
        cur = u3[g] + a3[g] * carry
        carry = cur[SUBLANES - 1:SUBLANES, :]
        out.append(cur)
    return jnp.concatenate(out, axis=0), carry


class _ZBuf:
    def __init__(self, refs):
        self.refs = refs

    def cols(self, c0, width, rows=slice(None)):
        blk, off = divmod(c0, Z_BLK)
        assert off + width <= Z_BLK
        return self.refs[blk][rows, off:off + width]


def _project_in_blocks(x, z, w):
    h = _rms(x, w["g_mix"][...]).astype(bf16)

    def block(c0):
        blk, off = divmod(c0, Z_BLK)
        z.refs[blk][:, off:off + MXU_TILE] = jnp.dot(h, _wt(w["w_in"][:, c0:c0 + MXU_TILE]),
                                                     preferred_element_type=f32)

    return [functools.partial(block, c0) for c0 in range(0, IN_COLS, MXU_TILE)]


def _project_in(x, z, w):
    for block in _project_in_blocks(x, z, w):
        block()


def _interleave(stages, blocks):
    blocks = list(blocks)
    while True:
        try:
            n = next(stages)
        except StopIteration as done:
            result = done.value
            break
        for _ in range(min(n, len(blocks))):
            blocks.pop(0)()
    for block in blocks:
        block()
    return result


def _mix(*args, **kwargs):
    return _interleave(_mix_stages(*args, **kwargs), [])


def _mix_stages(x, z, oa_ref, xpad_ref, kv, states, w, *, seg, chunk, first_rows_start):
    rows = x.shape[0]
    nseg = rows // seg
    nchunk = seg // chunk
    chunk_rows = [[slice(s * seg + c * chunk, s * seg + (c + 1) * chunk) for c in range(nchunk)]
                  for s in range(nseg)]
    all_chunks = [rs for per_seg in chunk_rows for rs in per_seg]
    head_sl = [slice(hd * HEAD_DIM, (hd + 1) * HEAD_DIM) for hd in range(A_HEADS)]
    cat_rows = lambda parts: parts[0] if len(parts) == 1 else jnp.concatenate(parts, axis=0)

    lg = w["lb_logits"][...]
    l0, l1 = lg[0:1, :], lg[1:2, :]
    lmax = jnp.maximum(l0, l1)
    e0, e1 = jnp.exp(l0 - lmax), jnp.exp(l1 - lmax)
    lb = e0 / (e0 + e1)

    qg, kg, v, kd, decay = [], [], [], [], []
    for rs in all_chunks:
        yield 1
        f = lb + (1.0 - lb) * jax.nn.sigmoid(z.cols(_FA, A_WIDTH, rs))
        b = _cumsum_rows(jnp.log(f))
        k = 1.0 - f
        qg.append(_silu(z.cols(_QA, A_WIDTH, rs)) * jnp.exp(b))
        kg.append(k * jnp.exp(-b))
        v.append(z.cols(_VA, A_WIDTH, rs))
        b_last = b[chunk - 1:chunk, :]
        kd.append(k * jnp.exp(b_last - b))
        decay.append(jnp.exp(b_last))
    qg_all, kg_all, v_all = cat_rows(qg), cat_rows(kg), cat_rows(v)

    scores = [_dot_nt(qg_all[:, sl], kg_all[:, sl]) for sl in head_sl]
    st_in = [[[states[s][0][hd]] for hd in range(A_HEADS)] for s in range(nseg)]
    for s in range(nseg):
        for c in range(nchunk):
            i = s * nchunk + c
            for hd, sl in enumerate(head_sl):
                st_in[s][hd].append(st_in[s][hd][c] * decay[i][:, sl] + _dot_tn(v[i][:, sl], kd[i][:, sl]))
    new_st = [[st_in[s][hd][nchunk] for hd in range(A_HEADS)] for s in range(nseg)]
    yield 1
    tt = lax.broadcasted_iota(jnp.int32, (rows, rows), 0)
    ss = lax.broadcasted_iota(jnp.int32, (rows, rows), 1)
    shift = chunk.bit_length() - 1
    causal = ((tt >> shift) == (ss >> shift)) & (ss <= tt)
    o_intra = [_dot(jnp.where(causal, scores[hd], 0.0), v_all[:, sl]) for hd, sl in enumerate(head_sl)]
    yield 1
    for hd, sl in enumerate(head_sl):
        for s in range(nseg):
            for c, rs in enumerate(chunk_rows[s]):
                oa_ref[rs, sl] = o_intra[hd][rs, :] + _dot_nt(qg[s * nchunk + c][:, sl], st_in[s][hd][c])

    g_a = w["g_a_out"][...]
    a_in = []
    for i, rs in enumerate(all_chunks):
        yield i % 2
        normed = jnp.concatenate([_rms(oa_ref[rs, sl], g_a[:, sl]) for sl in head_sl], axis=-1)
        a_in.append(normed * _silu(z.cols(_GA, A_WIDTH, rs)))
    pa = _dot(cat_rows(a_in), _wt(w["w_a_down"][...]))

    pad = SUBLANES
    for s in range(nseg):
        xpad_ref[s * pad:(s + 1) * pad, :] = jnp.zeros((pad, B_WIDTH), f32)
        xpad_ref[(s + 1) * pad - CTX_ROWS:(s + 1) * pad, :] = states[s][1]
    w_conv = w["w_conv"][...]
    sub = lax.broadcasted_iota(jnp.int32, (chunk // SUBLANES, SUBLANES, B_WIDTH), 1)
    xc, new_ctx = [], []
    for s in range(nseg):
        for c, rs in enumerate(chunk_rows[s]):
            yield 1
            if c == 0:
                ext = jnp.concatenate([xpad_ref[s * pad:(s + 1) * pad, :], z.cols(_XB, B_WIDTH, rs)], axis=0)
            else:
                ext = z.cols(_XB, B_WIDTH, slice(rs.start - pad, rs.stop))
            ext = _vreg_groups(ext)
            acc = w_conv[CONV_W - 1:CONV_W, :] * ext[1:]
            for j in range(1, CONV_W):
                rolled = pltpu.roll(ext, j, 1)
                shifted = jnp.where(sub >= j, rolled[1:], rolled[:-1])
                acc = acc + w_conv[CONV_W - 1 - j:CONV_W - j, :] * shifted
            xc.append(w["b_conv"][...] + acc.reshape(chunk, B_WIDTH))
        new_ctx.append(z.cols(_XB, B_WIDTH, slice((s + 1) * seg - CTX_ROWS, (s + 1) * seg)))

    xc_b = cat_rows(xc).astype(bf16)
    r_pre, i_pre = [], []
    for g in range(LRU_GROUPS):
        gs = slice(g * MXU_TILE, (g + 1) * MXU_TILE)
        r_pre.append(jnp.dot(xc_b[:, gs], _wt(w["w_lru_r"][g]), preferred_element_type=f32))
        i_pre.append(jnp.dot(xc_b[:, gs], _wt(w["w_lru_i"][g]), preferred_element_type=f32))
    r_pre, i_pre = jnp.concatenate(r_pre, axis=-1), jnp.concatenate(i_pre, axis=-1)
    neg_lam = -w["lru_lambda"][...]
    softplus = jnp.maximum(neg_lam, 0.0) + jnp.log1p(jnp.exp(-jnp.abs(neg_lam)))
    a_l, u_l = [], []
    for s in range(nseg):
        for c, rs in enumerate(chunk_rows[s]):
            yield 2
            r = jax.nn.sigmoid(r_pre[rs, :] + w["b_lru_r"][...])
            ig = jax.nn.sigmoid(i_pre[rs, :] + w["b_lru_i"][...])
            log_a = -LRU_C * r * softplus
            a = jnp.exp(log_a)
            mult = jnp.sqrt(-jnp.tanh(log_a) * (a * a + 1.0))
            if first_rows_start is not None and c == 0:
                first_row = lax.broadcasted_iota(jnp.int32, mult.shape, 0) == 0
                mult = jnp.where(first_row & first_rows_start, 1.0, mult)
            a_l.append(a)
            u_l.append(mult * ig * xc[s * nchunk + c])
    hb_gated, new_hl = [], []
    for s in range(nseg):
        carry = states[s][2]
        for c, rs in enumerate(chunk_rows[s]):
            yield 1
            i = s * nchunk + c
            hb, carry = _linear_scan_rows(a_l[i], u_l[i], carry)
            hb_gated.append(hb * _silu(z.cols(_GB, B_WIDTH, rs)))
        new_hl.append(carry)
    pb = _dot(cat_rows(hb_gated), _wt(w["w_b_down"][...]))

    scale = HEAD_DIM ** -0.5
    mem = [[kv(s, hd) for hd in range(C_HEADS)] for s in range(nseg)]
    seg_rows = [slice(s * seg, (s + 1) * seg) for s in range(nseg)]
    sc = [[_dot_nt(z.cols(_QC + hd * HEAD_DIM, HEAD_DIM, seg_rows[s]), mem[s][hd][0]) * scale
           for hd in range(C_HEADS)] for s in range(nseg)]
    pr = []
    for s in range(nseg):
        pr.append([])
        for hd in range(C_HEADS):
            yield 1
            p = jnp.exp(sc[s][hd] - jnp.max(sc[s][hd], axis=-1, keepdims=True))
            pr[s].append(p / jnp.sum(p, axis=-1, keepdims=True))
    oc = cat_rows([jnp.concatenate([_dot(pr[s][hd], mem[s][hd][1]) for hd in range(C_HEADS)], axis=-1)
                   for s in range(nseg)])
    yield 1
    pc = _dot(oc * _silu(z.cols(_GC, C_WIDTH)), _wt(w["w_c_down"][...]))

    merged = []
    for rs in all_chunks:
        yield 1
        merged.append(jax.nn.sigmoid(z.cols(_ZA, D_MODEL, rs)) * pa[rs, :]
                      + jax.nn.sigmoid(z.cols(_ZB, D_MODEL, rs)) * pb[rs, :]
                      + jax.nn.sigmoid(z.cols(_ZC, D_MODEL, rs)) * pc[rs, :])
    y = x + _dot(cat_rows(merged), _wt(w["w_out"][...]))
    y = _rms(y, w["g_final"][...])
    new_states = [(new_st[s], new_ctx[s], new_hl[s]) for s in range(nseg)]
    return y, new_states


_WEIGHT_NAMES = ("g_mix", "w_in", "lb_logits", "g_a_out", "w_a_down", "w_conv", "b_conv", "w_lru_r", "b_lru_r",
                 "w_lru_i", "b_lru_i", "lru_lambda", "w_b_down", "w_c_down", "w_out", "g_final")
_NW = len(_WEIGHT_NAMES)


def _store_seq_state(conv_ref, lru_ref, seq_idx, ctx, hl):
    for r in range(CTX_ROWS):
        conv_ref[r, pl.ds(seq_idx, 1), :] = ctx[r:r + 1, :]
    lru_ref[pl.ds(seq_idx, 1), :] = hl


def _prompt_kernel(steps_per_seq, *refs):
    x_ref, xn_ref, mk_ref, mv_ref = refs[:4]
    w = dict(zip(_WEIGHT_NAMES, refs[4:4 + _NW]))
    y_ref, hgrn_ref, conv_ref, lru_ref = refs[4 + _NW:8 + _NW]
    scratch = refs[8 + _NW:]
    st_ref, ctx_ref, hl_ref = scratch[:3]
    z_even, z_odd = _ZBuf(scratch[3:3 + Z_BLOCKS]), _ZBuf(scratch[3 + Z_BLOCKS:3 + 2 * Z_BLOCKS])
    oa_refs = scratch[3 + 2 * Z_BLOCKS:5 + 2 * Z_BLOCKS]
    xpad_refs = scratch[5 + 2 * Z_BLOCKS:7 + 2 * Z_BLOCKS]
    j = pl.program_id(0)
    tile = PROMPT_TILE
    seq_start = (j % steps_per_seq) == 0

    @pl.when(j == 0)
    def _():
        _project_in(x_ref[0:tile, :], z_even, w)

    @pl.when(seq_start)
    def _():
        st_ref[...] = jnp.zeros_like(st_ref)
        ctx_ref[...] = jnp.zeros_like(ctx_ref)
        hl_ref[...] = jnp.zeros_like(hl_ref)

    kv = lambda s, hd: (mk_ref[0, _head_rows(hd), :], mv_ref[0, _head_rows(hd), :])
    states = [([st_ref[hd] for hd in range(A_HEADS)], ctx_ref[...], hl_ref[...])]

    y, states = _interleave(
        _mix_stages(x_ref[0:tile, :], z_even, oa_refs[0], xpad_refs[0], kv, states, w,
                    seg=tile, chunk=HGRN_CHUNK, first_rows_start=seq_start),
        _project_in_blocks(x_ref[tile:2 * tile, :], z_odd, w))
    y_ref[0:tile, :] = y

    y, states = _interleave(
        _mix_stages(x_ref[tile:2 * tile, :], z_odd, oa_refs[1], xpad_refs[1], kv, states, w,
                    seg=tile, chunk=HGRN_CHUNK, first_rows_start=None),
        _project_in_blocks(xn_ref[...], z_even, w))
    y_ref[tile:2 * tile, :] = y

    st, ctx, hl = states[0]
    for hd in range(A_HEADS):
        st_ref[hd] = st[hd]
    ctx_ref[...] = ctx
    hl_ref[...] = hl

    @pl.when((j % steps_per_seq) == steps_per_seq - 1)
    def _():
        for hd in range(A_HEADS):
            hgrn_ref[0, hd] = st[hd].T
        _store_seq_state(conv_ref, lru_ref, j // steps_per_seq, ctx, hl)


def _sample_kernel(nseq, seg, *refs):
    x_ref, mk_ref, mv_ref, hgrn_in, conv_in, lru_in = refs[:6]
    w = dict(zip(_WEIGHT_NAMES, refs[6:6 + _NW]))
    y_ref, hgrn_ref, conv_ref, lru_ref = refs[6 + _NW:10 + _NW]
    scratch = refs[10 + _NW:]
    z = _ZBuf(scratch[:Z_BLOCKS])
    oa_ref, xpad_ref = scratch[Z_BLOCKS:]
    states = [([hgrn_in[s, hd].T for hd in range(A_HEADS)],
               jnp.concatenate([conv_in[r, s:s + 1, :] for r in range(CTX_ROWS)], axis=0),
               lru_in[s:s + 1, :]) for s in range(nseq)]
    _project_in(x_ref[...], z, w)
    y, new_states = _mix(
        x_ref[...], z, oa_ref, xpad_ref,
        lambda s, hd: (mk_ref[s, _head_rows(hd), :], mv_ref[s, _head_rows(hd), :]), states, w,
        seg=seg, chunk=min(HGRN_CHUNK, seg), first_rows_start=None)
    y_ref[...] = y
    for s in range(nseq):
        st, ctx, hl = new_states[s]
        for hd in range(A_HEADS):
            hgrn_ref[s, hd] = st[hd].T
        _store_seq_state(conv_ref, lru_ref, s, ctx, hl)


def _const_spec(shape):
    nd = len(shape)
    return pl.BlockSpec(shape, lambda *_: (0,) * nd, pipeline_mode=pl.Buffered(1))


PREP_STEPS = 8
_DENSE_WEIGHTS = ("w_in", "w_a_down", "w_b_down", "w_c_down", "w_out")
_U32 = jnp.uint32


def _prep_kernel(*refs):
    nd = len(_DENSE_WEIGHTS)
    dense_in, (lru_r_in, lru_i_in) = refs[:nd], refs[nd:nd + 2]
    mem_ref, g_mem_ref, wk_in, wv_in = refs[nd + 2:nd + 6]
    outs = refs[nd + 6:]
    dense_out, (lru_r_out, lru_i_out), (k_ref, v_ref) = outs[:nd], outs[nd:nd + 2], outs[nd + 2:nd + 4]
    tile_ref, wk_ref, wv_ref = outs[nd + 4:]
    for src, dst in zip(dense_in, dense_out):
        dst[...] = pltpu.bitcast(src[...].astype(bf16), _U32)

    @pl.when(pl.program_id(0) == 0)
    def _():
        wk_ref[...] = pltpu.bitcast(wk_in[...].astype(bf16), _U32)
        wv_ref[...] = pltpu.bitcast(wv_in[...].astype(bf16), _U32)
        per = MXU_TILE // B_BLOCK_DIM
        for src, dst in ((lru_r_in, lru_r_out), (lru_i_in, lru_i_out)):
            for g in range(LRU_GROUPS):
                tile_ref[...] = jnp.zeros_like(tile_ref)
                for p in range(per):
                    lo = p * B_BLOCK_DIM
                    tile_ref[lo:lo + B_BLOCK_DIM, lo:lo + B_BLOCK_DIM] = src[g * per + p]
                dst[g] = pltpu.bitcast(tile_ref[...].astype(bf16), _U32)

    hm = _rms(mem_ref[0], g_mem_ref[...]).astype(bf16)
    k = jnp.dot(hm, _wt(wk_ref[...]), preferred_element_type=f32)
    v = jnp.dot(hm, _wt(wv_ref[...]), preferred_element_type=f32)
    for hd in range(C_HEADS):
        sl = slice(hd * HEAD_DIM, (hd + 1) * HEAD_DIM)
        k_ref[0, _head_rows(hd), :] = k[:, sl]
        v_ref[0, _head_rows(hd), :] = v[:, sl]


def _prep_weights(dense, lru_r, lru_i, mem, g_mem, w_mem_k, w_mem_v):
    bsz = mem.shape[0]
    assert bsz == PREP_STEPS
    in_specs, out_specs, out_shape = [], [], []
    for wm in dense:
        k, n = wm.shape
        assert k % (4 * SUBLANES * PREP_STEPS) == 0
        in_specs.append(pl.BlockSpec((k // PREP_STEPS, n), lambda i: (i, 0)))
        out_specs.append(pl.BlockSpec((k // (2 * PREP_STEPS), n), lambda i: (i, 0)))
        out_shape.append(jax.ShapeDtypeStruct((k // 2, n), _U32))
    blk = (B_BLOCKS, B_BLOCK_DIM, B_BLOCK_DIM)
    tiles = (LRU_GROUPS, MXU_TILE // 2, MXU_TILE)
    in_specs += [pl.BlockSpec(blk, lambda i: (0, 0, 0))] * 2
    out_specs += [pl.BlockSpec(tiles, lambda i: (0, 0, 0))] * 2
    out_shape += [jax.ShapeDtypeStruct(tiles, _U32)] * 2
    in_specs += [pl.BlockSpec((1, N_MEM, D_MODEL), lambda i: (i, 0, 0)), _const_spec((1, D_MODEL)),
                 _const_spec((D_MODEL, C_WIDTH)), _const_spec((D_MODEL, C_WIDTH))]
    out_specs += [pl.BlockSpec((1,) + KV_ROWS, lambda i: (i, 0, 0))] * 2
    out_shape += [jax.ShapeDtypeStruct((bsz,) + KV_ROWS, f32)] * 2
    outs = pl.pallas_call(
        _prep_kernel, grid=(PREP_STEPS,), in_specs=in_specs, out_specs=out_specs, out_shape=out_shape,
        scratch_shapes=[pltpu.VMEM((MXU_TILE, MXU_TILE), f32),
                        pltpu.VMEM((D_MODEL // 2, C_WIDTH), _U32), pltpu.VMEM((D_MODEL // 2, C_WIDTH), _U32)],
        compiler_params=pltpu.CompilerParams(vmem_limit_bytes=VMEM_LIMIT_BYTES,
                                             dimension_semantics=("arbitrary",)),
        name="prep_weights",
    )(*dense, lru_r, lru_i, mem, g_mem, w_mem_k, w_mem_v)
    nd = len(dense)
    return outs[:nd], outs[nd], outs[nd + 1], outs[nd + 2], outs[nd + 3]


def kernel(x_prompt, x_sample, mem_prompt, cache_mem_k, cache_mem_v, state_hgrn, state_conv, state_lru, g_mix, w_in, lb_logits, g_a_out, w_a_down, w_conv, b_conv, w_lru_r, b_lru_r, w_lru_i, b_lru_i, lru_lambda, w_b_down, g_mem, w_mem_k, w_mem_v, w_c_down, w_out, g_final):
    bsz, seq, _ = x_prompt.shape
    dec_b, dec_seq, _ = x_sample.shape
    assert g_mix.shape[0] == 1, "single-layer stack only"
    assert seq % (2 * PROMPT_TILE) == 0 and PROMPT_TILE % HGRN_CHUNK == 0

    row = lambda a: a.reshape(1, -1).astype(f32)
    dense = dict(w_in=w_in[0], w_a_down=w_a_down[0], w_b_down=w_b_down[0], w_c_down=w_c_down[0], w_out=w_out[0])
    packed, lru_r_tiles, lru_i_tiles, mk, mv = _prep_weights(
        [dense[n] for n in _DENSE_WEIGHTS], w_lru_r[0], w_lru_i[0], mem_prompt, row(g_mem[0]), w_mem_k[0], w_mem_v[0])
    packed = dict(zip(_DENSE_WEIGHTS, packed))
    weights = dict(
        g_mix=row(g_mix[0]), w_in=packed["w_in"], lb_logits=lb_logits.astype(f32), g_a_out=row(g_a_out[0]),
        w_a_down=packed["w_a_down"], w_conv=w_conv[0].astype(f32), b_conv=row(b_conv[0]),
        w_lru_r=lru_r_tiles, b_lru_r=row(b_lru_r[0]), w_lru_i=lru_i_tiles, b_lru_i=row(b_lru_i[0]),
        lru_lambda=row(lru_lambda[0]), w_b_down=packed["w_b_down"], w_c_down=packed["w_c_down"],
        w_out=packed["w_out"], g_final=row(g_final))
    wlist = [weights[n] for n in _WEIGHT_NAMES]
    wspecs = [_const_spec(a.shape) for a in wlist]

    tile = PROMPT_TILE
    n_tiles = bsz * seq // tile
    steps_per_seq = seq // (2 * tile)
    zbuf = [pltpu.VMEM((tile, Z_BLK), f32)] * Z_BLOCKS
    y_p, hgrn_p, conv_p, lru_p = pl.pallas_call(
        functools.partial(_prompt_kernel, steps_per_seq),
        grid=(n_tiles // 2,),
        in_specs=[pl.BlockSpec((2 * tile, D_MODEL), lambda j: (j, 0)),
                  pl.BlockSpec((tile, D_MODEL), lambda j: (jnp.minimum(2 * j + 2, n_tiles - 1), 0)),
                  pl.BlockSpec((1,) + KV_ROWS, lambda j: (j // steps_per_seq, 0, 0)),
                  pl.BlockSpec((1,) + KV_ROWS, lambda j: (j // steps_per_seq, 0, 0))] + wspecs,
        out_specs=[pl.BlockSpec((2 * tile, D_MODEL), lambda j: (j, 0)),
                   pl.BlockSpec((1, A_HEADS, HEAD_DIM, HEAD_DIM), lambda j: (j // steps_per_seq, 0, 0, 0)),
                   pl.BlockSpec((CTX_ROWS, bsz, B_WIDTH), lambda j: (0, 0, 0)),
                   pl.BlockSpec((bsz, B_WIDTH), lambda j: (0, 0))],
        out_shape=[jax.ShapeDtypeStruct((bsz * seq, D_MODEL), f32),
                   jax.ShapeDtypeStruct((bsz, A_HEADS, HEAD_DIM, HEAD_DIM), f32),
                   jax.ShapeDtypeStruct((CTX_ROWS, bsz, B_WIDTH), f32),
                   jax.ShapeDtypeStruct((bsz, B_WIDTH), f32)],
        scratch_shapes=[pltpu.VMEM((A_HEADS, HEAD_DIM, HEAD_DIM), f32),
                        pltpu.VMEM((CTX_ROWS, B_WIDTH), f32),
                        pltpu.VMEM((1, B_WIDTH), f32)] + zbuf + zbuf
                       + [pltpu.VMEM((tile, A_WIDTH), f32)] * 2
                       + [pltpu.VMEM((SUBLANES, B_WIDTH), f32)] * 2,
        compiler_params=pltpu.CompilerParams(vmem_limit_bytes=VMEM_LIMIT_BYTES,
                                             dimension_semantics=("arbitrary",)),
        name="prompt_layer",
    )(x_prompt.reshape(bsz * seq, D_MODEL), x_prompt.reshape(bsz * seq, D_MODEL), mk, mv, *wlist)
    y_p = y_p.reshape(bsz, seq, D_MODEL)

    rows = dec_b * dec_seq
    full = lambda shape: pl.BlockSpec(shape, lambda *_: (0,) * len(shape))
    y_s, hgrn_s, conv_s, lru_s = pl.pallas_call(
        functools.partial(_sample_kernel, dec_b, dec_seq),
        grid=(1,),
        in_specs=[full((rows, D_MODEL)), full((dec_b,) + KV_ROWS), full((dec_b,) + KV_ROWS),
                  full((dec_b, A_HEADS, HEAD_DIM, HEAD_DIM)), full((CTX_ROWS, dec_b, B_WIDTH)),
                  full((dec_b, B_WIDTH))] + wspecs,
        out_specs=[full((rows, D_MODEL)), full((dec_b, A_HEADS, HEAD_DIM, HEAD_DIM)),
                   full((CTX_ROWS, dec_b, B_WIDTH)), full((dec_b, B_WIDTH))],
        out_shape=[jax.ShapeDtypeStruct((rows, D_MODEL), f32),
                   jax.ShapeDtypeStruct((dec_b, A_HEADS, HEAD_DIM, HEAD_DIM), f32),
                   jax.ShapeDtypeStruct((CTX_ROWS, dec_b, B_WIDTH), f32),
                   jax.ShapeDtypeStruct((dec_b, B_WIDTH), f32)],
        scratch_shapes=[pltpu.VMEM((rows, Z_BLK), f32)] * Z_BLOCKS
                       + [pltpu.VMEM((rows, A_WIDTH), f32),
                        pltpu.VMEM((dec_b * SUBLANES, B_WIDTH), f32)],
        compiler_params=pltpu.CompilerParams(vmem_limit_bytes=VMEM_LIMIT_BYTES),
        name="sample_layer",
    )(x_sample.reshape(rows, D_MODEL), cache_mem_k.reshape((dec_b,) + KV_ROWS),
      cache_mem_v.reshape((dec_b,) + KV_ROWS), state_hgrn[0], jnp.swapaxes(state_conv[0], 0, 1),
      state_lru[0], *wlist)

    return (y_p, y_s.reshape(dec_b, dec_seq, D_MODEL), hgrn_p[None], jnp.swapaxes(conv_p, 0, 1)[None],
            lru_p[None], mk.reshape(1, bsz, N_MEM, C_HEADS, HEAD_DIM),
            mv.reshape(1, bsz, N_MEM, C_HEADS, HEAD_DIM), hgrn_s[None], jnp.swapaxes(conv_s, 0, 1)[None],
            lru_s[None])
```

```python
import functools

import jax
import jax.numpy as jnp
from jax import lax
from jax.experimental import pallas as pl
from jax.experimental.pallas import tpu as pltpu

f32 = jnp.float32
bf16 = jnp.bfloat16

D_MODEL = 1024
N_MEM = 256
EPS = 1e-6
A_HEADS = 4
HEAD_DIM = 128
A_WIDTH = A_HEADS * HEAD_DIM
B_WIDTH = D_MODEL
B_BLOCKS = 16
B_BLOCK_DIM = B_WIDTH // B_BLOCKS
CONV_W = 4
LRU_C = 8.0
C_HEADS = 4
C_WIDTH = C_HEADS * HEAD_DIM
assert A_HEADS == C_HEADS
HGRN_CHUNK = 64
IN_COLS = 4 * A_WIDTH + 2 * B_WIDTH + 2 * C_WIDTH + 3 * D_MODEL

_QA, _FA, _VA, _GA = 0, A_WIDTH, 2 * A_WIDTH, 3 * A_WIDTH
_XB = 4 * A_WIDTH
_GB = _XB + B_WIDTH
_QC = _GB + B_WIDTH
_GC = _QC + C_WIDTH
_ZA = _GC + C_WIDTH
_ZB = _ZA + D_MODEL
_ZC = _ZB + D_MODEL

MXU_TILE = 256
LRU_GROUPS = B_WIDTH // MXU_TILE
SUBLANES = 8
CTX_ROWS = CONV_W - 1

PROMPT_TILE = 256
Z_BLK = 1024
Z_BLOCKS = IN_COLS // Z_BLK
VMEM_LIMIT_BYTES = 60 * 1024 * 1024


def _rms(x, g):
    return x * lax.rsqrt(jnp.mean(x * x, axis=-1, keepdims=True) + EPS) * g


def _wt(ref_or_val):
    return pltpu.bitcast(ref_or_val, bf16)


def _dot(a, b):
    return jnp.dot(a.astype(bf16), b.astype(bf16), preferred_element_type=f32)


def _dot_nt(a, b):
    return lax.dot_general(a.astype(bf16), b.astype(bf16), (((1,), (1,)), ((), ())),
                           preferred_element_type=f32)


def _dot_tn(a, b):
    return lax.dot_general(a.astype(bf16), b.astype(bf16), (((0,), (0,)), ((), ())),
                           preferred_element_type=f32)


def _silu(x):
    return x * jax.nn.sigmoid(x)


KV_ROWS = (N_MEM * C_HEADS, HEAD_DIM)


def _head_rows(hd):
    return pl.ds(hd, N_MEM, stride=C_HEADS)


def _vreg_groups(x):
    rows, width = x.shape
    return x.reshape(rows // SUBLANES, SUBLANES, width)


def _cumprod_rows(x):
    rows = x.shape[0]
    x3 = _vreg_groups(x)
    sub = lax.broadcasted_iota(jnp.int32, x3.shape, 1)
    d = 1
    while d < SUBLANES:
        x3 = x3 * jnp.where(sub >= d, pltpu.roll(x3, d, 1), 1.0)
        d *= 2
    out, carry = [], None
    for g in range(rows // SUBLANES):
        cur = x3[g] if carry is None else x3[g] * carry
        carry = cur[SUBLANES - 1:SUBLANES, :]
        out.append(cur)
    return jnp.concatenate(out, axis=0)


def _linear_scan_rows(a, u, carry):
    rows = a.shape[0]
    a3, u3 = _vreg_groups(a), _vreg_groups(u)
    sub = lax.broadcasted_iota(jnp.int32, a3.shape, 1)
    d = 1
    while d < SUBLANES:
        keep = sub >= d
        u3 = a3 * jnp.where(keep, pltpu.roll(u3, d, 1), 0.0) + u3
        a3 = a3 * jnp.where(keep, pltpu.roll(a3, d, 1), 1.0)
        d *= 2
    out = []
    for g in range(rows // SUBLANES):
        cur = u3[g] + a3[g] * carry
        carry = cur[SUBLANES - 1:SUBLANES, :]
        out.append(cur)
    return jnp.concatenate(out, axis=0), carry


class _ZBuf:
    def __init__(self, refs):
        self.refs = refs

    def cols(self, c0, width, rows=slice(None)):
        blk, off = divmod(c0, Z_BLK)
        assert off + width <= Z_BLK
        return self.refs[blk][rows, off:off + width]


def _project_in_blocks(x, z, w):
    h = _rms(x, w["g_mix"][...]).astype(bf16)

    def block(c0):
        blk, off = divmod(c0, Z_BLK)
        z.refs[blk][:, off:off + MXU_TILE] = jnp.dot(h, _wt(w["w_in"][:, c0:c0 + MXU_TILE]),
                                                     preferred_element_type=f32)

    return [functools.partial(block, c0) for c0 in range(0, IN_COLS, MXU_TILE)]


def _project_in(x, z, w):
    for block in _project_in_blocks(x, z, w):
        block()


def _interleave(stages, blocks):
    blocks = list(blocks)
    while True:
        try:
            n = next(stages)
        except StopIteration as done:
            result = done.value
            break
        for _ in range(min(n, len(blocks))):
            blocks.pop(0)()
    for block in blocks:
        block()
    return result


def _mix(*args, **kwargs):
    return _interleave(_mix_stages(*args, **kwargs), [])


def _mix_stages(x, z, oa_ref, xpad_ref, kv, states, w, *, seg, chunk, first_rows_start):
    rows = x.shape[0]
    nseg = rows // seg
    nchunk = seg // chunk
    chunk_rows = [[slice(s * seg + c * chunk, s * seg + (c + 1) * chunk) for c in range(nchunk)]
                  for s in range(nseg)]
    all_chunks = [rs for per_seg in chunk_rows for rs in per_seg]
    head_sl = [slice(hd * HEAD_DIM, (hd + 1) * HEAD_DIM) for hd in range(A_HEADS)]
    cat_rows = lambda parts: parts[0] if len(parts) == 1 else jnp.concatenate(parts, axis=0)

    lg = w["lb_logits"][...]
    l0, l1 = lg[0:1, :], lg[1:2, :]
    lmax = jnp.maximum(l0, l1)
    e0, e1 = jnp.exp(l0 - lmax), jnp.exp(l1 - lmax)
    lb = e0 / (e0 + e1)

    qg, kg, v, kd, decay = [], [], [], [], []
    for rs in all_chunks:
        yield 1
        f = lb + (1.0 - lb) * jax.nn.sigmoid(z.cols(_FA, A_WIDTH, rs))
        p = _cumprod_rows(f)
        inv_p = 1.0 / p
        k = 1.0 - f
        qg.append(_silu(z.cols(_QA, A_WIDTH, rs)) * p)
        kg.append(k * inv_p)
        v.append(z.cols(_VA, A_WIDTH, rs))
        p_last = p[chunk - 1:chunk, :]
        kd.append(k * (p_last * inv_p))
        decay.append(p_last)
    qg_all, kg_all, v_all = cat_rows(qg), cat_rows(kg), cat_rows(v)

    scores = [_dot_nt(qg_all[:, sl], kg_all[:, sl]) for sl in head_sl]
    st_in = [[[states[s][0][hd]] for hd in range(A_HEADS)] for s in range(nseg)]
    for s in range(nseg):
        for c in range(nchunk):
            i = s * nchunk + c
            for hd, sl in enumerate(head_sl):
                st_in[s][hd].append(st_in[s][hd][c] * decay[i][:, sl] + _dot_tn(v[i][:, sl], kd[i][:, sl]))
    new_st = [[st_in[s][hd][nchunk] for hd in range(A_HEADS)] for s in range(nseg)]
    yield 1
    tt = lax.broadcasted_iota(jnp.int32, (rows, rows), 0)
    ss = lax.broadcasted_iota(jnp.int32, (rows, rows), 1)
    shift = chunk.bit_length() - 1
    causal = ((tt >> shift) == (ss >> shift)) & (ss <= tt)
    o_intra = [_dot(jnp.where(causal, scores[hd], 0.0), v_all[:, sl]) for hd, sl in enumerate(head_sl)]
    yield 1
    for hd, sl in enumerate(head_sl):
        for s in range(nseg):
            for c, rs in enumerate(chunk_rows[s]):
                oa_ref[rs, sl] = o_intra[hd][rs, :] + _dot_nt(qg[s * nchunk + c][:, sl], st_in[s][hd][c])

    g_a = w["g_a_out"][...]
    a_in = []
    for i, rs in enumerate(all_chunks):
        yield i % 2
        normed = jnp.concatenate([_rms(oa_ref[rs, sl], g_a[:, sl]) for sl in head_sl], axis=-1)
        a_in.append(normed * _silu(z.cols(_GA, A_WIDTH, rs)))
    pa = _dot(cat_rows(a_in), _wt(w["w_a_down"][...]))

    pad = SUBLANES
    for s in range(nseg):
        xpad_ref[s * pad:(s + 1) * pad, :] = jnp.zeros((pad, B_WIDTH), f32)
        xpad_ref[(s + 1) * pad - CTX_ROWS:(s + 1) * pad, :] = states[s][1]
    w_conv = w["w_conv"][...]
    sub = lax.broadcasted_iota(jnp.int32, (chunk // SUBLANES, SUBLANES, B_WIDTH), 1)
    xc, new_ctx = [], []
    for s in range(nseg):
        for c, rs in enumerate(chunk_rows[s]):
            yield 1
            if c == 0:
                ext = jnp.concatenate([xpad_ref[s * pad:(s + 1) * pad, :], z.cols(_XB, B_WIDTH, rs)], axis=0)
            else:
                ext = z.cols(_XB, B_WIDTH, slice(rs.start - pad, rs.stop))
            ext = _vreg_groups(ext)
            acc = w_conv[CONV_W - 1:CONV_W, :] * ext[1:]
            for j in range(1, CONV_W):
                rolled = pltpu.roll(ext, j, 1)
                shifted = jnp.where(sub >= j, rolled[1:], rolled[:-1])
                acc = acc + w_conv[CONV_W - 1 - j:CONV_W - j, :] * shifted
            xc.append(w["b_conv"][...] + acc.reshape(chunk, B_WIDTH))
        new_ctx.append(z.cols(_XB, B_WIDTH, slice((s + 1) * seg - CTX_ROWS, (s + 1) * seg)))

    xc_b = cat_rows(xc).astype(bf16)
    r_pre, i_pre = [], []
    for g in range(LRU_GROUPS):
        gs = slice(g * MXU_TILE, (g + 1) * MXU_TILE)
        r_pre.append(jnp.dot(xc_b[:, gs], _wt(w["w_lru_r"][g]), preferred_element_type=f32))
        i_pre.append(jnp.dot(xc_b[:, gs], _wt(w["w_lru_i"][g]), preferred_element_type=f32))
    r_pre, i_pre = jnp.concatenate(r_pre, axis=-1), jnp.concatenate(i_pre, axis=-1)
    neg_lam = -w["lru_lambda"][...]
    softplus = jnp.maximum(neg_lam, 0.0) + jnp.log1p(jnp.exp(-jnp.abs(neg_lam)))
    a_l, u_l = [], []
    for s in range(nseg):
        for c, rs in enumerate(chunk_rows[s]):
            yield 2
            r = jax.nn.sigmoid(r_pre[rs, :] + w["b_lru_r"][...])
            ig = jax.nn.sigmoid(i_pre[rs, :] + w["b_lru_i"][...])
            log_a = -LRU_C * r * softplus
            a = jnp.exp(log_a)
            mult = jnp.sqrt(-jnp.tanh(log_a) * (a * a + 1.0))
            if first_rows_start is not None and c == 0:
                first_row = lax.broadcasted_iota(jnp.int32, mult.shape, 0) == 0
                mult = jnp.where(first_row & first_rows_start, 1.0, mult)
            a_l.append(a)
            u_l.append(mult * ig * xc[s * nchunk + c])
    hb_gated, new_hl = [], []
    for s in range(nseg):
        carry = states[s][2]
        for c, rs in enumerate(chunk_rows[s]):
            yield 1
            i = s * nchunk + c
            hb, carry = _linear_scan_rows(a_l[i], u_l[i], carry)
            hb_gated.append(hb * _silu(z.cols(_GB, B_WIDTH, rs)))
        new_hl.append(carry)
    pb = _dot(cat_rows(hb_gated), _wt(w["w_b_down"][...]))

    scale = HEAD_DIM ** -0.5
    mem = [[kv(s, hd) for hd in range(C_HEADS)] for s in range(nseg)]
    seg_rows = [slice(s * seg, (s + 1) * seg) for s in range(nseg)]
    sc = [[_dot_nt(z.cols(_QC + hd * HEAD_DIM, HEAD_DIM, seg_rows[s]), mem[s][hd][0]) * scale
           for hd in range(C_HEADS)] for s in range(nseg)]
    pr = []
    for s in range(nseg):
        pr.append([])
        for hd in range(C_HEADS):
            yield 1
            p = jnp.exp(sc[s][hd] - jnp.max(sc[s][hd], axis=-1, keepdims=True))
            pr[s].append(p / jnp.sum(p, axis=-1, keepdims=True))
    oc = cat_rows([jnp.concatenate([_dot(pr[s][hd], mem[s][hd][1]) for hd in range(C_HEADS)], axis=-1)
                   for s in range(nseg)])
    yield 1
    pc = _dot(oc * _silu(z.cols(_GC, C_WIDTH)), _wt(w["w_c_down"][...]))

    merged = []
    for rs in all_chunks:
        yield 1
        merged.append(jax.nn.sigmoid(z.cols(_ZA, D_MODEL, rs)) * pa[rs, :]
                      + jax.nn.sigmoid(z.cols(_ZB, D_MODEL, rs)) * pb[rs, :]
                      + jax.nn.sigmoid(z.cols(_ZC, D_MODEL, rs)) * pc[rs, :])
    y = x + _dot(cat_rows(merged), _wt(w["w_out"][...]))
    y = _rms(y, w["g_final"][...])
    new_states = [(new_st[s], new_ctx[s], new_hl[s]) for s in range(nseg)]
    return y, new_states


_WEIGHT_NAMES = ("g_mix", "w_in", "lb_logits", "g_a_out", "w_a_down", "w_conv", "b_conv", "w_lru_r", "b_lru_r",
                 "w_lru_i", "b_lru_i", "lru_lambda", "w_b_down", "w_c_down", "w_out", "g_final")
_NW = len(_WEIGHT_NAMES)


def _store_seq_state(conv_ref, lru_ref, seq_idx, ctx, hl):
    for r in range(CTX_ROWS):
        conv_ref[r, pl.ds(seq_idx, 1), :] = ctx[r:r + 1, :]
    lru_ref[pl.ds(seq_idx, 1), :] = hl


def _prompt_kernel(steps_per_seq, *refs):
    x_ref, xn_ref, mk_ref, mv_ref = refs[:4]
    w = dict(zip(_WEIGHT_NAMES, refs[4:4 + _NW]))
    y_ref, hgrn_ref, conv_ref, lru_ref = refs[4 + _NW:8 + _NW]
    scratch = refs[8 + _NW:]
    st_ref, ctx_ref, hl_ref = scratch[:3]
    z_even, z_odd = _ZBuf(scratch[3:3 + Z_BLOCKS]), _ZBuf(scratch[3 + Z_BLOCKS:3 + 2 * Z_BLOCKS])
    oa_refs = scratch[3 + 2 * Z_BLOCKS:5 + 2 * Z_BLOCKS]
    xpad_refs = scratch[5 + 2 * Z_BLOCKS:7 + 2 * Z_BLOCKS]
    j = pl.program_id(0)
    tile = PROMPT_TILE
    seq_start = (j % steps_per_seq) == 0

    @pl.when(j == 0)
    def _():
        _project_in(x_ref[0:tile, :], z_even, w)

    @pl.when(seq_start)
    def _():
        st_ref[...] = jnp.zeros_like(st_ref)
        ctx_ref[...] = jnp.zeros_like(ctx_ref)
        hl_ref[...] = jnp.zeros_like(hl_ref)

    kv = lambda s, hd: (mk_ref[0, _head_rows(hd), :], mv_ref[0, _head_rows(hd), :])
    states = [([st_ref[hd] for hd in range(A_HEADS)], ctx_ref[...], hl_ref[...])]

    y, states = _interleave(
        _mix_stages(x_ref[0:tile, :], z_even, oa_refs[0], xpad_refs[0], kv, states, w,
                    seg=tile, chunk=HGRN_CHUNK, first_rows_start=seq_start),
        _project_in_blocks(x_ref[tile:2 * tile, :], z_odd, w))
    y_ref[0:tile, :] = y

    y, states = _interleave(
        _mix_stages(x_ref[tile:2 * tile, :], z_odd, oa_refs[1], xpad_refs[1], kv, states, w,
                    seg=tile, chunk=HGRN_CHUNK, first_rows_start=None),
        _project_in_blocks(xn_ref[...], z_even, w))
    y_ref[tile:2 * tile, :] = y

    st, ctx, hl = states[0]
    for hd in range(A_HEADS):
        st_ref[hd] = st[hd]
    ctx_ref[...] = ctx
    hl_ref[...] = hl

    @pl.when((j % steps_per_seq) == steps_per_seq - 1)
    def _():
        for hd in range(A_HEADS):
            hgrn_ref[0, hd] = st[hd].T
        _store_seq_state(conv_ref, lru_ref, j // steps_per_seq, ctx, hl)


def _sample_kernel(nseq, seg, *refs):
    x_ref, mk_ref, mv_ref, hgrn_in, conv_in, lru_in = refs[:6]
    w = dict(zip(_WEIGHT_NAMES, refs[6:6 + _NW]))
    y_ref, hgrn_ref, conv_ref, lru_ref = refs[6 + _NW:10 + _NW]
    scratch = refs[10 + _NW:]
    z = _ZBuf(scratch[:Z_BLOCKS])
    oa_ref, xpad_ref = scratch[Z_BLOCKS:]
    states = [([hgrn_in[s, hd].T for hd in range(A_HEADS)],
               jnp.concatenate([conv_in[r, s:s + 1, :] for r in range(CTX_ROWS)], axis=0),
               lru_in[s:s + 1, :]) for s in range(nseq)]
    _project_in(x_ref[...], z, w)
    y, new_states = _mix(
        x_ref[...], z, oa_ref, xpad_ref,
        lambda s, hd: (mk_ref[s, _head_rows(hd), :], mv_ref[s, _head_rows(hd), :]), states, w,
        seg=seg, chunk=min(HGRN_CHUNK, seg), first_rows_start=None)
    y_ref[...] = y
    for s in range(nseq):
        st, ctx, hl = new_states[s]
        for hd in range(A_HEADS):
            hgrn_ref[s, hd] = st[hd].T
        _store_seq_state(conv_ref, lru_ref, s, ctx, hl)


def _const_spec(shape):
    nd = len(shape)
    return pl.BlockSpec(shape, lambda *_: (0,) * nd, pipeline_mode=pl.Buffered(1))


PREP_STEPS = 8
_DENSE_WEIGHTS = ("w_in", "w_a_down", "w_b_down", "w_c_down", "w_out")
_U32 = jnp.uint32


def _prep_kernel(*refs):
    nd = len(_DENSE_WEIGHTS)
    dense_in, (lru_r_in, lru_i_in) = refs[:nd], refs[nd:nd + 2]
    mem_ref, g_mem_ref, wk_in, wv_in = refs[nd + 2:nd + 6]
    outs = refs[nd + 6:]
    dense_out, (lru_r_out, lru_i_out), (k_ref, v_ref) = outs[:nd], outs[nd:nd + 2], outs[nd + 2:nd + 4]
    tile_ref, wk_ref, wv_ref = outs[nd + 4:]
    for src, dst in zip(dense_in, dense_out):
        dst[...] = pltpu.bitcast(src[...].astype(bf16), _U32)

    @pl.when(pl.program_id(0) == 0)
    def _():
        wk_ref[...] = pltpu.bitcast(wk_in[...].astype(bf16), _U32)
        wv_ref[...] = pltpu.bitcast(wv_in[...].astype(bf16), _U32)
        per = MXU_TILE // B_BLOCK_DIM
        for src, dst in ((lru_r_in, lru_r_out), (lru_i_in, lru_i_out)):
            for g in range(LRU_GROUPS):
                tile_ref[...] = jnp.zeros_like(tile_ref)
                for p in range(per):
                    lo = p * B_BLOCK_DIM
                    tile_ref[lo:lo + B_BLOCK_DIM, lo:lo + B_BLOCK_DIM] = src[g * per + p]
                dst[g] = pltpu.bitcast(tile_ref[...].astype(bf16), _U32)

    hm = _rms(mem_ref[0], g_mem_ref[...]).astype(bf16)
    k = jnp.dot(hm, _wt(wk_ref[...]), preferred_element_type=f32)
    v = jnp.dot(hm, _wt(wv_ref[...]), preferred_element_type=f32)
    for hd in range(C_HEADS):
        sl = slice(hd * HEAD_DIM, (hd + 1) * HEAD_DIM)
        k_ref[0, _head_rows(hd), :] = k[:, sl]
        v_ref[0, _head_rows(hd), :] = v[:, sl]


def _prep_weights(dense, lru_r, lru_i, mem, g_mem, w_mem_k, w_mem_v):
    bsz = mem.shape[0]
    assert bsz == PREP_STEPS
    in_specs, out_specs, out_shape = [], [], []
    for wm in dense:
        k, n = wm.shape
        assert k % (4 * SUBLANES * PREP_STEPS) == 0
        in_specs.append(pl.BlockSpec((k // PREP_STEPS, n), lambda i: (i, 0)))
        out_specs.append(pl.BlockSpec((k // (2 * PREP_STEPS), n), lambda i: (i, 0)))
        out_shape.append(jax.ShapeDtypeStruct((k // 2, n), _U32))
    blk = (B_BLOCKS, B_BLOCK_DIM, B_BLOCK_DIM)
    tiles = (LRU_GROUPS, MXU_TILE // 2, MXU_TILE)
    in_specs += [pl.BlockSpec(blk, lambda i: (0, 0, 0))] * 2
    out_specs += [pl.BlockSpec(tiles, lambda i: (0, 0, 0))] * 2
    out_shape += [jax.ShapeDtypeStruct(tiles, _U32)] * 2
    in_specs += [pl.BlockSpec((1, N_MEM, D_MODEL), lambda i: (i, 0, 0)), _const_spec((1, D_MODEL)),
                 _const_spec((D_MODEL, C_WIDTH)), _const_spec((D_MODEL, C_WIDTH))]
    out_specs += [pl.BlockSpec((1,) + KV_ROWS, lambda i: (i, 0, 0))] * 2
    out_shape += [jax.ShapeDtypeStruct((bsz,) + KV_ROWS, f32)] * 2
    outs = pl.pallas_call(
        _prep_kernel, grid=(PREP_STEPS,), in_specs=in_specs, out_specs=out_specs, out_shape=out_shape,
        scratch_shapes=[pltpu.VMEM((MXU_TILE, MXU_TILE), f32),
                        pltpu.VMEM((D_MODEL // 2, C_WIDTH), _U32), pltpu.VMEM((D_MODEL // 2, C_WIDTH), _U32)],
        compiler_params=pltpu.CompilerParams(vmem_limit_bytes=VMEM_LIMIT_BYTES,
                                             dimension_semantics=("arbitrary",)),
        name="prep_weights",
    )(*dense, lru_r, lru_i, mem, g_mem, w_mem_k, w_mem_v)
    nd = len(dense)
    return outs[:nd], outs[nd], outs[nd + 1], outs[nd + 2], outs[nd + 3]


def kernel(x_prompt, x_sample, mem_prompt, cache_mem_k, cache_mem_v, state_hgrn, state_conv, state_lru, g_mix, w_in, lb_logits, g_a_out, w_a_down, w_conv, b_conv, w_lru_r, b_lru_r, w_lru_i, b_lru_i, lru_lambda, w_b_down, g_mem, w_mem_k, w_mem_v, w_c_down, w_out, g_final):
    bsz, seq, _ = x_prompt.shape
    dec_b, dec_seq, _ = x_sample.shape
    assert g_mix.shape[0] == 1, "single-layer stack only"
    assert seq % (2 * PROMPT_TILE) == 0 and PROMPT_TILE % HGRN_CHUNK == 0

    row = lambda a: a.reshape(1, -1).astype(f32)
    dense = dict(w_in=w_in[0], w_a_down=w_a_down[0], w_b_down=w_b_down[0], w_c_down=w_c_down[0], w_out=w_out[0])
    packed, lru_r_tiles, lru_i_tiles, mk, mv = _prep_weights(
        [dense[n] for n in _DENSE_WEIGHTS], w_lru_r[0], w_lru_i[0], mem_prompt, row(g_mem[0]), w_mem_k[0], w_mem_v[0])
    packed = dict(zip(_DENSE_WEIGHTS, packed))
    weights = dict(
        g_mix=row(g_mix[0]), w_in=packed["w_in"], lb_logits=lb_logits.astype(f32), g_a_out=row(g_a_out[0]),
        w_a_down=packed["w_a_down"], w_conv=w_conv[0].astype(f32), b_conv=row(b_conv[0]),
        w_lru_r=lru_r_tiles, b_lru_r=row(b_lru_r[0]), w_lru_i=lru_i_tiles, b_lru_i=row(b_lru_i[0]),
        lru_lambda=row(lru_lambda[0]), w_b_down=packed["w_b_down"], w_c_down=packed["w_c_down"],
        w_out=packed["w_out"], g_final=row(g_final))
    wlist = [weights[n] for n in _WEIGHT_NAMES]
    wspecs = [_const_spec(a.shape) for a in wlist]

    tile = PROMPT_TILE
    n_tiles = bsz * seq // tile
    steps_per_seq = seq // (2 * tile)
    zbuf = [pltpu.VMEM((tile, Z_BLK), f32)] * Z_BLOCKS
    y_p, hgrn_p, conv_p, lru_p = pl.pallas_call(
        functools.partial(_prompt_kernel, steps_per_seq),
        grid=(n_tiles // 2,),
        in_specs=[pl.BlockSpec((2 * tile, D_MODEL), lambda j: (j, 0)),
                  pl.BlockSpec((tile, D_MODEL), lambda j: (jnp.minimum(2 * j + 2, n_tiles - 1), 0)),
                  pl.BlockSpec((1,) + KV_ROWS, lambda j: (j // steps_per_seq, 0, 0)),
                  pl.BlockSpec((1,) + KV_ROWS, lambda j: (j // steps_per_seq, 0, 0))] + wspecs,
        out_specs=[pl.BlockSpec((2 * tile, D_MODEL), lambda j: (j, 0)),
                   pl.BlockSpec((1, A_HEADS, HEAD_DIM, HEAD_DIM), lambda j: (j // steps_per_seq, 0, 0, 0)),
                   pl.BlockSpec((CTX_ROWS, bsz, B_WIDTH), lambda j: (0, 0, 0)),
                   pl.BlockSpec((bsz, B_WIDTH), lambda j: (0, 0))],
        out_shape=[jax.ShapeDtypeStruct((bsz * seq, D_MODEL), f32),
                   jax.ShapeDtypeStruct((bsz, A_HEADS, HEAD_DIM, HEAD_DIM), f32),
                   jax.ShapeDtypeStruct((CTX_ROWS, bsz, B_WIDTH), f32),
                   jax.ShapeDtypeStruct((bsz, B_WIDTH), f32)],
        scratch_shapes=[pltpu.VMEM((A_HEADS, HEAD_DIM, HEAD_DIM), f32),
                        pltpu.VMEM((CTX_ROWS, B_WIDTH), f32),
                        pltpu.VMEM((1, B_WIDTH), f32)] + zbuf + zbuf
                       + [pltpu.VMEM((tile, A_WIDTH), f32)] * 2
                       + [pltpu.VMEM((SUBLANES, B_WIDTH), f32)] * 2,
        compiler_params=pltpu.CompilerParams(vmem_limit_bytes=VMEM_LIMIT_BYTES,
                                             dimension_semantics=("arbitrary",)),
        name="prompt_layer",
    )(x_prompt.reshape(bsz * seq, D_MODEL), x_prompt.reshape(bsz * seq, D_MODEL), mk, mv, *wlist)
    y_p = y_p.reshape(bsz, seq, D_MODEL)

    rows = dec_b * dec_seq
    full = lambda shape: pl.BlockSpec(shape, lambda *_: (0,) * len(shape))
    y_s, hgrn_s, conv_s, lru_s = pl.pallas_call(
        functools.partial(_sample_kernel, dec_b, dec_seq),
        grid=(1,),
        in_specs=[full((rows, D_MODEL)), full((dec_b,) + KV_ROWS), full((dec_b,) + KV_ROWS),
                  full((dec_b, A_HEADS, HEAD_DIM, HEAD_DIM)), full((CTX_ROWS, dec_b, B_WIDTH)),
                  full((dec_b, B_WIDTH))] + wspecs,
        out_specs=[full((rows, D_MODEL)), full((dec_b, A_HEADS, HEAD_DIM, HEAD_DIM)),
                   full((CTX_ROWS, dec_b, B_WIDTH)), full((dec_b, B_WIDTH))],
        out_shape=[jax.ShapeDtypeStruct((rows, D_MODEL), f32),
                   jax.ShapeDtypeStruct((dec_b, A_HEADS, HEAD_DIM, HEAD_DIM), f32),
                   jax.ShapeDtypeStruct((CTX_ROWS, dec_b, B_WIDTH), f32),
                   jax.ShapeDtypeStruct((dec_b, B_WIDTH), f32)],
        scratch_shapes=[pltpu.VMEM((rows, Z_BLK), f32)] * Z_BLOCKS
                       + [pltpu.VMEM((rows, A_WIDTH), f32),
                        pltpu.VMEM((dec_b * SUBLANES, B_WIDTH), f32)],
        compiler_params=pltpu.CompilerParams(vmem_limit_bytes=VMEM_LIMIT_BYTES),
        name="sample_layer",
    )(x_sample.reshape(rows, D_MODEL), cache_mem_k.reshape((dec_b,) + KV_ROWS),
      cache_mem_v.reshape((dec_b,) + KV_ROWS), state_hgrn[0], jnp.swapaxes(state_conv[0], 0, 1),
      state_lru[0], *wlist)

    return (y_p, y_s.reshape(dec_b, dec_seq, D_MODEL), hgrn_p[None], jnp.swapaxes(conv_p, 0, 1)[None],
            lru_p[None], mk.reshape(1, bsz, N_MEM, C_HEADS, HEAD_DIM),
            mv.reshape(1, bsz, N_MEM, C_HEADS, HEAD_DIM), hgrn_s[None], jnp.swapaxes(conv_s, 0, 1)[None],
            lru_s[None])
```

```python
import functools

import jax
import jax.numpy as jnp
from jax import lax
from jax.experimental import pallas as pl
from jax.experimental.pallas import tpu as pltpu

f32 = jnp.float32
bf16 = jnp.bfloat16

D_MODEL = 1024
N_MEM = 256
EPS = 1e-6
A_HEADS = 4
HEAD_DIM = 128
A_WIDTH = A_HEADS * HEAD_DIM
B_WIDTH = D_MODEL
B_BLOCKS = 16
B_BLOCK_DIM = B_WIDTH // B_BLOCKS
CONV_W = 4
LRU_C = 8.0
LOG2_E = 1.4426950408889634
C_HEADS = 4
C_WIDTH = C_HEADS * HEAD_DIM
assert A_HEADS == C_HEADS
HGRN_CHUNK = 64
IN_COLS = 4 * A_WIDTH + 2 * B_WIDTH + 2 * C_WIDTH + 3 * D_MODEL

_QA, _FA, _VA, _GA = 0, A_WIDTH, 2 * A_WIDTH, 3 * A_WIDTH
_XB = 4 * A_WIDTH
_GB = _XB + B_WIDTH
_QC = _GB + B_WIDTH
_GC = _QC + C_WIDTH
_ZA = _GC + C_WIDTH
_ZB = _ZA + D_MODEL
_ZC = _ZB + D_MODEL

MXU_TILE = 256
LRU_GROUPS = B_WIDTH // MXU_TILE
SUBLANES = 8
CTX_ROWS = CONV_W - 1

PROMPT_TILE = 256
Z_BLK = 1024
Z_BLOCKS = IN_COLS // Z_BLK
VMEM_LIMIT_BYTES = 60 * 1024 * 1024


def _rms(x, g):
    return x * lax.rsqrt(jnp.mean(x * x, axis=-1, keepdims=True) + EPS) * g


def _wt(ref_or_val):
    return pltpu.bitcast(ref_or_val, bf16)


def _dot(a, b):
    return jnp.dot(a.astype(bf16), b.astype(bf16), preferred_element_type=f32)


def _dot_nt(a, b):
    return lax.dot_general(a.astype(bf16), b.astype(bf16), (((1,), (1,)), ((), ())),
                           preferred_element_type=f32)


def _dot_tn(a, b):
    return lax.dot_general(a.astype(bf16), b.astype(bf16), (((0,), (0,)), ((), ())),
                           preferred_element_type=f32)


def _silu(x):
    return x * jax.nn.sigmoid(x)


KV_ROWS = (N_MEM * C_HEADS, HEAD_DIM)


def _head_rows(hd):
    return pl.ds(hd, N_MEM, stride=C_HEADS)


def _vreg_groups(x):
    rows, width = x.shape
    return x.reshape(rows // SUBLANES, SUBLANES, width)


def _cumprod_rows(x):
    rows = x.shape[0]
    x3 = _vreg_groups(x)
    sub = lax.broadcasted_iota(jnp.int32, x3.shape, 1)
    d = 1
    while d < SUBLANES:
        x3 = x3 * jnp.where(sub >= d, pltpu.roll(x3, d, 1), 1.0)
        d *= 2
    out, carry = [], None
    for g in range(rows // SUBLANES):
        cur = x3[g] if carry is None else x3[g] * carry
        carry = cur[SUBLANES - 1:SUBLANES, :]
        out.append(cur)
    return jnp.concatenate(out, axis=0)


def _linear_scan_rows(a, u, carry):
    rows = a.shape[0]
    a3, u3 = _vreg_groups(a), _vreg_groups(u)
    sub = lax.broadcasted_iota(jnp.int32, a3.shape, 1)
    d = 1
    while d < SUBLANES:
        keep = sub >= d
        u3 = a3 * jnp.where(keep, pltpu.roll(u3, d, 1), 0.0) + u3
        a3 = a3 * jnp.where(keep, pltpu.roll(a3, d, 1), 1.0)
        d *= 2
    out = []
    for g in range(rows // SUBLANES):
        cur = u3[g] + a3[g] * carry
        carry = cur[SUBLANES - 1:SUBLANES, :]
        out.append(cur)
    return jnp.concatenate(out, axis=0), carry


class _ZBuf:
    def __init__(self, refs):
        self.refs = refs

    def cols(self, c0, width, rows=slice(None)):
        blk, off = divmod(c0, Z_BLK)
        assert off + width <= Z_BLK
        return self.refs[blk][rows, off:off + width]


def _project_in_blocks(x, z, w):
    h = _rms(x, w["g_mix"][...]).astype(bf16)

    def block(c0):
        blk, off = divmod(c0, Z_BLK)
        z.refs[blk][:, off:off + MXU_TILE] = jnp.dot(h, _wt(w["w_in"][:, c0:c0 + MXU_TILE]),
                                                     preferred_element_type=f32)

    return [functools.partial(block, c0) for c0 in range(0, IN_COLS, MXU_TILE)]


def _project_in(x, z, w):
    for block in _project_in_blocks(x, z, w):
        block()


def _interleave(stages, blocks):
    blocks = list(blocks)
    while True:
        try:
            n = next(stages)
        except StopIteration as done:
            result = done.value
            break
        for _ in range(min(n, len(blocks))):
            blocks.pop(0)()
    for block in blocks:
        block()
    return result


def _mix(*args, **kwargs):
    return _interleave(_mix_stages(*args, **kwargs), [])


def _mix_stages(x, z, oa_ref, xpad_ref, kv, states, w, *, seg, chunk, first_rows_start):
    rows = x.shape[0]
    nseg = rows // seg
    nchunk = seg // chunk
    chunk_rows = [[slice(s * seg + c * chunk, s * seg + (c + 1) * chunk) for c in range(nchunk)]
                  for s in range(nseg)]
    all_chunks = [rs for per_seg in chunk_rows for rs in per_seg]
    head_sl = [slice(hd * HEAD_DIM, (hd + 1) * HEAD_DIM) for hd in range(A_HEADS)]
    cat_rows = lambda parts: parts[0] if len(parts) == 1 else jnp.concatenate(parts, axis=0)

    lg = w["lb_logits"][...]
    l0, l1 = lg[0:1, :], lg[1:2, :]
    lmax = jnp.maximum(l0, l1)
    e0, e1 = jnp.exp(l0 - lmax), jnp.exp(l1 - lmax)
    lb = e0 / (e0 + e1)

    qg, kg, v, kd, decay = [], [], [], [], []
    for rs in all_chunks:
        yield 1
        f = lb + (1.0 - lb) * jax.nn.sigmoid(z.cols(_FA, A_WIDTH, rs))
        p = _cumprod_rows(f)
        inv_p = 1.0 / p
        k = 1.0 - f
        qg.append(_silu(z.cols(_QA, A_WIDTH, rs)) * p)
        kg.append(k * inv_p)
        v.append(z.cols(_VA, A_WIDTH, rs))
        p_last = p[chunk - 1:chunk, :]
        kd.append(k * (p_last * inv_p))
        decay.append(p_last)
    qg_all, kg_all, v_all = cat_rows(qg), cat_rows(kg), cat_rows(v)

    scores = [_dot_nt(qg_all[:, sl], kg_all[:, sl]) for sl in head_sl]
    st_in = [[[states[s][0][hd]] for hd in range(A_HEADS)] for s in range(nseg)]
    for s in range(nseg):
        for c in range(nchunk):
            i = s * nchunk + c
            for hd, sl in enumerate(head_sl):
                st_in[s][hd].append(st_in[s][hd][c] * decay[i][:, sl] + _dot_tn(v[i][:, sl], kd[i][:, sl]))
    new_st = [[st_in[s][hd][nchunk] for hd in range(A_HEADS)] for s in range(nseg)]
    yield 1
    tt = lax.broadcasted_iota(jnp.int32, (rows, rows), 0)
    ss = lax.broadcasted_iota(jnp.int32, (rows, rows), 1)
    shift = chunk.bit_length() - 1
    causal = ((tt >> shift) == (ss >> shift)) & (ss <= tt)
    o_intra = [_dot(jnp.where(causal, scores[hd], 0.0), v_all[:, sl]) for hd, sl in enumerate(head_sl)]
    yield 1
    for hd, sl in enumerate(head_sl):
        for s in range(nseg):
            for c, rs in enumerate(chunk_rows[s]):
                oa_ref[rs, sl] = o_intra[hd][rs, :] + _dot_nt(qg[s * nchunk + c][:, sl], st_in[s][hd][c])

    g_a = w["g_a_out"][...]
    a_in = []
    for i, rs in enumerate(all_chunks):
        yield i % 2
        normed = jnp.concatenate([_rms(oa_ref[rs, sl], g_a[:, sl]) for sl in head_sl], axis=-1)
        a_in.append(normed * _silu(z.cols(_GA, A_WIDTH, rs)))
    pa = _dot(cat_rows(a_in), _wt(w["w_a_down"][...]))

    pad = SUBLANES
    for s in range(nseg):
        xpad_ref[s * pad:(s + 1) * pad, :] = jnp.zeros((pad, B_WIDTH), f32)
        xpad_ref[(s + 1) * pad - CTX_ROWS:(s + 1) * pad, :] = states[s][1]
    w_conv = w["w_conv"][...]
    sub = lax.broadcasted_iota(jnp.int32, (chunk // SUBLANES, SUBLANES, B_WIDTH), 1)
    xc, new_ctx = [], []
    for s in range(nseg):
        for c, rs in enumerate(chunk_rows[s]):
            yield 1
            if c == 0:
                ext = jnp.concatenate([xpad_ref[s * pad:(s + 1) * pad, :], z.cols(_XB, B_WIDTH, rs)], axis=0)
            else:
                ext = z.cols(_XB, B_WIDTH, slice(rs.start - pad, rs.stop))
            ext = _vreg_groups(ext)
            acc = w_conv[CONV_W - 1:CONV_W, :] * ext[1:]
            for j in range(1, CONV_W):
                rolled = pltpu.roll(ext, j, 1)
                shifted = jnp.where(sub >= j, rolled[1:], rolled[:-1])
                acc = acc + w_conv[CONV_W - 1 - j:CONV_W - j, :] * shifted
            xc.append(w["b_conv"][...] + acc.reshape(chunk, B_WIDTH))
        new_ctx.append(z.cols(_XB, B_WIDTH, slice((s + 1) * seg - CTX_ROWS, (s + 1) * seg)))

    xc_b = cat_rows(xc).astype(bf16)
    r_pre, i_pre = [], []
    for g in range(LRU_GROUPS):
        gs = slice(g * MXU_TILE, (g + 1) * MXU_TILE)
        r_pre.append(jnp.dot(xc_b[:, gs], _wt(w["w_lru_r"][g]), preferred_element_type=f32))
        i_pre.append(jnp.dot(xc_b[:, gs], _wt(w["w_lru_i"][g]), preferred_element_type=f32))
    r_pre, i_pre = jnp.concatenate(r_pre, axis=-1), jnp.concatenate(i_pre, axis=-1)
    neg_lam = -w["lru_lambda"][...]
    softplus = jnp.maximum(neg_lam, 0.0) + jnp.log1p(jnp.exp(-jnp.abs(neg_lam)))
    a_l, u_l = [], []
    for s in range(nseg):
        for c, rs in enumerate(chunk_rows[s]):
            yield 2
            r = jax.nn.sigmoid(r_pre[rs, :] + w["b_lru_r"][...])
            ig = jax.nn.sigmoid(i_pre[rs, :] + w["b_lru_i"][...])
            log_a = -LRU_C * r * softplus
            a = jnp.exp(log_a)
            m2 = -jnp.tanh(log_a) * (a * a + 1.0)
            mult = jnp.where(m2 > 0.0, m2 * lax.rsqrt(m2), 0.0)
            if first_rows_start is not None and c == 0:
                first_row = lax.broadcasted_iota(jnp.int32, mult.shape, 0) == 0
                mult = jnp.where(first_row & first_rows_start, 1.0, mult)
            a_l.append(a)
            u_l.append(mult * ig * xc[s * nchunk + c])
    hb_gated, new_hl = [], []
    for s in range(nseg):
        carry = states[s][2]
        for c, rs in enumerate(chunk_rows[s]):
            yield 1
            i = s * nchunk + c
            hb, carry = _linear_scan_rows(a_l[i], u_l[i], carry)
            hb_gated.append(hb * _silu(z.cols(_GB, B_WIDTH, rs)))
        new_hl.append(carry)
    pb = _dot(cat_rows(hb_gated), _wt(w["w_b_down"][...]))

    scale = HEAD_DIM ** -0.5
    mem = [[kv(s, hd) for hd in range(C_HEADS)] for s in range(nseg)]
    seg_rows = [slice(s * seg, (s + 1) * seg) for s in range(nseg)]
    sc = [[_dot_nt(z.cols(_QC + hd * HEAD_DIM, HEAD_DIM, seg_rows[s]), mem[s][hd][0])
           for hd in range(C_HEADS)] for s in range(nseg)]
    pr = []
    for s in range(nseg):
        pr.append([])
        for hd in range(C_HEADS):
            yield 1
            p = jnp.exp2((sc[s][hd] - jnp.max(sc[s][hd], axis=-1, keepdims=True)) * (scale * LOG2_E))
            pr[s].append(p / jnp.sum(p, axis=-1, keepdims=True))
    oc = cat_rows([jnp.concatenate([_dot(pr[s][hd], mem[s][hd][1]) for hd in range(C_HEADS)], axis=-1)
                   for s in range(nseg)])
    yield 1
    pc = _dot(oc * _silu(z.cols(_GC, C_WIDTH)), _wt(w["w_c_down"][...]))

    merged = []
    for rs in all_chunks:
        yield 1
        merged.append(jax.nn.sigmoid(z.cols(_ZA, D_MODEL, rs)) * pa[rs, :]
                      + jax.nn.sigmoid(z.cols(_ZB, D_MODEL, rs)) * pb[rs, :]
                      + jax.nn.sigmoid(z.cols(_ZC, D_MODEL, rs)) * pc[rs, :])
    y = x + _dot(cat_rows(merged), _wt(w["w_out"][...]))
    y = _rms(y, w["g_final"][...])
    new_states = [(new_st[s], new_ctx[s], new_hl[s]) for s in range(nseg)]
    return y, new_states


_WEIGHT_NAMES = ("g_mix", "w_in", "lb_logits", "g_a_out", "w_a_down", "w_conv", "b_conv", "w_lru_r", "b_lru_r",
                 "w_lru_i", "b_lru_i", "lru_lambda", "w_b_down", "w_c_down", "w_out", "g_final")
_NW = len(_WEIGHT_NAMES)


def _store_seq_state(conv_ref, lru_ref, seq_idx, ctx, hl):
    for r in range(CTX_ROWS):
        conv_ref[r, pl.ds(seq_idx, 1), :] = ctx[r:r + 1, :]
    lru_ref[pl.ds(seq_idx, 1), :] = hl


def _prompt_kernel(steps_per_seq, *refs):
    x_ref, xn_ref, mk_ref, mv_ref = refs[:4]
    w = dict(zip(_WEIGHT_NAMES, refs[4:4 + _NW]))
    y_ref, hgrn_ref, conv_ref, lru_ref = refs[4 + _NW:8 + _NW]
    scratch = refs[8 + _NW:]
    st_ref, ctx_ref, hl_ref = scratch[:3]
    z_even, z_odd = _ZBuf(scratch[3:3 + Z_BLOCKS]), _ZBuf(scratch[3 + Z_BLOCKS:3 + 2 * Z_BLOCKS])
    oa_refs = scratch[3 + 2 * Z_BLOCKS:5 + 2 * Z_BLOCKS]
    xpad_refs = scratch[5 + 2 * Z_BLOCKS:7 + 2 * Z_BLOCKS]
    j = pl.program_id(0)
    tile = PROMPT_TILE
    seq_start = (j % steps_per_seq) == 0

    @pl.when(j == 0)
    def _():
        _project_in(x_ref[0:tile, :], z_even, w)

    @pl.when(seq_start)
    def _():
        st_ref[...] = jnp.zeros_like(st_ref)
        ctx_ref[...] = jnp.zeros_like(ctx_ref)
        hl_ref[...] = jnp.zeros_like(hl_ref)

    kv = lambda s, hd: (mk_ref[0, _head_rows(hd), :], mv_ref[0, _head_rows(hd), :])
    states = [([st_ref[hd] for hd in range(A_HEADS)], ctx_ref[...], hl_ref[...])]

    y, states = _interleave(
        _mix_stages(x_ref[0:tile, :], z_even, oa_refs[0], xpad_refs[0], kv, states, w,
                    seg=tile, chunk=HGRN_CHUNK, first_rows_start=seq_start),
        _project_in_blocks(x_ref[tile:2 * tile, :], z_odd, w))
    y_ref[0:tile, :] = y

    y, states = _interleave(
        _mix_stages(x_ref[tile:2 * tile, :], z_odd, oa_refs[1], xpad_refs[1], kv, states, w,
                    seg=tile, chunk=HGRN_CHUNK, first_rows_start=None),
        _project_in_blocks(xn_ref[...], z_even, w))
    y_ref[tile:2 * tile, :] = y

    st, ctx, hl = states[0]
    for hd in range(A_HEADS):
        st_ref[hd] = st[hd]
    ctx_ref[...] = ctx
    hl_ref[...] = hl

    @pl.when((j % steps_per_seq) == steps_per_seq - 1)
    def _():
        for hd in range(A_HEADS):
            hgrn_ref[0, hd] = st[hd].T
        _store_seq_state(conv_ref, lru_ref, j // steps_per_seq, ctx, hl)


def _sample_kernel(nseq, seg, *refs):
    x_ref, mk_ref, mv_ref, hgrn_in, conv_in, lru_in = refs[:6]
    w = dict(zip(_WEIGHT_NAMES, refs[6:6 + _NW]))
    y_ref, hgrn_ref, conv_ref, lru_ref = refs[6 + _NW:10 + _NW]
    scratch = refs[10 + _NW:]
    z = _ZBuf(scratch[:Z_BLOCKS])
    oa_ref, xpad_ref = scratch[Z_BLOCKS:]
    states = [([hgrn_in[s, hd].T for hd in range(A_HEADS)],
               jnp.concatenate([conv_in[r, s:s + 1, :] for r in range(CTX_ROWS)], axis=0),
               lru_in[s:s + 1, :]) for s in range(nseq)]
    _project_in(x_ref[...], z, w)
    y, new_states = _mix(
        x_ref[...], z, oa_ref, xpad_ref,
        lambda s, hd: (mk_ref[s, _head_rows(hd), :], mv_ref[s, _head_rows(hd), :]), states, w,
        seg=seg, chunk=min(HGRN_CHUNK, seg), first_rows_start=None)
    y_ref[...] = y
    for s in range(nseq):
        st, ctx, hl = new_states[s]
        for hd in range(A_HEADS):
            hgrn_ref[s, hd] = st[hd].T
        _store_seq_state(conv_ref, lru_ref, s, ctx, hl)


def _const_spec(shape):
    nd = len(shape)
    return pl.BlockSpec(shape, lambda *_: (0,) * nd, pipeline_mode=pl.Buffered(1))


PREP_STEPS = 8
_DENSE_WEIGHTS = ("w_in", "w_a_down", "w_b_down", "w_c_down", "w_out")
_U32 = jnp.uint32


def _prep_kernel(*refs):
    nd = len(_DENSE_WEIGHTS)
    dense_in, (lru_r_in, lru_i_in) = refs[:nd], refs[nd:nd + 2]
    mem_ref, g_mem_ref, wk_in, wv_in = refs[nd + 2:nd + 6]
    outs = refs[nd + 6:]
    dense_out, (lru_r_out, lru_i_out), (k_ref, v_ref) = outs[:nd], outs[nd:nd + 2], outs[nd + 2:nd + 4]
    tile_ref, wk_ref, wv_ref = outs[nd + 4:]
    for src, dst in zip(dense_in, dense_out):
        dst[...] = pltpu.bitcast(src[...].astype(bf16), _U32)

    @pl.when(pl.program_id(0) == 0)
    def _():
        wk_ref[...] = pltpu.bitcast(wk_in[...].astype(bf16), _U32)
        wv_ref[...] = pltpu.bitcast(wv_in[...].astype(bf16), _U32)
        per = MXU_TILE // B_BLOCK_DIM
        for src, dst in ((lru_r_in, lru_r_out), (lru_i_in, lru_i_out)):
            for g in range(LRU_GROUPS):
                tile_ref[...] = jnp.zeros_like(tile_ref)
                for p in range(per):
                    lo = p * B_BLOCK_DIM
                    tile_ref[lo:lo + B_BLOCK_DIM, lo:lo + B_BLOCK_DIM] = src[g * per + p]
                dst[g] = pltpu.bitcast(tile_ref[...].astype(bf16), _U32)

    hm = _rms(mem_ref[0], g_mem_ref[...]).astype(bf16)
    k = jnp.dot(hm, _wt(wk_ref[...]), preferred_element_type=f32)
    v = jnp.dot(hm, _wt(wv_ref[...]), preferred_element_type=f32)
    for hd in range(C_HEADS):
        sl = slice(hd * HEAD_DIM, (hd + 1) * HEAD_DIM)
        k_ref[0, _head_rows(hd), :] = k[:, sl]
        v_ref[0, _head_rows(hd), :] = v[:, sl]


def _prep_weights(dense, lru_r, lru_i, mem, g_mem, w_mem_k, w_mem_v):
    bsz = mem.shape[0]
    assert bsz == PREP_STEPS
    in_specs, out_specs, out_shape = [], [], []
    for wm in dense:
        k, n = wm.shape
        assert k % (4 * SUBLANES * PREP_STEPS) == 0
        in_specs.append(pl.BlockSpec((k // PREP_STEPS, n), lambda i: (i, 0)))
        out_specs.append(pl.BlockSpec((k // (2 * PREP_STEPS), n), lambda i: (i, 0)))
        out_shape.append(jax.ShapeDtypeStruct((k // 2, n), _U32))
    blk = (B_BLOCKS, B_BLOCK_DIM, B_BLOCK_DIM)
    tiles = (LRU_GROUPS, MXU_TILE // 2, MXU_TILE)
    in_specs += [pl.BlockSpec(blk, lambda i: (0, 0, 0))] * 2
    out_specs += [pl.BlockSpec(tiles, lambda i: (0, 0, 0))] * 2
    out_shape += [jax.ShapeDtypeStruct(tiles, _U32)] * 2
    in_specs += [pl.BlockSpec((1, N_MEM, D_MODEL), lambda i: (i, 0, 0)), _const_spec((1, D_MODEL)),
                 _const_spec((D_MODEL, C_WIDTH)), _const_spec((D_MODEL, C_WIDTH))]
    out_specs += [pl.BlockSpec((1,) + KV_ROWS, lambda i: (i, 0, 0))] * 2
    out_shape += [jax.ShapeDtypeStruct((bsz,) + KV_ROWS, f32)] * 2
    outs = pl.pallas_call(
        _prep_kernel, grid=(PREP_STEPS,), in_specs=in_specs, out_specs=out_specs, out_shape=out_shape,
        scratch_shapes=[pltpu.VMEM((MXU_TILE, MXU_TILE), f32),
                        pltpu.VMEM((D_MODEL // 2, C_WIDTH), _U32), pltpu.VMEM((D_MODEL // 2, C_WIDTH), _U32)],
        compiler_params=pltpu.CompilerParams(vmem_limit_bytes=VMEM_LIMIT_BYTES,
                                             dimension_semantics=("arbitrary",)),
        name="prep_weights",
    )(*dense, lru_r, lru_i, mem, g_mem, w_mem_k, w_mem_v)
    nd = len(dense)
    return outs[:nd], outs[nd], outs[nd + 1], outs[nd + 2], outs[nd + 3]


def kernel(x_prompt, x_sample, mem_prompt, cache_mem_k, cache_mem_v, state_hgrn, state_conv, state_lru, g_mix, w_in, lb_logits, g_a_out, w_a_down, w_conv, b_conv, w_lru_r, b_lru_r, w_lru_i, b_lru_i, lru_lambda, w_b_down, g_mem, w_mem_k, w_mem_v, w_c_down, w_out, g_final):
    bsz, seq, _ = x_prompt.shape
    dec_b, dec_seq, _ = x_sample.shape
    assert g_mix.shape[0] == 1, "single-layer stack only"
    assert seq % (2 * PROMPT_TILE) == 0 and PROMPT_TILE % HGRN_CHUNK == 0

    row = lambda a: a.reshape(1, -1).astype(f32)
    dense = dict(w_in=w_in[0], w_a_down=w_a_down[0], w_b_down=w_b_down[0], w_c_down=w_c_down[0], w_out=w_out[0])
    packed, lru_r_tiles, lru_i_tiles, mk, mv = _prep_weights(
        [dense[n] for n in _DENSE_WEIGHTS], w_lru_r[0], w_lru_i[0], mem_prompt, row(g_mem[0]), w_mem_k[0], w_mem_v[0])
    packed = dict(zip(_DENSE_WEIGHTS, packed))
    weights = dict(
        g_mix=row(g_mix[0]), w_in=packed["w_in"], lb_logits=lb_logits.astype(f32), g_a_out=row(g_a_out[0]),
        w_a_down=packed["w_a_down"], w_conv=w_conv[0].astype(f32), b_conv=row(b_conv[0]),
        w_lru_r=lru_r_tiles, b_lru_r=row(b_lru_r[0]), w_lru_i=lru_i_tiles, b_lru_i=row(b_lru_i[0]),
        lru_lambda=row(lru_lambda[0]), w_b_down=packed["w_b_down"], w_c_down=packed["w_c_down"],
        w_out=packed["w_out"], g_final=row(g_final))
    wlist = [weights[n] for n in _WEIGHT_NAMES]
    wspecs = [_const_spec(a.shape) for a in wlist]

    tile = PROMPT_TILE
    n_tiles = bsz * seq // tile
    steps_per_seq = seq // (2 * tile)
    zbuf = [pltpu.VMEM((tile, Z_BLK), f32)] * Z_BLOCKS
    y_p, hgrn_p, conv_p, lru_p = pl.pallas_call(
        functools.partial(_prompt_kernel, steps_per_seq),
        grid=(n_tiles // 2,),
        in_specs=[pl.BlockSpec((2 * tile, D_MODEL), lambda j: (j, 0)),
                  pl.BlockSpec((tile, D_MODEL), lambda j: (jnp.minimum(2 * j + 2, n_tiles - 1), 0)),
                  pl.BlockSpec((1,) + KV_ROWS, lambda j: (j // steps_per_seq, 0, 0)),
                  pl.BlockSpec((1,) + KV_ROWS, lambda j: (j // steps_per_seq, 0, 0))] + wspecs,
        out_specs=[pl.BlockSpec((2 * tile, D_MODEL), lambda j: (j, 0)),
                   pl.BlockSpec((1, A_HEADS, HEAD_DIM, HEAD_DIM), lambda j: (j // steps_per_seq, 0, 0, 0)),
                   pl.BlockSpec((CTX_ROWS, bsz, B_WIDTH), lambda j: (0, 0, 0)),
                   pl.BlockSpec((bsz, B_WIDTH), lambda j: (0, 0))],
        out_shape=[jax.ShapeDtypeStruct((bsz * seq, D_MODEL), f32),
                   jax.ShapeDtypeStruct((bsz, A_HEADS, HEAD_DIM, HEAD_DIM), f32),
                   jax.ShapeDtypeStruct((CTX_ROWS, bsz, B_WIDTH), f32),
                   jax.ShapeDtypeStruct((bsz, B_WIDTH), f32)],
        scratch_shapes=[pltpu.VMEM((A_HEADS, HEAD_DIM, HEAD_DIM), f32),
                        pltpu.VMEM((CTX_ROWS, B_WIDTH), f32),
                        pltpu.VMEM((1, B_WIDTH), f32)] + zbuf + zbuf
                       + [pltpu.VMEM((tile, A_WIDTH), f32)] * 2
                       + [pltpu.VMEM((SUBLANES, B_WIDTH), f32)] * 2,
        compiler_params=pltpu.CompilerParams(vmem_limit_bytes=VMEM_LIMIT_BYTES,
                                             dimension_semantics=("arbitrary",)),
        name="prompt_layer",
    )(x_prompt.reshape(bsz * seq, D_MODEL), x_prompt.reshape(bsz * seq, D_MODEL), mk, mv, *wlist)
    y_p = y_p.reshape(bsz, seq, D_MODEL)

    rows = dec_b * dec_seq
    full = lambda shape: pl.BlockSpec(shape, lambda *_: (0,) * len(shape))
    y_s, hgrn_s, conv_s, lru_s = pl.pallas_call(
        functools.partial(_sample_kernel, dec_b, dec_seq),
        grid=(1,),
        in_specs=[full((rows, D_MODEL)), full((dec_b,) + KV_ROWS), full((dec_b,) + KV_ROWS),
                  full((dec_b, A_HEADS, HEAD_DIM, HEAD_DIM)), full((CTX_ROWS, dec_b, B_WIDTH)),
                  full((dec_b, B_WIDTH))] + wspecs,
        out_specs=[full((rows, D_MODEL)), full((dec_b, A_HEADS, HEAD_DIM, HEAD_DIM)),
                   full((CTX_ROWS, dec_b, B_WIDTH)), full((dec_b, B_WIDTH))],
        out_shape=[jax.ShapeDtypeStruct((rows, D_MODEL), f32),
                   jax.ShapeDtypeStruct((dec_b, A_HEADS, HEAD_DIM, HEAD_DIM), f32),
                   jax.ShapeDtypeStruct((CTX_ROWS, dec_b, B_WIDTH), f32),
                   jax.ShapeDtypeStruct((dec_b, B_WIDTH), f32)],
        scratch_shapes=[pltpu.VMEM((rows, Z_BLK), f32)] * Z_BLOCKS
                       + [pltpu.VMEM((rows, A_WIDTH), f32),
                        pltpu.VMEM((dec_b * SUBLANES, B_WIDTH), f32)],
        compiler_params=pltpu.CompilerParams(vmem_limit_bytes=VMEM_LIMIT_BYTES),
        name="sample_layer",
    )(x_sample.reshape(rows, D_MODEL), cache_mem_k.reshape((dec_b,) + KV_ROWS),
      cache_mem_v.reshape((dec_b,) + KV_ROWS), state_hgrn[0], jnp.swapaxes(state_conv[0], 0, 1),
      state_lru[0], *wlist)

    return (y_p, y_s.reshape(dec_b, dec_seq, D_MODEL), hgrn_p[None], jnp.swapaxes(conv_p, 0, 1)[None],
            lru_p[None], mk.reshape(1, bsz, N_MEM, C_HEADS, HEAD_DIM),
            mv.reshape(1, bsz, N_MEM, C_HEADS, HEAD_DIM), hgrn_s[None], jnp.swapaxes(conv_s, 0, 1)[None],
            lru_s[None])
```

```python
import functools

import jax
import jax.numpy as jnp
from jax import lax
from jax.experimental import pallas as pl
from jax.experimental.pallas import tpu as pltpu

f32 = jnp.float32
bf16 = jnp.bfloat16

D_MODEL = 1024
N_MEM = 256
EPS = 1e-6
A_HEADS = 4
HEAD_DIM = 128
A_WIDTH = A_HEADS * HEAD_DIM
B_WIDTH = D_MODEL
B_BLOCKS = 16
B_BLOCK_DIM = B_WIDTH // B_BLOCKS
CONV_W = 4
LRU_C = 8.0
LOG2_E = 1.4426950408889634
C_HEADS = 4
C_WIDTH = C_HEADS * HEAD_DIM
assert A_HEADS == C_HEADS
HGRN_CHUNK = 64
IN_COLS = 4 * A_WIDTH + 2 * B_WIDTH + 2 * C_WIDTH + 3 * D_MODEL

_QA, _FA, _VA, _GA = 0, A_WIDTH, 2 * A_WIDTH, 3 * A_WIDTH
_XB = 4 * A_WIDTH
_GB = _XB + B_WIDTH
_QC = _GB + B_WIDTH
_GC = _QC + C_WIDTH
_ZA = _GC + C_WIDTH
_ZB = _ZA + D_MODEL
_ZC = _ZB + D_MODEL

MXU_TILE = 256
LRU_GROUPS = B_WIDTH // MXU_TILE
SUBLANES = 8
CTX_ROWS = CONV_W - 1

PROMPT_TILE = 256
Z_BLK = 1024
Z_BLOCKS = IN_COLS // Z_BLK
VMEM_LIMIT_BYTES = 60 * 1024 * 1024


def _rms(x, g):
    return x * lax.rsqrt(jnp.mean(x * x, axis=-1, keepdims=True) + EPS) * g


def _wt(ref_or_val):
    return pltpu.bitcast(ref_or_val, bf16)


def _dot(a, b):
    return jnp.dot(a.astype(bf16), b.astype(bf16), preferred_element_type=f32)


def _dot_nt(a, b):
    return lax.dot_general(a.astype(bf16), b.astype(bf16), (((1,), (1,)), ((), ())),
                           preferred_element_type=f32)


def _dot_tn(a, b):
    return lax.dot_general(a.astype(bf16), b.astype(bf16), (((0,), (0,)), ((), ())),
                           preferred_element_type=f32)


def _silu(x):
    return x * jax.nn.sigmoid(x)


KV_ROWS = (N_MEM * C_HEADS, HEAD_DIM)


def _head_rows(hd):
    return pl.ds(hd, N_MEM, stride=C_HEADS)


def _vreg_groups(x):
    rows, width = x.shape
    return x.reshape(rows // SUBLANES, SUBLANES, width)


def _cumprod_rows(x):
    rows = x.shape[0]
    x3 = _vreg_groups(x)
    sub = lax.broadcasted_iota(jnp.int32, x3.shape, 1)
    d = 1
    while d < SUBLANES:
        x3 = x3 * jnp.where(sub >= d, pltpu.roll(x3, d, 1), 1.0)
        d *= 2
    out, carry = [], None
    for g in range(rows // SUBLANES):
        cur = x3[g] if carry is None else x3[g] * carry
        carry = cur[SUBLANES - 1:SUBLANES, :]
        out.append(cur)
    return jnp.concatenate(out, axis=0)


def _linear_scan_rows(a, u, carry):
    rows = a.shape[0]
    a3, u3 = _vreg_groups(a), _vreg_groups(u)
    row0 = lax.broadcasted_iota(jnp.int32, a3.shape[1:], 0) == 0
    out = []
    for g in range(rows // SUBLANES):
        ag, ug = a3[g], u3[g]
        ug = jnp.where(row0, ag * carry + ug, ug)
        ag = jnp.where(row0, 0.0, ag)
        d = 1
        while d < SUBLANES:
            ug = ag * pltpu.roll(ug, d, 0) + ug
            if 2 * d < SUBLANES:
                ag = ag * pltpu.roll(ag, d, 0)
            d *= 2
        carry = ug[SUBLANES - 1:SUBLANES, :]
        out.append(ug)
    return jnp.concatenate(out, axis=0), carry


class _ZBuf:
    def __init__(self, refs):
        self.refs = refs

    def cols(self, c0, width, rows=slice(None)):
        blk, off = divmod(c0, Z_BLK)
        assert off + width <= Z_BLK
        return self.refs[blk][rows, off:off + width]


def _project_in_blocks(x, z, w):
    h = _rms(x, w["g_mix"][...]).astype(bf16)

    def block(c0):
        blk, off = divmod(c0, Z_BLK)
        z.refs[blk][:, off:off + MXU_TILE] = jnp.dot(h, _wt(w["w_in"][:, c0:c0 + MXU_TILE]),
                                                     preferred_element_type=f32)

    return [functools.partial(block, c0) for c0 in range(0, IN_COLS, MXU_TILE)]


def _project_in(x, z, w):
    for block in _project_in_blocks(x, z, w):
        block()


def _interleave(stages, blocks):
    blocks = list(blocks)
    while True:
        try:
            n = next(stages)
        except StopIteration as done:
            result = done.value
            break
        for _ in range(min(n, len(blocks))):
            blocks.pop(0)()
    for block in blocks:
        block()
    return result


def _mix(*args, **kwargs):
    return _interleave(_mix_stages(*args, **kwargs), [])


def _mix_stages(x, z, oa_ref, xpad_ref, kv, states, w, *, seg, chunk, first_rows_start):
    rows = x.shape[0]
    nseg = rows // seg
    nchunk = seg // chunk
    chunk_rows = [[slice(s * seg + c * chunk, s * seg + (c + 1) * chunk) for c in range(nchunk)]
                  for s in range(nseg)]
    all_chunks = [rs for per_seg in chunk_rows for rs in per_seg]
    head_sl = [slice(hd * HEAD_DIM, (hd + 1) * HEAD_DIM) for hd in range(A_HEADS)]
    cat_rows = lambda parts: parts[0] if len(parts) == 1 else jnp.concatenate(parts, axis=0)

    lg = w["lb_logits"][...]
    l0, l1 = lg[0:1, :], lg[1:2, :]
    lmax = jnp.maximum(l0, l1)
    e0, e1 = jnp.exp(l0 - lmax), jnp.exp(l1 - lmax)
    lb = e0 / (e0 + e1)

    qg, kg, v, kd, decay = [], [], [], [], []
    for rs in all_chunks:
        yield 1
        f = lb + (1.0 - lb) * jax.nn.sigmoid(z.cols(_FA, A_WIDTH, rs))
        p = _cumprod_rows(f)
        inv_p = 1.0 / p
        k = 1.0 - f
        qg.append(_silu(z.cols(_QA, A_WIDTH, rs)) * p)
        kg.append(k * inv_p)
        v.append(z.cols(_VA, A_WIDTH, rs))
        p_last = p[chunk - 1:chunk, :]
        kd.append(k * (p_last * inv_p))
        decay.append(p_last)
    qg_all, kg_all, v_all = cat_rows(qg), cat_rows(kg), cat_rows(v)

    scores = [_dot_nt(qg_all[:, sl], kg_all[:, sl]) for sl in head_sl]
    st_in = [[[states[s][0][hd]] for hd in range(A_HEADS)] for s in range(nseg)]
    for s in range(nseg):
        for c in range(nchunk):
            i = s * nchunk + c
            for hd, sl in enumerate(head_sl):
                st_in[s][hd].append(st_in[s][hd][c] * decay[i][:, sl] + _dot_tn(v[i][:, sl], kd[i][:, sl]))
    new_st = [[st_in[s][hd][nchunk] for hd in range(A_HEADS)] for s in range(nseg)]
    yield 1
    tt = lax.broadcasted_iota(jnp.int32, (rows, rows), 0)
    ss = lax.broadcasted_iota(jnp.int32, (rows, rows), 1)
    shift = chunk.bit_length() - 1
    causal = ((tt >> shift) == (ss >> shift)) & (ss <= tt)
    o_intra = [_dot(jnp.where(causal, scores[hd], 0.0), v_all[:, sl]) for hd, sl in enumerate(head_sl)]
    yield 1
    for hd, sl in enumerate(head_sl):
        for s in range(nseg):
            for c, rs in enumerate(chunk_rows[s]):
                oa_ref[rs, sl] = o_intra[hd][rs, :] + _dot_nt(qg[s * nchunk + c][:, sl], st_in[s][hd][c])

    g_a = w["g_a_out"][...]
    a_in = []
    for i, rs in enumerate(all_chunks):
        yield i % 2
        normed = jnp.concatenate([_rms(oa_ref[rs, sl], g_a[:, sl]) for sl in head_sl], axis=-1)
        a_in.append(normed * _silu(z.cols(_GA, A_WIDTH, rs)))
    pa = _dot(cat_rows(a_in), _wt(w["w_a_down"][...]))

    pad = SUBLANES
    for s in range(nseg):
        xpad_ref[s * pad:(s + 1) * pad, :] = jnp.zeros((pad, B_WIDTH), f32)
        xpad_ref[(s + 1) * pad - CTX_ROWS:(s + 1) * pad, :] = states[s][1]
    w_conv = w["w_conv"][...]
    sub = lax.broadcasted_iota(jnp.int32, (chunk // SUBLANES, SUBLANES, B_WIDTH), 1)
    xc, new_ctx = [], []
    for s in range(nseg):
        for c, rs in enumerate(chunk_rows[s]):
            yield 1
            if c == 0:
                ext = jnp.concatenate([xpad_ref[s * pad:(s + 1) * pad, :], z.cols(_XB, B_WIDTH, rs)], axis=0)
            else:
                ext = z.cols(_XB, B_WIDTH, slice(rs.start - pad, rs.stop))
            ext = _vreg_groups(ext)
            acc = w_conv[CONV_W - 1:CONV_W, :] * ext[1:]
            for j in range(1, CONV_W):
                rolled = pltpu.roll(ext, j, 1)
                shifted = jnp.where(sub >= j, rolled[1:], rolled[:-1])
                acc = acc + w_conv[CONV_W - 1 - j:CONV_W - j, :] * shifted
            xc.append(w["b_conv"][...] + acc.reshape(chunk, B_WIDTH))
        new_ctx.append(z.cols(_XB, B_WIDTH, slice((s + 1) * seg - CTX_ROWS, (s + 1) * seg)))

    xc_b = cat_rows(xc).astype(bf16)
    r_pre, i_pre = [], []
    for g in range(LRU_GROUPS):
        gs = slice(g * MXU_TILE, (g + 1) * MXU_TILE)
        r_pre.append(jnp.dot(xc_b[:, gs], _wt(w["w_lru_r"][g]), preferred_element_type=f32))
        i_pre.append(jnp.dot(xc_b[:, gs], _wt(w["w_lru_i"][g]), preferred_element_type=f32))
    r_pre, i_pre = jnp.concatenate(r_pre, axis=-1), jnp.concatenate(i_pre, axis=-1)
    neg_lam = -w["lru_lambda"][...]
    softplus = jnp.maximum(neg_lam, 0.0) + jnp.log1p(jnp.exp(-jnp.abs(neg_lam)))
    a_l, u_l = [], []
    for s in range(nseg):
        for c, rs in enumerate(chunk_rows[s]):
            yield 2
            r = jax.nn.sigmoid(r_pre[rs, :] + w["b_lru_r"][...])
            ig = jax.nn.sigmoid(i_pre[rs, :] + w["b_lru_i"][...])
            log_a = -LRU_C * r * softplus
            a = jnp.exp(log_a)
            m2 = -jnp.tanh(log_a) * (a * a + 1.0)
            mult = jnp.where(m2 > 0.0, m2 * lax.rsqrt(m2), 0.0)
            if first_rows_start is not None and c == 0:
                first_row = lax.broadcasted_iota(jnp.int32, mult.shape, 0) == 0
                mult = jnp.where(first_row & first_rows_start, 1.0, mult)
            a_l.append(a)
            u_l.append(mult * ig * xc[s * nchunk + c])
    hb_gated, new_hl = [], []
    for s in range(nseg):
        carry = states[s][2]
        for c, rs in enumerate(chunk_rows[s]):
            yield 1
            i = s * nchunk + c
            hb, carry = _linear_scan_rows(a_l[i], u_l[i], carry)
            hb_gated.append(hb * _silu(z.cols(_GB, B_WIDTH, rs)))
        new_hl.append(carry)
    pb = _dot(cat_rows(hb_gated), _wt(w["w_b_down"][...]))

    scale = HEAD_DIM ** -0.5
    mem = [[kv(s, hd) for hd in range(C_HEADS)] for s in range(nseg)]
    seg_rows = [slice(s * seg, (s + 1) * seg) for s in range(nseg)]
    sc = [[_dot_nt(z.cols(_QC + hd * HEAD_DIM, HEAD_DIM, seg_rows[s]), mem[s][hd][0])
           for hd in range(C_HEADS)] for s in range(nseg)]
    pr = []
    for s in range(nseg):
        pr.append([])
        for hd in range(C_HEADS):
            yield 1
            p = jnp.exp2((sc[s][hd] - jnp.max(sc[s][hd], axis=-1, keepdims=True)) * (scale * LOG2_E))
            pr[s].append(p / jnp.sum(p, axis=-1, keepdims=True))
    oc = cat_rows([jnp.concatenate([_dot(pr[s][hd], mem[s][hd][1]) for hd in range(C_HEADS)], axis=-1)
                   for s in range(nseg)])
    yield 1
    pc = _dot(oc * _silu(z.cols(_GC, C_WIDTH)), _wt(w["w_c_down"][...]))

    merged = []
    for rs in all_chunks:
        yield 1
        merged.append(jax.nn.sigmoid(z.cols(_ZA, D_MODEL, rs)) * pa[rs, :]
                      + jax.nn.sigmoid(z.cols(_ZB, D_MODEL, rs)) * pb[rs, :]
                      + jax.nn.sigmoid(z.cols(_ZC, D_MODEL, rs)) * pc[rs, :])
    y = x + _dot(cat_rows(merged), _wt(w["w_out"][...]))
    y = _rms(y, w["g_final"][...])
    new_states = [(new_st[s], new_ctx[s], new_hl[s]) for s in range(nseg)]
    return y, new_states


_WEIGHT_NAMES = ("g_mix", "w_in", "lb_logits", "g_a_out", "w_a_down", "w_conv", "b_conv", "w_lru_r", "b_lru_r",
                 "w_lru_i", "b_lru_i", "lru_lambda", "w_b_down", "w_c_down", "w_out", "g_final")
_NW = len(_WEIGHT_NAMES)


def _store_seq_state(conv_ref, lru_ref, seq_idx, ctx, hl):
    for r in range(CTX_ROWS):
        conv_ref[r, pl.ds(seq_idx, 1), :] = ctx[r:r + 1, :]
    lru_ref[pl.ds(seq_idx, 1), :] = hl


def _prompt_kernel(steps_per_seq, *refs):
    x_ref, xn_ref, mk_ref, mv_ref = refs[:4]
    w = dict(zip(_WEIGHT_NAMES, refs[4:4 + _NW]))
    y_ref, hgrn_ref, conv_ref, lru_ref = refs[4 + _NW:8 + _NW]
    scratch = refs[8 + _NW:]
    st_ref, ctx_ref, hl_ref = scratch[:3]
    z_even, z_odd = _ZBuf(scratch[3:3 + Z_BLOCKS]), _ZBuf(scratch[3 + Z_BLOCKS:3 + 2 * Z_BLOCKS])
    oa_refs = scratch[3 + 2 * Z_BLOCKS:5 + 2 * Z_BLOCKS]
    xpad_refs = scratch[5 + 2 * Z_BLOCKS:7 + 2 * Z_BLOCKS]
    j = pl.program_id(0)
    tile = PROMPT_TILE
    seq_start = (j % steps_per_seq) == 0

    @pl.when(j == 0)
    def _():
        _project_in(x_ref[0:tile, :], z_even, w)

    @pl.when(seq_start)
    def _():
        st_ref[...] = jnp.zeros_like(st_ref)
        ctx_ref[...] = jnp.zeros_like(ctx_ref)
        hl_ref[...] = jnp.zeros_like(hl_ref)

    kv = lambda s, hd: (mk_ref[0, _head_rows(hd), :], mv_ref[0, _head_rows(hd), :])
    states = [([st_ref[hd] for hd in range(A_HEADS)], ctx_ref[...], hl_ref[...])]

    y, states = _interleave(
        _mix_stages(x_ref[0:tile, :], z_even, oa_refs[0], xpad_refs[0], kv, states, w,
                    seg=tile, chunk=HGRN_CHUNK, first_rows_start=seq_start),
        _project_in_blocks(x_ref[tile:2 * tile, :], z_odd, w))
    y_ref[0:tile, :] = y

    y, states = _interleave(
        _mix_stages(x_ref[tile:2 * tile, :], z_odd, oa_refs[1], xpad_refs[1], kv, states, w,
                    seg=tile, chunk=HGRN_CHUNK, first_rows_start=None),
        _project_in_blocks(xn_ref[...], z_even, w))
    y_ref[tile:2 * tile, :] = y

    st, ctx, hl = states[0]
    for hd in range(A_HEADS):
        st_ref[hd] = st[hd]
    ctx_ref[...] = ctx
    hl_ref[...] = hl

    @pl.when((j % steps_per_seq) == steps_per_seq - 1)
    def _():
        for hd in range(A_HEADS):
            hgrn_ref[0, hd] = st[hd].T
        _store_seq_state(conv_ref, lru_ref, j // steps_per_seq, ctx, hl)


def _sample_kernel(nseq, seg, *refs):
    x_ref, mk_ref, mv_ref, hgrn_in, conv_in, lru_in = refs[:6]
    w = dict(zip(_WEIGHT_NAMES, refs[6:6 + _NW]))
    y_ref, hgrn_ref, conv_ref, lru_ref = refs[6 + _NW:10 + _NW]
    scratch = refs[10 + _NW:]
    z = _ZBuf(scratch[:Z_BLOCKS])
    oa_ref, xpad_ref = scratch[Z_BLOCKS:]
    states = [([hgrn_in[s, hd].T for hd in range(A_HEADS)],
               jnp.concatenate([conv_in[r, s:s + 1, :] for r in range(CTX_ROWS)], axis=0),
               lru_in[s:s + 1, :]) for s in range(nseq)]
    _project_in(x_ref[...], z, w)
    y, new_states = _mix(
        x_ref[...], z, oa_ref, xpad_ref,
        lambda s, hd: (mk_ref[s, _head_rows(hd), :], mv_ref[s, _head_rows(hd), :]), states, w,
        seg=seg, chunk=min(HGRN_CHUNK, seg), first_rows_start=None)
    y_ref[...] = y
    for s in range(nseq):
        st, ctx, hl = new_states[s]
        for hd in range(A_HEADS):
            hgrn_ref[s, hd] = st[hd].T
        _store_seq_state(conv_ref, lru_ref, s, ctx, hl)


def _const_spec(shape):
    nd = len(shape)
    return pl.BlockSpec(shape, lambda *_: (0,) * nd, pipeline_mode=pl.Buffered(1))


PREP_STEPS = 8
_DENSE_WEIGHTS = ("w_in", "w_a_down", "w_b_down", "w_c_down", "w_out")
_U32 = jnp.uint32


def _prep_kernel(*refs):
    nd = len(_DENSE_WEIGHTS)
    dense_in, (lru_r_in, lru_i_in) = refs[:nd], refs[nd:nd + 2]
    mem_ref, g_mem_ref, wk_in, wv_in = refs[nd + 2:nd + 6]
    outs = refs[nd + 6:]
    dense_out, (lru_r_out, lru_i_out), (k_ref, v_ref) = outs[:nd], outs[nd:nd + 2], outs[nd + 2:nd + 4]
    tile_ref, wk_ref, wv_ref = outs[nd + 4:]
    for src, dst in zip(dense_in, dense_out):
        dst[...] = pltpu.bitcast(src[...].astype(bf16), _U32)

    @pl.when(pl.program_id(0) == 0)
    def _():
        wk_ref[...] = pltpu.bitcast(wk_in[...].astype(bf16), _U32)
        wv_ref[...] = pltpu.bitcast(wv_in[...].astype(bf16), _U32)
        per = MXU_TILE // B_BLOCK_DIM
        for src, dst in ((lru_r_in, lru_r_out), (lru_i_in, lru_i_out)):
            for g in range(LRU_GROUPS):
                tile_ref[...] = jnp.zeros_like(tile_ref)
                for p in range(per):
                    lo = p * B_BLOCK_DIM
                    tile_ref[lo:lo + B_BLOCK_DIM, lo:lo + B_BLOCK_DIM] = src[g * per + p]
                dst[g] = pltpu.bitcast(tile_ref[...].astype(bf16), _U32)

    hm = _rms(mem_ref[0], g_mem_ref[...]).astype(bf16)
    k = jnp.dot(hm, _wt(wk_ref[...]), preferred_element_type=f32)
    v = jnp.dot(hm, _wt(wv_ref[...]), preferred_element_type=f32)
    for hd in range(C_HEADS):
        sl = slice(hd * HEAD_DIM, (hd + 1) * HEAD_DIM)
        k_ref[0, _head_rows(hd), :] = k[:, sl]
        v_ref[0, _head_rows(hd), :] = v[:, sl]


def _prep_weights(dense, lru_r, lru_i, mem, g_mem, w_mem_k, w_mem_v):
    bsz = mem.shape[0]
    assert bsz == PREP_STEPS
    in_specs, out_specs, out_shape = [], [], []
    for wm in dense:
        k, n = wm.shape
        assert k % (4 * SUBLANES * PREP_STEPS) == 0
        in_specs.append(pl.BlockSpec((k // PREP_STEPS, n), lambda i: (i, 0)))
        out_specs.append(pl.BlockSpec((k // (2 * PREP_STEPS), n), lambda i: (i, 0)))
        out_shape.append(jax.ShapeDtypeStruct((k // 2, n), _U32))
    blk = (B_BLOCKS, B_BLOCK_DIM, B_BLOCK_DIM)
    tiles = (LRU_GROUPS, MXU_TILE // 2, MXU_TILE)
    in_specs += [pl.BlockSpec(blk, lambda i: (0, 0, 0))] * 2
    out_specs += [pl.BlockSpec(tiles, lambda i: (0, 0, 0))] * 2
    out_shape += [jax.ShapeDtypeStruct(tiles, _U32)] * 2
    in_specs += [pl.BlockSpec((1, N_MEM, D_MODEL), lambda i: (i, 0, 0)), _const_spec((1, D_MODEL)),
                 _const_spec((D_MODEL, C_WIDTH)), _const_spec((D_MODEL, C_WIDTH))]
    out_specs += [pl.BlockSpec((1,) + KV_ROWS, lambda i: (i, 0, 0))] * 2
    out_shape += [jax.ShapeDtypeStruct((bsz,) + KV_ROWS, f32)] * 2
    outs = pl.pallas_call(
        _prep_kernel, grid=(PREP_STEPS,), in_specs=in_specs, out_specs=out_specs, out_shape=out_shape,
        scratch_shapes=[pltpu.VMEM((MXU_TILE, MXU_TILE), f32),
                        pltpu.VMEM((D_MODEL // 2, C_WIDTH), _U32), pltpu.VMEM((D_MODEL // 2, C_WIDTH), _U32)],
        compiler_params=pltpu.CompilerParams(vmem_limit_bytes=VMEM_LIMIT_BYTES,
                                             dimension_semantics=("arbitrary",)),
        name="prep_weights",
    )(*dense, lru_r, lru_i, mem, g_mem, w_mem_k, w_mem_v)
    nd = len(dense)
    return outs[:nd], outs[nd], outs[nd + 1], outs[nd + 2], outs[nd + 3]


def kernel(x_prompt, x_sample, mem_prompt, cache_mem_k, cache_mem_v, state_hgrn, state_conv, state_lru, g_mix, w_in, lb_logits, g_a_out, w_a_down, w_conv, b_conv, w_lru_r, b_lru_r, w_lru_i, b_lru_i, lru_lambda, w_b_down, g_mem, w_mem_k, w_mem_v, w_c_down, w_out, g_final):
    bsz, seq, _ = x_prompt.shape
    dec_b, dec_seq, _ = x_sample.shape
    assert g_mix.shape[0] == 1, "single-layer stack only"
    assert seq % (2 * PROMPT_TILE) == 0 and PROMPT_TILE % HGRN_CHUNK == 0

    row = lambda a: a.reshape(1, -1).astype(f32)
    dense = dict(w_in=w_in[0], w_a_down=w_a_down[0], w_b_down=w_b_down[0], w_c_down=w_c_down[0], w_out=w_out[0])
    packed, lru_r_tiles, lru_i_tiles, mk, mv = _prep_weights(
        [dense[n] for n in _DENSE_WEIGHTS], w_lru_r[0], w_lru_i[0], mem_prompt, row(g_mem[0]), w_mem_k[0], w_mem_v[0])
    packed = dict(zip(_DENSE_WEIGHTS, packed))
    weights = dict(
        g_mix=row(g_mix[0]), w_in=packed["w_in"], lb_logits=lb_logits.astype(f32), g_a_out=row(g_a_out[0]),
        w_a_down=packed["w_a_down"], w_conv=w_conv[0].astype(f32), b_conv=row(b_conv[0]),
        w_lru_r=lru_r_tiles, b_lru_r=row(b_lru_r[0]), w_lru_i=lru_i_tiles, b_lru_i=row(b_lru_i[0]),
        lru_lambda=row(lru_lambda[0]), w_b_down=packed["w_b_down"], w_c_down=packed["w_c_down"],
        w_out=packed["w_out"], g_final=row(g_final))
    wlist = [weights[n] for n in _WEIGHT_NAMES]
    wspecs = [_const_spec(a.shape) for a in wlist]

    tile = PROMPT_TILE
    n_tiles = bsz * seq // tile
    steps_per_seq = seq // (2 * tile)
    zbuf = [pltpu.VMEM((tile, Z_BLK), f32)] * Z_BLOCKS
    y_p, hgrn_p, conv_p, lru_p = pl.pallas_call(
        functools.partial(_prompt_kernel, steps_per_seq),
        grid=(n_tiles // 2,),
        in_specs=[pl.BlockSpec((2 * tile, D_MODEL), lambda j: (j, 0)),
                  pl.BlockSpec((tile, D_MODEL), lambda j: (jnp.minimum(2 * j + 2, n_tiles - 1), 0)),
                  pl.BlockSpec((1,) + KV_ROWS, lambda j: (j // steps_per_seq, 0, 0)),
                  pl.BlockSpec((1,) + KV_ROWS, lambda j: (j // steps_per_seq, 0, 0))] + wspecs,
        out_specs=[pl.BlockSpec((2 * tile, D_MODEL), lambda j: (j, 0)),
                   pl.BlockSpec((1, A_HEADS, HEAD_DIM, HEAD_DIM), lambda j: (j // steps_per_seq, 0, 0, 0)),
                   pl.BlockSpec((CTX_ROWS, bsz, B_WIDTH), lambda j: (0, 0, 0)),
                   pl.BlockSpec((bsz, B_WIDTH), lambda j: (0, 0))],
        out_shape=[jax.ShapeDtypeStruct((bsz * seq, D_MODEL), f32),
                   jax.ShapeDtypeStruct((bsz, A_HEADS, HEAD_DIM, HEAD_DIM), f32),
                   jax.ShapeDtypeStruct((CTX_ROWS, bsz, B_WIDTH), f32),
                   jax.ShapeDtypeStruct((bsz, B_WIDTH), f32)],
        scratch_shapes=[pltpu.VMEM((A_HEADS, HEAD_DIM, HEAD_DIM), f32),
                        pltpu.VMEM((CTX_ROWS, B_WIDTH), f32),
                        pltpu.VMEM((1, B_WIDTH), f32)] + zbuf + zbuf
                       + [pltpu.VMEM((tile, A_WIDTH), f32)] * 2
                       + [pltpu.VMEM((SUBLANES, B_WIDTH), f32)] * 2,
        compiler_params=pltpu.CompilerParams(vmem_limit_bytes=VMEM_LIMIT_BYTES,
                                             dimension_semantics=("arbitrary",)),
        name="prompt_layer",
    )(x_prompt.reshape(bsz * seq, D_MODEL), x_prompt.reshape(bsz * seq, D_MODEL), mk, mv, *wlist)
    y_p = y_p.reshape(bsz, seq, D_MODEL)

    rows = dec_b * dec_seq
    full = lambda shape: pl.BlockSpec(shape, lambda *_: (0,) * len(shape))
    y_s, hgrn_s, conv_s, lru_s = pl.pallas_call(
        functools.partial(_sample_kernel, dec_b, dec_seq),
        grid=(1,),
        in_specs=[full((rows, D_MODEL)), full((dec_b,) + KV_ROWS), full((dec_b,) + KV_ROWS),
                  full((dec_b, A_HEADS, HEAD_DIM, HEAD_DIM)), full((CTX_ROWS, dec_b, B_WIDTH)),
                  full((dec_b, B_WIDTH))] + wspecs,
        out_specs=[full((rows, D_MODEL)), full((dec_b, A_HEADS, HEAD_DIM, HEAD_DIM)),
                   full((CTX_ROWS, dec_b, B_WIDTH)), full((dec_b, B_WIDTH))],
        out_shape=[jax.ShapeDtypeStruct((rows, D_MODEL), f32),
                   jax.ShapeDtypeStruct((dec_b, A_HEADS, HEAD_DIM, HEAD_DIM), f32),
                   jax.ShapeDtypeStruct((CTX_ROWS, dec_b, B_WIDTH), f32),
                   jax.ShapeDtypeStruct((dec_b, B_WIDTH), f32)],
        scratch_shapes=[pltpu.VMEM((rows, Z_BLK), f32)] * Z_BLOCKS
                       + [pltpu.VMEM((rows, A_WIDTH), f32),
                        pltpu.VMEM((dec_b * SUBLANES, B_WIDTH), f32)],
        compiler_params=pltpu.CompilerParams(vmem_limit_bytes=VMEM_LIMIT_BYTES),
        name="sample_layer",
    )(x_sample.reshape(rows, D_MODEL), cache_mem_k.reshape((dec_b,) + KV_ROWS),
      cache_mem_v.reshape((dec_b,) + KV_ROWS), state_hgrn[0], jnp.swapaxes(state_conv[0], 0, 1),
      state_lru[0], *wlist)

    return (y_p, y_s.reshape(dec_b, dec_seq, D_MODEL), hgrn_p[None], jnp.swapaxes(conv_p, 0, 1)[None],
            lru_p[None], mk.reshape(1, bsz, N_MEM, C_HEADS, HEAD_DIM),
            mv.reshape(1, bsz, N_MEM, C_HEADS, HEAD_DIM), hgrn_s[None], jnp.swapaxes(conv_s, 0, 1)[None],
            lru_s[None])
```

```python
import functools

import jax
import jax.numpy as jnp
from jax import lax
from jax.experimental import pallas as pl
from jax.experimental.pallas import tpu as pltpu

f32 = jnp.float32
bf16 = jnp.bfloat16

D_MODEL = 1024
N_MEM = 256
EPS = 1e-6
A_HEADS = 4
HEAD_DIM = 128
A_WIDTH = A_HEADS * HEAD_DIM
B_WIDTH = D_MODEL
B_BLOCKS = 16
B_BLOCK_DIM = B_WIDTH // B_BLOCKS
CONV_W = 4
LRU_C = 8.0
LOG2_E = 1.4426950408889634
C_HEADS = 4
C_WIDTH = C_HEADS * HEAD_DIM
assert A_HEADS == C_HEADS
HGRN_CHUNK = 64
IN_COLS = 4 * A_WIDTH + 2 * B_WIDTH + 2 * C_WIDTH + 3 * D_MODEL

_QA, _FA, _VA, _GA = 0, A_WIDTH, 2 * A_WIDTH, 3 * A_WIDTH
_XB = 4 * A_WIDTH
_GB = _XB + B_WIDTH
_QC = _GB + B_WIDTH
_GC = _QC + C_WIDTH
_ZA = _GC + C_WIDTH
_ZB = _ZA + D_MODEL
_ZC = _ZB + D_MODEL

MXU_TILE = 256
LRU_GROUPS = B_WIDTH // MXU_TILE
SUBLANES = 8
CTX_ROWS = CONV_W - 1

PROMPT_TILE = 256
Z_BLK = 1024
Z_BLOCKS = IN_COLS // Z_BLK
VMEM_LIMIT_BYTES = 60 * 1024 * 1024


def _rms(x, g):
    return x * lax.rsqrt(jnp.mean(x * x, axis=-1, keepdims=True) + EPS) * g


def _wt(ref_or_val):
    return pltpu.bitcast(ref_or_val, bf16)


def _dot(a, b):
    return jnp.dot(a.astype(bf16), b.astype(bf16), preferred_element_type=f32)


def _dot_nt(a, b):
    return lax.dot_general(a.astype(bf16), b.astype(bf16), (((1,), (1,)), ((), ())),
                           preferred_element_type=f32)


def _dot_tn(a, b):
    return lax.dot_general(a.astype(bf16), b.astype(bf16), (((0,), (0,)), ((), ())),
                           preferred_element_type=f32)


def _silu(x):
    return x * jax.nn.sigmoid(x)


KV_ROWS = (N_MEM * C_HEADS, HEAD_DIM)


def _head_rows(hd):
    return pl.ds(hd, N_MEM, stride=C_HEADS)


def _as_column(row):
    return jnp.transpose(jnp.broadcast_to(row, (SUBLANES, row.shape[1])))[:, 0:1]


def _vreg_groups(x):
    rows, width = x.shape
    return x.reshape(rows // SUBLANES, SUBLANES, width)


def _cumprod_rows(x):
    rows = x.shape[0]
    x3 = _vreg_groups(x)
    sub = lax.broadcasted_iota(jnp.int32, x3.shape, 1)
    d = 1
    while d < SUBLANES:
        x3 = x3 * jnp.where(sub >= d, pltpu.roll(x3, d, 1), 1.0)
        d *= 2
    out, carry = [], None
    for g in range(rows // SUBLANES):
        cur = x3[g] if carry is None else x3[g] * carry
        carry = cur[SUBLANES - 1:SUBLANES, :]
        out.append(cur)
    return jnp.concatenate(out, axis=0)


def _linear_scan_rows(a, u, carry):
    rows = a.shape[0]
    a3, u3 = _vreg_groups(a), _vreg_groups(u)
    sub = lax.broadcasted_iota(jnp.int32, a3.shape, 1)
    d = 1
    while d < SUBLANES:
        keep = sub >= d
        u3 = a3 * jnp.where(keep, pltpu.roll(u3, d, 1), 0.0) + u3
        a3 = a3 * jnp.where(keep, pltpu.roll(a3, d, 1), 1.0)
        d *= 2
    out = []
    for g in range(rows // SUBLANES):
        cur = u3[g] + a3[g] * carry
        carry = cur[SUBLANES - 1:SUBLANES, :]
        out.append(cur)
    return jnp.concatenate(out, axis=0), carry


class _ZBuf:
    def __init__(self, refs):
        self.refs = refs

    def cols(self, c0, width, rows=slice(None)):
        blk, off = divmod(c0, Z_BLK)
        assert off + width <= Z_BLK
        return self.refs[blk][rows, off:off + width]


def _project_in_blocks(x, z, w):
    h = _rms(x, w["g_mix"][...]).astype(bf16)

    def block(c0):
        blk, off = divmod(c0, Z_BLK)
        z.refs[blk][:, off:off + MXU_TILE] = jnp.dot(h, _wt(w["w_in"][:, c0:c0 + MXU_TILE]),
                                                     preferred_element_type=f32)

    return [functools.partial(block, c0) for c0 in range(0, IN_COLS, MXU_TILE)]


def _project_in(x, z, w):
    for block in _project_in_blocks(x, z, w):
        block()


def _interleave(stages, blocks):
    blocks = list(blocks)
    while True:
        try:
            n = next(stages)
        except StopIteration as done:
            result = done.value
            break
        for _ in range(min(n, len(blocks))):
            blocks.pop(0)()
    for block in blocks:
        block()
    return result


def _mix(*args, **kwargs):
    return _interleave(_mix_stages(*args, **kwargs), [])


def _mix_stages(x, z, oa_ref, xpad_ref, kv, states, w, *, seg, chunk, first_rows_start):
    rows = x.shape[0]
    nseg = rows // seg
    nchunk = seg // chunk
    chunk_rows = [[slice(s * seg + c * chunk, s * seg + (c + 1) * chunk) for c in range(nchunk)]
                  for s in range(nseg)]
    all_chunks = [rs for per_seg in chunk_rows for rs in per_seg]
    head_sl = [slice(hd * HEAD_DIM, (hd + 1) * HEAD_DIM) for hd in range(A_HEADS)]
    cat_rows = lambda parts: parts[0] if len(parts) == 1 else jnp.concatenate(parts, axis=0)

    lg = w["lb_logits"][...]
    l0, l1 = lg[0:1, :], lg[1:2, :]
    lmax = jnp.maximum(l0, l1)
    e0, e1 = jnp.exp(l0 - lmax), jnp.exp(l1 - lmax)
    lb = e0 / (e0 + e1)

    qg, kg, v, kd, decay = [], [], [], [], []
    for rs in all_chunks:
        yield 1
        f = lb + (1.0 - lb) * jax.nn.sigmoid(z.cols(_FA, A_WIDTH, rs))
        p = _cumprod_rows(f)
        inv_p = 1.0 / p
        k = 1.0 - f
        qg.append(_silu(z.cols(_QA, A_WIDTH, rs)) * p)
        kg.append(k * inv_p)
        v.append(z.cols(_VA, A_WIDTH, rs))
        p_last = p[chunk - 1:chunk, :]
        kd.append(k * (p_last * inv_p))
        decay.append(p_last)
    qg_all, kg_all, v_all = cat_rows(qg), cat_rows(kg), cat_rows(v)

    scores = [_dot_nt(qg_all[:, sl], kg_all[:, sl]) for sl in head_sl]
    st_in = [[[states[s][0][hd]] for hd in range(A_HEADS)] for s in range(nseg)]
    for s in range(nseg):
        for c in range(nchunk):
            i = s * nchunk + c
            for hd, sl in enumerate(head_sl):
                st_in[s][hd].append(st_in[s][hd][c] * _as_column(decay[i][:, sl])
                                    + _dot_tn(kd[i][:, sl], v[i][:, sl]))
    new_st = [[st_in[s][hd][nchunk] for hd in range(A_HEADS)] for s in range(nseg)]
    yield 1
    tt = lax.broadcasted_iota(jnp.int32, (rows, rows), 0)
    ss = lax.broadcasted_iota(jnp.int32, (rows, rows), 1)
    shift = chunk.bit_length() - 1
    causal = ((tt >> shift) == (ss >> shift)) & (ss <= tt)
    o_intra = [_dot(jnp.where(causal, scores[hd], 0.0), v_all[:, sl]) for hd, sl in enumerate(head_sl)]
    yield 1
    for hd, sl in enumerate(head_sl):
        for s in range(nseg):
            for c, rs in enumerate(chunk_rows[s]):
                oa_ref[rs, sl] = o_intra[hd][rs, :] + _dot(qg[s * nchunk + c][:, sl], st_in[s][hd][c])

    g_a = w["g_a_out"][...]
    a_in = []
    for i, rs in enumerate(all_chunks):
        yield i % 2
        normed = jnp.concatenate([_rms(oa_ref[rs, sl], g_a[:, sl]) for sl in head_sl], axis=-1)
        a_in.append(normed * _silu(z.cols(_GA, A_WIDTH, rs)))
    pa = _dot(cat_rows(a_in), _wt(w["w_a_down"][...]))

    pad = SUBLANES
    for s in range(nseg):
        xpad_ref[s * pad:(s + 1) * pad, :] = jnp.zeros((pad, B_WIDTH), f32)
        xpad_ref[(s + 1) * pad - CTX_ROWS:(s + 1) * pad, :] = states[s][1]
    w_conv = w["w_conv"][...]
    sub = lax.broadcasted_iota(jnp.int32, (chunk // SUBLANES, SUBLANES, B_WIDTH), 1)
    xc, new_ctx = [], []
    for s in range(nseg):
        for c, rs in enumerate(chunk_rows[s]):
            yield 1
            if c == 0:
                ext = jnp.concatenate([xpad_ref[s * pad:(s + 1) * pad, :], z.cols(_XB, B_WIDTH, rs)], axis=0)
            else:
                ext = z.cols(_XB, B_WIDTH, slice(rs.start - pad, rs.stop))
            ext = _vreg_groups(ext)
            acc = w_conv[CONV_W - 1:CONV_W, :] * ext[1:]
            for j in range(1, CONV_W):
                rolled = pltpu.roll(ext, j, 1)
                shifted = jnp.where(sub >= j, rolled[1:], rolled[:-1])
                acc = acc + w_conv[CONV_W - 1 - j:CONV_W - j, :] * shifted
            xc.append(w["b_conv"][...] + acc.reshape(chunk, B_WIDTH))
        new_ctx.append(z.cols(_XB, B_WIDTH, slice((s + 1) * seg - CTX_ROWS, (s + 1) * seg)))

    xc_b = cat_rows(xc).astype(bf16)
    r_pre, i_pre = [], []
    for g in range(LRU_GROUPS):
        gs = slice(g * MXU_TILE, (g + 1) * MXU_TILE)
        r_pre.append(jnp.dot(xc_b[:, gs], _wt(w["w_lru_r"][g]), preferred_element_type=f32))
        i_pre.append(jnp.dot(xc_b[:, gs], _wt(w["w_lru_i"][g]), preferred_element_type=f32))
    r_pre, i_pre = jnp.concatenate(r_pre, axis=-1), jnp.concatenate(i_pre, axis=-1)
    neg_lam = -w["lru_lambda"][...]
    softplus = jnp.maximum(neg_lam, 0.0) + jnp.log1p(jnp.exp(-jnp.abs(neg_lam)))
    a_l, u_l = [], []
    for s in range(nseg):
        for c, rs in enumerate(chunk_rows[s]):
            yield 2
            r = jax.nn.sigmoid(r_pre[rs, :] + w["b_lru_r"][...])
            ig = jax.nn.sigmoid(i_pre[rs, :] + w["b_lru_i"][...])
            log_a = -LRU_C * r * softplus
            a = jnp.exp(log_a)
            m2 = -jnp.tanh(log_a) * (a * a + 1.0)
            mult = jnp.where(m2 > 0.0, m2 * lax.rsqrt(m2), 0.0)
            if first_rows_start is not None and c == 0:
                first_row = lax.broadcasted_iota(jnp.int32, mult.shape, 0) == 0
                mult = jnp.where(first_row & first_rows_start, 1.0, mult)
            a_l.append(a)
            u_l.append(mult * ig * xc[s * nchunk + c])
    hb_gated, new_hl = [], []
    for s in range(nseg):
        carry = states[s][2]
        for c, rs in enumerate(chunk_rows[s]):
            yield 1
            i = s * nchunk + c
            hb, carry = _linear_scan_rows(a_l[i], u_l[i], carry)
            hb_gated.append(hb * _silu(z.cols(_GB, B_WIDTH, rs)))
        new_hl.append(carry)
    pb = _dot(cat_rows(hb_gated), _wt(w["w_b_down"][...]))

    scale = HEAD_DIM ** -0.5
    mem = [[kv(s, hd) for hd in range(C_HEADS)] for s in range(nseg)]
    seg_rows = [slice(s * seg, (s + 1) * seg) for s in range(nseg)]
    sc = [[_dot_nt(z.cols(_QC + hd * HEAD_DIM, HEAD_DIM, seg_rows[s]), mem[s][hd][0])
           for hd in range(C_HEADS)] for s in range(nseg)]
    pr = []
    for s in range(nseg):
        pr.append([])
        for hd in range(C_HEADS):
            yield 1
            p = jnp.exp2((sc[s][hd] - jnp.max(sc[s][hd], axis=-1, keepdims=True)) * (scale * LOG2_E))
            pr[s].append(p / jnp.sum(p, axis=-1, keepdims=True))
    oc = cat_rows([jnp.concatenate([_dot(pr[s][hd], mem[s][hd][1]) for hd in range(C_HEADS)], axis=-1)
                   for s in range(nseg)])
    yield 1
    pc = _dot(oc * _silu(z.cols(_GC, C_WIDTH)), _wt(w["w_c_down"][...]))

    merged = []
    for rs in all_chunks:
        yield 1
        merged.append(jax.nn.sigmoid(z.cols(_ZA, D_MODEL, rs)) * pa[rs, :]
                      + jax.nn.sigmoid(z.cols(_ZB, D_MODEL, rs)) * pb[rs, :]
                      + jax.nn.sigmoid(z.cols(_ZC, D_MODEL, rs)) * pc[rs, :])
    y = x + _dot(cat_rows(merged), _wt(w["w_out"][...]))
    y = _rms(y, w["g_final"][...])
    new_states = [(new_st[s], new_ctx[s], new_hl[s]) for s in range(nseg)]
    return y, new_states


_WEIGHT_NAMES = ("g_mix", "w_in", "lb_logits", "g_a_out", "w_a_down", "w_conv", "b_conv", "w_lru_r", "b_lru_r",
                 "w_lru_i", "b_lru_i", "lru_lambda", "w_b_down", "w_c_down", "w_out", "g_final")
_NW = len(_WEIGHT_NAMES)


def _store_seq_state(conv_ref, lru_ref, seq_idx, ctx, hl):
    for r in range(CTX_ROWS):
        conv_ref[r, pl.ds(seq_idx, 1), :] = ctx[r:r + 1, :]
    lru_ref[pl.ds(seq_idx, 1), :] = hl


def _prompt_kernel(steps_per_seq, *refs):
    x_ref, xn_ref, mk_ref, mv_ref = refs[:4]
    w = dict(zip(_WEIGHT_NAMES, refs[4:4 + _NW]))
    y_ref, hgrn_ref, conv_ref, lru_ref = refs[4 + _NW:8 + _NW]
    scratch = refs[8 + _NW:]
    st_ref, ctx_ref, hl_ref = scratch[:3]
    z_even, z_odd = _ZBuf(scratch[3:3 + Z_BLOCKS]), _ZBuf(scratch[3 + Z_BLOCKS:3 + 2 * Z_BLOCKS])
    oa_refs = scratch[3 + 2 * Z_BLOCKS:5 + 2 * Z_BLOCKS]
    xpad_refs = scratch[5 + 2 * Z_BLOCKS:7 + 2 * Z_BLOCKS]
    j = pl.program_id(0)
    tile = PROMPT_TILE
    seq_start = (j % steps_per_seq) == 0

    @pl.when(j == 0)
    def _():
        _project_in(x_ref[0:tile, :], z_even, w)

    @pl.when(seq_start)
    def _():
        st_ref[...] = jnp.zeros_like(st_ref)
        ctx_ref[...] = jnp.zeros_like(ctx_ref)
        hl_ref[...] = jnp.zeros_like(hl_ref)

    kv = lambda s, hd: (mk_ref[0, _head_rows(hd), :], mv_ref[0, _head_rows(hd), :])
    states = [([st_ref[hd] for hd in range(A_HEADS)], ctx_ref[...], hl_ref[...])]

    y, states = _interleave(
        _mix_stages(x_ref[0:tile, :], z_even, oa_refs[0], xpad_refs[0], kv, states, w,
                    seg=tile, chunk=HGRN_CHUNK, first_rows_start=seq_start),
        _project_in_blocks(x_ref[tile:2 * tile, :], z_odd, w))
    y_ref[0:tile, :] = y

    y, states = _interleave(
        _mix_stages(x_ref[tile:2 * tile, :], z_odd, oa_refs[1], xpad_refs[1], kv, states, w,
                    seg=tile, chunk=HGRN_CHUNK, first_rows_start=None),
        _project_in_blocks(xn_ref[...], z_even, w))
    y_ref[tile:2 * tile, :] = y

    st, ctx, hl = states[0]
    for hd in range(A_HEADS):
        st_ref[hd] = st[hd]
    ctx_ref[...] = ctx
    hl_ref[...] = hl

    @pl.when((j % steps_per_seq) == steps_per_seq - 1)
    def _():
        for hd in range(A_HEADS):
            hgrn_ref[0, hd] = st[hd]
        _store_seq_state(conv_ref, lru_ref, j // steps_per_seq, ctx, hl)


def _sample_kernel(nseq, seg, *refs):
    x_ref, mk_ref, mv_ref, hgrn_in, conv_in, lru_in = refs[:6]
    w = dict(zip(_WEIGHT_NAMES, refs[6:6 + _NW]))
    y_ref, hgrn_ref, conv_ref, lru_ref = refs[6 + _NW:10 + _NW]
    scratch = refs[10 + _NW:]
    z = _ZBuf(scratch[:Z_BLOCKS])
    oa_ref, xpad_ref = scratch[Z_BLOCKS:]
    states = [([hgrn_in[s, hd] for hd in range(A_HEADS)],
               jnp.concatenate([conv_in[r, s:s + 1, :] for r in range(CTX_ROWS)], axis=0),
               lru_in[s:s + 1, :]) for s in range(nseq)]
    _project_in(x_ref[...], z, w)
    y, new_states = _mix(
        x_ref[...], z, oa_ref, xpad_ref,
        lambda s, hd: (mk_ref[s, _head_rows(hd), :], mv_ref[s, _head_rows(hd), :]), states, w,
        seg=seg, chunk=min(HGRN_CHUNK, seg), first_rows_start=None)
    y_ref[...] = y
    for s in range(nseq):
        st, ctx, hl = new_states[s]
        for hd in range(A_HEADS):
            hgrn_ref[s, hd] = st[hd]
        _store_seq_state(conv_ref, lru_ref, s, ctx, hl)


def _const_spec(shape):
    nd = len(shape)
    return pl.BlockSpec(shape, lambda *_: (0,) * nd, pipeline_mode=pl.Buffered(1))


PREP_STEPS = 8
_DENSE_WEIGHTS = ("w_in", "w_a_down", "w_b_down", "w_c_down", "w_out")
_U32 = jnp.uint32


def _prep_kernel(*refs):
    nd = len(_DENSE_WEIGHTS)
    dense_in, (lru_r_in, lru_i_in) = refs[:nd], refs[nd:nd + 2]
    mem_ref, g_mem_ref, wk_in, wv_in = refs[nd + 2:nd + 6]
    outs = refs[nd + 6:]
    dense_out, (lru_r_out, lru_i_out), (k_ref, v_ref) = outs[:nd], outs[nd:nd + 2], outs[nd + 2:nd + 4]
    tile_ref, wk_ref, wv_ref = outs[nd + 4:]
    for src, dst in zip(dense_in, dense_out):
        dst[...] = pltpu.bitcast(src[...].astype(bf16), _U32)

    @pl.when(pl.program_id(0) == 0)
    def _():
        wk_ref[...] = pltpu.bitcast(wk_in[...].astype(bf16), _U32)
        wv_ref[...] = pltpu.bitcast(wv_in[...].astype(bf16), _U32)
        per = MXU_TILE // B_BLOCK_DIM
        for src, dst in ((lru_r_in, lru_r_out), (lru_i_in, lru_i_out)):
            for g in range(LRU_GROUPS):
                tile_ref[...] = jnp.zeros_like(tile_ref)
                for p in range(per):
                    lo = p * B_BLOCK_DIM
                    tile_ref[lo:lo + B_BLOCK_DIM, lo:lo + B_BLOCK_DIM] = src[g * per + p]
                dst[g] = pltpu.bitcast(tile_ref[...].astype(bf16), _U32)

    hm = _rms(mem_ref[0], g_mem_ref[...]).astype(bf16)
    k = jnp.dot(hm, _wt(wk_ref[...]), preferred_element_type=f32)
    v = jnp.dot(hm, _wt(wv_ref[...]), preferred_element_type=f32)
    for hd in range(C_HEADS):
        sl = slice(hd * HEAD_DIM, (hd + 1) * HEAD_DIM)
        k_ref[0, _head_rows(hd), :] = k[:, sl]
        v_ref[0, _head_rows(hd), :] = v[:, sl]


def _prep_weights(dense, lru_r, lru_i, mem, g_mem, w_mem_k, w_mem_v):
    bsz = mem.shape[0]
    assert bsz == PREP_STEPS
    in_specs, out_specs, out_shape = [], [], []
    for wm in dense:
        k, n = wm.shape
        assert k % (4 * SUBLANES * PREP_STEPS) == 0
        in_specs.append(pl.BlockSpec((k // PREP_STEPS, n), lambda i: (i, 0)))
        out_specs.append(pl.BlockSpec((k // (2 * PREP_STEPS), n), lambda i: (i, 0)))
        out_shape.append(jax.ShapeDtypeStruct((k // 2, n), _U32))
    blk = (B_BLOCKS, B_BLOCK_DIM, B_BLOCK_DIM)
    tiles = (LRU_GROUPS, MXU_TILE // 2, MXU_TILE)
    in_specs += [pl.BlockSpec(blk, lambda i: (0, 0, 0))] * 2
    out_specs += [pl.BlockSpec(tiles, lambda i: (0, 0, 0))] * 2
    out_shape += [jax.ShapeDtypeStruct(tiles, _U32)] * 2
    in_specs += [pl.BlockSpec((1, N_MEM, D_MODEL), lambda i: (i, 0, 0)), _const_spec((1, D_MODEL)),
                 _const_spec((D_MODEL, C_WIDTH)), _const_spec((D_MODEL, C_WIDTH))]
    out_specs += [pl.BlockSpec((1,) + KV_ROWS, lambda i: (i, 0, 0))] * 2
    out_shape += [jax.ShapeDtypeStruct((bsz,) + KV_ROWS, f32)] * 2
    outs = pl.pallas_call(
        _prep_kernel, grid=(PREP_STEPS,), in_specs=in_specs, out_specs=out_specs, out_shape=out_shape,
        scratch_shapes=[pltpu.VMEM((MXU_TILE, MXU_TILE), f32),
                        pltpu.VMEM((D_MODEL // 2, C_WIDTH), _U32), pltpu.VMEM((D_MODEL // 2, C_WIDTH), _U32)],
        compiler_params=pltpu.CompilerParams(vmem_limit_bytes=VMEM_LIMIT_BYTES,
                                             dimension_semantics=("arbitrary",)),
        name="prep_weights",
    )(*dense, lru_r, lru_i, mem, g_mem, w_mem_k, w_mem_v)
    nd = len(dense)
    return outs[:nd], outs[nd], outs[nd + 1], outs[nd + 2], outs[nd + 3]


def kernel(x_prompt, x_sample, mem_prompt, cache_mem_k, cache_mem_v, state_hgrn, state_conv, state_lru, g_mix, w_in, lb_logits, g_a_out, w_a_down, w_conv, b_conv, w_lru_r, b_lru_r, w_lru_i, b_lru_i, lru_lambda, w_b_down, g_mem, w_mem_k, w_mem_v, w_c_down, w_out, g_final):
    bsz, seq, _ = x_prompt.shape
    dec_b, dec_seq, _ = x_sample.shape
    assert g_mix.shape[0] == 1, "single-layer stack only"
    assert seq % (2 * PROMPT_TILE) == 0 and PROMPT_TILE % HGRN_CHUNK == 0

    row = lambda a: a.reshape(1, -1).astype(f32)
    dense = dict(w_in=w_in[0], w_a_down=w_a_down[0], w_b_down=w_b_down[0], w_c_down=w_c_down[0], w_out=w_out[0])
    packed, lru_r_tiles, lru_i_tiles, mk, mv = _prep_weights(
        [dense[n] for n in _DENSE_WEIGHTS], w_lru_r[0], w_lru_i[0], mem_prompt, row(g_mem[0]), w_mem_k[0], w_mem_v[0])
    packed = dict(zip(_DENSE_WEIGHTS, packed))
    weights = dict(
        g_mix=row(g_mix[0]), w_in=packed["w_in"], lb_logits=lb_logits.astype(f32), g_a_out=row(g_a_out[0]),
        w_a_down=packed["w_a_down"], w_conv=w_conv[0].astype(f32), b_conv=row(b_conv[0]),
        w_lru_r=lru_r_tiles, b_lru_r=row(b_lru_r[0]), w_lru_i=lru_i_tiles, b_lru_i=row(b_lru_i[0]),
        lru_lambda=row(lru_lambda[0]), w_b_down=packed["w_b_down"], w_c_down=packed["w_c_down"],
        w_out=packed["w_out"], g_final=row(g_final))
    wlist = [weights[n] for n in _WEIGHT_NAMES]
    wspecs = [_const_spec(a.shape) for a in wlist]

    tile = PROMPT_TILE
    n_tiles = bsz * seq // tile
    steps_per_seq = seq // (2 * tile)
    zbuf = [pltpu.VMEM((tile, Z_BLK), f32)] * Z_BLOCKS
    y_p, hgrn_p, conv_p, lru_p = pl.pallas_call(
        functools.partial(_prompt_kernel, steps_per_seq),
        grid=(n_tiles // 2,),
        in_specs=[pl.BlockSpec((2 * tile, D_MODEL), lambda j: (j, 0)),
                  pl.BlockSpec((tile, D_MODEL), lambda j: (jnp.minimum(2 * j + 2, n_tiles - 1), 0)),
                  pl.BlockSpec((1,) + KV_ROWS, lambda j: (j // steps_per_seq, 0, 0)),
                  pl.BlockSpec((1,) + KV_ROWS, lambda j: (j // steps_per_seq, 0, 0))] + wspecs,
        out_specs=[pl.BlockSpec((2 * tile, D_MODEL), lambda j: (j, 0)),
                   pl.BlockSpec((1, A_HEADS, HEAD_DIM, HEAD_DIM), lambda j: (j // steps_per_seq, 0, 0, 0)),
                   pl.BlockSpec((CTX_ROWS, bsz, B_WIDTH), lambda j: (0, 0, 0)),
                   pl.BlockSpec((bsz, B_WIDTH), lambda j: (0, 0))],
        out_shape=[jax.ShapeDtypeStruct((bsz * seq, D_MODEL), f32),
                   jax.ShapeDtypeStruct((bsz, A_HEADS, HEAD_DIM, HEAD_DIM), f32),
                   jax.ShapeDtypeStruct((CTX_ROWS, bsz, B_WIDTH), f32),
                   jax.ShapeDtypeStruct((bsz, B_WIDTH), f32)],
        scratch_shapes=[pltpu.VMEM((A_HEADS, HEAD_DIM, HEAD_DIM), f32),
                        pltpu.VMEM((CTX_ROWS, B_WIDTH), f32),
                        pltpu.VMEM((1, B_WIDTH), f32)] + zbuf + zbuf
                       + [pltpu.VMEM((tile, A_WIDTH), f32)] * 2
                       + [pltpu.VMEM((SUBLANES, B_WIDTH), f32)] * 2,
        compiler_params=pltpu.CompilerParams(vmem_limit_bytes=VMEM_LIMIT_BYTES,
                                             dimension_semantics=("arbitrary",)),
        name="prompt_layer",
    )(x_prompt.reshape(bsz * seq, D_MODEL), x_prompt.reshape(bsz * seq, D_MODEL), mk, mv, *wlist)
    y_p = y_p.reshape(bsz, seq, D_MODEL)

    rows = dec_b * dec_seq
    full = lambda shape: pl.BlockSpec(shape, lambda *_: (0,) * len(shape))
    y_s, hgrn_s, conv_s, lru_s = pl.pallas_call(
        functools.partial(_sample_kernel, dec_b, dec_seq),
        grid=(1,),
        in_specs=[full((rows, D_MODEL)), full((dec_b,) + KV_ROWS), full((dec_b,) + KV_ROWS),
                  full((dec_b, A_HEADS, HEAD_DIM, HEAD_DIM)), full((CTX_ROWS, dec_b, B_WIDTH)),
                  full((dec_b, B_WIDTH))] + wspecs,
        out_specs=[full((rows, D_MODEL)), full((dec_b, A_HEADS, HEAD_DIM, HEAD_DIM)),
                   full((CTX_ROWS, dec_b, B_WIDTH)), full((dec_b, B_WIDTH))],
        out_shape=[jax.ShapeDtypeStruct((rows, D_MODEL), f32),
                   jax.ShapeDtypeStruct((dec_b, A_HEADS, HEAD_DIM, HEAD_DIM), f32),
                   jax.ShapeDtypeStruct((CTX_ROWS, dec_b, B_WIDTH), f32),
                   jax.ShapeDtypeStruct((dec_b, B_WIDTH), f32)],
        scratch_shapes=[pltpu.VMEM((rows, Z_BLK), f32)] * Z_BLOCKS
                       + [pltpu.VMEM((rows, A_WIDTH), f32),
                        pltpu.VMEM((dec_b * SUBLANES, B_WIDTH), f32)],
        compiler_params=pltpu.CompilerParams(vmem_limit_bytes=VMEM_LIMIT_BYTES),
        name="sample_layer",
    )(x_sample.reshape(rows, D_MODEL), cache_mem_k.reshape((dec_b,) + KV_ROWS),
      cache_mem_v.reshape((dec_b,) + KV_ROWS), state_hgrn[0], jnp.swapaxes(state_conv[0], 0, 1),
      state_lru[0], *wlist)

    return (y_p, y_s.reshape(dec_b, dec_seq, D_MODEL), hgrn_p[None], jnp.swapaxes(conv_p, 0, 1)[None],
            lru_p[None], mk.reshape(1, bsz, N_MEM, C_HEADS, HEAD_DIM),
            mv.reshape(1, bsz, N_MEM, C_HEADS, HEAD_DIM), hgrn_s[None], jnp.swapaxes(conv_s, 0, 1)[None],
            lru_s[None])
```

```python
import functools

import jax
import jax.numpy as jnp
from jax import lax
from jax.experimental import pallas as pl
from jax.experimental.pallas import tpu as pltpu

f32 = jnp.float32
bf16 = jnp.bfloat16

D_MODEL = 1024
N_MEM = 256
EPS = 1e-6
A_HEADS = 4
HEAD_DIM = 128
A_WIDTH = A_HEADS * HEAD_DIM
B_WIDTH = D_MODEL
B_BLOCKS = 16
B_BLOCK_DIM = B_WIDTH // B_BLOCKS
CONV_W = 4
LRU_C = 8.0
LOG2_E = 1.4426950408889634
C_HEADS = 4
C_WIDTH = C_HEADS * HEAD_DIM
assert A_HEADS == C_HEADS
HGRN_CHUNK = 64
IN_COLS = 4 * A_WIDTH + 2 * B_WIDTH + 2 * C_WIDTH + 3 * D_MODEL

_QA, _FA, _VA, _GA = 0, A_WIDTH, 2 * A_WIDTH, 3 * A_WIDTH
_XB = 4 * A_WIDTH
_GB = _XB + B_WIDTH
_QC = _GB + B_WIDTH
_GC = _QC + C_WIDTH
_ZA = _GC + C_WIDTH
_ZB = _ZA + D_MODEL
_ZC = _ZB + D_MODEL

MXU_TILE = 256
LRU_GROUPS = B_WIDTH // MXU_TILE
SUBLANES = 8
CTX_ROWS = CONV_W - 1

PROMPT_TILE = 256
Z_BLK = 1024
Z_BLOCKS = IN_COLS // Z_BLK
VMEM_LIMIT_BYTES = 60 * 1024 * 1024


def _rms(x, g):
    return x * lax.rsqrt(jnp.mean(x * x, axis=-1, keepdims=True) + EPS) * g


def _wt(ref_or_val):
    return pltpu.bitcast(ref_or_val, bf16)


def _dot(a, b):
    return jnp.dot(a.astype(bf16), b.astype(bf16), preferred_element_type=f32)


def _dot_nt(a, b):
    return lax.dot_general(a.astype(bf16), b.astype(bf16), (((1,), (1,)), ((), ())),
                           preferred_element_type=f32)


def _dot_tn(a, b):
    return lax.dot_general(a.astype(bf16), b.astype(bf16), (((0,), (0,)), ((), ())),
                           preferred_element_type=f32)


def _silu(x):
    return x * jax.nn.sigmoid(x)


KV_ROWS = (N_MEM * C_HEADS, HEAD_DIM)


def _head_rows(hd):
    return pl.ds(hd, N_MEM, stride=C_HEADS)


def _as_column(row):
    return jnp.transpose(jnp.broadcast_to(row, (SUBLANES, row.shape[1])))[:, 0:1]


def _vreg_groups(x):
    rows, width = x.shape
    return x.reshape(rows // SUBLANES, SUBLANES, width)


def _cumprod_rows(x):
    rows = x.shape[0]
    x3 = _vreg_groups(x)
    sub = lax.broadcasted_iota(jnp.int32, x3.shape, 1)
    d = 1
    while d < SUBLANES:
        x3 = x3 * jnp.where(sub >= d, pltpu.roll(x3, d, 1), 1.0)
        d *= 2
    out, carry = [], None
    for g in range(rows // SUBLANES):
        cur = x3[g] if carry is None else x3[g] * carry
        carry = cur[SUBLANES - 1:SUBLANES, :]
        out.append(cur)
    return jnp.concatenate(out, axis=0)


def _linear_scan_rows(a, u, carry):
    rows = a.shape[0]
    a3, u3 = _vreg_groups(a), _vreg_groups(u)
    sub = lax.broadcasted_iota(jnp.int32, a3.shape, 1)
    d = 1
    while d < SUBLANES:
        keep = sub >= d
        u3 = a3 * jnp.where(keep, pltpu.roll(u3, d, 1), 0.0) + u3
        a3 = a3 * jnp.where(keep, pltpu.roll(a3, d, 1), 1.0)
        d *= 2
    out = []
    for g in range(rows // SUBLANES):
        cur = u3[g] + a3[g] * carry
        carry = cur[SUBLANES - 1:SUBLANES, :]
        out.append(cur)
    return jnp.concatenate(out, axis=0), carry


class _ZBuf:
    def __init__(self, refs):
        self.refs = refs

    def cols(self, c0, width, rows=slice(None)):
        blk, off = divmod(c0, Z_BLK)
        assert off + width <= Z_BLK
        return self.refs[blk][rows, off:off + width]


def _project_in_blocks(x, z, w):
    h = _rms(x, w["g_mix"][...]).astype(bf16)

    def block(c0):
        blk, off = divmod(c0, Z_BLK)
        z.refs[blk][:, off:off + MXU_TILE] = jnp.dot(h, _wt(w["w_in"][:, c0:c0 + MXU_TILE]),
                                                     preferred_element_type=f32)

    return [functools.partial(block, c0) for c0 in range(0, IN_COLS, MXU_TILE)]


def _project_in(x, z, w):
    for block in _project_in_blocks(x, z, w):
        block()


def _interleave(stages, blocks):
    blocks = list(blocks)
    while True:
        try:
            n = next(stages)
        except StopIteration as done:
            result = done.value
            break
        for _ in range(min(n, len(blocks))):
            blocks.pop(0)()
    for block in blocks:
        block()
    return result


def _mix(*args, **kwargs):
    return _interleave(_mix_stages(*args, **kwargs), [])


def _mix_stages(x, z, oa_ref, xpad_ref, kv, states, w, *, seg, chunk, first_rows_start):
    rows = x.shape[0]
    nseg = rows // seg
    nchunk = seg // chunk
    chunk_rows = [[slice(s * seg + c * chunk, s * seg + (c + 1) * chunk) for c in range(nchunk)]
                  for s in range(nseg)]
    all_chunks = [rs for per_seg in chunk_rows for rs in per_seg]
    head_sl = [slice(hd * HEAD_DIM, (hd + 1) * HEAD_DIM) for hd in range(A_HEADS)]
    cat_rows = lambda parts: parts[0] if len(parts) == 1 else jnp.concatenate(parts, axis=0)

    lg = w["lb_logits"][...]
    l0, l1 = lg[0:1, :], lg[1:2, :]
    lmax = jnp.maximum(l0, l1)
    e0, e1 = jnp.exp(l0 - lmax), jnp.exp(l1 - lmax)
    lb = e0 / (e0 + e1)

    qg, kg, v, kd, decay = [], [], [], [], []
    for rs in all_chunks:
        yield 1
        f = lb + (1.0 - lb) * jax.nn.sigmoid(z.cols(_FA, A_WIDTH, rs))
        p = _cumprod_rows(f)
        inv_p = 1.0 / p
        k = 1.0 - f
        qg.append(_silu(z.cols(_QA, A_WIDTH, rs)) * p)
        kg.append(k * inv_p)
        v.append(z.cols(_VA, A_WIDTH, rs))
        p_last = p[chunk - 1:chunk, :]
        kd.append(kg[-1] * p_last)
        decay.append(p_last)
    qg_all, kg_all, v_all = cat_rows(qg), cat_rows(kg), cat_rows(v)

    scores = [_dot_nt(qg_all[:, sl], kg_all[:, sl]) for sl in head_sl]
    st_in = [[[states[s][0][hd]] for hd in range(A_HEADS)] for s in range(nseg)]
    for s in range(nseg):
        for c in range(nchunk):
            i = s * nchunk + c
            for hd, sl in enumerate(head_sl):
                st_in[s][hd].append(st_in[s][hd][c] * _as_column(decay[i][:, sl])
                                    + _dot_tn(kd[i][:, sl], v[i][:, sl]))
    new_st = [[st_in[s][hd][nchunk] for hd in range(A_HEADS)] for s in range(nseg)]
    yield 1
    tt = lax.broadcasted_iota(jnp.int32, (rows, rows), 0)
    ss = lax.broadcasted_iota(jnp.int32, (rows, rows), 1)
    shift = chunk.bit_length() - 1
    causal = ((tt >> shift) == (ss >> shift)) & (ss <= tt)
    o_intra = [_dot(jnp.where(causal, scores[hd], 0.0), v_all[:, sl]) for hd, sl in enumerate(head_sl)]
    yield 1
    for hd, sl in enumerate(head_sl):
        for s in range(nseg):
            for c, rs in enumerate(chunk_rows[s]):
                oa_ref[rs, sl] = o_intra[hd][rs, :] + _dot(qg[s * nchunk + c][:, sl], st_in[s][hd][c])

    g_a = w["g_a_out"][...]
    a_in = []
    for i, rs in enumerate(all_chunks):
        yield i % 2
        normed = jnp.concatenate([_rms(oa_ref[rs, sl], g_a[:, sl]) for sl in head_sl], axis=-1)
        a_in.append(normed * _silu(z.cols(_GA, A_WIDTH, rs)))
    pa = _dot(cat_rows(a_in), _wt(w["w_a_down"][...]))

    pad = SUBLANES
    for s in range(nseg):
        xpad_ref[s * pad:(s + 1) * pad, :] = jnp.zeros((pad, B_WIDTH), f32)
        xpad_ref[(s + 1) * pad - CTX_ROWS:(s + 1) * pad, :] = states[s][1]
    w_conv = w["w_conv"][...]
    sub = lax.broadcasted_iota(jnp.int32, (chunk // SUBLANES, SUBLANES, B_WIDTH), 1)
    xc, new_ctx = [], []
    for s in range(nseg):
        for c, rs in enumerate(chunk_rows[s]):
            yield 1
            if c == 0:
                ext = jnp.concatenate([xpad_ref[s * pad:(s + 1) * pad, :], z.cols(_XB, B_WIDTH, rs)], axis=0)
            else:
                ext = z.cols(_XB, B_WIDTH, slice(rs.start - pad, rs.stop))
            ext = _vreg_groups(ext)
            acc = w_conv[CONV_W - 1:CONV_W, :] * ext[1:]
            for j in range(1, CONV_W):
                rolled = pltpu.roll(ext, j, 1)
                shifted = jnp.where(sub >= j, rolled[1:], rolled[:-1])
                acc = acc + w_conv[CONV_W - 1 - j:CONV_W - j, :] * shifted
            xc.append(w["b_conv"][...] + acc.reshape(chunk, B_WIDTH))
        new_ctx.append(z.cols(_XB, B_WIDTH, slice((s + 1) * seg - CTX_ROWS, (s + 1) * seg)))

    xc_b = cat_rows(xc).astype(bf16)
    r_pre, i_pre = [], []
    for g in range(LRU_GROUPS):
        gs = slice(g * MXU_TILE, (g + 1) * MXU_TILE)
        r_pre.append(jnp.dot(xc_b[:, gs], _wt(w["w_lru_r"][g]), preferred_element_type=f32))
        i_pre.append(jnp.dot(xc_b[:, gs], _wt(w["w_lru_i"][g]), preferred_element_type=f32))
    r_pre, i_pre = jnp.concatenate(r_pre, axis=-1), jnp.concatenate(i_pre, axis=-1)
    neg_lam = -w["lru_lambda"][...]
    softplus = jnp.maximum(neg_lam, 0.0) + jnp.log1p(jnp.exp(-jnp.abs(neg_lam)))
    decay_rate = LRU_C * softplus
    a_l, u_l = [], []
    for s in range(nseg):
        for c, rs in enumerate(chunk_rows[s]):
            yield 2
            r = jax.nn.sigmoid(r_pre[rs, :] + w["b_lru_r"][...])
            ig = jax.nn.sigmoid(i_pre[rs, :] + w["b_lru_i"][...])
            nlog_a = r * decay_rate
            a = jnp.exp2(nlog_a * (-LOG2_E))
            m2 = jnp.tanh(nlog_a) * (a * a + 1.0)
            mult = jnp.where(m2 > 0.0, m2 * lax.rsqrt(m2), 0.0)
            if first_rows_start is not None and c == 0:
                first_row = lax.broadcasted_iota(jnp.int32, mult.shape, 0) == 0
                mult = jnp.where(first_row & first_rows_start, 1.0, mult)
            a_l.append(a)
            u_l.append(mult * ig * xc[s * nchunk + c])
    hb_gated, new_hl = [], []
    for s in range(nseg):
        carry = states[s][2]
        for c, rs in enumerate(chunk_rows[s]):
            yield 1
            i = s * nchunk + c
            hb, carry = _linear_scan_rows(a_l[i], u_l[i], carry)
            hb_gated.append(hb * _silu(z.cols(_GB, B_WIDTH, rs)))
        new_hl.append(carry)
    pb = _dot(cat_rows(hb_gated), _wt(w["w_b_down"][...]))

    scale = HEAD_DIM ** -0.5
    mem = [[kv(s, hd) for hd in range(C_HEADS)] for s in range(nseg)]
    seg_rows = [slice(s * seg, (s + 1) * seg) for s in range(nseg)]
    sc = [[_dot_nt(z.cols(_QC + hd * HEAD_DIM, HEAD_DIM, seg_rows[s]), mem[s][hd][0])
           for hd in range(C_HEADS)] for s in range(nseg)]
    pr = []
    for s in range(nseg):
        pr.append([])
        for hd in range(C_HEADS):
            yield 1
            p = jnp.exp2((sc[s][hd] - jnp.max(sc[s][hd], axis=-1, keepdims=True)) * (scale * LOG2_E))
            pr[s].append(p / jnp.sum(p, axis=-1, keepdims=True))
    oc = cat_rows([jnp.concatenate([_dot(pr[s][hd], mem[s][hd][1]) for hd in range(C_HEADS)], axis=-1)
                   for s in range(nseg)])
    yield 1
    pc = _dot(oc * _silu(z.cols(_GC, C_WIDTH)), _wt(w["w_c_down"][...]))

    merged = []
    for rs in all_chunks:
        yield 1
        merged.append(jax.nn.sigmoid(z.cols(_ZA, D_MODEL, rs)) * pa[rs, :]
                      + jax.nn.sigmoid(z.cols(_ZB, D_MODEL, rs)) * pb[rs, :]
                      + jax.nn.sigmoid(z.cols(_ZC, D_MODEL, rs)) * pc[rs, :])
    y = x + _dot(cat_rows(merged), _wt(w["w_out"][...]))
    y = _rms(y, w["g_final"][...])
    new_states = [(new_st[s], new_ctx[s], new_hl[s]) for s in range(nseg)]
    return y, new_states


_WEIGHT_NAMES = ("g_mix", "w_in", "lb_logits", "g_a_out", "w_a_down", "w_conv", "b_conv", "w_lru_r", "b_lru_r",
                 "w_lru_i", "b_lru_i", "lru_lambda", "w_b_down", "w_c_down", "w_out", "g_final")
_NW = len(_WEIGHT_NAMES)


def _store_seq_state(conv_ref, lru_ref, seq_idx, ctx, hl):
    for r in range(CTX_ROWS):
        conv_ref[r, pl.ds(seq_idx, 1), :] = ctx[r:r + 1, :]
    lru_ref[pl.ds(seq_idx, 1), :] = hl


def _prompt_kernel(steps_per_seq, *refs):
    x_ref, xn_ref, mk_ref, mv_ref = refs[:4]
    w = dict(zip(_WEIGHT_NAMES, refs[4:4 + _NW]))
    y_ref, hgrn_ref, conv_ref, lru_ref = refs[4 + _NW:8 + _NW]
    scratch = refs[8 + _NW:]
    st_ref, ctx_ref, hl_ref = scratch[:3]
    z_even, z_odd = _ZBuf(scratch[3:3 + Z_BLOCKS]), _ZBuf(scratch[3 + Z_BLOCKS:3 + 2 * Z_BLOCKS])
    oa_refs = scratch[3 + 2 * Z_BLOCKS:5 + 2 * Z_BLOCKS]
    xpad_refs = scratch[5 + 2 * Z_BLOCKS:7 + 2 * Z_BLOCKS]
    j = pl.program_id(0)
    tile = PROMPT_TILE
    seq_start = (j % steps_per_seq) == 0

    @pl.when(j == 0)
    def _():
        _project_in(x_ref[0:tile, :], z_even, w)

    @pl.when(seq_start)
    def _():
        st_ref[...] = jnp.zeros_like(st_ref)
        ctx_ref[...] = jnp.zeros_like(ctx_ref)
        hl_ref[...] = jnp.zeros_like(hl_ref)

    kv = lambda s, hd: (mk_ref[0, _head_rows(hd), :], mv_ref[0, _head_rows(hd), :])
    states = [([st_ref[hd] for hd in range(A_HEADS)], ctx_ref[...], hl_ref[...])]

    y, states = _interleave(
        _mix_stages(x_ref[0:tile, :], z_even, oa_refs[0], xpad_refs[0], kv, states, w,
                    seg=tile, chunk=HGRN_CHUNK, first_rows_start=seq_start),
        _project_in_blocks(x_ref[tile:2 * tile, :], z_odd, w))
    y_ref[0:tile, :] = y

    y, states = _interleave(
        _mix_stages(x_ref[tile:2 * tile, :], z_odd, oa_refs[1], xpad_refs[1], kv, states, w,
                    seg=tile, chunk=HGRN_CHUNK, first_rows_start=None),
        _project_in_blocks(xn_ref[...], z_even, w))
    y_ref[tile:2 * tile, :] = y

    st, ctx, hl = states[0]
    for hd in range(A_HEADS):
        st_ref[hd] = st[hd]
    ctx_ref[...] = ctx
    hl_ref[...] = hl

    @pl.when((j % steps_per_seq) == steps_per_seq - 1)
    def _():
        for hd in range(A_HEADS):
            hgrn_ref[0, hd] = st[hd]
        _store_seq_state(conv_ref, lru_ref, j // steps_per_seq, ctx, hl)


def _sample_kernel(nseq, seg, *refs):
    x_ref, mk_ref, mv_ref, hgrn_in, conv_in, lru_in = refs[:6]
    w = dict(zip(_WEIGHT_NAMES, refs[6:6 + _NW]))
    y_ref, hgrn_ref, conv_ref, lru_ref = refs[6 + _NW:10 + _NW]
    scratch = refs[10 + _NW:]
    z = _ZBuf(scratch[:Z_BLOCKS])
    oa_ref, xpad_ref = scratch[Z_BLOCKS:]
    states = [([hgrn_in[s, hd] for hd in range(A_HEADS)],
               jnp.concatenate([conv_in[r, s:s + 1, :] for r in range(CTX_ROWS)], axis=0),
               lru_in[s:s + 1, :]) for s in range(nseq)]
    _project_in(x_ref[...], z, w)
    y, new_states = _mix(
        x_ref[...], z, oa_ref, xpad_ref,
        lambda s, hd: (mk_ref[s, _head_rows(hd), :], mv_ref[s, _head_rows(hd), :]), states, w,
        seg=seg, chunk=min(HGRN_CHUNK, seg), first_rows_start=None)
    y_ref[...] = y
    for s in range(nseq):
        st, ctx, hl = new_states[s]
        for hd in range(A_HEADS):
            hgrn_ref[s, hd] = st[hd]
        _store_seq_state(conv_ref, lru_ref, s, ctx, hl)


def _const_spec(shape):
    nd = len(shape)
    return pl.BlockSpec(shape, lambda *_: (0,) * nd, pipeline_mode=pl.Buffered(1))


PREP_STEPS = 8
_DENSE_WEIGHTS = ("w_in", "w_a_down", "w_b_down", "w_c_down", "w_out")
_U32 = jnp.uint32


def _prep_kernel(*refs):
    nd = len(_DENSE_WEIGHTS)
    dense_in, (lru_r_in, lru_i_in) = refs[:nd], refs[nd:nd + 2]
    mem_ref, g_mem_ref, wk_in, wv_in = refs[nd + 2:nd + 6]
    outs = refs[nd + 6:]
    dense_out, (lru_r_out, lru_i_out), (k_ref, v_ref) = outs[:nd], outs[nd:nd + 2], outs[nd + 2:nd + 4]
    tile_ref, wk_ref, wv_ref = outs[nd + 4:]
    for src, dst in zip(dense_in, dense_out):
        dst[...] = pltpu.bitcast(src[...].astype(bf16), _U32)

    @pl.when(pl.program_id(0) == 0)
    def _():
        wk_ref[...] = pltpu.bitcast(wk_in[...].astype(bf16), _U32)
        wv_ref[...] = pltpu.bitcast(wv_in[...].astype(bf16), _U32)
        per = MXU_TILE // B_BLOCK_DIM
        for src, dst in ((lru_r_in, lru_r_out), (lru_i_in, lru_i_out)):
            for g in range(LRU_GROUPS):
                tile_ref[...] = jnp.zeros_like(tile_ref)
                for p in range(per):
                    lo = p * B_BLOCK_DIM
                    tile_ref[lo:lo + B_BLOCK_DIM, lo:lo + B_BLOCK_DIM] = src[g * per + p]
                dst[g] = pltpu.bitcast(tile_ref[...].astype(bf16), _U32)

    hm = _rms(mem_ref[0], g_mem_ref[...]).astype(bf16)
    k = jnp.dot(hm, _wt(wk_ref[...]), preferred_element_type=f32)
    v = jnp.dot(hm, _wt(wv_ref[...]), preferred_element_type=f32)
    for hd in range(C_HEADS):
        sl = slice(hd * HEAD_DIM, (hd + 1) * HEAD_DIM)
        k_ref[0, _head_rows(hd), :] = k[:, sl]
        v_ref[0, _head_rows(hd), :] = v[:, sl]


def _prep_weights(dense, lru_r, lru_i, mem, g_mem, w_mem_k, w_mem_v):
    bsz = mem.shape[0]
    assert bsz == PREP_STEPS
    in_specs, out_specs, out_shape = [], [], []
    for wm in dense:
        k, n = wm.shape
        assert k % (4 * SUBLANES * PREP_STEPS) == 0
        in_specs.append(pl.BlockSpec((k // PREP_STEPS, n), lambda i: (i, 0)))
        out_specs.append(pl.BlockSpec((k // (2 * PREP_STEPS), n), lambda i: (i, 0)))
        out_shape.append(jax.ShapeDtypeStruct((k // 2, n), _U32))
    blk = (B_BLOCKS, B_BLOCK_DIM, B_BLOCK_DIM)
    tiles = (LRU_GROUPS, MXU_TILE // 2, MXU_TILE)
    in_specs += [pl.BlockSpec(blk, lambda i: (0, 0, 0))] * 2
    out_specs += [pl.BlockSpec(tiles, lambda i: (0, 0, 0))] * 2
    out_shape += [jax.ShapeDtypeStruct(tiles, _U32)] * 2
    in_specs += [pl.BlockSpec((1, N_MEM, D_MODEL), lambda i: (i, 0, 0)), _const_spec((1, D_MODEL)),
                 _const_spec((D_MODEL, C_WIDTH)), _const_spec((D_MODEL, C_WIDTH))]
    out_specs += [pl.BlockSpec((1,) + KV_ROWS, lambda i: (i, 0, 0))] * 2
    out_shape += [jax.ShapeDtypeStruct((bsz,) + KV_ROWS, f32)] * 2
    outs = pl.pallas_call(
        _prep_kernel, grid=(PREP_STEPS,), in_specs=in_specs, out_specs=out_specs, out_shape=out_shape,
        scratch_shapes=[pltpu.VMEM((MXU_TILE, MXU_TILE), f32),
                        pltpu.VMEM((D_MODEL // 2, C_WIDTH), _U32), pltpu.VMEM((D_MODEL // 2, C_WIDTH), _U32)],
        compiler_params=pltpu.CompilerParams(vmem_limit_bytes=VMEM_LIMIT_BYTES,
                                             dimension_semantics=("arbitrary",)),
        name="prep_weights",
    )(*dense, lru_r, lru_i, mem, g_mem, w_mem_k, w_mem_v)
    nd = len(dense)
    return outs[:nd], outs[nd], outs[nd + 1], outs[nd + 2], outs[nd + 3]


def kernel(x_prompt, x_sample, mem_prompt, cache_mem_k, cache_mem_v, state_hgrn, state_conv, state_lru, g_mix, w_in, lb_logits, g_a_out, w_a_down, w_conv, b_conv, w_lru_r, b_lru_r, w_lru_i, b_lru_i, lru_lambda, w_b_down, g_mem, w_mem_k, w_mem_v, w_c_down, w_out, g_final):
    bsz, seq, _ = x_prompt.shape
    dec_b, dec_seq, _ = x_sample.shape
    assert g_mix.shape[0] == 1, "single-layer stack only"
    assert seq % (2 * PROMPT_TILE) == 0 and PROMPT_TILE % HGRN_CHUNK == 0

    row = lambda a: a.reshape(1, -1).astype(f32)
    dense = dict(w_in=w_in[0], w_a_down=w_a_down[0], w_b_down=w_b_down[0], w_c_down=w_c_down[0], w_out=w_out[0])
    packed, lru_r_tiles, lru_i_tiles, mk, mv = _prep_weights(
        [dense[n] for n in _DENSE_WEIGHTS], w_lru_r[0], w_lru_i[0], mem_prompt, row(g_mem[0]), w_mem_k[0], w_mem_v[0])
    packed = dict(zip(_DENSE_WEIGHTS, packed))
    weights = dict(
        g_mix=row(g_mix[0]), w_in=packed["w_in"], lb_logits=lb_logits.astype(f32), g_a_out=row(g_a_out[0]),
        w_a_down=packed["w_a_down"], w_conv=w_conv[0].astype(f32), b_conv=row(b_conv[0]),
        w_lru_r=lru_r_tiles, b_lru_r=row(b_lru_r[0]), w_lru_i=lru_i_tiles, b_lru_i=row(b_lru_i[0]),
        lru_lambda=row(lru_lambda[0]), w_b_down=packed["w_b_down"], w_c_down=packed["w_c_down"],
        w_out=packed["w_out"], g_final=row(g_final))
    wlist = [weights[n] for n in _WEIGHT_NAMES]
    wspecs = [_const_spec(a.shape) for a in wlist]

    tile = PROMPT_TILE
    n_tiles = bsz * seq // tile
    steps_per_seq = seq // (2 * tile)
    zbuf = [pltpu.VMEM((tile, Z_BLK), f32)] * Z_BLOCKS
    y_p, hgrn_p, conv_p, lru_p = pl.pallas_call(
        functools.partial(_prompt_kernel, steps_per_seq),
        grid=(n_tiles // 2,),
        in_specs=[pl.BlockSpec((2 * tile, D_MODEL), lambda j: (j, 0)),
                  pl.BlockSpec((tile, D_MODEL), lambda j: (jnp.minimum(2 * j + 2, n_tiles - 1), 0)),
                  pl.BlockSpec((1,) + KV_ROWS, lambda j: (j // steps_per_seq, 0, 0)),
                  pl.BlockSpec((1,) + KV_ROWS, lambda j: (j // steps_per_seq, 0, 0))] + wspecs,
        out_specs=[pl.BlockSpec((2 * tile, D_MODEL), lambda j: (j, 0)),
                   pl.BlockSpec((1, A_HEADS, HEAD_DIM, HEAD_DIM), lambda j: (j // steps_per_seq, 0, 0, 0)),
                   pl.BlockSpec((CTX_ROWS, bsz, B_WIDTH), lambda j: (0, 0, 0)),
                   pl.BlockSpec((bsz, B_WIDTH), lambda j: (0, 0))],
        out_shape=[jax.ShapeDtypeStruct((bsz * seq, D_MODEL), f32),
                   jax.ShapeDtypeStruct((bsz, A_HEADS, HEAD_DIM, HEAD_DIM), f32),
                   jax.ShapeDtypeStruct((CTX_ROWS, bsz, B_WIDTH), f32),
                   jax.ShapeDtypeStruct((bsz, B_WIDTH), f32)],
        scratch_shapes=[pltpu.VMEM((A_HEADS, HEAD_DIM, HEAD_DIM), f32),
                        pltpu.VMEM((CTX_ROWS, B_WIDTH), f32),
                        pltpu.VMEM((1, B_WIDTH), f32)] + zbuf + zbuf
                       + [pltpu.VMEM((tile, A_WIDTH), f32)] * 2
                       + [pltpu.VMEM((SUBLANES, B_WIDTH), f32)] * 2,
        compiler_params=pltpu.CompilerParams(vmem_limit_bytes=VMEM_LIMIT_BYTES,
                                             dimension_semantics=("arbitrary",)),
        name="prompt_layer",
    )(x_prompt.reshape(bsz * seq, D_MODEL), x_prompt.reshape(bsz * seq, D_MODEL), mk, mv, *wlist)
    y_p = y_p.reshape(bsz, seq, D_MODEL)

    rows = dec_b * dec_seq
    full = lambda shape: pl.BlockSpec(shape, lambda *_: (0,) * len(shape))
    y_s, hgrn_s, conv_s, lru_s = pl.pallas_call(
        functools.partial(_sample_kernel, dec_b, dec_seq),
        grid=(1,),
        in_specs=[full((rows, D_MODEL)), full((dec_b,) + KV_ROWS), full((dec_b,) + KV_ROWS),
                  full((dec_b, A_HEADS, HEAD_DIM, HEAD_DIM)), full((CTX_ROWS, dec_b, B_WIDTH)),
                  full((dec_b, B_WIDTH))] + wspecs,
        out_specs=[full((rows, D_MODEL)), full((dec_b, A_HEADS, HEAD_DIM, HEAD_DIM)),
                   full((CTX_ROWS, dec_b, B_WIDTH)), full((dec_b, B_WIDTH))],
        out_shape=[jax.ShapeDtypeStruct((rows, D_MODEL), f32),
                   jax.ShapeDtypeStruct((dec_b, A_HEADS, HEAD_DIM, HEAD_DIM), f32),
                   jax.ShapeDtypeStruct((CTX_ROWS, dec_b, B_WIDTH), f32),
                   jax.ShapeDtypeStruct((dec_b, B_WIDTH), f32)],
        scratch_shapes=[pltpu.VMEM((rows, Z_BLK), f32)] * Z_BLOCKS
                       + [pltpu.VMEM((rows, A_WIDTH), f32),
                        pltpu.VMEM((dec_b * SUBLANES, B_WIDTH), f32)],
        compiler_params=pltpu.CompilerParams(vmem_limit_bytes=VMEM_LIMIT_BYTES),
        name="sample_layer",
    )(x_sample.reshape(rows, D_MODEL), cache_mem_k.reshape((dec_b,) + KV_ROWS),
      cache_mem_v.reshape((dec_b,) + KV_ROWS), state_hgrn[0], jnp.swapaxes(state_conv[0], 0, 1),
      state_lru[0], *wlist)

    return (y_p, y_s.reshape(dec_b, dec_seq, D_MODEL), hgrn_p[None], jnp.swapaxes(conv_p, 0, 1)[None],
            lru_p[None], mk.reshape(1, bsz, N_MEM, C_HEADS, HEAD_DIM),
            mv.reshape(1, bsz, N_MEM, C_HEADS, HEAD_DIM), hgrn_s[None], jnp.swapaxes(conv_s, 0, 1)[None],
            lru_s[None])
```

```python
import functools

import jax
import jax.numpy as jnp
from jax import lax
from jax.experimental import pallas as pl
from jax.experimental.pallas import tpu as pltpu

f32 = jnp.float32
bf16 = jnp.bfloat16

D_MODEL = 1024
N_MEM = 256
EPS = 1e-6
A_HEADS = 4
HEAD_DIM = 128
A_WIDTH = A_HEADS * HEAD_DIM
B_WIDTH = D_MODEL
B_BLOCKS = 16
B_BLOCK_DIM = B_WIDTH // B_BLOCKS
CONV_W = 4
LRU_C = 8.0
LOG2_E = 1.4426950408889634
C_HEADS = 4
C_WIDTH = C_HEADS * HEAD_DIM
assert A_HEADS == C_HEADS
HGRN_CHUNK = 64
IN_COLS = 4 * A_WIDTH + 2 * B_WIDTH + 2 * C_WIDTH + 3 * D_MODEL

_QA, _FA, _VA, _GA = 0, A_WIDTH, 2 * A_WIDTH, 3 * A_WIDTH
_XB = 4 * A_WIDTH
_GB = _XB + B_WIDTH
_QC = _GB + B_WIDTH
_GC = _QC + C_WIDTH
_ZA = _GC + C_WIDTH
_ZB = _ZA + D_MODEL
_ZC = _ZB + D_MODEL

MXU_TILE = 256
LRU_GROUPS = B_WIDTH // MXU_TILE
SUBLANES = 8
CTX_ROWS = CONV_W - 1

PROMPT_TILE = 256
Z_BLK = 1024
Z_BLOCKS = IN_COLS // Z_BLK
VMEM_LIMIT_BYTES = 60 * 1024 * 1024


def _rms(x, g):
    return x * lax.rsqrt(jnp.mean(x * x, axis=-1, keepdims=True) + EPS) * g


def _wt(ref_or_val):
    return pltpu.bitcast(ref_or_val, bf16)


def _dot(a, b):
    return jnp.dot(a.astype(bf16), b.astype(bf16), preferred_element_type=f32)


def _dot_nt(a, b):
    return lax.dot_general(a.astype(bf16), b.astype(bf16), (((1,), (1,)), ((), ())),
                           preferred_element_type=f32)


def _dot_tn(a, b):
    return lax.dot_general(a.astype(bf16), b.astype(bf16), (((0,), (0,)), ((), ())),
                           preferred_element_type=f32)


def _silu(x):
    return x * jax.nn.sigmoid(x)


KV_ROWS = (N_MEM * C_HEADS, HEAD_DIM)


def _head_rows(hd):
    return pl.ds(hd, N_MEM, stride=C_HEADS)


def _as_column(row):
    return jnp.transpose(jnp.broadcast_to(row, (SUBLANES, row.shape[1])))[:, 0:1]


def _vreg_groups(x):
    rows, width = x.shape
    return x.reshape(rows // SUBLANES, SUBLANES, width)


def _cumprod_rows(x):
    rows = x.shape[0]
    x3 = _vreg_groups(x)
    sub = lax.broadcasted_iota(jnp.int32, x3.shape, 1)
    d = 1
    while d < SUBLANES:
        x3 = x3 * jnp.where(sub >= d, pltpu.roll(x3, d, 1), 1.0)
        d *= 2
    out, carry = [], None
    for g in range(rows // SUBLANES):
        cur = x3[g] if carry is None else x3[g] * carry
        carry = cur[SUBLANES - 1:SUBLANES, :]
        out.append(cur)
    return jnp.concatenate(out, axis=0)


def _linear_scan_rows(a, u, carry):
    rows = a.shape[0]
    a3, u3 = _vreg_groups(a), _vreg_groups(u)
    sub = lax.broadcasted_iota(jnp.int32, a3.shape, 1)
    d = 1
    while d < SUBLANES:
        keep = sub >= d
        u3 = a3 * jnp.where(keep, pltpu.roll(u3, d, 1), 0.0) + u3
        a3 = a3 * jnp.where(keep, pltpu.roll(a3, d, 1), 1.0)
        d *= 2
    out = []
    for g in range(rows // SUBLANES):
        cur = u3[g] + a3[g] * carry
        carry = cur[SUBLANES - 1:SUBLANES, :]
        out.append(cur)
    return jnp.concatenate(out, axis=0), carry


class _ZBuf:
    def __init__(self, refs):
        self.refs = refs

    def cols(self, c0, width, rows=slice(None)):
        blk, off = divmod(c0, Z_BLK)
        assert off + width <= Z_BLK
        return self.refs[blk][rows, off:off + width]


def _project_in_blocks(x, z, w):
    h = _rms(x, w["g_mix"][...]).astype(bf16)

    def block(c0):
        blk, off = divmod(c0, Z_BLK)
        z.refs[blk][:, off:off + MXU_TILE] = jnp.dot(h, _wt(w["w_in"][:, c0:c0 + MXU_TILE]),
                                                     preferred_element_type=f32)

    return [functools.partial(block, c0) for c0 in range(0, IN_COLS, MXU_TILE)]


def _project_in(x, z, w):
    for block in _project_in_blocks(x, z, w):
        block()


def _interleave(stages, blocks):
    blocks = list(blocks)
    while True:
        try:
            n = next(stages)
        except StopIteration as done:
            result = done.value
            break
        for _ in range(min(n, len(blocks))):
            blocks.pop(0)()
    for block in blocks:
        block()
    return result


def _mix(*args, **kwargs):
    return _interleave(_mix_stages(*args, **kwargs), [])


def _mix_stages(x, z, oa_ref, xpad_ref, kv, states, w, *, seg, chunk, first_rows_start):
    rows = x.shape[0]
    nseg = rows // seg
    nchunk = seg // chunk
    chunk_rows = [[slice(s * seg + c * chunk, s * seg + (c + 1) * chunk) for c in range(nchunk)]
                  for s in range(nseg)]
    all_chunks = [rs for per_seg in chunk_rows for rs in per_seg]
    head_sl = [slice(hd * HEAD_DIM, (hd + 1) * HEAD_DIM) for hd in range(A_HEADS)]
    cat_rows = lambda parts: parts[0] if len(parts) == 1 else jnp.concatenate(parts, axis=0)

    lg = w["lb_logits"][...]
    l0, l1 = lg[0:1, :], lg[1:2, :]
    lmax = jnp.maximum(l0, l1)
    e0, e1 = jnp.exp(l0 - lmax), jnp.exp(l1 - lmax)
    lb = e0 / (e0 + e1)

    qg, kg, v, kd, decay = [], [], [], [], []
    for rs in all_chunks:
        yield 1
        f = lb + (1.0 - lb) * jax.nn.sigmoid(z.cols(_FA, A_WIDTH, rs))
        p = _cumprod_rows(f)
        inv_p = 1.0 / p
        k = 1.0 - f
        qg.append(_silu(z.cols(_QA, A_WIDTH, rs)) * p)
        kg.append(k * inv_p)
        v.append(z.cols(_VA, A_WIDTH, rs))
        p_last = p[chunk - 1:chunk, :]
        kd.append(kg[-1] * p_last)
        decay.append(p_last)

    scores = [[_dot_nt(qg[i][:, sl], kg[i][:, sl]) for sl in head_sl] for i in range(len(all_chunks))]
    st_in = [[[states[s][0][hd]] for hd in range(A_HEADS)] for s in range(nseg)]
    for s in range(nseg):
        for c in range(nchunk):
            i = s * nchunk + c
            for hd, sl in enumerate(head_sl):
                st_in[s][hd].append(st_in[s][hd][c] * _as_column(decay[i][:, sl])
                                    + _dot_tn(kd[i][:, sl], v[i][:, sl]))
    new_st = [[st_in[s][hd][nchunk] for hd in range(A_HEADS)] for s in range(nseg)]
    yield 1
    causal = (lax.broadcasted_iota(jnp.int32, (chunk, chunk), 1)
              <= lax.broadcasted_iota(jnp.int32, (chunk, chunk), 0))
    o_intra = [[_dot(jnp.where(causal, scores[i][hd], 0.0), v[i][:, sl]) for hd, sl in enumerate(head_sl)]
               for i in range(len(all_chunks))]
    yield 1
    for hd, sl in enumerate(head_sl):
        for s in range(nseg):
            for c, rs in enumerate(chunk_rows[s]):
                i = s * nchunk + c
                oa_ref[rs, sl] = o_intra[i][hd] + _dot(qg[i][:, sl], st_in[s][hd][c])

    g_a = w["g_a_out"][...]
    a_in = []
    for i, rs in enumerate(all_chunks):
        yield i % 2
        normed = jnp.concatenate([_rms(oa_ref[rs, sl], g_a[:, sl]) for sl in head_sl], axis=-1)
        a_in.append(normed * _silu(z.cols(_GA, A_WIDTH, rs)))
    pa = _dot(cat_rows(a_in), _wt(w["w_a_down"][...]))

    pad = SUBLANES
    for s in range(nseg):
        xpad_ref[s * pad:(s + 1) * pad, :] = jnp.zeros((pad, B_WIDTH), f32)
        xpad_ref[(s + 1) * pad - CTX_ROWS:(s + 1) * pad, :] = states[s][1]
    w_conv = w["w_conv"][...]
    sub = lax.broadcasted_iota(jnp.int32, (chunk // SUBLANES, SUBLANES, B_WIDTH), 1)
    xc, new_ctx = [], []
    for s in range(nseg):
        for c, rs in enumerate(chunk_rows[s]):
            yield 1
            if c == 0:
                ext = jnp.concatenate([xpad_ref[s * pad:(s + 1) * pad, :], z.cols(_XB, B_WIDTH, rs)], axis=0)
            else:
                ext = z.cols(_XB, B_WIDTH, slice(rs.start - pad, rs.stop))
            ext = _vreg_groups(ext)
            acc = w_conv[CONV_W - 1:CONV_W, :] * ext[1:]
            for j in range(1, CONV_W):
                rolled = pltpu.roll(ext, j, 1)
                shifted = jnp.where(sub >= j, rolled[1:], rolled[:-1])
                acc = acc + w_conv[CONV_W - 1 - j:CONV_W - j, :] * shifted
            xc.append(w["b_conv"][...] + acc.reshape(chunk, B_WIDTH))
        new_ctx.append(z.cols(_XB, B_WIDTH, slice((s + 1) * seg - CTX_ROWS, (s + 1) * seg)))

    xc_b = cat_rows(xc).astype(bf16)
    r_pre, i_pre = [], []
    for g in range(LRU_GROUPS):
        gs = slice(g * MXU_TILE, (g + 1) * MXU_TILE)
        r_pre.append(jnp.dot(xc_b[:, gs], _wt(w["w_lru_r"][g]), preferred_element_type=f32))
        i_pre.append(jnp.dot(xc_b[:, gs], _wt(w["w_lru_i"][g]), preferred_element_type=f32))
    r_pre, i_pre = jnp.concatenate(r_pre, axis=-1), jnp.concatenate(i_pre, axis=-1)
    neg_lam = -w["lru_lambda"][...]
    softplus = jnp.maximum(neg_lam, 0.0) + jnp.log1p(jnp.exp(-jnp.abs(neg_lam)))
    decay_rate = LRU_C * softplus
    a_l, u_l = [], []
    for s in range(nseg):
        for c, rs in enumerate(chunk_rows[s]):
            yield 2
            r = jax.nn.sigmoid(r_pre[rs, :] + w["b_lru_r"][...])
            ig = jax.nn.sigmoid(i_pre[rs, :] + w["b_lru_i"][...])
            nlog_a = r * decay_rate
            a = jnp.exp2(nlog_a * (-LOG2_E))
            m2 = jnp.tanh(nlog_a) * (a * a + 1.0)
            mult = jnp.where(m2 > 0.0, m2 * lax.rsqrt(m2), 0.0)
            if first_rows_start is not None and c == 0:
                first_row = lax.broadcasted_iota(jnp.int32, mult.shape, 0) == 0
                mult = jnp.where(first_row & first_rows_start, 1.0, mult)
            a_l.append(a)
            u_l.append(mult * ig * xc[s * nchunk + c])
    hb_gated, new_hl = [], []
    for s in range(nseg):
        carry = states[s][2]
        for c, rs in enumerate(chunk_rows[s]):
            yield 1
            i = s * nchunk + c
            hb, carry = _linear_scan_rows(a_l[i], u_l[i], carry)
            hb_gated.append(hb * _silu(z.cols(_GB, B_WIDTH, rs)))
        new_hl.append(carry)
    pb = _dot(cat_rows(hb_gated), _wt(w["w_b_down"][...]))

    scale = HEAD_DIM ** -0.5
    mem = [[kv(s, hd) for hd in range(C_HEADS)] for s in range(nseg)]
    seg_rows = [slice(s * seg, (s + 1) * seg) for s in range(nseg)]
    sc = [[_dot_nt(z.cols(_QC + hd * HEAD_DIM, HEAD_DIM, seg_rows[s]), mem[s][hd][0])
           for hd in range(C_HEADS)] for s in range(nseg)]
    pr = []
    for s in range(nseg):
        pr.append([])
        for hd in range(C_HEADS):
            yield 1
            p = jnp.exp2((sc[s][hd] - jnp.max(sc[s][hd], axis=-1, keepdims=True)) * (scale * LOG2_E))
            pr[s].append(p / jnp.sum(p, axis=-1, keepdims=True))
    oc = cat_rows([jnp.concatenate([_dot(pr[s][hd], mem[s][hd][1]) for hd in range(C_HEADS)], axis=-1)
                   for s in range(nseg)])
    yield 1
    pc = _dot(oc * _silu(z.cols(_GC, C_WIDTH)), _wt(w["w_c_down"][...]))

    merged = []
    for rs in all_chunks:
        yield 1
        merged.append(jax.nn.sigmoid(z.cols(_ZA, D_MODEL, rs)) * pa[rs, :]
                      + jax.nn.sigmoid(z.cols(_ZB, D_MODEL, rs)) * pb[rs, :]
                      + jax.nn.sigmoid(z.cols(_ZC, D_MODEL, rs)) * pc[rs, :])
    y = x + _dot(cat_rows(merged), _wt(w["w_out"][...]))
    y = _rms(y, w["g_final"][...])
    new_states = [(new_st[s], new_ctx[s], new_hl[s]) for s in range(nseg)]
    return y, new_states


_WEIGHT_NAMES = ("g_mix", "w_in", "lb_logits", "g_a_out", "w_a_down", "w_conv", "b_conv", "w_lru_r", "b_lru_r",
                 "w_lru_i", "b_lru_i", "lru_lambda", "w_b_down", "w_c_down", "w_out", "g_final")
_NW = len(_WEIGHT_NAMES)


def _store_seq_state(conv_ref, lru_ref, seq_idx, ctx, hl):
    for r in range(CTX_ROWS):
        conv_ref[r, pl.ds(seq_idx, 1), :] = ctx[r:r + 1, :]
    lru_ref[pl.ds(seq_idx, 1), :] = hl


def _prompt_kernel(steps_per_seq, *refs):
    x_ref, xn_ref, mk_ref, mv_ref = refs[:4]
    w = dict(zip(_WEIGHT_NAMES, refs[4:4 + _NW]))
    y_ref, hgrn_ref, conv_ref, lru_ref = refs[4 + _NW:8 + _NW]
    scratch = refs[8 + _NW:]
    st_ref, ctx_ref, hl_ref = scratch[:3]
    z_even, z_odd = _ZBuf(scratch[3:3 + Z_BLOCKS]), _ZBuf(scratch[3 + Z_BLOCKS:3 + 2 * Z_BLOCKS])
    oa_refs = scratch[3 + 2 * Z_BLOCKS:5 + 2 * Z_BLOCKS]
    xpad_refs = scratch[5 + 2 * Z_BLOCKS:7 + 2 * Z_BLOCKS]
    j = pl.program_id(0)
    tile = PROMPT_TILE
    seq_start = (j % steps_per_seq) == 0

    @pl.when(j == 0)
    def _():
        _project_in(x_ref[0:tile, :], z_even, w)

    @pl.when(seq_start)
    def _():
        st_ref[...] = jnp.zeros_like(st_ref)
        ctx_ref[...] = jnp.zeros_like(ctx_ref)
        hl_ref[...] = jnp.zeros_like(hl_ref)

    kv = lambda s, hd: (mk_ref[0, _head_rows(hd), :], mv_ref[0, _head_rows(hd), :])
    states = [([st_ref[hd] for hd in range(A_HEADS)], ctx_ref[...], hl_ref[...])]

    y, states = _interleave(
        _mix_stages(x_ref[0:tile, :], z_even, oa_refs[0], xpad_refs[0], kv, states, w,
                    seg=tile, chunk=HGRN_CHUNK, first_rows_start=seq_start),
        _project_in_blocks(x_ref[tile:2 * tile, :], z_odd, w))
    y_ref[0:tile, :] = y

    y, states = _interleave(
        _mix_stages(x_ref[tile:2 * tile, :], z_odd, oa_refs[1], xpad_refs[1], kv, states, w,
                    seg=tile, chunk=HGRN_CHUNK, first_rows_start=None),
        _project_in_blocks(xn_ref[...], z_even, w))
    y_ref[tile:2 * tile, :] = y

    st, ctx, hl = states[0]
    for hd in range(A_HEADS):
        st_ref[hd] = st[hd]
    ctx_ref[...] = ctx
    hl_ref[...] = hl

    @pl.when((j % steps_per_seq) == steps_per_seq - 1)
    def _():
        for hd in range(A_HEADS):
            hgrn_ref[0, hd] = st[hd]
        _store_seq_state(conv_ref, lru_ref, j // steps_per_seq, ctx, hl)


def _sample_kernel(nseq, seg, *refs):
    x_ref, mk_ref, mv_ref, hgrn_in, conv_in, lru_in = refs[:6]
    w = dict(zip(_WEIGHT_NAMES, refs[6:6 + _NW]))
    y_ref, hgrn_ref, conv_ref, lru_ref = refs[6 + _NW:10 + _NW]
    scratch = refs[10 + _NW:]
    z = _ZBuf(scratch[:Z_BLOCKS])
    oa_ref, xpad_ref = scratch[Z_BLOCKS:]
    states = [([hgrn_in[s, hd] for hd in range(A_HEADS)],
               jnp.concatenate([conv_in[r, s:s + 1, :] for r in range(CTX_ROWS)], axis=0),
               lru_in[s:s + 1, :]) for s in range(nseq)]
    _project_in(x_ref[...], z, w)
    y, new_states = _mix(
        x_ref[...], z, oa_ref, xpad_ref,
        lambda s, hd: (mk_ref[s, _head_rows(hd), :], mv_ref[s, _head_rows(hd), :]), states, w,
        seg=seg, chunk=min(HGRN_CHUNK, seg), first_rows_start=None)
    y_ref[...] = y
    for s in range(nseq):
        st, ctx, hl = new_states[s]
        for hd in range(A_HEADS):
            hgrn_ref[s, hd] = st[hd]
        _store_seq_state(conv_ref, lru_ref, s, ctx, hl)


def _const_spec(shape):
    nd = len(shape)
    return pl.BlockSpec(shape, lambda *_: (0,) * nd, pipeline_mode=pl.Buffered(1))


PREP_STEPS = 8
_DENSE_WEIGHTS = ("w_in", "w_a_down", "w_b_down", "w_c_down", "w_out")
_U32 = jnp.uint32


def _prep_kernel(*refs):
    nd = len(_DENSE_WEIGHTS)
    dense_in, (lru_r_in, lru_i_in) = refs[:nd], refs[nd:nd + 2]
    mem_ref, g_mem_ref, wk_in, wv_in = refs[nd + 2:nd + 6]
    outs = refs[nd + 6:]
    dense_out, (lru_r_out, lru_i_out), (k_ref, v_ref) = outs[:nd], outs[nd:nd + 2], outs[nd + 2:nd + 4]
    tile_ref, wk_ref, wv_ref = outs[nd + 4:]
    for src, dst in zip(dense_in, dense_out):
        dst[...] = pltpu.bitcast(src[...].astype(bf16), _U32)

    @pl.when(pl.program_id(0) == 0)
    def _():
        wk_ref[...] = pltpu.bitcast(wk_in[...].astype(bf16), _U32)
        wv_ref[...] = pltpu.bitcast(wv_in[...].astype(bf16), _U32)
        per = MXU_TILE // B_BLOCK_DIM
        for src, dst in ((lru_r_in, lru_r_out), (lru_i_in, lru_i_out)):
            for g in range(LRU_GROUPS):
                tile_ref[...] = jnp.zeros_like(tile_ref)
                for p in range(per):
                    lo = p * B_BLOCK_DIM
                    tile_ref[lo:lo + B_BLOCK_DIM, lo:lo + B_BLOCK_DIM] = src[g * per + p]
                dst[g] = pltpu.bitcast(tile_ref[...].astype(bf16), _U32)

    hm = _rms(mem_ref[0], g_mem_ref[...]).astype(bf16)
    k = jnp.dot(hm, _wt(wk_ref[...]), preferred_element_type=f32)
    v = jnp.dot(hm, _wt(wv_ref[...]), preferred_element_type=f32)
    for hd in range(C_HEADS):
        sl = slice(hd * HEAD_DIM, (hd + 1) * HEAD_DIM)
        k_ref[0, _head_rows(hd), :] = k[:, sl]
        v_ref[0, _head_rows(hd), :] = v[:, sl]


def _prep_weights(dense, lru_r, lru_i, mem, g_mem, w_mem_k, w_mem_v):
    bsz = mem.shape[0]
    assert bsz == PREP_STEPS
    in_specs, out_specs, out_shape = [], [], []
    for wm in dense:
        k, n = wm.shape
        assert k % (4 * SUBLANES * PREP_STEPS) == 0
        in_specs.append(pl.BlockSpec((k // PREP_STEPS, n), lambda i: (i, 0)))
        out_specs.append(pl.BlockSpec((k // (2 * PREP_STEPS), n), lambda i: (i, 0)))
        out_shape.append(jax.ShapeDtypeStruct((k // 2, n), _U32))
    blk = (B_BLOCKS, B_BLOCK_DIM, B_BLOCK_DIM)
    tiles = (LRU_GROUPS, MXU_TILE // 2, MXU_TILE)
    in_specs += [pl.BlockSpec(blk, lambda i: (0, 0, 0))] * 2
    out_specs += [pl.BlockSpec(tiles, lambda i: (0, 0, 0))] * 2
    out_shape += [jax.ShapeDtypeStruct(tiles, _U32)] * 2
    in_specs += [pl.BlockSpec((1, N_MEM, D_MODEL), lambda i: (i, 0, 0)), _const_spec((1, D_MODEL)),
                 _const_spec((D_MODEL, C_WIDTH)), _const_spec((D_MODEL, C_WIDTH))]
    out_specs += [pl.BlockSpec((1,) + KV_ROWS, lambda i: (i, 0, 0))] * 2
    out_shape += [jax.ShapeDtypeStruct((bsz,) + KV_ROWS, f32)] * 2
    outs = pl.pallas_call(
        _prep_kernel, grid=(PREP_STEPS,), in_specs=in_specs, out_specs=out_specs, out_shape=out_shape,
        scratch_shapes=[pltpu.VMEM((MXU_TILE, MXU_TILE), f32),
                        pltpu.VMEM((D_MODEL // 2, C_WIDTH), _U32), pltpu.VMEM((D_MODEL // 2, C_WIDTH), _U32)],
        compiler_params=pltpu.CompilerParams(vmem_limit_bytes=VMEM_LIMIT_BYTES,
                                             dimension_semantics=("arbitrary",)),
        name="prep_weights",
    )(*dense, lru_r, lru_i, mem, g_mem, w_mem_k, w_mem_v)
    nd = len(dense)
    return outs[:nd], outs[nd], outs[nd + 1], outs[nd + 2], outs[nd + 3]


def kernel(x_prompt, x_sample, mem_prompt, cache_mem_k, cache_mem_v, state_hgrn, state_conv, state_lru, g_mix, w_in, lb_logits, g_a_out, w_a_down, w_conv, b_conv, w_lru_r, b_lru_r, w_lru_i, b_lru_i, lru_lambda, w_b_down, g_mem, w_mem_k, w_mem_v, w_c_down, w_out, g_final):
    bsz, seq, _ = x_prompt.shape
    dec_b, dec_seq, _ = x_sample.shape
    assert g_mix.shape[0] == 1, "single-layer stack only"
    assert seq % (2 * PROMPT_TILE) == 0 and PROMPT_TILE % HGRN_CHUNK == 0

    row = lambda a: a.reshape(1, -1).astype(f32)
    dense = dict(w_in=w_in[0], w_a_down=w_a_down[0], w_b_down=w_b_down[0], w_c_down=w_c_down[0], w_out=w_out[0])
    packed, lru_r_tiles, lru_i_tiles, mk, mv = _prep_weights(
        [dense[n] for n in _DENSE_WEIGHTS], w_lru_r[0], w_lru_i[0], mem_prompt, row(g_mem[0]), w_mem_k[0], w_mem_v[0])
    packed = dict(zip(_DENSE_WEIGHTS, packed))
    weights = dict(
        g_mix=row(g_mix[0]), w_in=packed["w_in"], lb_logits=lb_logits.astype(f32), g_a_out=row(g_a_out[0]),
        w_a_down=packed["w_a_down"], w_conv=w_conv[0].astype(f32), b_conv=row(b_conv[0]),
        w_lru_r=lru_r_tiles, b_lru_r=row(b_lru_r[0]), w_lru_i=lru_i_tiles, b_lru_i=row(b_lru_i[0]),
        lru_lambda=row(lru_lambda[0]), w_b_down=packed["w_b_down"], w_c_down=packed["w_c_down"],
        w_out=packed["w_out"], g_final=row(g_final))
    wlist = [weights[n] for n in _WEIGHT_NAMES]
    wspecs = [_const_spec(a.shape) for a in wlist]

    tile = PROMPT_TILE
    n_tiles = bsz * seq // tile
    steps_per_seq = seq // (2 * tile)
    zbuf = [pltpu.VMEM((tile, Z_BLK), f32)] * Z_BLOCKS
    y_p, hgrn_p, conv_p, lru_p = pl.pallas_call(
        functools.partial(_prompt_kernel, steps_per_seq),
        grid=(n_tiles // 2,),
        in_specs=[pl.BlockSpec((2 * tile, D_MODEL), lambda j: (j, 0)),
                  pl.BlockSpec((tile, D_MODEL), lambda j: (jnp.minimum(2 * j + 2, n_tiles - 1), 0)),
                  pl.BlockSpec((1,) + KV_ROWS, lambda j: (j // steps_per_seq, 0, 0)),
                  pl.BlockSpec((1,) + KV_ROWS, lambda j: (j // steps_per_seq, 0, 0))] + wspecs,
        out_specs=[pl.BlockSpec((2 * tile, D_MODEL), lambda j: (j, 0)),
                   pl.BlockSpec((1, A_HEADS, HEAD_DIM, HEAD_DIM), lambda j: (j // steps_per_seq, 0, 0, 0)),
                   pl.BlockSpec((CTX_ROWS, bsz, B_WIDTH), lambda j: (0, 0, 0)),
                   pl.BlockSpec((bsz, B_WIDTH), lambda j: (0, 0))],
        out_shape=[jax.ShapeDtypeStruct((bsz * seq, D_MODEL), f32),
                   jax.ShapeDtypeStruct((bsz, A_HEADS, HEAD_DIM, HEAD_DIM), f32),
                   jax.ShapeDtypeStruct((CTX_ROWS, bsz, B_WIDTH), f32),
                   jax.ShapeDtypeStruct((bsz, B_WIDTH), f32)],
        scratch_shapes=[pltpu.VMEM((A_HEADS, HEAD_DIM, HEAD_DIM), f32),
                        pltpu.VMEM((CTX_ROWS, B_WIDTH), f32),
                        pltpu.VMEM((1, B_WIDTH), f32)] + zbuf + zbuf
                       + [pltpu.VMEM((tile, A_WIDTH), f32)] * 2
                       + [pltpu.VMEM((SUBLANES, B_WIDTH), f32)] * 2,
        compiler_params=pltpu.CompilerParams(vmem_limit_bytes=VMEM_LIMIT_BYTES,
                                             dimension_semantics=("arbitrary",)),
        name="prompt_layer",
    )(x_prompt.reshape(bsz * seq, D_MODEL), x_prompt.reshape(bsz * seq, D_MODEL), mk, mv, *wlist)
    y_p = y_p.reshape(bsz, seq, D_MODEL)

    rows = dec_b * dec_seq
    full = lambda shape: pl.BlockSpec(shape, lambda *_: (0,) * len(shape))
    y_s, hgrn_s, conv_s, lru_s = pl.pallas_call(
        functools.partial(_sample_kernel, dec_b, dec_seq),
        grid=(1,),
        in_specs=[full((rows, D_MODEL)), full((dec_b,) + KV_ROWS), full((dec_b,) + KV_ROWS),
                  full((dec_b, A_HEADS, HEAD_DIM, HEAD_DIM)), full((CTX_ROWS, dec_b, B_WIDTH)),
                  full((dec_b, B_WIDTH))] + wspecs,
        out_specs=[full((rows, D_MODEL)), full((dec_b, A_HEADS, HEAD_DIM, HEAD_DIM)),
                   full((CTX_ROWS, dec_b, B_WIDTH)), full((dec_b, B_WIDTH))],
        out_shape=[jax.ShapeDtypeStruct((rows, D_MODEL), f32),
                   jax.ShapeDtypeStruct((dec_b, A_HEADS, HEAD_DIM, HEAD_DIM), f32),
                   jax.ShapeDtypeStruct((CTX_ROWS, dec_b, B_WIDTH), f32),
                   jax.ShapeDtypeStruct((dec_b, B_WIDTH), f32)],
        scratch_shapes=[pltpu.VMEM((rows, Z_BLK), f32)] * Z_BLOCKS
                       + [pltpu.VMEM((rows, A_WIDTH), f32),
                        pltpu.VMEM((dec_b * SUBLANES, B_WIDTH), f32)],
        compiler_params=pltpu.CompilerParams(vmem_limit_bytes=VMEM_LIMIT_BYTES),
        name="sample_layer",
    )(x_sample.reshape(rows, D_MODEL), cache_mem_k.reshape((dec_b,) + KV_ROWS),
      cache_mem_v.reshape((dec_b,) + KV_ROWS), state_hgrn[0], jnp.swapaxes(state_conv[0], 0, 1),
      state_lru[0], *wlist)

    return (y_p, y_s.reshape(dec_b, dec_seq, D_MODEL), hgrn_p[None], jnp.swapaxes(conv_p, 0, 1)[None],
            lru_p[None], mk.reshape(1, bsz, N_MEM, C_HEADS, HEAD_DIM),
            mv.reshape(1, bsz, N_MEM, C_HEADS, HEAD_DIM), hgrn_s[None], jnp.swapaxes(conv_s, 0, 1)[None],
            lru_s[None])
```

```python
import functools

import jax
import jax.numpy as jnp
from jax import lax
from jax.experimental import pallas as pl
from jax.experimental.pallas import tpu as pltpu

f32 = jnp.float32
bf16 = jnp.bfloat16

D_MODEL = 1024
N_MEM = 256
EPS = 1e-6
A_HEADS = 4
HEAD_DIM = 128
A_WIDTH = A_HEADS * HEAD_DIM
B_WIDTH = D_MODEL
B_BLOCKS = 16
B_BLOCK_DIM = B_WIDTH // B_BLOCKS
CONV_W = 4
LRU_C = 8.0
LOG2_E = 1.4426950408889634
C_HEADS = 4
C_WIDTH = C_HEADS * HEAD_DIM
assert A_HEADS == C_HEADS
HGRN_CHUNK = 64
IN_COLS = 4 * A_WIDTH + 2 * B_WIDTH + 2 * C_WIDTH + 3 * D_MODEL

_QA, _FA, _VA, _GA = 0, A_WIDTH, 2 * A_WIDTH, 3 * A_WIDTH
_XB = 4 * A_WIDTH
_GB = _XB + B_WIDTH
_QC = _GB + B_WIDTH
_GC = _QC + C_WIDTH
_ZA = _GC + C_WIDTH
_ZB = _ZA + D_MODEL
_ZC = _ZB + D_MODEL
_EARLY_COLS = _GB
_LATE_READS = -1

MXU_TILE = 256
LRU_GROUPS = B_WIDTH // MXU_TILE
SUBLANES = 8
CTX_ROWS = CONV_W - 1

PROMPT_TILE = 256
Z_BLK = 1024
Z_BLOCKS = IN_COLS // Z_BLK
VMEM_LIMIT_BYTES = 60 * 1024 * 1024


def _rms(x, g):
    return x * lax.rsqrt(jnp.mean(x * x, axis=-1, keepdims=True) + EPS) * g


def _wt(ref_or_val):
    return pltpu.bitcast(ref_or_val, bf16)


def _dot(a, b):
    return jnp.dot(a.astype(bf16), b.astype(bf16), preferred_element_type=f32)


def _dot_nt(a, b):
    return lax.dot_general(a.astype(bf16), b.astype(bf16), (((1,), (1,)), ((), ())),
                           preferred_element_type=f32)


def _dot_tn(a, b):
    return lax.dot_general(a.astype(bf16), b.astype(bf16), (((0,), (0,)), ((), ())),
                           preferred_element_type=f32)


def _silu(x):
    return x * jax.nn.sigmoid(x)


KV_ROWS = (N_MEM * C_HEADS, HEAD_DIM)


def _head_rows(hd):
    return pl.ds(hd, N_MEM, stride=C_HEADS)


def _as_column(row):
    return jnp.transpose(jnp.broadcast_to(row, (SUBLANES, row.shape[1])))[:, 0:1]


def _vreg_groups(x):
    rows, width = x.shape
    return x.reshape(rows // SUBLANES, SUBLANES, width)


def _cumprod_rows(x):
    rows = x.shape[0]
    x3 = _vreg_groups(x)
    sub = lax.broadcasted_iota(jnp.int32, x3.shape, 1)
    d = 1
    while d < SUBLANES:
        x3 = x3 * jnp.where(sub >= d, pltpu.roll(x3, d, 1), 1.0)
        d *= 2
    out, carry = [], None
    for g in range(rows // SUBLANES):
        cur = x3[g] if carry is None else x3[g] * carry
        carry = cur[SUBLANES - 1:SUBLANES, :]
        out.append(cur)
    return jnp.concatenate(out, axis=0)


def _linear_scan_rows(a, u, carry):
    rows = a.shape[0]
    a3, u3 = _vreg_groups(a), _vreg_groups(u)
    sub = lax.broadcasted_iota(jnp.int32, a3.shape, 1)
    d = 1
    while d < SUBLANES:
        keep = sub >= d
        u3 = a3 * jnp.where(keep, pltpu.roll(u3, d, 1), 0.0) + u3
        a3 = a3 * jnp.where(keep, pltpu.roll(a3, d, 1), 1.0)
        d *= 2
    out = []
    for g in range(rows // SUBLANES):
        cur = u3[g] + a3[g] * carry
        carry = cur[SUBLANES - 1:SUBLANES, :]
        out.append(cur)
    return jnp.concatenate(out, axis=0), carry


class _ZBuf:
    def __init__(self, refs):
        self.refs = refs

    def cols(self, c0, width, rows=slice(None)):
        blk, off = divmod(c0, Z_BLK)
        assert off + width <= Z_BLK
        return self.refs[blk][rows, off:off + width]


def _project_in_blocks(x, z, w, cols=(0, IN_COLS)):
    h = _rms(x, w["g_mix"][...]).astype(bf16)

    def block(c0):
        blk, off = divmod(c0, Z_BLK)
        z.refs[blk][:, off:off + MXU_TILE] = jnp.dot(h, _wt(w["w_in"][:, c0:c0 + MXU_TILE]),
                                                     preferred_element_type=f32)

    return [functools.partial(block, c0) for c0 in range(cols[0], cols[1], MXU_TILE)]


def _project_in(x, z, w):
    for block in _project_in_blocks(x, z, w):
        block()


def _interleave(stages, blocks, own_late_cols=False):
    blocks = list(blocks)
    while True:
        try:
            n = next(stages)
        except StopIteration as done:
            result = done.value
            break
        if n == _LATE_READS:
            n = len(blocks) if own_late_cols else 0
        for _ in range(min(n, len(blocks))):
            blocks.pop(0)()
    for block in blocks:
        block()
    return result


def _mix_stages(x, z, oa_ref, xpad_ref, kv, states, w, *, seg, chunk, first_rows_start):
    rows = x.shape[0]
    nseg = rows // seg
    nchunk = seg // chunk
    chunk_rows = [[slice(s * seg + c * chunk, s * seg + (c + 1) * chunk) for c in range(nchunk)]
                  for s in range(nseg)]
    all_chunks = [rs for per_seg in chunk_rows for rs in per_seg]
    head_sl = [slice(hd * HEAD_DIM, (hd + 1) * HEAD_DIM) for hd in range(A_HEADS)]
    cat_rows = lambda parts: parts[0] if len(parts) == 1 else jnp.concatenate(parts, axis=0)

    lg = w["lb_logits"][...]
    l0, l1 = lg[0:1, :], lg[1:2, :]
    lmax = jnp.maximum(l0, l1)
    e0, e1 = jnp.exp(l0 - lmax), jnp.exp(l1 - lmax)
    lb = e0 / (e0 + e1)

    qg, kg, v, kd, decay = [], [], [], [], []
    for rs in all_chunks:
        yield 1
        f = lb + (1.0 - lb) * jax.nn.sigmoid(z.cols(_FA, A_WIDTH, rs))
        p = _cumprod_rows(f)
        inv_p = 1.0 / p
        k = 1.0 - f
        qg.append(_silu(z.cols(_QA, A_WIDTH, rs)) * p)
        kg.append(k * inv_p)
        v.append(z.cols(_VA, A_WIDTH, rs))
        p_last = p[chunk - 1:chunk, :]
        kd.append(kg[-1] * p_last)
        decay.append(p_last)
    qg_all, kg_all, v_all = cat_rows(qg), cat_rows(kg), cat_rows(v)

    scores = [_dot_nt(qg_all[:, sl], kg_all[:, sl]) for sl in head_sl]
    st_in = [[[states[s][0][hd]] for hd in range(A_HEADS)] for s in range(nseg)]
    for s in range(nseg):
        for c in range(nchunk):
            i = s * nchunk + c
            for hd, sl in enumerate(head_sl):
                st_in[s][hd].append(st_in[s][hd][c] * _as_column(decay[i][:, sl])
                                    + _dot_tn(kd[i][:, sl], v[i][:, sl]))
    new_st = [[st_in[s][hd][nchunk] for hd in range(A_HEADS)] for s in range(nseg)]
    yield 1
    tt = lax.broadcasted_iota(jnp.int32, (rows, rows), 0)
    ss = lax.broadcasted_iota(jnp.int32, (rows, rows), 1)
    shift = chunk.bit_length() - 1
    causal = ((tt >> shift) == (ss >> shift)) & (ss <= tt)
    o_intra = [_dot(jnp.where(causal, scores[hd], 0.0), v_all[:, sl]) for hd, sl in enumerate(head_sl)]
    yield 1
    for hd, sl in enumerate(head_sl):
        for s in range(nseg):
            for c, rs in enumerate(chunk_rows[s]):
                oa_ref[rs, sl] = o_intra[hd][rs, :] + _dot(qg[s * nchunk + c][:, sl], st_in[s][hd][c])

    g_a = w["g_a_out"][...]
    a_in = []
    for i, rs in enumerate(all_chunks):
        yield i % 2
        normed = jnp.concatenate([_rms(oa_ref[rs, sl], g_a[:, sl]) for sl in head_sl], axis=-1)
        a_in.append(normed * _silu(z.cols(_GA, A_WIDTH, rs)))
    pa = _dot(cat_rows(a_in), _wt(w["w_a_down"][...]))

    pad = SUBLANES
    for s in range(nseg):
        xpad_ref[s * pad:(s + 1) * pad, :] = jnp.zeros((pad, B_WIDTH), f32)
        xpad_ref[(s + 1) * pad - CTX_ROWS:(s + 1) * pad, :] = states[s][1]
    w_conv = w["w_conv"][...]
    sub = lax.broadcasted_iota(jnp.int32, (chunk // SUBLANES, SUBLANES, B_WIDTH), 1)
    xc, new_ctx = [], []
    for s in range(nseg):
        for c, rs in enumerate(chunk_rows[s]):
            yield 1
            if c == 0:
                ext = jnp.concatenate([xpad_ref[s * pad:(s + 1) * pad, :], z.cols(_XB, B_WIDTH, rs)], axis=0)
            else:
                ext = z.cols(_XB, B_WIDTH, slice(rs.start - pad, rs.stop))
            ext = _vreg_groups(ext)
            acc = w_conv[CONV_W - 1:CONV_W, :] * ext[1:]
            for j in range(1, CONV_W):
                rolled = pltpu.roll(ext, j, 1)
                shifted = jnp.where(sub >= j, rolled[1:], rolled[:-1])
                acc = acc + w_conv[CONV_W - 1 - j:CONV_W - j, :] * shifted
            xc.append(w["b_conv"][...] + acc.reshape(chunk, B_WIDTH))
        new_ctx.append(z.cols(_XB, B_WIDTH, slice((s + 1) * seg - CTX_ROWS, (s + 1) * seg)))

    xc_b = cat_rows(xc).astype(bf16)
    r_pre, i_pre = [], []
    for g in range(LRU_GROUPS):
        gs = slice(g * MXU_TILE, (g + 1) * MXU_TILE)
        r_pre.append(jnp.dot(xc_b[:, gs], _wt(w["w_lru_r"][g]), preferred_element_type=f32))
        i_pre.append(jnp.dot(xc_b[:, gs], _wt(w["w_lru_i"][g]), preferred_element_type=f32))
    r_pre, i_pre = jnp.concatenate(r_pre, axis=-1), jnp.concatenate(i_pre, axis=-1)
    neg_lam = -w["lru_lambda"][...]
    softplus = jnp.maximum(neg_lam, 0.0) + jnp.log1p(jnp.exp(-jnp.abs(neg_lam)))
    decay_rate = LRU_C * softplus
    a_l, u_l = [], []
    for s in range(nseg):
        for c, rs in enumerate(chunk_rows[s]):
            yield 2
            r = jax.nn.sigmoid(r_pre[rs, :] + w["b_lru_r"][...])
            ig = jax.nn.sigmoid(i_pre[rs, :] + w["b_lru_i"][...])
            nlog_a = r * decay_rate
            a = jnp.exp2(nlog_a * (-LOG2_E))
            m2 = jnp.tanh(nlog_a) * (a * a + 1.0)
            mult = jnp.where(m2 > 0.0, m2 * lax.rsqrt(m2), 0.0)
            if first_rows_start is not None and c == 0:
                first_row = lax.broadcasted_iota(jnp.int32, mult.shape, 0) == 0
                mult = jnp.where(first_row & first_rows_start, 1.0, mult)
            a_l.append(a)
            u_l.append(mult * ig * xc[s * nchunk + c])
    yield _LATE_READS
    hb_gated, new_hl = [], []
    for s in range(nseg):
        carry = states[s][2]
        for c, rs in enumerate(chunk_rows[s]):
            yield 1
            i = s * nchunk + c
            hb, carry = _linear_scan_rows(a_l[i], u_l[i], carry)
            hb_gated.append(hb * _silu(z.cols(_GB, B_WIDTH, rs)))
        new_hl.append(carry)
    pb = _dot(cat_rows(hb_gated), _wt(w["w_b_down"][...]))

    scale = HEAD_DIM ** -0.5
    mem = [[kv(s, hd) for hd in range(C_HEADS)] for s in range(nseg)]
    seg_rows = [slice(s * seg, (s + 1) * seg) for s in range(nseg)]
    sc = [[_dot_nt(z.cols(_QC + hd * HEAD_DIM, HEAD_DIM, seg_rows[s]), mem[s][hd][0])
           for hd in range(C_HEADS)] for s in range(nseg)]
    pr = []
    for s in range(nseg):
        pr.append([])
        for hd in range(C_HEADS):
            yield 1
            p = jnp.exp2((sc[s][hd] - jnp.max(sc[s][hd], axis=-1, keepdims=True)) * (scale * LOG2_E))
            pr[s].append(p / jnp.sum(p, axis=-1, keepdims=True))
    oc = cat_rows([jnp.concatenate([_dot(pr[s][hd], mem[s][hd][1]) for hd in range(C_HEADS)], axis=-1)
                   for s in range(nseg)])
    yield 1
    pc = _dot(oc * _silu(z.cols(_GC, C_WIDTH)), _wt(w["w_c_down"][...]))

    merged = []
    for rs in all_chunks:
        yield 1
        merged.append(jax.nn.sigmoid(z.cols(_ZA, D_MODEL, rs)) * pa[rs, :]
                      + jax.nn.sigmoid(z.cols(_ZB, D_MODEL, rs)) * pb[rs, :]
                      + jax.nn.sigmoid(z.cols(_ZC, D_MODEL, rs)) * pc[rs, :])
    y = x + _dot(cat_rows(merged), _wt(w["w_out"][...]))
    y = _rms(y, w["g_final"][...])
    new_states = [(new_st[s], new_ctx[s], new_hl[s]) for s in range(nseg)]
    return y, new_states


_WEIGHT_NAMES = ("g_mix", "w_in", "lb_logits", "g_a_out", "w_a_down", "w_conv", "b_conv", "w_lru_r", "b_lru_r",
                 "w_lru_i", "b_lru_i", "lru_lambda", "w_b_down", "w_c_down", "w_out", "g_final")
_NW = len(_WEIGHT_NAMES)


def _store_seq_state(conv_ref, lru_ref, seq_idx, ctx, hl):
    for r in range(CTX_ROWS):
        conv_ref[r, pl.ds(seq_idx, 1), :] = ctx[r:r + 1, :]
    lru_ref[pl.ds(seq_idx, 1), :] = hl


def _prompt_kernel(steps_per_seq, *refs):
    x_ref, xn_ref, mk_ref, mv_ref = refs[:4]
    w = dict(zip(_WEIGHT_NAMES, refs[4:4 + _NW]))
    y_ref, hgrn_ref, conv_ref, lru_ref = refs[4 + _NW:8 + _NW]
    scratch = refs[8 + _NW:]
    st_ref, ctx_ref, hl_ref = scratch[:3]
    z_even, z_odd = _ZBuf(scratch[3:3 + Z_BLOCKS]), _ZBuf(scratch[3 + Z_BLOCKS:3 + 2 * Z_BLOCKS])
    oa_refs = scratch[3 + 2 * Z_BLOCKS:5 + 2 * Z_BLOCKS]
    xpad_refs = scratch[5 + 2 * Z_BLOCKS:7 + 2 * Z_BLOCKS]
    j = pl.program_id(0)
    tile = PROMPT_TILE
    seq_start = (j % steps_per_seq) == 0

    @pl.when(j == 0)
    def _():
        _project_in(x_ref[0:tile, :], z_even, w)

    @pl.when(seq_start)
    def _():
        st_ref[...] = jnp.zeros_like(st_ref)
        ctx_ref[...] = jnp.zeros_like(ctx_ref)
        hl_ref[...] = jnp.zeros_like(hl_ref)

    kv = lambda s, hd: (mk_ref[0, _head_rows(hd), :], mv_ref[0, _head_rows(hd), :])
    states = [([st_ref[hd] for hd in range(A_HEADS)], ctx_ref[...], hl_ref[...])]

    y, states = _interleave(
        _mix_stages(x_ref[0:tile, :], z_even, oa_refs[0], xpad_refs[0], kv, states, w,
                    seg=tile, chunk=HGRN_CHUNK, first_rows_start=seq_start),
        _project_in_blocks(x_ref[tile:2 * tile, :], z_odd, w))
    y_ref[0:tile, :] = y

    y, states = _interleave(
        _mix_stages(x_ref[tile:2 * tile, :], z_odd, oa_refs[1], xpad_refs[1], kv, states, w,
                    seg=tile, chunk=HGRN_CHUNK, first_rows_start=None),
        _project_in_blocks(xn_ref[...], z_even, w))
    y_ref[tile:2 * tile, :] = y

    st, ctx, hl = states[0]
    for hd in range(A_HEADS):
        st_ref[hd] = st[hd]
    ctx_ref[...] = ctx
    hl_ref[...] = hl

    @pl.when((j % steps_per_seq) == steps_per_seq - 1)
    def _():
        for hd in range(A_HEADS):
            hgrn_ref[0, hd] = st[hd]
        _store_seq_state(conv_ref, lru_ref, j // steps_per_seq, ctx, hl)


def _sample_kernel(nseq, seg, *refs):
    x_ref, mk_ref, mv_ref, hgrn_in, conv_in, lru_in = refs[:6]
    w = dict(zip(_WEIGHT_NAMES, refs[6:6 + _NW]))
    y_ref, hgrn_ref, conv_ref, lru_ref = refs[6 + _NW:10 + _NW]
    scratch = refs[10 + _NW:]
    z = _ZBuf(scratch[:Z_BLOCKS])
    oa_ref, xpad_ref = scratch[Z_BLOCKS:]
    states = [([hgrn_in[s, hd] for hd in range(A_HEADS)],
               jnp.concatenate([conv_in[r, s:s + 1, :] for r in range(CTX_ROWS)], axis=0),
               lru_in[s:s + 1, :]) for s in range(nseq)]
    for block in _project_in_blocks(x_ref[...], z, w, (0, _EARLY_COLS)):
        block()
    y, new_states = _interleave(
        _mix_stages(x_ref[...], z, oa_ref, xpad_ref,
                    lambda s, hd: (mk_ref[s, _head_rows(hd), :], mv_ref[s, _head_rows(hd), :]), states, w,
                    seg=seg, chunk=min(HGRN_CHUNK, seg), first_rows_start=None),
        _project_in_blocks(x_ref[...], z, w, (_EARLY_COLS, IN_COLS)), own_late_cols=True)
    y_ref[...] = y
    for s in range(nseq):
        st, ctx, hl = new_states[s]
        for hd in range(A_HEADS):
            hgrn_ref[s, hd] = st[hd]
        _store_seq_state(conv_ref, lru_ref, s, ctx, hl)


def _const_spec(shape):
    nd = len(shape)
    return pl.BlockSpec(shape, lambda *_: (0,) * nd, pipeline_mode=pl.Buffered(1))


PREP_STEPS = 8
_DENSE_WEIGHTS = ("w_in", "w_a_down", "w_b_down", "w_c_down", "w_out")
_U32 = jnp.uint32


def _prep_kernel(*refs):
    nd = len(_DENSE_WEIGHTS)
    dense_in, (lru_r_in, lru_i_in) = refs[:nd], refs[nd:nd + 2]
    mem_ref, g_mem_ref, wk_in, wv_in = refs[nd + 2:nd + 6]
    outs = refs[nd + 6:]
    dense_out, (lru_r_out, lru_i_out), (k_ref, v_ref) = outs[:nd], outs[nd:nd + 2], outs[nd + 2:nd + 4]
    tile_ref, wk_ref, wv_ref = outs[nd + 4:]
    for src, dst in zip(dense_in, dense_out):
        dst[...] = pltpu.bitcast(src[...].astype(bf16), _U32)

    @pl.when(pl.program_id(0) == 0)
    def _():
        wk_ref[...] = pltpu.bitcast(wk_in[...].astype(bf16), _U32)
        wv_ref[...] = pltpu.bitcast(wv_in[...].astype(bf16), _U32)
        per = MXU_TILE // B_BLOCK_DIM
        for src, dst in ((lru_r_in, lru_r_out), (lru_i_in, lru_i_out)):
            for g in range(LRU_GROUPS):
                tile_ref[...] = jnp.zeros_like(tile_ref)
                for p in range(per):
                    lo = p * B_BLOCK_DIM
                    tile_ref[lo:lo + B_BLOCK_DIM, lo:lo + B_BLOCK_DIM] = src[g * per + p]
                dst[g] = pltpu.bitcast(tile_ref[...].astype(bf16), _U32)

    hm = _rms(mem_ref[0], g_mem_ref[...]).astype(bf16)
    k = jnp.dot(hm, _wt(wk_ref[...]), preferred_element_type=f32)
    v = jnp.dot(hm, _wt(wv_ref[...]), preferred_element_type=f32)
    for hd in range(C_HEADS):
        sl = slice(hd * HEAD_DIM, (hd + 1) * HEAD_DIM)
        k_ref[0, _head_rows(hd), :] = k[:, sl]
        v_ref[0, _head_rows(hd), :] = v[:, sl]


def _prep_weights(dense, lru_r, lru_i, mem, g_mem, w_mem_k, w_mem_v):
    bsz = mem.shape[0]
    assert bsz == PREP_STEPS
    in_specs, out_specs, out_shape = [], [], []
    for wm in dense:
        k, n = wm.shape
        assert k % (4 * SUBLANES * PREP_STEPS) == 0
        in_specs.append(pl.BlockSpec((k // PREP_STEPS, n), lambda i: (i, 0)))
        out_specs.append(pl.BlockSpec((k // (2 * PREP_STEPS), n), lambda i: (i, 0)))
        out_shape.append(jax.ShapeDtypeStruct((k // 2, n), _U32))
    blk = (B_BLOCKS, B_BLOCK_DIM, B_BLOCK_DIM)
    tiles = (LRU_GROUPS, MXU_TILE // 2, MXU_TILE)
    in_specs += [pl.BlockSpec(blk, lambda i: (0, 0, 0))] * 2
    out_specs += [pl.BlockSpec(tiles, lambda i: (0, 0, 0))] * 2
    out_shape += [jax.ShapeDtypeStruct(tiles, _U32)] * 2
    in_specs += [pl.BlockSpec((1, N_MEM, D_MODEL), lambda i: (i, 0, 0)), _const_spec((1, D_MODEL)),
                 _const_spec((D_MODEL, C_WIDTH)), _const_spec((D_MODEL, C_WIDTH))]
    out_specs += [pl.BlockSpec((1,) + KV_ROWS, lambda i: (i, 0, 0))] * 2
    out_shape += [jax.ShapeDtypeStruct((bsz,) + KV_ROWS, f32)] * 2
    outs = pl.pallas_call(
        _prep_kernel, grid=(PREP_STEPS,), in_specs=in_specs, out_specs=out_specs, out_shape=out_shape,
        scratch_shapes=[pltpu.VMEM((MXU_TILE, MXU_TILE), f32),
                        pltpu.VMEM((D_MODEL // 2, C_WIDTH), _U32), pltpu.VMEM((D_MODEL // 2, C_WIDTH), _U32)],
        compiler_params=pltpu.CompilerParams(vmem_limit_bytes=VMEM_LIMIT_BYTES,
                                             dimension_semantics=("arbitrary",)),
        name="prep_weights",
    )(*dense, lru_r, lru_i, mem, g_mem, w_mem_k, w_mem_v)
    nd = len(dense)
    return outs[:nd], outs[nd], outs[nd + 1], outs[nd + 2], outs[nd + 3]


def kernel(x_prompt, x_sample, mem_prompt, cache_mem_k, cache_mem_v, state_hgrn, state_conv, state_lru, g_mix, w_in, lb_logits, g_a_out, w_a_down, w_conv, b_conv, w_lru_r, b_lru_r, w_lru_i, b_lru_i, lru_lambda, w_b_down, g_mem, w_mem_k, w_mem_v, w_c_down, w_out, g_final):
    bsz, seq, _ = x_prompt.shape
    dec_b, dec_seq, _ = x_sample.shape
    assert g_mix.shape[0] == 1, "single-layer stack only"
    assert seq % (2 * PROMPT_TILE) == 0 and PROMPT_TILE % HGRN_CHUNK == 0

    row = lambda a: a.reshape(1, -1).astype(f32)
    dense = dict(w_in=w_in[0], w_a_down=w_a_down[0], w_b_down=w_b_down[0], w_c_down=w_c_down[0], w_out=w_out[0])
    packed, lru_r_tiles, lru_i_tiles, mk, mv = _prep_weights(
        [dense[n] for n in _DENSE_WEIGHTS], w_lru_r[0], w_lru_i[0], mem_prompt, row(g_mem[0]), w_mem_k[0], w_mem_v[0])
    packed = dict(zip(_DENSE_WEIGHTS, packed))
    weights = dict(
        g_mix=row(g_mix[0]), w_in=packed["w_in"], lb_logits=lb_logits.astype(f32), g_a_out=row(g_a_out[0]),
        w_a_down=packed["w_a_down"], w_conv=w_conv[0].astype(f32), b_conv=row(b_conv[0]),
        w_lru_r=lru_r_tiles, b_lru_r=row(b_lru_r[0]), w_lru_i=lru_i_tiles, b_lru_i=row(b_lru_i[0]),
        lru_lambda=row(lru_lambda[0]), w_b_down=packed["w_b_down"], w_c_down=packed["w_c_down"],
        w_out=packed["w_out"], g_final=row(g_final))
    wlist = [weights[n] for n in _WEIGHT_NAMES]
    wspecs = [_const_spec(a.shape) for a in wlist]

    tile = PROMPT_TILE
    n_tiles = bsz * seq // tile
    steps_per_seq = seq // (2 * tile)
    zbuf = [pltpu.VMEM((tile, Z_BLK), f32)] * Z_BLOCKS
    y_p, hgrn_p, conv_p, lru_p = pl.pallas_call(
        functools.partial(_prompt_kernel, steps_per_seq),
        grid=(n_tiles // 2,),
        in_specs=[pl.BlockSpec((2 * tile, D_MODEL), lambda j: (j, 0)),
                  pl.BlockSpec((tile, D_MODEL), lambda j: (jnp.minimum(2 * j + 2, n_tiles - 1), 0)),
                  pl.BlockSpec((1,) + KV_ROWS, lambda j: (j // steps_per_seq, 0, 0)),
                  pl.BlockSpec((1,) + KV_ROWS, lambda j: (j // steps_per_seq, 0, 0))] + wspecs,
        out_specs=[pl.BlockSpec((2 * tile, D_MODEL), lambda j: (j, 0)),
                   pl.BlockSpec((1, A_HEADS, HEAD_DIM, HEAD_DIM), lambda j: (j // steps_per_seq, 0, 0, 0)),
                   pl.BlockSpec((CTX_ROWS, bsz, B_WIDTH), lambda j: (0, 0, 0)),
                   pl.BlockSpec((bsz, B_WIDTH), lambda j: (0, 0))],
        out_shape=[jax.ShapeDtypeStruct((bsz * seq, D_MODEL), f32),
                   jax.ShapeDtypeStruct((bsz, A_HEADS, HEAD_DIM, HEAD_DIM), f32),
                   jax.ShapeDtypeStruct((CTX_ROWS, bsz, B_WIDTH), f32),
                   jax.ShapeDtypeStruct((bsz, B_WIDTH), f32)],
        scratch_shapes=[pltpu.VMEM((A_HEADS, HEAD_DIM, HEAD_DIM), f32),
                        pltpu.VMEM((CTX_ROWS, B_WIDTH), f32),
                        pltpu.VMEM((1, B_WIDTH), f32)] + zbuf + zbuf
                       + [pltpu.VMEM((tile, A_WIDTH), f32)] * 2
                       + [pltpu.VMEM((SUBLANES, B_WIDTH), f32)] * 2,
        compiler_params=pltpu.CompilerParams(vmem_limit_bytes=VMEM_LIMIT_BYTES,
                                             dimension_semantics=("arbitrary",)),
        name="prompt_layer",
    )(x_prompt.reshape(bsz * seq, D_MODEL), x_prompt.reshape(bsz * seq, D_MODEL), mk, mv, *wlist)
    y_p = y_p.reshape(bsz, seq, D_MODEL)

    rows = dec_b * dec_seq
    full = lambda shape: pl.BlockSpec(shape, lambda *_: (0,) * len(shape))
    y_s, hgrn_s, conv_s, lru_s = pl.pallas_call(
        functools.partial(_sample_kernel, dec_b, dec_seq),
        grid=(1,),
        in_specs=[full((rows, D_MODEL)), full((dec_b,) + KV_ROWS), full((dec_b,) + KV_ROWS),
                  full((dec_b, A_HEADS, HEAD_DIM, HEAD_DIM)), full((CTX_ROWS, dec_b, B_WIDTH)),
                  full((dec_b, B_WIDTH))] + wspecs,
        out_specs=[full((rows, D_MODEL)), full((dec_b, A_HEADS, HEAD_DIM, HEAD_DIM)),
                   full((CTX_ROWS, dec_b, B_WIDTH)), full((dec_b, B_WIDTH))],
        out_shape=[jax.ShapeDtypeStruct((rows, D_MODEL), f32),
                   jax.ShapeDtypeStruct((dec_b, A_HEADS, HEAD_DIM, HEAD_DIM), f32),
                   jax.ShapeDtypeStruct((CTX_ROWS, dec_b, B_WIDTH), f32),
                   jax.ShapeDtypeStruct((dec_b, B_WIDTH), f32)],
        scratch_shapes=[pltpu.VMEM((rows, Z_BLK), f32)] * Z_BLOCKS
                       + [pltpu.VMEM((rows, A_WIDTH), f32),
                        pltpu.VMEM((dec_b * SUBLANES, B_WIDTH), f32)],
        compiler_params=pltpu.CompilerParams(vmem_limit_bytes=VMEM_LIMIT_BYTES),
        name="sample_layer",
    )(x_sample.reshape(rows, D_MODEL), cache_mem_k.reshape((dec_b,) + KV_ROWS),
      cache_mem_v.reshape((dec_b,) + KV_ROWS), state_hgrn[0], jnp.swapaxes(state_conv[0], 0, 1),
      state_lru[0], *wlist)

    return (y_p, y_s.reshape(dec_b, dec_seq, D_MODEL), hgrn_p[None], jnp.swapaxes(conv_p, 0, 1)[None],
            lru_p[None], mk.reshape(1, bsz, N_MEM, C_HEADS, HEAD_DIM),
            mv.reshape(1, bsz, N_MEM, C_HEADS, HEAD_DIM), hgrn_s[None], jnp.swapaxes(conv_s, 0, 1)[None],
            lru_s[None])
```

```python
import functools

import jax
import jax.numpy as jnp
from jax import lax
from jax.experimental import pallas as pl
from jax.experimental.pallas import tpu as pltpu

f32 = jnp.float32
bf16 = jnp.bfloat16

D_MODEL = 1024
N_MEM = 256
EPS = 1e-6
A_HEADS = 4
HEAD_DIM = 128
A_WIDTH = A_HEADS * HEAD_DIM
B_WIDTH = D_MODEL
B_BLOCKS = 16
B_BLOCK_DIM = B_WIDTH // B_BLOCKS
CONV_W = 4
LRU_C = 8.0
LOG2_E = 1.4426950408889634
C_HEADS = 4
C_WIDTH = C_HEADS * HEAD_DIM
assert A_HEADS == C_HEADS
HGRN_CHUNK = 64
IN_COLS = 4 * A_WIDTH + 2 * B_WIDTH + 2 * C_WIDTH + 3 * D_MODEL

_QA, _FA, _VA, _GA = 0, A_WIDTH, 2 * A_WIDTH, 3 * A_WIDTH
_XB = 4 * A_WIDTH
_GB = _XB + B_WIDTH
_QC = _GB + B_WIDTH
_GC = _QC + C_WIDTH
_ZA = _GC + C_WIDTH
_ZB = _ZA + D_MODEL
_ZC = _ZB + D_MODEL
_EARLY_COLS = _GB
_LATE_READS = -1

MXU_TILE = 256
LRU_GROUPS = B_WIDTH // MXU_TILE
SUBLANES = 8
CTX_ROWS = CONV_W - 1

PROMPT_TILE = 256
Z_BLK = 1024
Z_BLOCKS = IN_COLS // Z_BLK
VMEM_LIMIT_BYTES = 60 * 1024 * 1024


def _rms(x, g):
    return x * lax.rsqrt(jnp.mean(x * x, axis=-1, keepdims=True) + EPS) * g


def _wt(ref_or_val):
    return pltpu.bitcast(ref_or_val, bf16)


def _dot(a, b):
    return jnp.dot(a.astype(bf16), b.astype(bf16), preferred_element_type=f32)


def _dot_nt(a, b):
    return lax.dot_general(a.astype(bf16), b.astype(bf16), (((1,), (1,)), ((), ())),
                           preferred_element_type=f32)


def _dot_tn(a, b):
    return lax.dot_general(a.astype(bf16), b.astype(bf16), (((0,), (0,)), ((), ())),
                           preferred_element_type=f32)


def _silu(x):
    return x * jax.nn.sigmoid(x)


KV_ROWS = (N_MEM * C_HEADS, HEAD_DIM)


def _head_rows(hd):
    return pl.ds(hd, N_MEM, stride=C_HEADS)


def _as_column(row):
    return jnp.transpose(jnp.broadcast_to(row, (SUBLANES, row.shape[1])))[:, 0:1]


def _vreg_groups(x):
    rows, width = x.shape
    return x.reshape(rows // SUBLANES, SUBLANES, width)


def _cumprod_rows(x):
    rows = x.shape[0]
    x3 = _vreg_groups(x)
    sub = lax.broadcasted_iota(jnp.int32, x3.shape, 1)
    d = 1
    while d < SUBLANES:
        x3 = x3 * jnp.where(sub >= d, pltpu.roll(x3, d, 1), 1.0)
        d *= 2
    out, carry = [], None
    for g in range(rows // SUBLANES):
        cur = x3[g] if carry is None else x3[g] * carry
        carry = cur[SUBLANES - 1:SUBLANES, :]
        out.append(cur)
    return jnp.concatenate(out, axis=0)


def _linear_scan_rows(a, u, carry):
    rows = a.shape[0]
    a3, u3 = _vreg_groups(a), _vreg_groups(u)
    sub = lax.broadcasted_iota(jnp.int32, a3.shape, 1)
    d = 1
    while d < SUBLANES:
        keep = sub >= d
        u3 = a3 * jnp.where(keep, pltpu.roll(u3, d, 1), 0.0) + u3
        a3 = a3 * jnp.where(keep, pltpu.roll(a3, d, 1), 1.0)
        d *= 2
    out = []
    for g in range(rows // SUBLANES):
        cur = u3[g] + a3[g] * carry
        carry = cur[SUBLANES - 1:SUBLANES, :]
        out.append(cur)
    return jnp.concatenate(out, axis=0), carry


class _ZBuf:
    def __init__(self, refs):
        self.refs = refs

    def cols(self, c0, width, rows=slice(None)):
        blk, off = divmod(c0, Z_BLK)
        assert off + width <= Z_BLK
        return self.refs[blk][rows, off:off + width]


def _project_in_blocks(x, z, w, cols=(0, IN_COLS)):
    h = _rms(x, w["g_mix"][...]).astype(bf16)

    def block(c0):
        blk, off = divmod(c0, Z_BLK)
        z.refs[blk][:, off:off + MXU_TILE] = jnp.dot(h, _wt(w["w_in"][:, c0:c0 + MXU_TILE]),
                                                     preferred_element_type=f32)

    return [functools.partial(block, c0) for c0 in range(cols[0], cols[1], MXU_TILE)]


def _project_in(x, z, w):
    for block in _project_in_blocks(x, z, w):
        block()


def _after(copy, block):
    def run():
        copy.wait()
        block()
    return run


class _Arriving:
    def __init__(self, ref, copy):
        self._ref, self._copy = ref, copy

    def __getitem__(self, idx):
        if self._copy is not None:
            self._copy.wait()
            self._copy = None
        return self._ref[idx]


def _interleave(stages, blocks, own_late_cols=False):
    blocks = list(blocks)
    while True:
        try:
            n = next(stages)
        except StopIteration as done:
            result = done.value
            break
        if n == _LATE_READS:
            n = len(blocks) if own_late_cols else 0
        for _ in range(min(n, len(blocks))):
            blocks.pop(0)()
    for block in blocks:
        block()
    return result


def _mix_stages(x, z, oa_ref, xpad_ref, kv, states, w, *, seg, chunk, first_rows_start):
    rows = x.shape[0]
    nseg = rows // seg
    nchunk = seg // chunk
    chunk_rows = [[slice(s * seg + c * chunk, s * seg + (c + 1) * chunk) for c in range(nchunk)]
                  for s in range(nseg)]
    all_chunks = [rs for per_seg in chunk_rows for rs in per_seg]
    head_sl = [slice(hd * HEAD_DIM, (hd + 1) * HEAD_DIM) for hd in range(A_HEADS)]
    cat_rows = lambda parts: parts[0] if len(parts) == 1 else jnp.concatenate(parts, axis=0)

    lg = w["lb_logits"][...]
    l0, l1 = lg[0:1, :], lg[1:2, :]
    lmax = jnp.maximum(l0, l1)
    e0, e1 = jnp.exp(l0 - lmax), jnp.exp(l1 - lmax)
    lb = e0 / (e0 + e1)

    qg, kg, v, kd, decay = [], [], [], [], []
    for rs in all_chunks:
        yield 1
        f = lb + (1.0 - lb) * jax.nn.sigmoid(z.cols(_FA, A_WIDTH, rs))
        p = _cumprod_rows(f)
        inv_p = 1.0 / p
        k = 1.0 - f
        qg.append(_silu(z.cols(_QA, A_WIDTH, rs)) * p)
        kg.append(k * inv_p)
        v.append(z.cols(_VA, A_WIDTH, rs))
        p_last = p[chunk - 1:chunk, :]
        kd.append(kg[-1] * p_last)
        decay.append(p_last)
    qg_all, kg_all, v_all = cat_rows(qg), cat_rows(kg), cat_rows(v)

    scores = [_dot_nt(qg_all[:, sl], kg_all[:, sl]) for sl in head_sl]
    st_in = [[[states[s][0][hd]] for hd in range(A_HEADS)] for s in range(nseg)]
    for s in range(nseg):
        for c in range(nchunk):
            i = s * nchunk + c
            for hd, sl in enumerate(head_sl):
                st_in[s][hd].append(st_in[s][hd][c] * _as_column(decay[i][:, sl])
                                    + _dot_tn(kd[i][:, sl], v[i][:, sl]))
    new_st = [[st_in[s][hd][nchunk] for hd in range(A_HEADS)] for s in range(nseg)]
    yield 1
    tt = lax.broadcasted_iota(jnp.int32, (rows, rows), 0)
    ss = lax.broadcasted_iota(jnp.int32, (rows, rows), 1)
    shift = chunk.bit_length() - 1
    causal = ((tt >> shift) == (ss >> shift)) & (ss <= tt)
    o_intra = [_dot(jnp.where(causal, scores[hd], 0.0), v_all[:, sl]) for hd, sl in enumerate(head_sl)]
    yield 1
    for hd, sl in enumerate(head_sl):
        for s in range(nseg):
            for c, rs in enumerate(chunk_rows[s]):
                oa_ref[rs, sl] = o_intra[hd][rs, :] + _dot(qg[s * nchunk + c][:, sl], st_in[s][hd][c])

    g_a = w["g_a_out"][...]
    a_in = []
    for i, rs in enumerate(all_chunks):
        yield i % 2
        normed = jnp.concatenate([_rms(oa_ref[rs, sl], g_a[:, sl]) for sl in head_sl], axis=-1)
        a_in.append(normed * _silu(z.cols(_GA, A_WIDTH, rs)))
    pa = _dot(cat_rows(a_in), _wt(w["w_a_down"][...]))

    pad = SUBLANES
    for s in range(nseg):
        xpad_ref[s * pad:(s + 1) * pad, :] = jnp.zeros((pad, B_WIDTH), f32)
        xpad_ref[(s + 1) * pad - CTX_ROWS:(s + 1) * pad, :] = states[s][1]
    w_conv = w["w_conv"][...]
    sub = lax.broadcasted_iota(jnp.int32, (chunk // SUBLANES, SUBLANES, B_WIDTH), 1)
    xc, new_ctx = [], []
    for s in range(nseg):
        for c, rs in enumerate(chunk_rows[s]):
            yield 1
            if c == 0:
                ext = jnp.concatenate([xpad_ref[s * pad:(s + 1) * pad, :], z.cols(_XB, B_WIDTH, rs)], axis=0)
            else:
                ext = z.cols(_XB, B_WIDTH, slice(rs.start - pad, rs.stop))
            ext = _vreg_groups(ext)
            acc = w_conv[CONV_W - 1:CONV_W, :] * ext[1:]
            for j in range(1, CONV_W):
                rolled = pltpu.roll(ext, j, 1)
                shifted = jnp.where(sub >= j, rolled[1:], rolled[:-1])
                acc = acc + w_conv[CONV_W - 1 - j:CONV_W - j, :] * shifted
            xc.append(w["b_conv"][...] + acc.reshape(chunk, B_WIDTH))
        new_ctx.append(z.cols(_XB, B_WIDTH, slice((s + 1) * seg - CTX_ROWS, (s + 1) * seg)))

    xc_b = cat_rows(xc).astype(bf16)
    r_pre, i_pre = [], []
    for g in range(LRU_GROUPS):
        gs = slice(g * MXU_TILE, (g + 1) * MXU_TILE)
        r_pre.append(jnp.dot(xc_b[:, gs], _wt(w["w_lru_r"][g]), preferred_element_type=f32))
        i_pre.append(jnp.dot(xc_b[:, gs], _wt(w["w_lru_i"][g]), preferred_element_type=f32))
    r_pre, i_pre = jnp.concatenate(r_pre, axis=-1), jnp.concatenate(i_pre, axis=-1)
    neg_lam = -w["lru_lambda"][...]
    softplus = jnp.maximum(neg_lam, 0.0) + jnp.log1p(jnp.exp(-jnp.abs(neg_lam)))
    decay_rate = LRU_C * softplus
    a_l, u_l = [], []
    for s in range(nseg):
        for c, rs in enumerate(chunk_rows[s]):
            yield 2
            r = jax.nn.sigmoid(r_pre[rs, :] + w["b_lru_r"][...])
            ig = jax.nn.sigmoid(i_pre[rs, :] + w["b_lru_i"][...])
            nlog_a = r * decay_rate
            a = jnp.exp2(nlog_a * (-LOG2_E))
            m2 = jnp.tanh(nlog_a) * (a * a + 1.0)
            mult = jnp.where(m2 > 0.0, m2 * lax.rsqrt(m2), 0.0)
            if first_rows_start is not None and c == 0:
                first_row = lax.broadcasted_iota(jnp.int32, mult.shape, 0) == 0
                mult = jnp.where(first_row & first_rows_start, 1.0, mult)
            a_l.append(a)
            u_l.append(mult * ig * xc[s * nchunk + c])
    yield _LATE_READS
    hb_gated, new_hl = [], []
    for s in range(nseg):
        carry = states[s][2]
        for c, rs in enumerate(chunk_rows[s]):
            yield 1
            i = s * nchunk + c
            hb, carry = _linear_scan_rows(a_l[i], u_l[i], carry)
            hb_gated.append(hb * _silu(z.cols(_GB, B_WIDTH, rs)))
        new_hl.append(carry)
    pb = _dot(cat_rows(hb_gated), _wt(w["w_b_down"][...]))

    scale = HEAD_DIM ** -0.5
    mem = [[kv(s, hd) for hd in range(C_HEADS)] for s in range(nseg)]
    seg_rows = [slice(s * seg, (s + 1) * seg) for s in range(nseg)]
    sc = [[_dot_nt(z.cols(_QC + hd * HEAD_DIM, HEAD_DIM, seg_rows[s]), mem[s][hd][0])
           for hd in range(C_HEADS)] for s in range(nseg)]
    pr = []
    for s in range(nseg):
        pr.append([])
        for hd in range(C_HEADS):
            yield 1
            p = jnp.exp2((sc[s][hd] - jnp.max(sc[s][hd], axis=-1, keepdims=True)) * (scale * LOG2_E))
            pr[s].append(p / jnp.sum(p, axis=-1, keepdims=True))
    oc = cat_rows([jnp.concatenate([_dot(pr[s][hd], mem[s][hd][1]) for hd in range(C_HEADS)], axis=-1)
                   for s in range(nseg)])
    yield 1
    pc = _dot(oc * _silu(z.cols(_GC, C_WIDTH)), _wt(w["w_c_down"][...]))

    merged = []
    for rs in all_chunks:
        yield 1
        merged.append(jax.nn.sigmoid(z.cols(_ZA, D_MODEL, rs)) * pa[rs, :]
                      + jax.nn.sigmoid(z.cols(_ZB, D_MODEL, rs)) * pb[rs, :]
                      + jax.nn.sigmoid(z.cols(_ZC, D_MODEL, rs)) * pc[rs, :])
    y = x + _dot(cat_rows(merged), _wt(w["w_out"][...]))
    y = _rms(y, w["g_final"][...])
    new_states = [(new_st[s], new_ctx[s], new_hl[s]) for s in range(nseg)]
    return y, new_states


_WEIGHT_NAMES = ("g_mix", "w_in", "lb_logits", "g_a_out", "w_a_down", "w_conv", "b_conv", "w_lru_r", "b_lru_r",
                 "w_lru_i", "b_lru_i", "lru_lambda", "w_b_down", "w_c_down", "w_out", "g_final")
_NW = len(_WEIGHT_NAMES)


def _store_seq_state(conv_ref, lru_ref, seq_idx, ctx, hl):
    for r in range(CTX_ROWS):
        conv_ref[r, pl.ds(seq_idx, 1), :] = ctx[r:r + 1, :]
    lru_ref[pl.ds(seq_idx, 1), :] = hl


def _prompt_kernel(steps_per_seq, *refs):
    x_ref, xn_ref, mk_ref, mv_ref = refs[:4]
    w = dict(zip(_WEIGHT_NAMES, refs[4:4 + _NW]))
    y_ref, hgrn_ref, conv_ref, lru_ref = refs[4 + _NW:8 + _NW]
    scratch = refs[8 + _NW:]
    st_ref, ctx_ref, hl_ref = scratch[:3]
    z_even, z_odd = _ZBuf(scratch[3:3 + Z_BLOCKS]), _ZBuf(scratch[3 + Z_BLOCKS:3 + 2 * Z_BLOCKS])
    oa_refs = scratch[3 + 2 * Z_BLOCKS:5 + 2 * Z_BLOCKS]
    xpad_refs = scratch[5 + 2 * Z_BLOCKS:7 + 2 * Z_BLOCKS]
    j = pl.program_id(0)
    tile = PROMPT_TILE
    seq_start = (j % steps_per_seq) == 0

    @pl.when(j == 0)
    def _():
        _project_in(x_ref[0:tile, :], z_even, w)

    @pl.when(seq_start)
    def _():
        st_ref[...] = jnp.zeros_like(st_ref)
        ctx_ref[...] = jnp.zeros_like(ctx_ref)
        hl_ref[...] = jnp.zeros_like(hl_ref)

    kv = lambda s, hd: (mk_ref[0, _head_rows(hd), :], mv_ref[0, _head_rows(hd), :])
    states = [([st_ref[hd] for hd in range(A_HEADS)], ctx_ref[...], hl_ref[...])]

    y, states = _interleave(
        _mix_stages(x_ref[0:tile, :], z_even, oa_refs[0], xpad_refs[0], kv, states, w,
                    seg=tile, chunk=HGRN_CHUNK, first_rows_start=seq_start),
        _project_in_blocks(x_ref[tile:2 * tile, :], z_odd, w))
    y_ref[0:tile, :] = y

    y, states = _interleave(
        _mix_stages(x_ref[tile:2 * tile, :], z_odd, oa_refs[1], xpad_refs[1], kv, states, w,
                    seg=tile, chunk=HGRN_CHUNK, first_rows_start=None),
        _project_in_blocks(xn_ref[...], z_even, w))
    y_ref[tile:2 * tile, :] = y

    st, ctx, hl = states[0]
    for hd in range(A_HEADS):
        st_ref[hd] = st[hd]
    ctx_ref[...] = ctx
    hl_ref[...] = hl

    @pl.when((j % steps_per_seq) == steps_per_seq - 1)
    def _():
        for hd in range(A_HEADS):
            hgrn_ref[0, hd] = st[hd]
        _store_seq_state(conv_ref, lru_ref, j // steps_per_seq, ctx, hl)


def _sample_kernel(nseq, seg, *refs):
    x_ref, mk_ref, mv_ref, hgrn_in, conv_in, lru_in = refs[:6]
    w = dict(zip(_WEIGHT_NAMES, refs[6:6 + _NW]))
    y_ref, hgrn_ref, conv_ref, lru_ref = refs[6 + _NW:10 + _NW]
    scratch = refs[10 + _NW:]
    z = _ZBuf(scratch[:Z_BLOCKS])
    oa_ref, xpad_ref, w_in_vmem, w_out_vmem, sems = scratch[Z_BLOCKS:]
    w_in_copies = []
    for blk in range(Z_BLOCKS):
        cols = slice(blk * Z_BLK, (blk + 1) * Z_BLK)
        w_in_copies.append(pltpu.make_async_copy(w["w_in"].at[:, cols], w_in_vmem.at[:, cols], sems.at[blk]))
        w_in_copies[-1].start()
    w_out_copy = pltpu.make_async_copy(w["w_out"], w_out_vmem, sems.at[Z_BLOCKS])
    w_out_copy.start()
    w = dict(w, w_in=w_in_vmem, w_out=_Arriving(w_out_vmem, w_out_copy))
    states = [([hgrn_in[s, hd] for hd in range(A_HEADS)],
               jnp.concatenate([conv_in[r, s:s + 1, :] for r in range(CTX_ROWS)], axis=0),
               lru_in[s:s + 1, :]) for s in range(nseq)]
    per_chunk = Z_BLK // MXU_TILE
    blocks = [_after(w_in_copies[i // per_chunk], block) if i % per_chunk == 0 else block
              for i, block in enumerate(_project_in_blocks(x_ref[...], z, w))]
    n_early = _EARLY_COLS // MXU_TILE
    for block in blocks[:n_early]:
        block()
    y, new_states = _interleave(
        _mix_stages(x_ref[...], z, oa_ref, xpad_ref,
                    lambda s, hd: (mk_ref[s, _head_rows(hd), :], mv_ref[s, _head_rows(hd), :]), states, w,
                    seg=seg, chunk=min(HGRN_CHUNK, seg), first_rows_start=None),
        blocks[n_early:], own_late_cols=True)
    y_ref[...] = y
    for s in range(nseq):
        st, ctx, hl = new_states[s]
        for hd in range(A_HEADS):
            hgrn_ref[s, hd] = st[hd]
        _store_seq_state(conv_ref, lru_ref, s, ctx, hl)


def _const_spec(shape):
    nd = len(shape)
    return pl.BlockSpec(shape, lambda *_: (0,) * nd, pipeline_mode=pl.Buffered(1))


PREP_STEPS = 8
_DENSE_WEIGHTS = ("w_in", "w_a_down", "w_b_down", "w_c_down", "w_out")
_U32 = jnp.uint32


def _prep_kernel(*refs):
    nd = len(_DENSE_WEIGHTS)
    dense_in, (lru_r_in, lru_i_in) = refs[:nd], refs[nd:nd + 2]
    mem_ref, g_mem_ref, wk_in, wv_in = refs[nd + 2:nd + 6]
    outs = refs[nd + 6:]
    dense_out, (lru_r_out, lru_i_out), (k_ref, v_ref) = outs[:nd], outs[nd:nd + 2], outs[nd + 2:nd + 4]
    tile_ref, wk_ref, wv_ref = outs[nd + 4:]
    for src, dst in zip(dense_in, dense_out):
        dst[...] = pltpu.bitcast(src[...].astype(bf16), _U32)

    @pl.when(pl.program_id(0) == 0)
    def _():
        wk_ref[...] = pltpu.bitcast(wk_in[...].astype(bf16), _U32)
        wv_ref[...] = pltpu.bitcast(wv_in[...].astype(bf16), _U32)
        per = MXU_TILE // B_BLOCK_DIM
        for src, dst in ((lru_r_in, lru_r_out), (lru_i_in, lru_i_out)):
            for g in range(LRU_GROUPS):
                tile_ref[...] = jnp.zeros_like(tile_ref)
                for p in range(per):
                    lo = p * B_BLOCK_DIM
                    tile_ref[lo:lo + B_BLOCK_DIM, lo:lo + B_BLOCK_DIM] = src[g * per + p]
                dst[g] = pltpu.bitcast(tile_ref[...].astype(bf16), _U32)

    hm = _rms(mem_ref[0], g_mem_ref[...]).astype(bf16)
    k = jnp.dot(hm, _wt(wk_ref[...]), preferred_element_type=f32)
    v = jnp.dot(hm, _wt(wv_ref[...]), preferred_element_type=f32)
    for hd in range(C_HEADS):
        sl = slice(hd * HEAD_DIM, (hd + 1) * HEAD_DIM)
        k_ref[0, _head_rows(hd), :] = k[:, sl]
        v_ref[0, _head_rows(hd), :] = v[:, sl]


def _prep_weights(dense, lru_r, lru_i, mem, g_mem, w_mem_k, w_mem_v):
    bsz = mem.shape[0]
    assert bsz == PREP_STEPS
    in_specs, out_specs, out_shape = [], [], []
    for wm in dense:
        k, n = wm.shape
        assert k % (4 * SUBLANES * PREP_STEPS) == 0
        in_specs.append(pl.BlockSpec((k // PREP_STEPS, n), lambda i: (i, 0)))
        out_specs.append(pl.BlockSpec((k // (2 * PREP_STEPS), n), lambda i: (i, 0)))
        out_shape.append(jax.ShapeDtypeStruct((k // 2, n), _U32))
    blk = (B_BLOCKS, B_BLOCK_DIM, B_BLOCK_DIM)
    tiles = (LRU_GROUPS, MXU_TILE // 2, MXU_TILE)
    in_specs += [pl.BlockSpec(blk, lambda i: (0, 0, 0))] * 2
    out_specs += [pl.BlockSpec(tiles, lambda i: (0, 0, 0))] * 2
    out_shape += [jax.ShapeDtypeStruct(tiles, _U32)] * 2
    in_specs += [pl.BlockSpec((1, N_MEM, D_MODEL), lambda i: (i, 0, 0)), _const_spec((1, D_MODEL)),
                 _const_spec((D_MODEL, C_WIDTH)), _const_spec((D_MODEL, C_WIDTH))]
    out_specs += [pl.BlockSpec((1,) + KV_ROWS, lambda i: (i, 0, 0))] * 2
    out_shape += [jax.ShapeDtypeStruct((bsz,) + KV_ROWS, f32)] * 2
    outs = pl.pallas_call(
        _prep_kernel, grid=(PREP_STEPS,), in_specs=in_specs, out_specs=out_specs, out_shape=out_shape,
        scratch_shapes=[pltpu.VMEM((MXU_TILE, MXU_TILE), f32),
                        pltpu.VMEM((D_MODEL // 2, C_WIDTH), _U32), pltpu.VMEM((D_MODEL // 2, C_WIDTH), _U32)],
        compiler_params=pltpu.CompilerParams(vmem_limit_bytes=VMEM_LIMIT_BYTES,
                                             dimension_semantics=("arbitrary",)),
        name="prep_weights",
    )(*dense, lru_r, lru_i, mem, g_mem, w_mem_k, w_mem_v)
    nd = len(dense)
    return outs[:nd], outs[nd], outs[nd + 1], outs[nd + 2], outs[nd + 3]


def kernel(x_prompt, x_sample, mem_prompt, cache_mem_k, cache_mem_v, state_hgrn, state_conv, state_lru, g_mix, w_in, lb_logits, g_a_out, w_a_down, w_conv, b_conv, w_lru_r, b_lru_r, w_lru_i, b_lru_i, lru_lambda, w_b_down, g_mem, w_mem_k, w_mem_v, w_c_down, w_out, g_final):
    bsz, seq, _ = x_prompt.shape
    dec_b, dec_seq, _ = x_sample.shape
    assert g_mix.shape[0] == 1, "single-layer stack only"
    assert seq % (2 * PROMPT_TILE) == 0 and PROMPT_TILE % HGRN_CHUNK == 0

    row = lambda a: a.reshape(1, -1).astype(f32)
    dense = dict(w_in=w_in[0], w_a_down=w_a_down[0], w_b_down=w_b_down[0], w_c_down=w_c_down[0], w_out=w_out[0])
    packed, lru_r_tiles, lru_i_tiles, mk, mv = _prep_weights(
        [dense[n] for n in _DENSE_WEIGHTS], w_lru_r[0], w_lru_i[0], mem_prompt, row(g_mem[0]), w_mem_k[0], w_mem_v[0])
    packed = dict(zip(_DENSE_WEIGHTS, packed))
    weights = dict(
        g_mix=row(g_mix[0]), w_in=packed["w_in"], lb_logits=lb_logits.astype(f32), g_a_out=row(g_a_out[0]),
        w_a_down=packed["w_a_down"], w_conv=w_conv[0].astype(f32), b_conv=row(b_conv[0]),
        w_lru_r=lru_r_tiles, b_lru_r=row(b_lru_r[0]), w_lru_i=lru_i_tiles, b_lru_i=row(b_lru_i[0]),
        lru_lambda=row(lru_lambda[0]), w_b_down=packed["w_b_down"], w_c_down=packed["w_c_down"],
        w_out=packed["w_out"], g_final=row(g_final))
    wlist = [weights[n] for n in _WEIGHT_NAMES]
    wspecs = [_const_spec(a.shape) for a in wlist]

    tile = PROMPT_TILE
    n_tiles = bsz * seq // tile
    steps_per_seq = seq // (2 * tile)
    zbuf = [pltpu.VMEM((tile, Z_BLK), f32)] * Z_BLOCKS
    y_p, hgrn_p, conv_p, lru_p = pl.pallas_call(
        functools.partial(_prompt_kernel, steps_per_seq),
        grid=(n_tiles // 2,),
        in_specs=[pl.BlockSpec((2 * tile, D_MODEL), lambda j: (j, 0)),
                  pl.BlockSpec((tile, D_MODEL), lambda j: (jnp.minimum(2 * j + 2, n_tiles - 1), 0)),
                  pl.BlockSpec((1,) + KV_ROWS, lambda j: (j // steps_per_seq, 0, 0)),
                  pl.BlockSpec((1,) + KV_ROWS, lambda j: (j // steps_per_seq, 0, 0))] + wspecs,
        out_specs=[pl.BlockSpec((2 * tile, D_MODEL), lambda j: (j, 0)),
                   pl.BlockSpec((1, A_HEADS, HEAD_DIM, HEAD_DIM), lambda j: (j // steps_per_seq, 0, 0, 0)),
                   pl.BlockSpec((CTX_ROWS, bsz, B_WIDTH), lambda j: (0, 0, 0)),
                   pl.BlockSpec((bsz, B_WIDTH), lambda j: (0, 0))],
        out_shape=[jax.ShapeDtypeStruct((bsz * seq, D_MODEL), f32),
                   jax.ShapeDtypeStruct((bsz, A_HEADS, HEAD_DIM, HEAD_DIM), f32),
                   jax.ShapeDtypeStruct((CTX_ROWS, bsz, B_WIDTH), f32),
                   jax.ShapeDtypeStruct((bsz, B_WIDTH), f32)],
        scratch_shapes=[pltpu.VMEM((A_HEADS, HEAD_DIM, HEAD_DIM), f32),
                        pltpu.VMEM((CTX_ROWS, B_WIDTH), f32),
                        pltpu.VMEM((1, B_WIDTH), f32)] + zbuf + zbuf
                       + [pltpu.VMEM((tile, A_WIDTH), f32)] * 2
                       + [pltpu.VMEM((SUBLANES, B_WIDTH), f32)] * 2,
        compiler_params=pltpu.CompilerParams(vmem_limit_bytes=VMEM_LIMIT_BYTES,
                                             dimension_semantics=("arbitrary",)),
        name="prompt_layer",
    )(x_prompt.reshape(bsz * seq, D_MODEL), x_prompt.reshape(bsz * seq, D_MODEL), mk, mv, *wlist)
    y_p = y_p.reshape(bsz, seq, D_MODEL)

    rows = dec_b * dec_seq
    full = lambda shape: pl.BlockSpec(shape, lambda *_: (0,) * len(shape))
    y_s, hgrn_s, conv_s, lru_s = pl.pallas_call(
        functools.partial(_sample_kernel, dec_b, dec_seq),
        grid=(1,),
        in_specs=[full((rows, D_MODEL)), full((dec_b,) + KV_ROWS), full((dec_b,) + KV_ROWS),
                  full((dec_b, A_HEADS, HEAD_DIM, HEAD_DIM)), full((CTX_ROWS, dec_b, B_WIDTH)),
                  full((dec_b, B_WIDTH))]
                 + [pl.BlockSpec(memory_space=pl.ANY) if n in ("w_in", "w_out") else s
                    for n, s in zip(_WEIGHT_NAMES, wspecs)],
        out_specs=[full((rows, D_MODEL)), full((dec_b, A_HEADS, HEAD_DIM, HEAD_DIM)),
                   full((CTX_ROWS, dec_b, B_WIDTH)), full((dec_b, B_WIDTH))],
        out_shape=[jax.ShapeDtypeStruct((rows, D_MODEL), f32),
                   jax.ShapeDtypeStruct((dec_b, A_HEADS, HEAD_DIM, HEAD_DIM), f32),
                   jax.ShapeDtypeStruct((CTX_ROWS, dec_b, B_WIDTH), f32),
                   jax.ShapeDtypeStruct((dec_b, B_WIDTH), f32)],
        scratch_shapes=[pltpu.VMEM((rows, Z_BLK), f32)] * Z_BLOCKS
                       + [pltpu.VMEM((rows, A_WIDTH), f32),
                        pltpu.VMEM((dec_b * SUBLANES, B_WIDTH), f32),
                        pltpu.VMEM(weights["w_in"].shape, _U32), pltpu.VMEM(weights["w_out"].shape, _U32),
                        pltpu.SemaphoreType.DMA((Z_BLOCKS + 1,))],
        compiler_params=pltpu.CompilerParams(vmem_limit_bytes=VMEM_LIMIT_BYTES),
        name="sample_layer",
    )(x_sample.reshape(rows, D_MODEL), cache_mem_k.reshape((dec_b,) + KV_ROWS),
      cache_mem_v.reshape((dec_b,) + KV_ROWS), state_hgrn[0], jnp.swapaxes(state_conv[0], 0, 1),
      state_lru[0], *wlist)

    return (y_p, y_s.reshape(dec_b, dec_seq, D_MODEL), hgrn_p[None], jnp.swapaxes(conv_p, 0, 1)[None],
            lru_p[None], mk.reshape(1, bsz, N_MEM, C_HEADS, HEAD_DIM),
            mv.reshape(1, bsz, N_MEM, C_HEADS, HEAD_DIM), hgrn_s[None], jnp.swapaxes(conv_s, 0, 1)[None],
            lru_s[None])
```

```python
import functools

import jax
import jax.numpy as jnp
from jax import lax
from jax.experimental import pallas as pl
from jax.experimental.pallas import tpu as pltpu

f32 = jnp.float32
bf16 = jnp.bfloat16

D_MODEL = 1024
N_MEM = 256
EPS = 1e-6
A_HEADS = 4
HEAD_DIM = 128
A_WIDTH = A_HEADS * HEAD_DIM
B_WIDTH = D_MODEL
B_BLOCKS = 16
B_BLOCK_DIM = B_WIDTH // B_BLOCKS
CONV_W = 4
LRU_C = 8.0
LOG2_E = 1.4426950408889634
C_HEADS = 4
C_WIDTH = C_HEADS * HEAD_DIM
assert A_HEADS == C_HEADS
HGRN_CHUNK = 64
IN_COLS = 4 * A_WIDTH + 2 * B_WIDTH + 2 * C_WIDTH + 3 * D_MODEL

_QA, _FA, _VA, _GA = 0, A_WIDTH, 2 * A_WIDTH, 3 * A_WIDTH
_XB = 4 * A_WIDTH
_GB = _XB + B_WIDTH
_QC = _GB + B_WIDTH
_GC = _QC + C_WIDTH
_ZA = _GC + C_WIDTH
_ZB = _ZA + D_MODEL
_ZC = _ZB + D_MODEL
_EARLY_COLS = _GA


class _Reads(int):
    pass

MXU_TILE = 256
LRU_GROUPS = B_WIDTH // MXU_TILE
SUBLANES = 8
CTX_ROWS = CONV_W - 1

PROMPT_TILE = 256
Z_BLK = 1024
Z_BLOCKS = IN_COLS // Z_BLK
VMEM_LIMIT_BYTES = 60 * 1024 * 1024


def _rms(x, g):
    return x * lax.rsqrt(jnp.mean(x * x, axis=-1, keepdims=True) + EPS) * g


def _wt(ref_or_val):
    return pltpu.bitcast(ref_or_val, bf16)


def _dot(a, b):
    return jnp.dot(a.astype(bf16), b.astype(bf16), preferred_element_type=f32)


def _dot_nt(a, b):
    return lax.dot_general(a.astype(bf16), b.astype(bf16), (((1,), (1,)), ((), ())),
                           preferred_element_type=f32)


def _dot_tn(a, b):
    return lax.dot_general(a.astype(bf16), b.astype(bf16), (((0,), (0,)), ((), ())),
                           preferred_element_type=f32)


def _silu(x):
    return x * jax.nn.sigmoid(x)


KV_ROWS = (N_MEM * C_HEADS, HEAD_DIM)


def _head_rows(hd):
    return pl.ds(hd, N_MEM, stride=C_HEADS)


def _as_column(row):
    return jnp.transpose(jnp.broadcast_to(row, (SUBLANES, row.shape[1])))[:, 0:1]


def _vreg_groups(x):
    rows, width = x.shape
    return x.reshape(rows // SUBLANES, SUBLANES, width)


def _cumprod_rows(x):
    rows = x.shape[0]
    x3 = _vreg_groups(x)
    sub = lax.broadcasted_iota(jnp.int32, x3.shape, 1)
    d = 1
    while d < SUBLANES:
        x3 = x3 * jnp.where(sub >= d, pltpu.roll(x3, d, 1), 1.0)
        d *= 2
    out, carry = [], None
    for g in range(rows // SUBLANES):
        cur = x3[g] if carry is None else x3[g] * carry
        carry = cur[SUBLANES - 1:SUBLANES, :]
        out.append(cur)
    return jnp.concatenate(out, axis=0)


def _linear_scan_rows(a, u, carry):
    rows = a.shape[0]
    a3, u3 = _vreg_groups(a), _vreg_groups(u)
    sub = lax.broadcasted_iota(jnp.int32, a3.shape, 1)
    d = 1
    while d < SUBLANES:
        keep = sub >= d
        u3 = a3 * jnp.where(keep, pltpu.roll(u3, d, 1), 0.0) + u3
        a3 = a3 * jnp.where(keep, pltpu.roll(a3, d, 1), 1.0)
        d *= 2
    out = []
    for g in range(rows // SUBLANES):
        cur = u3[g] + a3[g] * carry
        carry = cur[SUBLANES - 1:SUBLANES, :]
        out.append(cur)
    return jnp.concatenate(out, axis=0), carry


class _ZBuf:
    def __init__(self, refs):
        self.refs = refs

    def cols(self, c0, width, rows=slice(None)):
        blk, off = divmod(c0, Z_BLK)
        assert off + width <= Z_BLK
        return self.refs[blk][rows, off:off + width]


def _project_in_blocks(x, z, w, cols=(0, IN_COLS)):
    h = _rms(x, w["g_mix"][...]).astype(bf16)

    def block(c0):
        blk, off = divmod(c0, Z_BLK)
        z.refs[blk][:, off:off + MXU_TILE] = jnp.dot(h, _wt(w["w_in"][:, c0:c0 + MXU_TILE]),
                                                     preferred_element_type=f32)

    return [functools.partial(block, c0) for c0 in range(cols[0], cols[1], MXU_TILE)]


def _project_in(x, z, w):
    for block in _project_in_blocks(x, z, w):
        block()


def _interleave(stages, blocks, own_late_cols=False):
    blocks = list(blocks)
    while True:
        try:
            n = next(stages)
        except StopIteration as done:
            result = done.value
            break
        if isinstance(n, _Reads):
            n = sum(block.args[0] < n for block in blocks) if own_late_cols else 0
        for _ in range(min(n, len(blocks))):
            blocks.pop(0)()
    for block in blocks:
        block()
    return result


def _mix_stages(x, z, oa_ref, xpad_ref, kv, states, w, *, seg, chunk, first_rows_start):
    rows = x.shape[0]
    nseg = rows // seg
    nchunk = seg // chunk
    chunk_rows = [[slice(s * seg + c * chunk, s * seg + (c + 1) * chunk) for c in range(nchunk)]
                  for s in range(nseg)]
    all_chunks = [rs for per_seg in chunk_rows for rs in per_seg]
    head_sl = [slice(hd * HEAD_DIM, (hd + 1) * HEAD_DIM) for hd in range(A_HEADS)]
    cat_rows = lambda parts: parts[0] if len(parts) == 1 else jnp.concatenate(parts, axis=0)

    lg = w["lb_logits"][...]
    l0, l1 = lg[0:1, :], lg[1:2, :]
    lmax = jnp.maximum(l0, l1)
    e0, e1 = jnp.exp(l0 - lmax), jnp.exp(l1 - lmax)
    lb = e0 / (e0 + e1)

    qg, kg, v, kd, decay = [], [], [], [], []
    for rs in all_chunks:
        yield 1
        f = lb + (1.0 - lb) * jax.nn.sigmoid(z.cols(_FA, A_WIDTH, rs))
        p = _cumprod_rows(f)
        inv_p = 1.0 / p
        k = 1.0 - f
        qg.append(_silu(z.cols(_QA, A_WIDTH, rs)) * p)
        kg.append(k * inv_p)
        v.append(z.cols(_VA, A_WIDTH, rs))
        p_last = p[chunk - 1:chunk, :]
        kd.append(kg[-1] * p_last)
        decay.append(p_last)
    qg_all, kg_all, v_all = cat_rows(qg), cat_rows(kg), cat_rows(v)

    scores = [_dot_nt(qg_all[:, sl], kg_all[:, sl]) for sl in head_sl]
    st_in = [[[states[s][0][hd]] for hd in range(A_HEADS)] for s in range(nseg)]
    for s in range(nseg):
        for c in range(nchunk):
            i = s * nchunk + c
            for hd, sl in enumerate(head_sl):
                st_in[s][hd].append(st_in[s][hd][c] * _as_column(decay[i][:, sl])
                                    + _dot_tn(kd[i][:, sl], v[i][:, sl]))
    new_st = [[st_in[s][hd][nchunk] for hd in range(A_HEADS)] for s in range(nseg)]
    yield 1
    tt = lax.broadcasted_iota(jnp.int32, (rows, rows), 0)
    ss = lax.broadcasted_iota(jnp.int32, (rows, rows), 1)
    shift = chunk.bit_length() - 1
    causal = ((tt >> shift) == (ss >> shift)) & (ss <= tt)
    o_intra = [_dot(jnp.where(causal, scores[hd], 0.0), v_all[:, sl]) for hd, sl in enumerate(head_sl)]
    yield 1
    for hd, sl in enumerate(head_sl):
        for s in range(nseg):
            for c, rs in enumerate(chunk_rows[s]):
                oa_ref[rs, sl] = o_intra[hd][rs, :] + _dot(qg[s * nchunk + c][:, sl], st_in[s][hd][c])

    yield _Reads(_XB)
    g_a = w["g_a_out"][...]
    a_in = []
    for i, rs in enumerate(all_chunks):
        yield i % 2
        normed = jnp.concatenate([_rms(oa_ref[rs, sl], g_a[:, sl]) for sl in head_sl], axis=-1)
        a_in.append(normed * _silu(z.cols(_GA, A_WIDTH, rs)))
    pa = _dot(cat_rows(a_in), _wt(w["w_a_down"][...]))

    pad = SUBLANES
    for s in range(nseg):
        xpad_ref[s * pad:(s + 1) * pad, :] = jnp.zeros((pad, B_WIDTH), f32)
        xpad_ref[(s + 1) * pad - CTX_ROWS:(s + 1) * pad, :] = states[s][1]
    yield _Reads(_GB)
    w_conv = w["w_conv"][...]
    sub = lax.broadcasted_iota(jnp.int32, (chunk // SUBLANES, SUBLANES, B_WIDTH), 1)
    xc, new_ctx = [], []
    for s in range(nseg):
        for c, rs in enumerate(chunk_rows[s]):
            yield 1
            if c == 0:
                ext = jnp.concatenate([xpad_ref[s * pad:(s + 1) * pad, :], z.cols(_XB, B_WIDTH, rs)], axis=0)
            else:
                ext = z.cols(_XB, B_WIDTH, slice(rs.start - pad, rs.stop))
            ext = _vreg_groups(ext)
            acc = w_conv[CONV_W - 1:CONV_W, :] * ext[1:]
            for j in range(1, CONV_W):
                rolled = pltpu.roll(ext, j, 1)
                shifted = jnp.where(sub >= j, rolled[1:], rolled[:-1])
                acc = acc + w_conv[CONV_W - 1 - j:CONV_W - j, :] * shifted
            xc.append(w["b_conv"][...] + acc.reshape(chunk, B_WIDTH))
        new_ctx.append(z.cols(_XB, B_WIDTH, slice((s + 1) * seg - CTX_ROWS, (s + 1) * seg)))

    xc_b = cat_rows(xc).astype(bf16)
    r_pre, i_pre = [], []
    for g in range(LRU_GROUPS):
        gs = slice(g * MXU_TILE, (g + 1) * MXU_TILE)
        r_pre.append(jnp.dot(xc_b[:, gs], _wt(w["w_lru_r"][g]), preferred_element_type=f32))
        i_pre.append(jnp.dot(xc_b[:, gs], _wt(w["w_lru_i"][g]), preferred_element_type=f32))
    r_pre, i_pre = jnp.concatenate(r_pre, axis=-1), jnp.concatenate(i_pre, axis=-1)
    neg_lam = -w["lru_lambda"][...]
    softplus = jnp.maximum(neg_lam, 0.0) + jnp.log1p(jnp.exp(-jnp.abs(neg_lam)))
    decay_rate = LRU_C * softplus
    a_l, u_l = [], []
    for s in range(nseg):
        for c, rs in enumerate(chunk_rows[s]):
            yield 2
            r = jax.nn.sigmoid(r_pre[rs, :] + w["b_lru_r"][...])
            ig = jax.nn.sigmoid(i_pre[rs, :] + w["b_lru_i"][...])
            nlog_a = r * decay_rate
            a = jnp.exp2(nlog_a * (-LOG2_E))
            m2 = jnp.tanh(nlog_a) * (a * a + 1.0)
            mult = jnp.where(m2 > 0.0, m2 * lax.rsqrt(m2), 0.0)
            if first_rows_start is not None and c == 0:
                first_row = lax.broadcasted_iota(jnp.int32, mult.shape, 0) == 0
                mult = jnp.where(first_row & first_rows_start, 1.0, mult)
            a_l.append(a)
            u_l.append(mult * ig * xc[s * nchunk + c])
    yield _Reads(IN_COLS)
    hb_gated, new_hl = [], []
    for s in range(nseg):
        carry = states[s][2]
        for c, rs in enumerate(chunk_rows[s]):
            yield 1
            i = s * nchunk + c
            hb, carry = _linear_scan_rows(a_l[i], u_l[i], carry)
            hb_gated.append(hb * _silu(z.cols(_GB, B_WIDTH, rs)))
        new_hl.append(carry)
    pb = _dot(cat_rows(hb_gated), _wt(w["w_b_down"][...]))

    scale = HEAD_DIM ** -0.5
    mem = [[kv(s, hd) for hd in range(C_HEADS)] for s in range(nseg)]
    seg_rows = [slice(s * seg, (s + 1) * seg) for s in range(nseg)]
    sc = [[_dot_nt(z.cols(_QC + hd * HEAD_DIM, HEAD_DIM, seg_rows[s]), mem[s][hd][0])
           for hd in range(C_HEADS)] for s in range(nseg)]
    pr = []
    for s in range(nseg):
        pr.append([])
        for hd in range(C_HEADS):
            yield 1
            p = jnp.exp2((sc[s][hd] - jnp.max(sc[s][hd], axis=-1, keepdims=True)) * (scale * LOG2_E))
            pr[s].append(p / jnp.sum(p, axis=-1, keepdims=True))
    oc = cat_rows([jnp.concatenate([_dot(pr[s][hd], mem[s][hd][1]) for hd in range(C_HEADS)], axis=-1)
                   for s in range(nseg)])
    yield 1
    pc = _dot(oc * _silu(z.cols(_GC, C_WIDTH)), _wt(w["w_c_down"][...]))

    merged = []
    for rs in all_chunks:
        yield 1
        merged.append(jax.nn.sigmoid(z.cols(_ZA, D_MODEL, rs)) * pa[rs, :]
                      + jax.nn.sigmoid(z.cols(_ZB, D_MODEL, rs)) * pb[rs, :]
                      + jax.nn.sigmoid(z.cols(_ZC, D_MODEL, rs)) * pc[rs, :])
    y = x + _dot(cat_rows(merged), _wt(w["w_out"][...]))
    y = _rms(y, w["g_final"][...])
    new_states = [(new_st[s], new_ctx[s], new_hl[s]) for s in range(nseg)]
    return y, new_states


_WEIGHT_NAMES = ("g_mix", "w_in", "lb_logits", "g_a_out", "w_a_down", "w_conv", "b_conv", "w_lru_r", "b_lru_r",
                 "w_lru_i", "b_lru_i", "lru_lambda", "w_b_down", "w_c_down", "w_out", "g_final")
_NW = len(_WEIGHT_NAMES)


def _store_seq_state(conv_ref, lru_ref, seq_idx, ctx, hl):
    for r in range(CTX_ROWS):
        conv_ref[r, pl.ds(seq_idx, 1), :] = ctx[r:r + 1, :]
    lru_ref[pl.ds(seq_idx, 1), :] = hl


def _prompt_kernel(steps_per_seq, *refs):
    x_ref, xn_ref, mk_ref, mv_ref = refs[:4]
    w = dict(zip(_WEIGHT_NAMES, refs[4:4 + _NW]))
    y_ref, hgrn_ref, conv_ref, lru_ref = refs[4 + _NW:8 + _NW]
    scratch = refs[8 + _NW:]
    st_ref, ctx_ref, hl_ref = scratch[:3]
    z_even, z_odd = _ZBuf(scratch[3:3 + Z_BLOCKS]), _ZBuf(scratch[3 + Z_BLOCKS:3 + 2 * Z_BLOCKS])
    oa_refs = scratch[3 + 2 * Z_BLOCKS:5 + 2 * Z_BLOCKS]
    xpad_refs = scratch[5 + 2 * Z_BLOCKS:7 + 2 * Z_BLOCKS]
    j = pl.program_id(0)
    tile = PROMPT_TILE
    seq_start = (j % steps_per_seq) == 0

    @pl.when(j == 0)
    def _():
        _project_in(x_ref[0:tile, :], z_even, w)

    @pl.when(seq_start)
    def _():
        st_ref[...] = jnp.zeros_like(st_ref)
        ctx_ref[...] = jnp.zeros_like(ctx_ref)
        hl_ref[...] = jnp.zeros_like(hl_ref)

    kv = lambda s, hd: (mk_ref[0, _head_rows(hd), :], mv_ref[0, _head_rows(hd), :])
    states = [([st_ref[hd] for hd in range(A_HEADS)], ctx_ref[...], hl_ref[...])]

    y, states = _interleave(
        _mix_stages(x_ref[0:tile, :], z_even, oa_refs[0], xpad_refs[0], kv, states, w,
                    seg=tile, chunk=HGRN_CHUNK, first_rows_start=seq_start),
        _project_in_blocks(x_ref[tile:2 * tile, :], z_odd, w))
    y_ref[0:tile, :] = y

    y, states = _interleave(
        _mix_stages(x_ref[tile:2 * tile, :], z_odd, oa_refs[1], xpad_refs[1], kv, states, w,
                    seg=tile, chunk=HGRN_CHUNK, first_rows_start=None),
        _project_in_blocks(xn_ref[...], z_even, w))
    y_ref[tile:2 * tile, :] = y

    st, ctx, hl = states[0]
    for hd in range(A_HEADS):
        st_ref[hd] = st[hd]
    ctx_ref[...] = ctx
    hl_ref[...] = hl

    @pl.when((j % steps_per_seq) == steps_per_seq - 1)
    def _():
        for hd in range(A_HEADS):
            hgrn_ref[0, hd] = st[hd]
        _store_seq_state(conv_ref, lru_ref, j // steps_per_seq, ctx, hl)


def _sample_kernel(nseq, seg, *refs):
    x_ref, mk_ref, mv_ref, hgrn_in, conv_in, lru_in = refs[:6]
    w = dict(zip(_WEIGHT_NAMES, refs[6:6 + _NW]))
    y_ref, hgrn_ref, conv_ref, lru_ref = refs[6 + _NW:10 + _NW]
    scratch = refs[10 + _NW:]
    z = _ZBuf(scratch[:Z_BLOCKS])
    oa_ref, xpad_ref = scratch[Z_BLOCKS:]
    states = [([hgrn_in[s, hd] for hd in range(A_HEADS)],
               jnp.concatenate([conv_in[r, s:s + 1, :] for r in range(CTX_ROWS)], axis=0),
               lru_in[s:s + 1, :]) for s in range(nseq)]
    for block in _project_in_blocks(x_ref[...], z, w, (0, _EARLY_COLS)):
        block()
    y, new_states = _interleave(
        _mix_stages(x_ref[...], z, oa_ref, xpad_ref,
                    lambda s, hd: (mk_ref[s, _head_rows(hd), :], mv_ref[s, _head_rows(hd), :]), states, w,
                    seg=seg, chunk=min(HGRN_CHUNK, seg), first_rows_start=None),
        _project_in_blocks(x_ref[...], z, w, (_EARLY_COLS, IN_COLS)), own_late_cols=True)
    y_ref[...] = y
    for s in range(nseq):
        st, ctx, hl = new_states[s]
        for hd in range(A_HEADS):
            hgrn_ref[s, hd] = st[hd]
        _store_seq_state(conv_ref, lru_ref, s, ctx, hl)


def _const_spec(shape):
    nd = len(shape)
    return pl.BlockSpec(shape, lambda *_: (0,) * nd, pipeline_mode=pl.Buffered(1))


PREP_STEPS = 8
_DENSE_WEIGHTS = ("w_in", "w_a_down", "w_b_down", "w_c_down", "w_out")
_U32 = jnp.uint32


def _prep_kernel(*refs):
    nd = len(_DENSE_WEIGHTS)
    dense_in, (lru_r_in, lru_i_in) = refs[:nd], refs[nd:nd + 2]
    mem_ref, g_mem_ref, wk_in, wv_in = refs[nd + 2:nd + 6]
    outs = refs[nd + 6:]
    dense_out, (lru_r_out, lru_i_out), (k_ref, v_ref) = outs[:nd], outs[nd:nd + 2], outs[nd + 2:nd + 4]
    tile_ref, wk_ref, wv_ref = outs[nd + 4:]
    for src, dst in zip(dense_in, dense_out):
        dst[...] = pltpu.bitcast(src[...].astype(bf16), _U32)

    @pl.when(pl.program_id(0) == 0)
    def _():
        wk_ref[...] = pltpu.bitcast(wk_in[...].astype(bf16), _U32)
        wv_ref[...] = pltpu.bitcast(wv_in[...].astype(bf16), _U32)
        per = MXU_TILE // B_BLOCK_DIM
        for src, dst in ((lru_r_in, lru_r_out), (lru_i_in, lru_i_out)):
            for g in range(LRU_GROUPS):
                tile_ref[...] = jnp.zeros_like(tile_ref)
                for p in range(per):
                    lo = p * B_BLOCK_DIM
                    tile_ref[lo:lo + B_BLOCK_DIM, lo:lo + B_BLOCK_DIM] = src[g * per + p]
                dst[g] = pltpu.bitcast(tile_ref[...].astype(bf16), _U32)

    hm = _rms(mem_ref[0], g_mem_ref[...]).astype(bf16)
    k = jnp.dot(hm, _wt(wk_ref[...]), preferred_element_type=f32)
    v = jnp.dot(hm, _wt(wv_ref[...]), preferred_element_type=f32)
    for hd in range(C_HEADS):
        sl = slice(hd * HEAD_DIM, (hd + 1) * HEAD_DIM)
        k_ref[0, _head_rows(hd), :] = k[:, sl]
        v_ref[0, _head_rows(hd), :] = v[:, sl]


def _prep_weights(dense, lru_r, lru_i, mem, g_mem, w_mem_k, w_mem_v):
    bsz = mem.shape[0]
    assert bsz == PREP_STEPS
    in_specs, out_specs, out_shape = [], [], []
    for wm in dense:
        k, n = wm.shape
        assert k % (4 * SUBLANES * PREP_STEPS) == 0
        in_specs.append(pl.BlockSpec((k // PREP_STEPS, n), lambda i: (i, 0)))
        out_specs.append(pl.BlockSpec((k // (2 * PREP_STEPS), n), lambda i: (i, 0)))
        out_shape.append(jax.ShapeDtypeStruct((k // 2, n), _U32))
    blk = (B_BLOCKS, B_BLOCK_DIM, B_BLOCK_DIM)
    tiles = (LRU_GROUPS, MXU_TILE // 2, MXU_TILE)
    in_specs += [pl.BlockSpec(blk, lambda i: (0, 0, 0))] * 2
    out_specs += [pl.BlockSpec(tiles, lambda i: (0, 0, 0))] * 2
    out_shape += [jax.ShapeDtypeStruct(tiles, _U32)] * 2
    in_specs += [pl.BlockSpec((1, N_MEM, D_MODEL), lambda i: (i, 0, 0)), _const_spec((1, D_MODEL)),
                 _const_spec((D_MODEL, C_WIDTH)), _const_spec((D_MODEL, C_WIDTH))]
    out_specs += [pl.BlockSpec((1,) + KV_ROWS, lambda i: (i, 0, 0))] * 2
    out_shape += [jax.ShapeDtypeStruct((bsz,) + KV_ROWS, f32)] * 2
    outs = pl.pallas_call(
        _prep_kernel, grid=(PREP_STEPS,), in_specs=in_specs, out_specs=out_specs, out_shape=out_shape,
        scratch_shapes=[pltpu.VMEM((MXU_TILE, MXU_TILE), f32),
                        pltpu.VMEM((D_MODEL // 2, C_WIDTH), _U32), pltpu.VMEM((D_MODEL // 2, C_WIDTH), _U32)],
        compiler_params=pltpu.CompilerParams(vmem_limit_bytes=VMEM_LIMIT_BYTES,
                                             dimension_semantics=("arbitrary",)),
        name="prep_weights",
    )(*dense, lru_r, lru_i, mem, g_mem, w_mem_k, w_mem_v)
    nd = len(dense)
    return outs[:nd], outs[nd], outs[nd + 1], outs[nd + 2], outs[nd + 3]


def kernel(x_prompt, x_sample, mem_prompt, cache_mem_k, cache_mem_v, state_hgrn, state_conv, state_lru, g_mix, w_in, lb_logits, g_a_out, w_a_down, w_conv, b_conv, w_lru_r, b_lru_r, w_lru_i, b_lru_i, lru_lambda, w_b_down, g_mem, w_mem_k, w_mem_v, w_c_down, w_out, g_final):
    bsz, seq, _ = x_prompt.shape
    dec_b, dec_seq, _ = x_sample.shape
    assert g_mix.shape[0] == 1, "single-layer stack only"
    assert seq % (2 * PROMPT_TILE) == 0 and PROMPT_TILE % HGRN_CHUNK == 0

    row = lambda a: a.reshape(1, -1).astype(f32)
    dense = dict(w_in=w_in[0], w_a_down=w_a_down[0], w_b_down=w_b_down[0], w_c_down=w_c_down[0], w_out=w_out[0])
    packed, lru_r_tiles, lru_i_tiles, mk, mv = _prep_weights(
        [dense[n] for n in _DENSE_WEIGHTS], w_lru_r[0], w_lru_i[0], mem_prompt, row(g_mem[0]), w_mem_k[0], w_mem_v[0])
    packed = dict(zip(_DENSE_WEIGHTS, packed))
    weights = dict(
        g_mix=row(g_mix[0]), w_in=packed["w_in"], lb_logits=lb_logits.astype(f32), g_a_out=row(g_a_out[0]),
        w_a_down=packed["w_a_down"], w_conv=w_conv[0].astype(f32), b_conv=row(b_conv[0]),
        w_lru_r=lru_r_tiles, b_lru_r=row(b_lru_r[0]), w_lru_i=lru_i_tiles, b_lru_i=row(b_lru_i[0]),
        lru_lambda=row(lru_lambda[0]), w_b_down=packed["w_b_down"], w_c_down=packed["w_c_down"],
        w_out=packed["w_out"], g_final=row(g_final))
    wlist = [weights[n] for n in _WEIGHT_NAMES]
    wspecs = [_const_spec(a.shape) for a in wlist]

    tile = PROMPT_TILE
    n_tiles = bsz * seq // tile
    steps_per_seq = seq // (2 * tile)
    zbuf = [pltpu.VMEM((tile, Z_BLK), f32)] * Z_BLOCKS
    y_p, hgrn_p, conv_p, lru_p = pl.pallas_call(
        functools.partial(_prompt_kernel, steps_per_seq),
        grid=(n_tiles // 2,),
        in_specs=[pl.BlockSpec((2 * tile, D_MODEL), lambda j: (j, 0)),
                  pl.BlockSpec((tile, D_MODEL), lambda j: (jnp.minimum(2 * j + 2, n_tiles - 1), 0)),
                  pl.BlockSpec((1,) + KV_ROWS, lambda j: (j // steps_per_seq, 0, 0)),
                  pl.BlockSpec((1,) + KV_ROWS, lambda j: (j // steps_per_seq, 0, 0))] + wspecs,
        out_specs=[pl.BlockSpec((2 * tile, D_MODEL), lambda j: (j, 0)),
                   pl.BlockSpec((1, A_HEADS, HEAD_DIM, HEAD_DIM), lambda j: (j // steps_per_seq, 0, 0, 0)),
                   pl.BlockSpec((CTX_ROWS, bsz, B_WIDTH), lambda j: (0, 0, 0)),
                   pl.BlockSpec((bsz, B_WIDTH), lambda j: (0, 0))],
        out_shape=[jax.ShapeDtypeStruct((bsz * seq, D_MODEL), f32),
                   jax.ShapeDtypeStruct((bsz, A_HEADS, HEAD_DIM, HEAD_DIM), f32),
                   jax.ShapeDtypeStruct((CTX_ROWS, bsz, B_WIDTH), f32),
                   jax.ShapeDtypeStruct((bsz, B_WIDTH), f32)],
        scratch_shapes=[pltpu.VMEM((A_HEADS, HEAD_DIM, HEAD_DIM), f32),
                        pltpu.VMEM((CTX_ROWS, B_WIDTH), f32),
                        pltpu.VMEM((1, B_WIDTH), f32)] + zbuf + zbuf
                       + [pltpu.VMEM((tile, A_WIDTH), f32)] * 2
                       + [pltpu.VMEM((SUBLANES, B_WIDTH), f32)] * 2,
        compiler_params=pltpu.CompilerParams(vmem_limit_bytes=VMEM_LIMIT_BYTES,
                                             dimension_semantics=("arbitrary",)),
        name="prompt_layer",
    )(x_prompt.reshape(bsz * seq, D_MODEL), x_prompt.reshape(bsz * seq, D_MODEL), mk, mv, *wlist)
    y_p = y_p.reshape(bsz, seq, D_MODEL)

    rows = dec_b * dec_seq
    full = lambda shape: pl.BlockSpec(shape, lambda *_: (0,) * len(shape))
    y_s, hgrn_s, conv_s, lru_s = pl.pallas_call(
        functools.partial(_sample_kernel, dec_b, dec_seq),
        grid=(1,),
        in_specs=[full((rows, D_MODEL)), full((dec_b,) + KV_ROWS), full((dec_b,) + KV_ROWS),
                  full((dec_b, A_HEADS, HEAD_DIM, HEAD_DIM)), full((CTX_ROWS, dec_b, B_WIDTH)),
                  full((dec_b, B_WIDTH))] + wspecs,
        out_specs=[full((rows, D_MODEL)), full((dec_b, A_HEADS, HEAD_DIM, HEAD_DIM)),
                   full((CTX_ROWS, dec_b, B_WIDTH)), full((dec_b, B_WIDTH))],
        out_shape=[jax.ShapeDtypeStruct((rows, D_MODEL), f32),
                   jax.ShapeDtypeStruct((dec_b, A_HEADS, HEAD_DIM, HEAD_DIM), f32),
                   jax.ShapeDtypeStruct((CTX_ROWS, dec_b, B_WIDTH), f32),
                   jax.ShapeDtypeStruct((dec_b, B_WIDTH), f32)],
        scratch_shapes=[pltpu.VMEM((rows, Z_BLK), f32)] * Z_BLOCKS
                       + [pltpu.VMEM((rows, A_WIDTH), f32),
                        pltpu.VMEM((dec_b * SUBLANES, B_WIDTH), f32)],
        compiler_params=pltpu.CompilerParams(vmem_limit_bytes=VMEM_LIMIT_BYTES),
        name="sample_layer",
    )(x_sample.reshape(rows, D_MODEL), cache_mem_k.reshape((dec_b,) + KV_ROWS),
      cache_mem_v.reshape((dec_b,) + KV_ROWS), state_hgrn[0], jnp.swapaxes(state_conv[0], 0, 1),
      state_lru[0], *wlist)

    return (y_p, y_s.reshape(dec_b, dec_seq, D_MODEL), hgrn_p[None], jnp.swapaxes(conv_p, 0, 1)[None],
            lru_p[None], mk.reshape(1, bsz, N_MEM, C_HEADS, HEAD_DIM),
            mv.reshape(1, bsz, N_MEM, C_HEADS, HEAD_DIM), hgrn_s[None], jnp.swapaxes(conv_s, 0, 1)[None],
            lru_s[None])
```

```python
import functools

import jax
import jax.numpy as jnp
from jax import lax
from jax.experimental import pallas as pl
from jax.experimental.pallas import tpu as pltpu

f32 = jnp.float32
bf16 = jnp.bfloat16

D_MODEL = 1024
N_MEM = 256
EPS = 1e-6
A_HEADS = 4
HEAD_DIM = 128
A_WIDTH = A_HEADS * HEAD_DIM
B_WIDTH = D_MODEL
B_BLOCKS = 16
B_BLOCK_DIM = B_WIDTH // B_BLOCKS
CONV_W = 4
LRU_C = 8.0
LOG2_E = 1.4426950408889634
C_HEADS = 4
C_WIDTH = C_HEADS * HEAD_DIM
assert A_HEADS == C_HEADS
HGRN_CHUNK = 64
IN_COLS = 4 * A_WIDTH + 2 * B_WIDTH + 2 * C_WIDTH + 3 * D_MODEL

_QA, _FA, _VA, _GA = 0, A_WIDTH, 2 * A_WIDTH, 3 * A_WIDTH
_XB = 4 * A_WIDTH
_GB = _XB + B_WIDTH
_QC = _GB + B_WIDTH
_GC = _QC + C_WIDTH
_ZA = _GC + C_WIDTH
_ZB = _ZA + D_MODEL
_ZC = _ZB + D_MODEL
_EARLY_COLS = _GB
_LATE_READS = -1

MXU_TILE = 256
LRU_GROUPS = B_WIDTH // MXU_TILE
SUBLANES = 8
CTX_ROWS = CONV_W - 1

PROMPT_TILE = 256
Z_BLK = 1024
Z_BLOCKS = IN_COLS // Z_BLK
VMEM_LIMIT_BYTES = 60 * 1024 * 1024


def _rms(x, g):
    return x * lax.rsqrt(jnp.mean(x * x, axis=-1, keepdims=True) + EPS) * g


def _wt(ref_or_val):
    return pltpu.bitcast(ref_or_val, bf16)


def _dot(a, b):
    return jnp.dot(a.astype(bf16), b.astype(bf16), preferred_element_type=f32)


def _dot_nt(a, b):
    return lax.dot_general(a.astype(bf16), b.astype(bf16), (((1,), (1,)), ((), ())),
                           preferred_element_type=f32)


def _dot_tn(a, b):
    return lax.dot_general(a.astype(bf16), b.astype(bf16), (((0,), (0,)), ((), ())),
                           preferred_element_type=f32)


def _silu(x):
    return x * jax.nn.sigmoid(x)


KV_ROWS = (N_MEM * C_HEADS, HEAD_DIM)


def _head_rows(hd):
    return pl.ds(hd, N_MEM, stride=C_HEADS)


def _as_column(row):
    return jnp.transpose(jnp.broadcast_to(row, (SUBLANES, row.shape[1])))[:, 0:1]


def _vreg_groups(x):
    rows, width = x.shape
    return x.reshape(rows // SUBLANES, SUBLANES, width)


def _cumprod_rows(x):
    rows = x.shape[0]
    x3 = _vreg_groups(x)
    sub = lax.broadcasted_iota(jnp.int32, x3.shape, 1)
    d = 1
    while d < SUBLANES:
        x3 = x3 * jnp.where(sub >= d, pltpu.roll(x3, d, 1), 1.0)
        d *= 2
    out, carry = [], None
    for g in range(rows // SUBLANES):
        cur = x3[g] if carry is None else x3[g] * carry
        carry = cur[SUBLANES - 1:SUBLANES, :]
        out.append(cur)
    return jnp.concatenate(out, axis=0)


def _linear_scan_rows(a, u, carry):
    rows = a.shape[0]
    a3, u3 = _vreg_groups(a), _vreg_groups(u)
    sub = lax.broadcasted_iota(jnp.int32, a3.shape, 1)
    d = 1
    while d < SUBLANES:
        keep = sub >= d
        u3 = a3 * jnp.where(keep, pltpu.roll(u3, d, 1), 0.0) + u3
        a3 = a3 * jnp.where(keep, pltpu.roll(a3, d, 1), 1.0)
        d *= 2
    out = []
    for g in range(rows // SUBLANES):
        cur = u3[g] + a3[g] * carry
        carry = cur[SUBLANES - 1:SUBLANES, :]
        out.append(cur)
    return jnp.concatenate(out, axis=0), carry


class _ZBuf:
    def __init__(self, refs):
        self.refs = refs

    def cols(self, c0, width, rows=slice(None)):
        blk, off = divmod(c0, Z_BLK)
        assert off + width <= Z_BLK
        return self.refs[blk][rows, off:off + width]


def _project_in_blocks(x, z, w, cols=(0, IN_COLS)):
    h = _rms(x, w["g_mix"][...]).astype(bf16)

    def block(c0):
        blk, off = divmod(c0, Z_BLK)
        z.refs[blk][:, off:off + MXU_TILE] = jnp.dot(h, _wt(w["w_in"][:, c0:c0 + MXU_TILE]),
                                                     preferred_element_type=f32)

    return [functools.partial(block, c0) for c0 in range(cols[0], cols[1], MXU_TILE)]


def _project_in(x, z, w):
    for block in _project_in_blocks(x, z, w):
        block()


def _interleave(stages, blocks, own_late_cols=False):
    blocks = list(blocks)
    while True:
        try:
            n = next(stages)
        except StopIteration as done:
            result = done.value
            break
        if n == _LATE_READS:
            n = len(blocks) if own_late_cols else 0
        for _ in range(min(n, len(blocks))):
            blocks.pop(0)()
    for block in blocks:
        block()
    return result


def _mix_stages(x, z, oa_ref, xpad_ref, kv, states, w, *, seg, chunk, first_rows_start):
    rows = x.shape[0]
    nseg = rows // seg
    nchunk = seg // chunk
    chunk_rows = [[slice(s * seg + c * chunk, s * seg + (c + 1) * chunk) for c in range(nchunk)]
                  for s in range(nseg)]
    all_chunks = [rs for per_seg in chunk_rows for rs in per_seg]
    head_sl = [slice(hd * HEAD_DIM, (hd + 1) * HEAD_DIM) for hd in range(A_HEADS)]
    cat_rows = lambda parts: parts[0] if len(parts) == 1 else jnp.concatenate(parts, axis=0)

    lg = w["lb_logits"][...]
    l0, l1 = lg[0:1, :], lg[1:2, :]
    lmax = jnp.maximum(l0, l1)
    e0, e1 = jnp.exp(l0 - lmax), jnp.exp(l1 - lmax)
    lb = e0 / (e0 + e1)

    qg, kg, v, kd, decay = [], [], [], [], []
    for rs in all_chunks:
        yield 1
        f = lb + (1.0 - lb) * jax.nn.sigmoid(z.cols(_FA, A_WIDTH, rs))
        p = _cumprod_rows(f)
        inv_p = 1.0 / p
        k = 1.0 - f
        qg.append(_silu(z.cols(_QA, A_WIDTH, rs)) * p)
        kg.append(k * inv_p)
        v.append(z.cols(_VA, A_WIDTH, rs))
        p_last = p[chunk - 1:chunk, :]
        kd.append(kg[-1] * p_last)
        decay.append(p_last)
    qg_all, kg_all, v_all = cat_rows(qg), cat_rows(kg), cat_rows(v)

    scores = [_dot_nt(qg_all[:, sl], kg_all[:, sl]) for sl in head_sl]
    st_in = [[[states[s][0][hd]] for hd in range(A_HEADS)] for s in range(nseg)]
    for s in range(nseg):
        for c in range(nchunk):
            i = s * nchunk + c
            for hd, sl in enumerate(head_sl):
                st_in[s][hd].append(st_in[s][hd][c] * _as_column(decay[i][:, sl])
                                    + _dot_tn(kd[i][:, sl], v[i][:, sl]))
    new_st = [[st_in[s][hd][nchunk] for hd in range(A_HEADS)] for s in range(nseg)]
    yield 1
    tt = lax.broadcasted_iota(jnp.int32, (rows, rows), 0)
    ss = lax.broadcasted_iota(jnp.int32, (rows, rows), 1)
    shift = chunk.bit_length() - 1
    causal = ((tt >> shift) == (ss >> shift)) & (ss <= tt)
    o_intra = [_dot(jnp.where(causal, scores[hd], 0.0), v_all[:, sl]) for hd, sl in enumerate(head_sl)]
    yield 1
    for hd, sl in enumerate(head_sl):
        for s in range(nseg):
            for c, rs in enumerate(chunk_rows[s]):
                oa_ref[rs, sl] = o_intra[hd][rs, :] + _dot(qg[s * nchunk + c][:, sl], st_in[s][hd][c])

    g_a = w["g_a_out"][...]
    a_in = []
    for i, rs in enumerate(all_chunks):
        yield i % 2
        normed = jnp.concatenate([_rms(oa_ref[rs, sl], g_a[:, sl]) for sl in head_sl], axis=-1)
        a_in.append(normed * _silu(z.cols(_GA, A_WIDTH, rs)))
    pa = _dot(cat_rows(a_in), _wt(w["w_a_down"][...]))

    pad = SUBLANES
    for s in range(nseg):
        xpad_ref[s * pad:(s + 1) * pad, :] = jnp.zeros((pad, B_WIDTH), f32)
        xpad_ref[(s + 1) * pad - CTX_ROWS:(s + 1) * pad, :] = states[s][1]
    w_conv = w["w_conv"][...]
    sub = lax.broadcasted_iota(jnp.int32, (chunk // SUBLANES, SUBLANES, B_WIDTH), 1)
    xc, new_ctx = [], []
    for s in range(nseg):
        for c, rs in enumerate(chunk_rows[s]):
            yield 1
            if c == 0:
                ext = jnp.concatenate([xpad_ref[s * pad:(s + 1) * pad, :], z.cols(_XB, B_WIDTH, rs)], axis=0)
            else:
                ext = z.cols(_XB, B_WIDTH, slice(rs.start - pad, rs.stop))
            ext = _vreg_groups(ext)
            acc = w_conv[CONV_W - 1:CONV_W, :] * ext[1:]
            for j in range(1, CONV_W):
                rolled = pltpu.roll(ext, j, 1)
                shifted = jnp.where(sub >= j, rolled[1:], rolled[:-1])
                acc = acc + w_conv[CONV_W - 1 - j:CONV_W - j, :] * shifted
            xc.append(w["b_conv"][...] + acc.reshape(chunk, B_WIDTH))
        new_ctx.append(z.cols(_XB, B_WIDTH, slice((s + 1) * seg - CTX_ROWS, (s + 1) * seg)))

    xc_b = cat_rows(xc).astype(bf16)
    r_pre, i_pre = [], []
    for g in range(LRU_GROUPS):
        gs = slice(g * MXU_TILE, (g + 1) * MXU_TILE)
        r_pre.append(jnp.dot(xc_b[:, gs], _wt(w["w_lru_r"][g]), preferred_element_type=f32))
        i_pre.append(jnp.dot(xc_b[:, gs], _wt(w["w_lru_i"][g]), preferred_element_type=f32))
    r_pre, i_pre = jnp.concatenate(r_pre, axis=-1), jnp.concatenate(i_pre, axis=-1)
    neg_lam = -w["lru_lambda"][...]
    softplus = jnp.maximum(neg_lam, 0.0) + jnp.log1p(jnp.exp(-jnp.abs(neg_lam)))
    decay_rate = LRU_C * softplus
    a_l, u_l = [], []
    for s in range(nseg):
        for c, rs in enumerate(chunk_rows[s]):
            yield 2
            r = jax.nn.sigmoid(r_pre[rs, :] + w["b_lru_r"][...])
            ig = jax.nn.sigmoid(i_pre[rs, :] + w["b_lru_i"][...])
            nlog_a = r * decay_rate
            a = jnp.exp2(nlog_a * (-LOG2_E))
            m2 = jnp.tanh(nlog_a) * (a * a + 1.0)
            mult = jnp.where(m2 > 0.0, m2 * lax.rsqrt(m2), 0.0)
            if first_rows_start is not None and c == 0:
                first_row = lax.broadcasted_iota(jnp.int32, mult.shape, 0) == 0
                mult = jnp.where(first_row & first_rows_start, 1.0, mult)
            a_l.append(a)
            u_l.append(mult * ig * xc[s * nchunk + c])
    yield _LATE_READS
    hb_gated, new_hl = [], []
    for s in range(nseg):
        carry = states[s][2]
        for c, rs in enumerate(chunk_rows[s]):
            yield 1
            i = s * nchunk + c
            hb, carry = _linear_scan_rows(a_l[i], u_l[i], carry)
            hb_gated.append(hb * _silu(z.cols(_GB, B_WIDTH, rs)))
        new_hl.append(carry)
    pb = _dot(cat_rows(hb_gated), _wt(w["w_b_down"][...]))

    scale = HEAD_DIM ** -0.5
    mem = [[kv(s, hd) for hd in range(C_HEADS)] for s in range(nseg)]
    seg_rows = [slice(s * seg, (s + 1) * seg) for s in range(nseg)]
    sc = [[_dot_nt(z.cols(_QC + hd * HEAD_DIM, HEAD_DIM, seg_rows[s]), mem[s][hd][0])
           for hd in range(C_HEADS)] for s in range(nseg)]
    pr = []
    for s in range(nseg):
        pr.append([])
        for hd in range(C_HEADS):
            yield 1
            p = jnp.exp2((sc[s][hd] - jnp.max(sc[s][hd], axis=-1, keepdims=True)) * (scale * LOG2_E))
            pr[s].append(p / jnp.sum(p, axis=-1, keepdims=True))
    oc = cat_rows([jnp.concatenate([_dot(pr[s][hd], mem[s][hd][1]) for hd in range(C_HEADS)], axis=-1)
                   for s in range(nseg)])
    yield 1
    pc = _dot(oc * _silu(z.cols(_GC, C_WIDTH)), _wt(w["w_c_down"][...]))

    merged = []
    for rs in all_chunks:
        yield 1
        merged.append(jax.nn.sigmoid(z.cols(_ZA, D_MODEL, rs)) * pa[rs, :]
                      + jax.nn.sigmoid(z.cols(_ZB, D_MODEL, rs)) * pb[rs, :]
                      + jax.nn.sigmoid(z.cols(_ZC, D_MODEL, rs)) * pc[rs, :])
    y = x[...] + _dot(cat_rows(merged), _wt(w["w_out"][...]))
    y = _rms(y, w["g_final"][...])
    new_states = [(new_st[s], new_ctx[s], new_hl[s]) for s in range(nseg)]
    return y, new_states


_WEIGHT_NAMES = ("g_mix", "w_in", "lb_logits", "g_a_out", "w_a_down", "w_conv", "b_conv", "w_lru_r", "b_lru_r",
                 "w_lru_i", "b_lru_i", "lru_lambda", "w_b_down", "w_c_down", "w_out", "g_final")
_NW = len(_WEIGHT_NAMES)


def _store_seq_state(conv_ref, lru_ref, seq_idx, ctx, hl):
    for r in range(CTX_ROWS):
        conv_ref[r, pl.ds(seq_idx, 1), :] = ctx[r:r + 1, :]
    lru_ref[pl.ds(seq_idx, 1), :] = hl


def _prompt_kernel(steps_per_seq, *refs):
    x_ref, xn_ref, mk_ref, mv_ref = refs[:4]
    w = dict(zip(_WEIGHT_NAMES, refs[4:4 + _NW]))
    y_ref, hgrn_ref, conv_ref, lru_ref = refs[4 + _NW:8 + _NW]
    scratch = refs[8 + _NW:]
    st_ref, ctx_ref, hl_ref = scratch[:3]
    z_even, z_odd = _ZBuf(scratch[3:3 + Z_BLOCKS]), _ZBuf(scratch[3 + Z_BLOCKS:3 + 2 * Z_BLOCKS])
    oa_refs = scratch[3 + 2 * Z_BLOCKS:5 + 2 * Z_BLOCKS]
    xpad_refs = scratch[5 + 2 * Z_BLOCKS:7 + 2 * Z_BLOCKS]
    j = pl.program_id(0)
    tile = PROMPT_TILE
    seq_start = (j % steps_per_seq) == 0

    @pl.when(j == 0)
    def _():
        _project_in(x_ref[0:tile, :], z_even, w)

    @pl.when(seq_start)
    def _():
        st_ref[...] = jnp.zeros_like(st_ref)
        ctx_ref[...] = jnp.zeros_like(ctx_ref)
        hl_ref[...] = jnp.zeros_like(hl_ref)

    kv = lambda s, hd: (mk_ref[0, _head_rows(hd), :], mv_ref[0, _head_rows(hd), :])
    states = [([st_ref[hd] for hd in range(A_HEADS)], ctx_ref[...], hl_ref[...])]

    y, states = _interleave(
        _mix_stages(x_ref.at[0:tile, :], z_even, oa_refs[0], xpad_refs[0], kv, states, w,
                    seg=tile, chunk=HGRN_CHUNK, first_rows_start=seq_start),
        _project_in_blocks(x_ref[tile:2 * tile, :], z_odd, w))
    y_ref[0:tile, :] = y

    y, states = _interleave(
        _mix_stages(x_ref.at[tile:2 * tile, :], z_odd, oa_refs[1], xpad_refs[1], kv, states, w,
                    seg=tile, chunk=HGRN_CHUNK, first_rows_start=None),
        _project_in_blocks(xn_ref[...], z_even, w))
    y_ref[tile:2 * tile, :] = y

    st, ctx, hl = states[0]
    for hd in range(A_HEADS):
        st_ref[hd] = st[hd]
    ctx_ref[...] = ctx
    hl_ref[...] = hl

    @pl.when((j % steps_per_seq) == steps_per_seq - 1)
    def _():
        for hd in range(A_HEADS):
            hgrn_ref[0, hd] = st[hd]
        _store_seq_state(conv_ref, lru_ref, j // steps_per_seq, ctx, hl)


def _sample_kernel(nseq, seg, *refs):
    x_ref, mk_ref, mv_ref, hgrn_in, conv_in, lru_in = refs[:6]
    w = dict(zip(_WEIGHT_NAMES, refs[6:6 + _NW]))
    y_ref, hgrn_ref, conv_ref, lru_ref = refs[6 + _NW:10 + _NW]
    scratch = refs[10 + _NW:]
    z = _ZBuf(scratch[:Z_BLOCKS])
    oa_ref, xpad_ref = scratch[Z_BLOCKS:]
    states = [([hgrn_in[s, hd] for hd in range(A_HEADS)],
               jnp.concatenate([conv_in[r, s:s + 1, :] for r in range(CTX_ROWS)], axis=0),
               lru_in[s:s + 1, :]) for s in range(nseq)]
    for block in _project_in_blocks(x_ref[...], z, w, (0, _EARLY_COLS)):
        block()
    y, new_states = _interleave(
        _mix_stages(x_ref, z, oa_ref, xpad_ref,
                    lambda s, hd: (mk_ref[s, _head_rows(hd), :], mv_ref[s, _head_rows(hd), :]), states, w,
                    seg=seg, chunk=min(HGRN_CHUNK, seg), first_rows_start=None),
        _project_in_blocks(x_ref[...], z, w, (_EARLY_COLS, IN_COLS)), own_late_cols=True)
    y_ref[...] = y
    for s in range(nseq):
        st, ctx, hl = new_states[s]
        for hd in range(A_HEADS):
            hgrn_ref[s, hd] = st[hd]
        _store_seq_state(conv_ref, lru_ref, s, ctx, hl)


def _const_spec(shape):
    nd = len(shape)
    return pl.BlockSpec(shape, lambda *_: (0,) * nd, pipeline_mode=pl.Buffered(1))


PREP_STEPS = 8
_DENSE_WEIGHTS = ("w_in", "w_a_down", "w_b_down", "w_c_down", "w_out")
_U32 = jnp.uint32


def _prep_kernel(*refs):
    nd = len(_DENSE_WEIGHTS)
    dense_in, (lru_r_in, lru_i_in) = refs[:nd], refs[nd:nd + 2]
    mem_ref, g_mem_ref, wk_in, wv_in = refs[nd + 2:nd + 6]
    outs = refs[nd + 6:]
    dense_out, (lru_r_out, lru_i_out), (k_ref, v_ref) = outs[:nd], outs[nd:nd + 2], outs[nd + 2:nd + 4]
    tile_ref, wk_ref, wv_ref = outs[nd + 4:]
    for src, dst in zip(dense_in, dense_out):
        dst[...] = pltpu.bitcast(src[...].astype(bf16), _U32)

    @pl.when(pl.program_id(0) == 0)
    def _():
        wk_ref[...] = pltpu.bitcast(wk_in[...].astype(bf16), _U32)
        wv_ref[...] = pltpu.bitcast(wv_in[...].astype(bf16), _U32)
        per = MXU_TILE // B_BLOCK_DIM
        for src, dst in ((lru_r_in, lru_r_out), (lru_i_in, lru_i_out)):
            for g in range(LRU_GROUPS):
                tile_ref[...] = jnp.zeros_like(tile_ref)
                for p in range(per):
                    lo = p * B_BLOCK_DIM
                    tile_ref[lo:lo + B_BLOCK_DIM, lo:lo + B_BLOCK_DIM] = src[g * per + p]
                dst[g] = pltpu.bitcast(tile_ref[...].astype(bf16), _U32)

    hm = _rms(mem_ref[0], g_mem_ref[...]).astype(bf16)
    k = jnp.dot(hm, _wt(wk_ref[...]), preferred_element_type=f32)
    v = jnp.dot(hm, _wt(wv_ref[...]), preferred_element_type=f32)
    for hd in range(C_HEADS):
        sl = slice(hd * HEAD_DIM, (hd + 1) * HEAD_DIM)
        k_ref[0, _head_rows(hd), :] = k[:, sl]
        v_ref[0, _head_rows(hd), :] = v[:, sl]


def _prep_weights(dense, lru_r, lru_i, mem, g_mem, w_mem_k, w_mem_v):
    bsz = mem.shape[0]
    assert bsz == PREP_STEPS
    in_specs, out_specs, out_shape = [], [], []
    for wm in dense:
        k, n = wm.shape
        assert k % (4 * SUBLANES * PREP_STEPS) == 0
        in_specs.append(pl.BlockSpec((k // PREP_STEPS, n), lambda i: (i, 0)))
        out_specs.append(pl.BlockSpec((k // (2 * PREP_STEPS), n), lambda i: (i, 0)))
        out_shape.append(jax.ShapeDtypeStruct((k // 2, n), _U32))
    blk = (B_BLOCKS, B_BLOCK_DIM, B_BLOCK_DIM)
    tiles = (LRU_GROUPS, MXU_TILE // 2, MXU_TILE)
    in_specs += [pl.BlockSpec(blk, lambda i: (0, 0, 0))] * 2
    out_specs += [pl.BlockSpec(tiles, lambda i: (0, 0, 0))] * 2
    out_shape += [jax.ShapeDtypeStruct(tiles, _U32)] * 2
    in_specs += [pl.BlockSpec((1, N_MEM, D_MODEL), lambda i: (i, 0, 0)), _const_spec((1, D_MODEL)),
                 _const_spec((D_MODEL, C_WIDTH)), _const_spec((D_MODEL, C_WIDTH))]
    out_specs += [pl.BlockSpec((1,) + KV_ROWS, lambda i: (i, 0, 0))] * 2
    out_shape += [jax.ShapeDtypeStruct((bsz,) + KV_ROWS, f32)] * 2
    outs = pl.pallas_call(
        _prep_kernel, grid=(PREP_STEPS,), in_specs=in_specs, out_specs=out_specs, out_shape=out_shape,
        scratch_shapes=[pltpu.VMEM((MXU_TILE, MXU_TILE), f32),
                        pltpu.VMEM((D_MODEL // 2, C_WIDTH), _U32), pltpu.VMEM((D_MODEL // 2, C_WIDTH), _U32)],
        compiler_params=pltpu.CompilerParams(vmem_limit_bytes=VMEM_LIMIT_BYTES,
                                             dimension_semantics=("arbitrary",)),
        name="prep_weights",
    )(*dense, lru_r, lru_i, mem, g_mem, w_mem_k, w_mem_v)
    nd = len(dense)
    return outs[:nd], outs[nd], outs[nd + 1], outs[nd + 2], outs[nd + 3]


def kernel(x_prompt, x_sample, mem_prompt, cache_mem_k, cache_mem_v, state_hgrn, state_conv, state_lru, g_mix, w_in, lb_logits, g_a_out, w_a_down, w_conv, b_conv, w_lru_r, b_lru_r, w_lru_i, b_lru_i, lru_lambda, w_b_down, g_mem, w_mem_k, w_mem_v, w_c_down, w_out, g_final):
    bsz, seq, _ = x_prompt.shape
    dec_b, dec_seq, _ = x_sample.shape
    assert g_mix.shape[0] == 1, "single-layer stack only"
    assert seq % (2 * PROMPT_TILE) == 0 and PROMPT_TILE % HGRN_CHUNK == 0

    row = lambda a: a.reshape(1, -1).astype(f32)
    dense = dict(w_in=w_in[0], w_a_down=w_a_down[0], w_b_down=w_b_down[0], w_c_down=w_c_down[0], w_out=w_out[0])
    packed, lru_r_tiles, lru_i_tiles, mk, mv = _prep_weights(
        [dense[n] for n in _DENSE_WEIGHTS], w_lru_r[0], w_lru_i[0], mem_prompt, row(g_mem[0]), w_mem_k[0], w_mem_v[0])
    packed = dict(zip(_DENSE_WEIGHTS, packed))
    weights = dict(
        g_mix=row(g_mix[0]), w_in=packed["w_in"], lb_logits=lb_logits.astype(f32), g_a_out=row(g_a_out[0]),
        w_a_down=packed["w_a_down"], w_conv=w_conv[0].astype(f32), b_conv=row(b_conv[0]),
        w_lru_r=lru_r_tiles, b_lru_r=row(b_lru_r[0]), w_lru_i=lru_i_tiles, b_lru_i=row(b_lru_i[0]),
        lru_lambda=row(lru_lambda[0]), w_b_down=packed["w_b_down"], w_c_down=packed["w_c_down"],
        w_out=packed["w_out"], g_final=row(g_final))
    wlist = [weights[n] for n in _WEIGHT_NAMES]
    wspecs = [_const_spec(a.shape) for a in wlist]

    tile = PROMPT_TILE
    n_tiles = bsz * seq // tile
    steps_per_seq = seq // (2 * tile)
    zbuf = [pltpu.VMEM((tile, Z_BLK), f32)] * Z_BLOCKS
    y_p, hgrn_p, conv_p, lru_p = pl.pallas_call(
        functools.partial(_prompt_kernel, steps_per_seq),
        grid=(n_tiles // 2,),
        in_specs=[pl.BlockSpec((2 * tile, D_MODEL), lambda j: (j, 0)),
                  pl.BlockSpec((tile, D_MODEL), lambda j: (jnp.minimum(2 * j + 2, n_tiles - 1), 0)),
                  pl.BlockSpec((1,) + KV_ROWS, lambda j: (j // steps_per_seq, 0, 0)),
                  pl.BlockSpec((1,) + KV_ROWS, lambda j: (j // steps_per_seq, 0, 0))] + wspecs,
        out_specs=[pl.BlockSpec((2 * tile, D_MODEL), lambda j: (j, 0)),
                   pl.BlockSpec((1, A_HEADS, HEAD_DIM, HEAD_DIM), lambda j: (j // steps_per_seq, 0, 0, 0)),
                   pl.BlockSpec((CTX_ROWS, bsz, B_WIDTH), lambda j: (0, 0, 0)),
                   pl.BlockSpec((bsz, B_WIDTH), lambda j: (0, 0))],
        out_shape=[jax.ShapeDtypeStruct((bsz * seq, D_MODEL), f32),
                   jax.ShapeDtypeStruct((bsz, A_HEADS, HEAD_DIM, HEAD_DIM), f32),
                   jax.ShapeDtypeStruct((CTX_ROWS, bsz, B_WIDTH), f32),
                   jax.ShapeDtypeStruct((bsz, B_WIDTH), f32)],
        scratch_shapes=[pltpu.VMEM((A_HEADS, HEAD_DIM, HEAD_DIM), f32),
                        pltpu.VMEM((CTX_ROWS, B_WIDTH), f32),
                        pltpu.VMEM((1, B_WIDTH), f32)] + zbuf + zbuf
                       + [pltpu.VMEM((tile, A_WIDTH), f32)] * 2
                       + [pltpu.VMEM((SUBLANES, B_WIDTH), f32)] * 2,
        compiler_params=pltpu.CompilerParams(vmem_limit_bytes=VMEM_LIMIT_BYTES,
                                             dimension_semantics=("arbitrary",)),
        name="prompt_layer",
    )(x_prompt.reshape(bsz * seq, D_MODEL), x_prompt.reshape(bsz * seq, D_MODEL), mk, mv, *wlist)
    y_p = y_p.reshape(bsz, seq, D_MODEL)

    rows = dec_b * dec_seq
    full = lambda shape: pl.BlockSpec(shape, lambda *_: (0,) * len(shape))
    y_s, hgrn_s, conv_s, lru_s = pl.pallas_call(
        functools.partial(_sample_kernel, dec_b, dec_seq),
        grid=(1,),
        in_specs=[full((rows, D_MODEL)), full((dec_b,) + KV_ROWS), full((dec_b,) + KV_ROWS),
                  full((dec_b, A_HEADS, HEAD_DIM, HEAD_DIM)), full((CTX_ROWS, dec_b, B_WIDTH)),
                  full((dec_b, B_WIDTH))] + wspecs,
        out_specs=[full((rows, D_MODEL)), full((dec_b, A_HEADS, HEAD_DIM, HEAD_DIM)),
                   full((CTX_ROWS, dec_b, B_WIDTH)), full((dec_b, B_WIDTH))],
        out_shape=[jax.ShapeDtypeStruct((rows, D_MODEL), f32),
                   jax.ShapeDtypeStruct((dec_b, A_HEADS, HEAD_DIM, HEAD_DIM), f32),
                   jax.ShapeDtypeStruct((CTX_ROWS, dec_b, B_WIDTH), f32),
                   jax.ShapeDtypeStruct((dec_b, B_WIDTH), f32)],
        scratch_shapes=[pltpu.VMEM((rows, Z_BLK), f32)] * Z_BLOCKS
                       + [pltpu.VMEM((rows, A_WIDTH), f32),
                        pltpu.VMEM((dec_b * SUBLANES, B_WIDTH), f32)],
        compiler_params=pltpu.CompilerParams(vmem_limit_bytes=VMEM_LIMIT_BYTES),
        name="sample_layer",
    )(x_sample.reshape(rows, D_MODEL), cache_mem_k.reshape((dec_b,) + KV_ROWS),
      cache_mem_v.reshape((dec_b,) + KV_ROWS), state_hgrn[0], jnp.swapaxes(state_conv[0], 0, 1),
      state_lru[0], *wlist)

    return (y_p, y_s.reshape(dec_b, dec_seq, D_MODEL), hgrn_p[None], jnp.swapaxes(conv_p, 0, 1)[None],
            lru_p[None], mk.reshape(1, bsz, N_MEM, C_HEADS, HEAD_DIM),
            mv.reshape(1, bsz, N_MEM, C_HEADS, HEAD_DIM), hgrn_s[None], jnp.swapaxes(conv_s, 0, 1)[None],
            lru_s[None])
```
